```python
import jax, jax.numpy as jnp
from jax import lax
import numpy as np

D_MODEL = 1024
BATCH = 8
SEQ = 8192
DEPTH = 4

CHUNK = 64
EXPAND = 2
D_MIX = EXPAND * D_MODEL
D_A = D_MIX // 2
D_B = D_MIX - D_A
D_C = D_MIX // 2
D_D = D_MIX - D_C
POOL_WINDOWS = (2, 4, 8, 16)
N_POOL_GROUPS = len(POOL_WINDOWS)
POOL_GROUP = D_A // N_POOL_GROUPS
SHORT_CONV = 3
SGU_BLOCK = 128
SGU_HEADS = 4
SGU_HEAD_DIM = D_C // SGU_HEADS
CONF_CONV = 31
IN_EVEN = 2 * D_A + 4 * D_B
IN_ODD = 3 * D_C + 3 * D_D
N_EVEN = (DEPTH + 1) // 2
N_ODD = DEPTH // 2
DEEPNORM_ALPHA = (2 * DEPTH) ** 0.25
DEEPNORM_BETA = (8 * DEPTH) ** -0.25
LN_EPS = 1e-5

kernel_name = "hybrid_pool_conv_sgu_conformer_deepnorm"


def layer_norm(x, g, b):
    xf = x.astype(jnp.float32)
    mu = jnp.mean(xf, axis=-1, keepdims=True)
    var = jnp.mean(jnp.square(xf - mu), axis=-1, keepdims=True)
    return ((xf - mu) * lax.rsqrt(var + LN_EPS) * g.astype(jnp.float32) + b.astype(jnp.float32)).astype(x.dtype)


def split_cols(z, sizes):
    points = list(np.cumsum(sizes)[:-1])
    return jnp.split(z, points, axis=-1)


def causal_depthwise_conv(z, w, b):
    k = w.shape[0]
    y = lax.conv_general_dilated(
        z, w[:, None, :].astype(z.dtype), window_strides=(1,), padding=[(k - 1, 0)],
        dimension_numbers=('NWC', 'WIO', 'NWC'), feature_group_count=z.shape[-1])
    return y + b.astype(z.dtype)


def multi_scale_pool(z):
    s = z.shape[1]
    zf = z.astype(jnp.float32)
    cs0 = jnp.pad(jnp.cumsum(zf, axis=1), ((0, 0), (1, 0), (0, 0)))
    pos = jnp.arange(1, s + 1, dtype=jnp.float32)
    means = []
    for g, w in enumerate(POOL_WINDOWS):
        cg = cs0[..., g * POOL_GROUP:(g + 1) * POOL_GROUP]
        lagged = jnp.pad(cg[:, :s + 1 - w], ((0, 0), (w - 1, 0), (0, 0)))
        count = jnp.minimum(pos, float(w))[None, :, None]
        means.append((cg[:, 1:] - lagged) / count)
    return (jnp.concatenate(means, axis=-1) - zf).astype(z.dtype)


def sgu_mask():
    idx = jnp.arange(SGU_BLOCK)
    return (idx[None, :] // CHUNK) <= (idx[:, None] // CHUNK)


def pool_conv_layer(x, w_in, w_out, pool_w, pool_scale, sconv_w, sconv_b):
    bsz, s, _ = x.shape
    xa, ga, h, bg, cg, gb = split_cols(x @ w_in, [D_A, D_A, D_B, D_B, D_B, D_B])
    pooled = multi_scale_pool(xa).reshape(bsz, s, N_POOL_GROUPS, POOL_GROUP)
    ya = jnp.einsum('bsgc,gcd->bsgd', pooled, pool_w).reshape(bsz, s, D_A) * pool_scale
    ya = ya * jax.nn.silu(ga)
    yb = bg * causal_depthwise_conv(cg * h, sconv_w, sconv_b)
    yb = yb * jax.nn.silu(gb)
    return jnp.concatenate([ya, yb], axis=-1) @ w_out


def sgu_conformer_layer(x, w_in, w_out, sgu_ln_g, sgu_ln_b, sgu_w, sgu_b,
                        dconv_w, dconv_b, dnorm_g, dnorm_b):
    bsz, s, _ = x.shape
    u, v, gc, a, bglu, gd = split_cols(x @ w_in, [D_C, D_C, D_C, D_D, D_D, D_D])
    v = layer_norm(v, sgu_ln_g, sgu_ln_b)
    v = v.reshape(bsz, s // SGU_BLOCK, SGU_BLOCK, SGU_HEADS, SGU_HEAD_DIM)
    ws = jnp.where(sgu_mask()[None], sgu_w, 0.0).astype(v.dtype)
    sv = jnp.einsum('hij,bnjhc->bnihc', ws, v) + jnp.transpose(sgu_b)[:, :, None].astype(v.dtype)
    yc = u * sv.reshape(bsz, s, D_C) * jax.nn.silu(gc)
    z = a * jax.nn.sigmoid(bglu)
    z = causal_depthwise_conv(z, dconv_w, dconv_b)
    z = jax.nn.silu(layer_norm(z, dnorm_g, dnorm_b))
    yd = z * jax.nn.silu(gd)
    return jnp.concatenate([yc, yd], axis=-1) @ w_out


def _fwd_setup_inputs(seed: int = 0) -> dict:
    key = jax.random.key(seed)
    ks = jax.random.split(key, 20)
    f32 = jnp.float32
    nrm = lambda k, shp, sc: (jax.random.normal(k, shp, f32) * sc).astype(f32)
    return {
        "x": nrm(ks[0], (BATCH, SEQ, D_MODEL), 1.0),
        "ln_g": 1.0 + nrm(ks[1], (DEPTH, D_MODEL), 0.02),
        "ln_b": nrm(ks[2], (DEPTH, D_MODEL), 0.02),
        "w_in_even": nrm(ks[3], (N_EVEN, D_MODEL, IN_EVEN), D_MODEL ** -0.5),
        "w_out_even": nrm(ks[4], (N_EVEN, D_MIX, D_MODEL), DEEPNORM_BETA * D_MIX ** -0.5),
        "pool_w": nrm(ks[5], (N_EVEN, N_POOL_GROUPS, POOL_GROUP, POOL_GROUP), POOL_GROUP ** -0.5),
        "pool_scale": 1.0 + nrm(ks[6], (N_EVEN, D_A), 0.1),
        "sconv_w": nrm(ks[7], (N_EVEN, SHORT_CONV, D_B), SHORT_CONV ** -0.5),
        "sconv_b": nrm(ks[8], (N_EVEN, D_B), 0.02),
        "w_in_odd": nrm(ks[9], (N_ODD, D_MODEL, IN_ODD), D_MODEL ** -0.5),
        "w_out_odd": nrm(ks[10], (N_ODD, D_MIX, D_MODEL), DEEPNORM_BETA * D_MIX ** -0.5),
        "sgu_ln_g": 1.0 + nrm(ks[11], (N_ODD, D_C), 0.02),
        "sgu_ln_b": nrm(ks[12], (N_ODD, D_C), 0.02),
        "sgu_w": nrm(ks[13], (N_ODD, SGU_HEADS, SGU_BLOCK, SGU_BLOCK), SGU_BLOCK ** -0.5),
        "sgu_b": 1.0 + nrm(ks[14], (N_ODD, SGU_HEADS, SGU_BLOCK), 0.01),
        "dconv_w": nrm(ks[15], (N_ODD, CONF_CONV, D_D), CONF_CONV ** -0.5),
        "dconv_b": nrm(ks[16], (N_ODD, D_D), 0.02),
        "dnorm_g": 1.0 + nrm(ks[17], (N_ODD, D_D), 0.02),
        "dnorm_b": nrm(ks[18], (N_ODD, D_D), 0.02),
    }


def _fwd_reference(x, ln_g, ln_b, w_in_even, w_out_even, pool_w, pool_scale, sconv_w, sconv_b,
              w_in_odd, w_out_odd, sgu_ln_g, sgu_ln_b, sgu_w, sgu_b,
              dconv_w, dconv_b, dnorm_g, dnorm_b):
    for layer in range(DEPTH):
        i = layer // 2
        if layer % 2 == 0:
            y = pool_conv_layer(x, w_in_even[i], w_out_even[i], pool_w[i], pool_scale[i],
                                sconv_w[i], sconv_b[i])
        else:
            y = sgu_conformer_layer(x, w_in_odd[i], w_out_odd[i], sgu_ln_g[i], sgu_ln_b[i],
                                    sgu_w[i], sgu_b[i], dconv_w[i], dconv_b[i],
                                    dnorm_g[i], dnorm_b[i])
        x = layer_norm(DEEPNORM_ALPHA * x + y, ln_g[layer], ln_b[layer])
    return x


import jax as _jax
import jax.numpy as _jnp

TWIN_FORMAT = 'train_step'
FWD_PARAMS = ['x', 'ln_g', 'ln_b', 'w_in_even', 'w_out_even', 'pool_w', 'pool_scale', 'sconv_w', 'sconv_b', 'w_in_odd', 'w_out_odd', 'sgu_ln_g', 'sgu_ln_b', 'sgu_w', 'sgu_b', 'dconv_w', 'dconv_b', 'dnorm_g', 'dnorm_b']
TWIN_WEIGHTS = ['ln_g', 'ln_b', 'w_in_even', 'w_out_even', 'pool_w', 'pool_scale', 'sconv_w', 'sconv_b', 'w_in_odd', 'w_out_odd', 'sgu_ln_g', 'sgu_ln_b', 'sgu_w', 'sgu_b', 'dconv_w', 'dconv_b', 'dnorm_g', 'dnorm_b']
TWIN_DIFF_INPUT = 'x'
TWIN_INPUTS = ['x', 'ln_g', 'ln_b', 'w_in_even', 'w_out_even', 'pool_w', 'pool_scale', 'sconv_w', 'sconv_b', 'w_in_odd', 'w_out_odd', 'sgu_ln_g', 'sgu_ln_b', 'sgu_w', 'sgu_b', 'dconv_w', 'dconv_b', 'dnorm_g', 'dnorm_b', 'loss_target', 'm_ln_g', 'm_ln_b', 'm_w_in_even', 'm_w_out_even', 'm_pool_w', 'm_pool_scale', 'm_sconv_w', 'm_sconv_b', 'm_w_in_odd', 'm_w_out_odd', 'm_sgu_ln_g', 'm_sgu_ln_b', 'm_sgu_w', 'm_sgu_b', 'm_dconv_w', 'm_dconv_b', 'm_dnorm_g', 'm_dnorm_b', 'v_ln_g', 'v_ln_b', 'v_w_in_even', 'v_w_out_even', 'v_pool_w', 'v_pool_scale', 'v_sconv_w', 'v_sconv_b', 'v_w_in_odd', 'v_w_out_odd', 'v_sgu_ln_g', 'v_sgu_ln_b', 'v_sgu_w', 'v_sgu_b', 'v_dconv_w', 'v_dconv_b', 'v_dnorm_g', 'v_dnorm_b']
TWIN_OUTPUTS = ['loss', 'grad_x', 'grad_ln_g', 'grad_ln_b', 'grad_w_in_even', 'grad_w_out_even', 'grad_pool_w', 'grad_pool_scale', 'grad_sconv_w', 'grad_sconv_b', 'grad_w_in_odd', 'grad_w_out_odd', 'grad_sgu_ln_g', 'grad_sgu_ln_b', 'grad_sgu_w', 'grad_sgu_b', 'grad_dconv_w', 'grad_dconv_b', 'grad_dnorm_g', 'grad_dnorm_b', 'delta_ln_g', 'delta_ln_b', 'delta_w_in_even', 'delta_w_out_even', 'delta_pool_w', 'delta_pool_scale', 'delta_sconv_w', 'delta_sconv_b', 'delta_w_in_odd', 'delta_w_out_odd', 'delta_sgu_ln_g', 'delta_sgu_ln_b', 'delta_sgu_w', 'delta_sgu_b', 'delta_dconv_w', 'delta_dconv_b', 'delta_dnorm_g', 'delta_dnorm_b', 'new_m_ln_g', 'new_m_ln_b', 'new_m_w_in_even', 'new_m_w_out_even', 'new_m_pool_w', 'new_m_pool_scale', 'new_m_sconv_w', 'new_m_sconv_b', 'new_m_w_in_odd', 'new_m_w_out_odd', 'new_m_sgu_ln_g', 'new_m_sgu_ln_b', 'new_m_sgu_w', 'new_m_sgu_b', 'new_m_dconv_w', 'new_m_dconv_b', 'new_m_dnorm_g', 'new_m_dnorm_b', 'new_v_ln_g', 'new_v_ln_b', 'new_v_w_in_even', 'new_v_w_out_even', 'new_v_pool_w', 'new_v_pool_scale', 'new_v_sconv_w', 'new_v_sconv_b', 'new_v_w_in_odd', 'new_v_w_out_odd', 'new_v_sgu_ln_g', 'new_v_sgu_ln_b', 'new_v_sgu_w', 'new_v_sgu_b', 'new_v_dconv_w', 'new_v_dconv_b', 'new_v_dnorm_g', 'new_v_dnorm_b']
TWIN_LEAF_KINDS = {'loss': 'loss', 'grad_x': 'grad_x', 'grad_ln_g': 'grad_w', 'grad_ln_b': 'grad_w', 'grad_w_in_even': 'grad_w', 'grad_w_out_even': 'grad_w', 'grad_pool_w': 'grad_w', 'grad_pool_scale': 'grad_w', 'grad_sconv_w': 'grad_w', 'grad_sconv_b': 'grad_w', 'grad_w_in_odd': 'grad_w', 'grad_w_out_odd': 'grad_w', 'grad_sgu_ln_g': 'grad_w', 'grad_sgu_ln_b': 'grad_w', 'grad_sgu_w': 'grad_w', 'grad_sgu_b': 'grad_w', 'grad_dconv_w': 'grad_w', 'grad_dconv_b': 'grad_w', 'grad_dnorm_g': 'grad_w', 'grad_dnorm_b': 'grad_w', 'delta_ln_g': 'delta_w', 'delta_ln_b': 'delta_w', 'delta_w_in_even': 'delta_w', 'delta_w_out_even': 'delta_w', 'delta_pool_w': 'delta_w', 'delta_pool_scale': 'delta_w', 'delta_sconv_w': 'delta_w', 'delta_sconv_b': 'delta_w', 'delta_w_in_odd': 'delta_w', 'delta_w_out_odd': 'delta_w', 'delta_sgu_ln_g': 'delta_w', 'delta_sgu_ln_b': 'delta_w', 'delta_sgu_w': 'delta_w', 'delta_sgu_b': 'delta_w', 'delta_dconv_w': 'delta_w', 'delta_dconv_b': 'delta_w', 'delta_dnorm_g': 'delta_w', 'delta_dnorm_b': 'delta_w', 'new_m_ln_g': 'new_m', 'new_m_ln_b': 'new_m', 'new_m_w_in_even': 'new_m', 'new_m_w_out_even': 'new_m', 'new_m_pool_w': 'new_m', 'new_m_pool_scale': 'new_m', 'new_m_sconv_w': 'new_m', 'new_m_sconv_b': 'new_m', 'new_m_w_in_odd': 'new_m', 'new_m_w_out_odd': 'new_m', 'new_m_sgu_ln_g': 'new_m', 'new_m_sgu_ln_b': 'new_m', 'new_m_sgu_w': 'new_m', 'new_m_sgu_b': 'new_m', 'new_m_dconv_w': 'new_m', 'new_m_dconv_b': 'new_m', 'new_m_dnorm_g': 'new_m', 'new_m_dnorm_b': 'new_m', 'new_v_ln_g': 'new_v', 'new_v_ln_b': 'new_v', 'new_v_w_in_even': 'new_v', 'new_v_w_out_even': 'new_v', 'new_v_pool_w': 'new_v', 'new_v_pool_scale': 'new_v', 'new_v_sconv_w': 'new_v', 'new_v_sconv_b': 'new_v', 'new_v_w_in_odd': 'new_v', 'new_v_w_out_odd': 'new_v', 'new_v_sgu_ln_g': 'new_v', 'new_v_sgu_ln_b': 'new_v', 'new_v_sgu_w': 'new_v', 'new_v_sgu_b': 'new_v', 'new_v_dconv_w': 'new_v', 'new_v_dconv_b': 'new_v', 'new_v_dnorm_g': 'new_v', 'new_v_dnorm_b': 'new_v'}


def _forward(args):
    return _fwd_reference(*[args[k] for k in FWD_PARAMS])


def _output_shape():
    def fwd():
        inp = _fwd_setup_inputs(0)
        return _fwd_reference(*[inp[k] for k in FWD_PARAMS])
    out = _jax.eval_shape(fwd)
    return out.shape, out.dtype

N_MICROBATCH = 1
ADAM_LR = 0.001
ADAM_B1 = 0.9
ADAM_B2 = 0.999
ADAM_EPS = 1e-08
ADAM_WD = 0.01
ADAM_STEP = 10
PER_EXAMPLE_BATCH_AXIS = {'x': 0, 'loss_target': 0}
SHARED_INPUTS = []
_WEIGHT_DTYPES = {'ln_g': _jnp.float32, 'ln_b': _jnp.float32, 'w_in_even': _jnp.float32, 'w_out_even': _jnp.float32, 'pool_w': _jnp.float32, 'pool_scale': _jnp.float32, 'sconv_w': _jnp.float32, 'sconv_b': _jnp.float32, 'w_in_odd': _jnp.float32, 'w_out_odd': _jnp.float32, 'sgu_ln_g': _jnp.float32, 'sgu_ln_b': _jnp.float32, 'sgu_w': _jnp.float32, 'sgu_b': _jnp.float32, 'dconv_w': _jnp.float32, 'dconv_b': _jnp.float32, 'dnorm_g': _jnp.float32, 'dnorm_b': _jnp.float32}
MOMENT_SCALE = {'ln_g': 3.210399e+01, 'ln_b': 1.476591e+00, 'w_in_even': 2.621659e-02, 'w_out_even': 8.569008e-02, 'pool_w': 2.382631e-02, 'pool_scale': 2.335981e-02, 'sconv_w': 2.776037e-02, 'sconv_b': 2.758229e-02, 'w_in_odd': 2.481728e-02, 'w_out_odd': 9.287155e-02, 'sgu_ln_g': 2.317044e-02, 'sgu_ln_b': 2.348823e-02, 'sgu_w': 3.258161e-02, 'sgu_b': 3.972792e-02, 'dconv_w': 1.651327e-02, 'dconv_b': 3.595550e-02, 'dnorm_g': 2.147890e-02, 'dnorm_b': 2.101713e-02}


def _to_microbatches(a, axis):
    t = _jnp.moveaxis(a, axis, 0)
    t = t.reshape((N_MICROBATCH, t.shape[0] // N_MICROBATCH) + t.shape[1:])
    return _jnp.moveaxis(t, 1, axis + 1)


def setup_inputs(seed: int = 0) -> dict:
    inp = _fwd_setup_inputs(seed)
    key = _jax.random.fold_in(_jax.random.key(seed), 7919)
    shape, _ = _output_shape()
    out = dict(inp)
    out["loss_target"] = _jax.random.normal(_jax.random.fold_in(key, 0), shape, _jnp.float32)
    for i, name in enumerate(TWIN_WEIGHTS):
        w = inp[name].astype(_jnp.float32)
        if MOMENT_SCALE is None:
            s = _jnp.sqrt(_jnp.mean(_jnp.square(w)) + 1e-30)
        else:
            s = MOMENT_SCALE[name]
        km, kv = _jax.random.split(_jax.random.fold_in(key, i + 1))
        out[name] = w
        out["m_" + name] = s * _jax.random.normal(km, w.shape, _jnp.float32)
        out["v_" + name] = (s * s) * _jax.random.uniform(kv, w.shape, _jnp.float32, 0.5, 1.5)
    if N_MICROBATCH > 1:
        for name, axis in PER_EXAMPLE_BATCH_AXIS.items():
            out[name] = _to_microbatches(out[name], axis)
    return {'x': out['x'], 'ln_g': out['ln_g'], 'ln_b': out['ln_b'], 'w_in_even': out['w_in_even'], 'w_out_even': out['w_out_even'], 'pool_w': out['pool_w'], 'pool_scale': out['pool_scale'], 'sconv_w': out['sconv_w'], 'sconv_b': out['sconv_b'], 'w_in_odd': out['w_in_odd'], 'w_out_odd': out['w_out_odd'], 'sgu_ln_g': out['sgu_ln_g'], 'sgu_ln_b': out['sgu_ln_b'], 'sgu_w': out['sgu_w'], 'sgu_b': out['sgu_b'], 'dconv_w': out['dconv_w'], 'dconv_b': out['dconv_b'], 'dnorm_g': out['dnorm_g'], 'dnorm_b': out['dnorm_b'], 'loss_target': out['loss_target'], 'm_ln_g': out['m_ln_g'], 'm_ln_b': out['m_ln_b'], 'm_w_in_even': out['m_w_in_even'], 'm_w_out_even': out['m_w_out_even'], 'm_pool_w': out['m_pool_w'], 'm_pool_scale': out['m_pool_scale'], 'm_sconv_w': out['m_sconv_w'], 'm_sconv_b': out['m_sconv_b'], 'm_w_in_odd': out['m_w_in_odd'], 'm_w_out_odd': out['m_w_out_odd'], 'm_sgu_ln_g': out['m_sgu_ln_g'], 'm_sgu_ln_b': out['m_sgu_ln_b'], 'm_sgu_w': out['m_sgu_w'], 'm_sgu_b': out['m_sgu_b'], 'm_dconv_w': out['m_dconv_w'], 'm_dconv_b': out['m_dconv_b'], 'm_dnorm_g': out['m_dnorm_g'], 'm_dnorm_b': out['m_dnorm_b'], 'v_ln_g': out['v_ln_g'], 'v_ln_b': out['v_ln_b'], 'v_w_in_even': out['v_w_in_even'], 'v_w_out_even': out['v_w_out_even'], 'v_pool_w': out['v_pool_w'], 'v_pool_scale': out['v_pool_scale'], 'v_sconv_w': out['v_sconv_w'], 'v_sconv_b': out['v_sconv_b'], 'v_w_in_odd': out['v_w_in_odd'], 'v_w_out_odd': out['v_w_out_odd'], 'v_sgu_ln_g': out['v_sgu_ln_g'], 'v_sgu_ln_b': out['v_sgu_ln_b'], 'v_sgu_w': out['v_sgu_w'], 'v_sgu_b': out['v_sgu_b'], 'v_dconv_w': out['v_dconv_w'], 'v_dconv_b': out['v_dconv_b'], 'v_dnorm_g': out['v_dnorm_g'], 'v_dnorm_b': out['v_dnorm_b']}


def _loss(weights, diff, rest, loss_target):
    with _jax.named_scope("forward"):
        args = {**rest, TWIN_DIFF_INPUT: diff, **{k: w.astype(_WEIGHT_DTYPES[k]) for k, w in weights.items()}}
        y = _forward(args)
    with _jax.named_scope("loss_head"):
        err = _jnp.square(y.astype(_jnp.float32) - loss_target)
        return 0.5 * _jnp.sum(_jnp.mean(err, axis=-1)) if err.ndim else 0.5 * err


def _adamw(w, g, m, v):
    m = ADAM_B1 * m + (1.0 - ADAM_B1) * g
    v = ADAM_B2 * v + (1.0 - ADAM_B2) * _jnp.square(g)
    m_hat = m / (1.0 - ADAM_B1 ** ADAM_STEP)
    v_hat = v / (1.0 - ADAM_B2 ** ADAM_STEP)
    delta = -ADAM_LR * (m_hat / (_jnp.sqrt(v_hat) + ADAM_EPS) + ADAM_WD * w)
    return delta, m, v


def reference(x, ln_g, ln_b, w_in_even, w_out_even, pool_w, pool_scale, sconv_w, sconv_b, w_in_odd, w_out_odd, sgu_ln_g, sgu_ln_b, sgu_w, sgu_b, dconv_w, dconv_b, dnorm_g, dnorm_b, loss_target, m_ln_g, m_ln_b, m_w_in_even, m_w_out_even, m_pool_w, m_pool_scale, m_sconv_w, m_sconv_b, m_w_in_odd, m_w_out_odd, m_sgu_ln_g, m_sgu_ln_b, m_sgu_w, m_sgu_b, m_dconv_w, m_dconv_b, m_dnorm_g, m_dnorm_b, v_ln_g, v_ln_b, v_w_in_even, v_w_out_even, v_pool_w, v_pool_scale, v_sconv_w, v_sconv_b, v_w_in_odd, v_w_out_odd, v_sgu_ln_g, v_sgu_ln_b, v_sgu_w, v_sgu_b, v_dconv_w, v_dconv_b, v_dnorm_g, v_dnorm_b):
    given = dict(x=x, ln_g=ln_g, ln_b=ln_b, w_in_even=w_in_even, w_out_even=w_out_even, pool_w=pool_w, pool_scale=pool_scale, sconv_w=sconv_w, sconv_b=sconv_b, w_in_odd=w_in_odd, w_out_odd=w_out_odd, sgu_ln_g=sgu_ln_g, sgu_ln_b=sgu_ln_b, sgu_w=sgu_w, sgu_b=sgu_b, dconv_w=dconv_w, dconv_b=dconv_b, dnorm_g=dnorm_g, dnorm_b=dnorm_b, loss_target=loss_target, m_ln_g=m_ln_g, m_ln_b=m_ln_b, m_w_in_even=m_w_in_even, m_w_out_even=m_w_out_even, m_pool_w=m_pool_w, m_pool_scale=m_pool_scale, m_sconv_w=m_sconv_w, m_sconv_b=m_sconv_b, m_w_in_odd=m_w_in_odd, m_w_out_odd=m_w_out_odd, m_sgu_ln_g=m_sgu_ln_g, m_sgu_ln_b=m_sgu_ln_b, m_sgu_w=m_sgu_w, m_sgu_b=m_sgu_b, m_dconv_w=m_dconv_w, m_dconv_b=m_dconv_b, m_dnorm_g=m_dnorm_g, m_dnorm_b=m_dnorm_b, v_ln_g=v_ln_g, v_ln_b=v_ln_b, v_w_in_even=v_w_in_even, v_w_out_even=v_w_out_even, v_pool_w=v_pool_w, v_pool_scale=v_pool_scale, v_sconv_w=v_sconv_w, v_sconv_b=v_sconv_b, v_w_in_odd=v_w_in_odd, v_w_out_odd=v_w_out_odd, v_sgu_ln_g=v_sgu_ln_g, v_sgu_ln_b=v_sgu_ln_b, v_sgu_w=v_sgu_w, v_sgu_b=v_sgu_b, v_dconv_w=v_dconv_w, v_dconv_b=v_dconv_b, v_dnorm_g=v_dnorm_g, v_dnorm_b=v_dnorm_b)
    weights = {n: given[n] for n in TWIN_WEIGHTS}
    shared = {n: given[n] for n in SHARED_INPUTS}
    per_example = {n: given[n] for n in ['x']}
    grad_fn = _jax.value_and_grad(_loss, argnums=(0, 1))

    def one_microbatch(ex, loss_target):
        ex = dict(ex)
        diff = ex.pop(TWIN_DIFF_INPUT)
        return grad_fn(weights, diff, {**shared, **ex}, loss_target)

    if N_MICROBATCH == 1:
        loss, (grad_w, grad_x) = one_microbatch(per_example, given["loss_target"])
    else:
        def body(carry, xs):
            loss_sum, grad_sum = carry
            l_k, (gw_k, gx_k) = one_microbatch(xs[0], xs[1])
            with _jax.named_scope("update"):
                return (loss_sum + l_k, _jax.tree.map(_jnp.add, grad_sum, gw_k)), gx_k

        init = (_jnp.zeros((), _jnp.float32), _jax.tree.map(_jnp.zeros_like, weights))
        (loss, grad_w), grad_x = _jax.lax.scan(body, init, (per_example, given["loss_target"]))
    with _jax.named_scope("update"):
        delta_w, new_m, new_v = {}, {}, {}
        for n in TWIN_WEIGHTS:
            delta_w[n], new_m[n], new_v[n] = _adamw(weights[n], grad_w[n], given["m_" + n], given["v_" + n])
    return (loss, grad_x, *[grad_w[n] for n in TWIN_WEIGHTS], *[delta_w[n] for n in TWIN_WEIGHTS],
            *[new_m[n] for n in TWIN_WEIGHTS], *[new_v[n] for n in TWIN_WEIGHTS])
```

```python
import functools

import jax
import jax.numpy as jnp
from jax import lax
from jax.experimental import pallas as pl
from jax.experimental.pallas import tpu as pltpu

F32 = jnp.float32
BF16 = jnp.bfloat16

D = 1024
DMIX = 2048
NIN = 6144
NDEV = 8
CW = NIN // NDEV
RW = DMIX // NDEV
DEPTH = 4
ALPHA = (2 * DEPTH) ** 0.25
LN_EPS = 1e-5
POOL_WINDOWS = (2, 4, 8, 16)
PG = 256
SGU_BLOCK = 128
HEADS = 4
HD = 256
KD = 31
KS = 3
HP = 16
HS = 8
HC = 32

ADAM_LR = 0.001
ADAM_B1 = 0.9
ADAM_B2 = 0.999
ADAM_EPS = 1e-08
ADAM_WD = 0.01
ADAM_STEP = 10

VMEM_LIMIT = 56 * 1024 * 1024
AXES = ("x", "y", "c")
MESH = pl.DeviceIdType.MESH
ANY = pl.BlockSpec(memory_space=pl.ANY)


def _cp(sem=None):
    if sem is None:
        return pltpu.CompilerParams(vmem_limit_bytes=VMEM_LIMIT)
    return pltpu.CompilerParams(dimension_semantics=sem, vmem_limit_bytes=VMEM_LIMIT)


def _sig(x):
    return 1.0 / (1.0 + jnp.exp(-x))


def _dot(a, b):
    return jnp.dot(a, b, preferred_element_type=F32)


def _dot_nt(a, b):
    return lax.dot_general(a, b, (((1,), (1,)), ((), ())), preferred_element_type=F32)


def _dot_tn(a, b):
    return lax.dot_general(a, b, (((0,), (0,)), ((), ())), preferred_element_type=F32)


def _rowsum(x):
    return jnp.sum(x, axis=0, keepdims=True)


def all_gather(block, name):
    def body(x_ref, out_ref, send_sems, recv_sems, local_sem):
        x, y, c = lax.axis_index("x"), lax.axis_index("y"), lax.axis_index("c")
        me, sibling = (x, y, c), (x, y, 1 - c)
        chips = [(1 - x, y), (x, 1 - y), (1 - x, 1 - y)]

        def slot(px, py, pc):
            return out_ref.at[4 * px + 2 * py + pc]

        def copy(k, blk, to, src=None):
            return pltpu.make_async_remote_copy(
                src_ref=slot(*blk) if src is None else src, dst_ref=slot(*blk),
                send_sem=send_sems.at[k], recv_sem=recv_sems.at[k],
                device_id=to, device_id_type=MESH)

        mine = pltpu.make_async_copy(x_ref, slot(*me), local_sem)
        mine.start()
        first = [copy(0, me, sibling, src=x_ref)]
        first += [copy(1 + j, me, (*chip, c), src=x_ref) for j, chip in enumerate(chips)]
        for cp in first:
            cp.start()
        passed = [copy(4 + j, (*chip, c), sibling) for j, chip in enumerate(chips)]
        for j, chip in enumerate(chips):
            copy(1 + j, (*chip, c), me).wait_recv()
            passed[j].start()
        copy(0, sibling, me).wait_recv()
        for j, chip in enumerate(chips):
            copy(4 + j, (*chip, 1 - c), me).wait_recv()
        for cp in first + passed:
            cp.wait_send()
        mine.wait()

    return pl.pallas_call(
        body, name=name,
        out_shape=jax.ShapeDtypeStruct((NDEV,) + block.shape, block.dtype),
        in_specs=[ANY], out_specs=ANY,
        scratch_shapes=[pltpu.SemaphoreType.DMA((7,)), pltpu.SemaphoreType.DMA((7,)),
                        pltpu.SemaphoreType.DMA],
    )(block)


def exchange_blocks(parts, name):
    def body(p_ref, out_ref, send_sems, recv_sems, local_sem):
        x, y, c = lax.axis_index("x"), lax.axis_index("y"), lax.axis_index("c")
        me = 4 * x + 2 * y + c
        mine = pltpu.make_async_copy(p_ref.at[me], out_ref.at[me], local_sem)
        mine.start()
        copies = []
        for r in range(1, NDEV):
            px, py, pc = x ^ (r >> 2), y ^ ((r >> 1) & 1), c ^ (r & 1)
            peer = 4 * px + 2 * py + pc
            copies.append(pltpu.make_async_remote_copy(
                src_ref=p_ref.at[peer], dst_ref=out_ref.at[me],
                send_sem=send_sems.at[r - 1], recv_sem=recv_sems.at[r - 1],
                device_id=(px, py, pc), device_id_type=MESH))
        for cp in copies:
            cp.start()
        for r in range(1, NDEV):
            px, py, pc = x ^ (r >> 2), y ^ ((r >> 1) & 1), c ^ (r & 1)
            peer = 4 * px + 2 * py + pc
            pltpu.make_async_remote_copy(
                src_ref=p_ref.at[peer], dst_ref=out_ref.at[peer],
                send_sem=send_sems.at[r - 1], recv_sem=recv_sems.at[r - 1],
                device_id=(px, py, pc), device_id_type=MESH).wait_recv()
        for cp in copies:
            cp.wait_send()
        mine.wait()

    return pl.pallas_call(
        body, name=name,
        out_shape=jax.ShapeDtypeStruct(parts.shape, parts.dtype),
        in_specs=[ANY], out_specs=ANY,
        scratch_shapes=[pltpu.SemaphoreType.DMA((7,)), pltpu.SemaphoreType.DMA((7,)),
                        pltpu.SemaphoreType.DMA],
    )(parts)


def cast_x(x):
    s = x.shape[0]
    tm = min(s, 512)

    def body(x_ref, o_ref):
        o_ref[...] = x_ref[...].astype(BF16)

    return pl.pallas_call(
        body, name="cast_x", grid=(s // tm,),
        in_specs=[pl.BlockSpec((tm, D), lambda i: (i, 0))],
        out_specs=pl.BlockSpec((tm, D), lambda i: (i, 0)),
        out_shape=jax.ShapeDtypeStruct((s, D), BF16), compiler_params=_cp(("parallel",)),
    )(x)


def cast_weights(w_even, w_odd, name):
    _, r, c = w_even.shape

    def body(e_ref, o_ref, out_ref):
        layer = pl.program_id(0)

        @pl.when(layer % 2 == 0)
        def _():
            out_ref[...] = e_ref[...].astype(BF16)

        @pl.when(layer % 2 == 1)
        def _():
            out_ref[...] = o_ref[...].astype(BF16)

    spec = pl.BlockSpec((None, r, c), lambda l: (l // 2, 0, 0))
    return pl.pallas_call(
        body, name=name, grid=(DEPTH,), in_specs=[spec, spec],
        out_specs=pl.BlockSpec((None, r, c), lambda l: (l, 0, 0)),
        out_shape=jax.ShapeDtypeStruct((DEPTH, r, c), BF16), compiler_params=_cp(("parallel",)),
    )(w_even, w_odd)


def proj_in(xb, wg, layer):
    s = xb.shape[0]
    tm = min(s, 512)

    def body(x_ref, w_ref, o_ref):
        o_ref[...] = _dot(x_ref[...], w_ref[...])

    return pl.pallas_call(
        body, name=f"proj_in_{layer}", grid=(NDEV, s // tm),
        in_specs=[pl.BlockSpec((tm, D), lambda k, i: (i, 0)),
                  pl.BlockSpec((None, None, D, CW), lambda k, i: (k, layer, 0, 0))],
        out_specs=pl.BlockSpec((tm, CW), lambda k, i: (i, k)),
        out_shape=jax.ShapeDtypeStruct((s, NIN), F32),
        compiler_params=_cp(("parallel", "parallel")),
    )(xb, wg)


def proj_out_ln(ycat, wog, layer, xres, g_prev, b_prev, g, b, target=None):
    s = ycat.shape[0]
    tm = min(s, 256)
    nt = s // tm
    final = target is not None

    def body(*refs):
        if final:
            y_ref, w_ref, xr_ref, gp_ref, bp_ref, g_ref, b_ref, t_ref, xh_ref, rs_ref, xb_ref, dy_ref, loss_ref = refs
        else:
            y_ref, w_ref, xr_ref, gp_ref, bp_ref, g_ref, b_ref, xh_ref, rs_ref, xb_ref = refs
        xin = xr_ref[...] * gp_ref[...] + bp_ref[...]
        y = _dot(y_ref[...], w_ref[...].reshape(DMIX, D))
        r = ALPHA * xin + y
        mu = jnp.mean(r, axis=-1, keepdims=True)
        d = r - mu
        var = jnp.mean(d * d, axis=-1, keepdims=True)
        rstd = lax.rsqrt(var + LN_EPS)
        xh = d * rstd
        xh_ref[...] = xh
        rs_ref[...] = rstd
        xn = xh * g_ref[...] + b_ref[...]
        xb_ref[...] = xn.astype(BF16)
        if final:
            err = xn - t_ref[...]
            dy_ref[...] = err * (1.0 / D)

            @pl.when(pl.program_id(0) == 0)
            def _():
                loss_ref[...] = jnp.zeros_like(loss_ref)

            loss_ref[...] += 0.5 * jnp.sum(jnp.mean(err * err, axis=-1, keepdims=True), axis=0, keepdims=True)

    row = pl.BlockSpec((tm, D), lambda i: (i, 0))
    vec = pl.BlockSpec((1, D), lambda i: (0, 0))
    in_specs = [pl.BlockSpec((tm, DMIX), lambda i: (i, 0)),
                pl.BlockSpec((NDEV, None, RW, D), lambda i: (0, layer, 0, 0)),
                row, vec, vec, vec, vec]
    out_specs = [row, pl.BlockSpec((tm, 1), lambda i: (i, 0)), row]
    out_shape = [jax.ShapeDtypeStruct((s, D), F32), jax.ShapeDtypeStruct((s, 1), F32),
                 jax.ShapeDtypeStruct((s, D), BF16)]
    args = [ycat, wog, xres, g_prev, b_prev, g, b]
    if final:
        in_specs.append(row)
        args.append(target)
        out_specs += [row, pl.BlockSpec((1, 1), lambda i: (0, 0))]
        out_shape += [jax.ShapeDtypeStruct((s, D), F32), jax.ShapeDtypeStruct((1, 1), F32)]
    return pl.pallas_call(
        body, name=f"proj_out_ln_{layer}", grid=(nt,), in_specs=in_specs, out_specs=out_specs,
        out_shape=out_shape, compiler_params=_cp(("arbitrary",)),
    )(*args)


def ln_bwd_dycat(dxn, xhat, rstd, g, wog, layer):
    s = dxn.shape[0]
    tm = min(s, 256)

    def body(dx_ref, xh_ref, rs_ref, g_ref, w_ref, dr_ref, drb_ref, dyc_ref, dg_ref, db_ref):
        @pl.when(pl.program_id(0) == 0)
        def _():
            dg_ref[...] = jnp.zeros_like(dg_ref)
            db_ref[...] = jnp.zeros_like(db_ref)

        dxo = dx_ref[...]
        xh = xh_ref[...]
        dg_ref[...] += _rowsum(dxo * xh)
        db_ref[...] += _rowsum(dxo)
        dxh = dxo * g_ref[...]
        m1 = jnp.mean(dxh, axis=-1, keepdims=True)
        m2 = jnp.mean(dxh * xh, axis=-1, keepdims=True)
        dr = rs_ref[...] * (dxh - m1 - xh * m2)
        dr_ref[...] = dr
        drb = dr.astype(BF16)
        drb_ref[...] = drb
        dyc_ref[...] = _dot_nt(drb, w_ref[...].reshape(DMIX, D))

    row = pl.BlockSpec((tm, D), lambda i: (i, 0))
    vec = pl.BlockSpec((1, D), lambda i: (0, 0))
    return pl.pallas_call(
        body, name=f"ln_bwd_dycat_{layer}", grid=(s // tm,),
        in_specs=[row, row, pl.BlockSpec((tm, 1), lambda i: (i, 0)), vec,
                  pl.BlockSpec((NDEV, None, RW, D), lambda i: (0, layer, 0, 0))],
        out_specs=[row, row, pl.BlockSpec((tm, DMIX), lambda i: (i, 0)), vec, vec],
        out_shape=[jax.ShapeDtypeStruct((s, D), F32), jax.ShapeDtypeStruct((s, D), BF16),
                   jax.ShapeDtypeStruct((s, DMIX), F32), jax.ShapeDtypeStruct((1, D), F32),
                   jax.ShapeDtypeStruct((1, D), F32)],
        compiler_params=_cp(("arbitrary",)),
    )(dxn, xhat, rstd, g, wog)


def dx_in(dzb, wg, layer, dr):
    s = dzb.shape[0]
    tm = min(s, 256)

    def body(dz_ref, w_ref, dr_ref, o_ref):
        acc = ALPHA * dr_ref[...]
        for k in range(NDEV):
            acc += _dot_nt(dz_ref[:, k * CW:(k + 1) * CW], w_ref[k])
        o_ref[...] = acc

    row = pl.BlockSpec((tm, D), lambda i: (i, 0))
    return pl.pallas_call(
        body, name=f"dx_in_{layer}", grid=(s // tm,),
        in_specs=[pl.BlockSpec((tm, NIN), lambda i: (i, 0)),
                  pl.BlockSpec((NDEV, None, D, CW), lambda i: (0, layer, 0, 0)), row],
        out_specs=row, out_shape=jax.ShapeDtypeStruct((s, D), F32),
        compiler_params=_cp(("parallel",)),
    )(dzb, wg, dr)


def dw_in(xb, dzb, layer):
    s = xb.shape[0]
    tm = min(s, 512)
    nt = s // tm

    def body(x_ref, dz_ref, o_ref, acc):
        t = pl.program_id(1)

        @pl.when(t == 0)
        def _():
            acc[...] = jnp.zeros_like(acc)

        acc[...] += _dot_tn(x_ref[...], dz_ref[...])

        @pl.when(t == nt - 1)
        def _():
            o_ref[...] = acc[...].astype(BF16)

    return pl.pallas_call(
        body, name=f"dw_in_{layer}", grid=(NDEV, nt),
        in_specs=[pl.BlockSpec((tm, D), lambda k, t: (t, 0)),
                  pl.BlockSpec((tm, CW), lambda k, t: (t, k))],
        out_specs=pl.BlockSpec((None, D, CW), lambda k, t: (k, 0, 0)),
        out_shape=jax.ShapeDtypeStruct((NDEV, D, CW), BF16),
        scratch_shapes=[pltpu.VMEM((D, CW), F32)],
        compiler_params=_cp(("parallel", "arbitrary")),
    )(xb, dzb)


def dw_out(ycat, drb, layer):
    s = ycat.shape[0]
    tm = min(s, 512)
    nt = s // tm

    def body(y_ref, dr_ref, o_ref, acc):
        t = pl.program_id(0)

        @pl.when(t == 0)
        def _():
            acc[...] = jnp.zeros_like(acc)

        acc[...] += _dot_tn(y_ref[...], dr_ref[...])

        @pl.when(t == nt - 1)
        def _():
            o_ref[...] = acc[...].reshape(NDEV, RW, D).astype(BF16)

    return pl.pallas_call(
        body, name=f"dw_out_{layer}", grid=(nt,),
        in_specs=[pl.BlockSpec((tm, DMIX), lambda t: (t, 0)), pl.BlockSpec((tm, D), lambda t: (t, 0))],
        out_specs=pl.BlockSpec((NDEV, RW, D), lambda t: (0, 0, 0)),
        out_shape=jax.ShapeDtypeStruct((NDEV, RW, D), BF16),
        scratch_shapes=[pltpu.VMEM((DMIX, D), F32)],
        compiler_params=_cp(("arbitrary",)),
    )(ycat, drb)


def _cols(j):
    return slice(j * D, (j + 1) * D)


def mix_even_fwd(z, pwb, pscale, sw, sb, layer):
    s = z.shape[0]
    tm = min(s, 256)

    def body(z_ref, pw_ref, ps_ref, sw_ref, sb_ref, yc_ref, pb_ref, cv_ref, exa, eq):
        i = pl.program_id(0)

        @pl.when(i == 0)
        def _():
            exa[0:HP, :] = jnp.zeros((HP, D), F32)
            eq[0:HS, :] = jnp.zeros((HS, D), F32)

        exa[HP:HP + tm, :] = z_ref[:, _cols(0)]
        q = z_ref[:, _cols(4)] * z_ref[:, _cols(2)]
        eq[HS:HS + tm, :] = q
        pos = (i * tm + lax.broadcasted_iota(jnp.int32, (tm, 1), 0) + 1).astype(F32)
        for gi, w in enumerate(POOL_WINDOWS):
            c0, c1 = gi * PG, (gi + 1) * PG
            acc = exa[HP:HP + tm, c0:c1]
            for j in range(1, w):
                acc = acc + exa[HP - j:HP - j + tm, c0:c1]
            pooled = acc / jnp.minimum(pos, float(w)) - exa[HP:HP + tm, c0:c1]
            pb = pooled.astype(BF16)
            pb_ref[:, c0:c1] = pb
            ga = z_ref[:, D + c0:D + c1]
            ya = _dot(pb, pw_ref[gi]) * ps_ref[:, c0:c1] * (ga * _sig(ga))
            yc_ref[:, c0:c1] = ya.astype(BF16)
        cv = (sw_ref[2:3, :] * q + sw_ref[1:2, :] * eq[HS - 1:HS - 1 + tm, :]
              + sw_ref[0:1, :] * eq[HS - 2:HS - 2 + tm, :] + sb_ref[...])
        cv_ref[...] = cv
        gb = z_ref[:, _cols(5)]
        yc_ref[:, D:2 * D] = (z_ref[:, _cols(3)] * cv * (gb * _sig(gb))).astype(BF16)
        exa[0:HP, :] = exa[tm:tm + HP, :]
        eq[0:HS, :] = eq[tm:tm + HS, :]

    vec = pl.BlockSpec((1, D), lambda i: (0, 0))
    return pl.pallas_call(
        body, name=f"mix_even_fwd_{layer}", grid=(s // tm,),
        in_specs=[pl.BlockSpec((tm, NIN), lambda i: (i, 0)),
                  pl.BlockSpec((4, PG, PG), lambda i: (0, 0, 0)), vec,
                  pl.BlockSpec((KS, D), lambda i: (0, 0)), vec],
        out_specs=[pl.BlockSpec((tm, DMIX), lambda i: (i, 0)), pl.BlockSpec((tm, D), lambda i: (i, 0)),
                   pl.BlockSpec((tm, D), lambda i: (i, 0))],
        out_shape=[jax.ShapeDtypeStruct((s, DMIX), BF16), jax.ShapeDtypeStruct((s, D), BF16),
                   jax.ShapeDtypeStruct((s, D), F32)],
        scratch_shapes=[pltpu.VMEM((HP + tm, D), F32), pltpu.VMEM((HS + tm, D), F32)],
        compiler_params=_cp(("arbitrary",)),
    )(z, pwb, pscale, sw, sb)


def mix_even_bwd(z, dycat, pooled, cv, pwb, pscale, sw, layer):
    s = z.shape[0]
    tm = min(s, 256)
    nt = s // tm

    def body(z_ref, dy_ref, pb_ref, cv_ref, pw_ref, ps_ref, sw_ref,
             dz_ref, dpw_ref, dps_ref, dsw_ref, dsb_ref, edp, edc):
        i = pl.program_id(0)

        @pl.when(i == 0)
        def _():
            edp[tm:tm + HP, :] = jnp.zeros((HP, D), F32)
            edc[tm:tm + HS, :] = jnp.zeros((HS, D), F32)
            dpw_ref[...] = jnp.zeros_like(dpw_ref)
            dps_ref[...] = jnp.zeros_like(dps_ref)
            dsw_ref[...] = jnp.zeros_like(dsw_ref)
            dsb_ref[...] = jnp.zeros_like(dsb_ref)

        pos = ((nt - 1 - i) * tm + lax.broadcasted_iota(jnp.int32, (tm, 1), 0) + 1).astype(F32)
        for gi, w in enumerate(POOL_WINDOWS):
            c0, c1 = gi * PG, (gi + 1) * PG
            ga = z_ref[:, D + c0:D + c1]
            sg = _sig(ga)
            sil = ga * sg
            dya = dy_ref[:, c0:c1]
            pb = pb_ref[:, c0:c1]
            ya0 = _dot(pb, pw_ref[gi])
            ps = ps_ref[:, c0:c1]
            dps_ref[:, c0:c1] += _rowsum(dya * ya0 * sil)
            dz_ref[:, D + c0:D + c1] = (dya * ya0 * ps * (sg * (1.0 + ga * (1.0 - sg)))).astype(BF16)
            t1 = (dya * ps * sil).astype(BF16)
            dpool = _dot_nt(t1, pw_ref[gi])
            dpw_ref[gi] += _dot_tn(pb, t1)
            edp[0:tm, c0:c1] = dpool / jnp.minimum(pos, float(w))
            acc = -dpool
            for j in range(w):
                acc = acc + edp[j:j + tm, c0:c1]
            dz_ref[:, c0:c1] = acc.astype(BF16)
        gb = z_ref[:, _cols(5)]
        sgb = _sig(gb)
        silb = gb * sgb
        dyb = dy_ref[:, D:2 * D]
        cvv = cv_ref[...]
        bg = z_ref[:, _cols(3)]
        dz_ref[:, _cols(3)] = (dyb * cvv * silb).astype(BF16)
        dz_ref[:, _cols(5)] = (dyb * bg * cvv * (sgb * (1.0 + gb * (1.0 - sgb)))).astype(BF16)
        dcv = dyb * bg * silb
        dsb_ref[...] += _rowsum(dcv)
        edc[0:tm, :] = dcv
        h = z_ref[:, _cols(2)]
        cg = z_ref[:, _cols(4)]
        q = cg * h
        d1 = edc[1:1 + tm, :]
        d2 = edc[2:2 + tm, :]
        dq = sw_ref[2:3, :] * dcv + sw_ref[1:2, :] * d1 + sw_ref[0:1, :] * d2
        dsw_ref[2:3, :] += _rowsum(q * dcv)
        dsw_ref[1:2, :] += _rowsum(q * d1)
        dsw_ref[0:1, :] += _rowsum(q * d2)
        dz_ref[:, _cols(4)] = (dq * h).astype(BF16)
        dz_ref[:, _cols(2)] = (dq * cg).astype(BF16)
        edp[tm:tm + HP, :] = edp[0:HP, :]
        edc[tm:tm + HS, :] = edc[0:HS, :]

    rev = lambda i: (nt - 1 - i, 0)
    vec = pl.BlockSpec((1, D), lambda i: (0, 0))
    return pl.pallas_call(
        body, name=f"mix_even_bwd_{layer}", grid=(nt,),
        in_specs=[pl.BlockSpec((tm, NIN), rev), pl.BlockSpec((tm, DMIX), rev),
                  pl.BlockSpec((tm, D), rev), pl.BlockSpec((tm, D), rev),
                  pl.BlockSpec((4, PG, PG), lambda i: (0, 0, 0)), vec,
                  pl.BlockSpec((KS, D), lambda i: (0, 0))],
        out_specs=[pl.BlockSpec((tm, NIN), rev), pl.BlockSpec((4, PG, PG), lambda i: (0, 0, 0)), vec,
                   pl.BlockSpec((KS, D), lambda i: (0, 0)), vec],
        out_shape=[jax.ShapeDtypeStruct((s, NIN), BF16), jax.ShapeDtypeStruct((4, PG, PG), F32),
                   jax.ShapeDtypeStruct((1, D), F32), jax.ShapeDtypeStruct((KS, D), F32),
                   jax.ShapeDtypeStruct((1, D), F32)],
        scratch_shapes=[pltpu.VMEM((tm + HP, D), F32), pltpu.VMEM((tm + HS, D), F32)],
        compiler_params=_cp(("arbitrary",)),
    )(z, dycat, pooled, cv, pwb, pscale, sw)


def _ln_rows(v):
    mu = jnp.mean(v, axis=-1, keepdims=True)
    d = v - mu
    var = jnp.mean(d * d, axis=-1, keepdims=True)
    rstd = lax.rsqrt(var + LN_EPS)
    return d * rstd, rstd


def _ln_rows_bwd(dn, xh, rstd, g):
    dxh = dn * g
    m1 = jnp.mean(dxh, axis=-1, keepdims=True)
    m2 = jnp.mean(dxh * xh, axis=-1, keepdims=True)
    return rstd * (dxh - m1 - xh * m2)


def mix_odd_fwd(z, slg, slb, wsb, sbcol, dw, dcb, dng, dnb, layer):
    s = z.shape[0]
    tm = min(s, 256)

    def body(z_ref, slg_ref, slb_ref, ws_ref, sb_ref, dw_ref, dcb_ref, dng_ref, dnb_ref,
             yc_ref, cz_ref, ezg):
        i = pl.program_id(0)

        @pl.when(i == 0)
        def _():
            ezg[0:HC, :] = jnp.zeros((HC, D), F32)

        vh, _ = _ln_rows(z_ref[:, _cols(1)])
        vnb = (vh * slg_ref[...] + slb_ref[...]).astype(BF16)
        for n in range(tm // SGU_BLOCK):
            r0, r1 = n * SGU_BLOCK, (n + 1) * SGU_BLOCK
            for hd in range(HEADS):
                c0, c1 = hd * HD, (hd + 1) * HD
                sv = _dot(ws_ref[hd], vnb[r0:r1, c0:c1]) + sb_ref[hd]
                gc = z_ref[r0:r1, 2 * D + c0:2 * D + c1]
                yc_ref[r0:r1, c0:c1] = (z_ref[r0:r1, c0:c1] * sv * (gc * _sig(gc))).astype(BF16)
        ezg[HC:HC + tm, :] = z_ref[:, _cols(3)] * _sig(z_ref[:, _cols(4)])
        cz = jnp.zeros((tm, D), F32) + dcb_ref[...]
        for k in range(KD):
            off = HC - (KD - 1) + k
            cz = cz + dw_ref[k:k + 1, :] * ezg[off:off + tm, :]
        cz_ref[...] = cz
        zh, _ = _ln_rows(cz)
        zn = zh * dng_ref[...] + dnb_ref[...]
        gd = z_ref[:, _cols(5)]
        yc_ref[:, D:2 * D] = ((zn * _sig(zn)) * (gd * _sig(gd))).astype(BF16)
        ezg[0:HC, :] = ezg[tm:tm + HC, :]

    vec = pl.BlockSpec((1, D), lambda i: (0, 0))
    return pl.pallas_call(
        body, name=f"mix_odd_fwd_{layer}", grid=(s // tm,),
        in_specs=[pl.BlockSpec((tm, NIN), lambda i: (i, 0)), vec, vec,
                  pl.BlockSpec((HEADS, SGU_BLOCK, SGU_BLOCK), lambda i: (0, 0, 0)),
                  pl.BlockSpec((HEADS, SGU_BLOCK, 1), lambda i: (0, 0, 0)),
                  pl.BlockSpec((KD, D), lambda i: (0, 0)), vec, vec, vec],
        out_specs=[pl.BlockSpec((tm, DMIX), lambda i: (i, 0)), pl.BlockSpec((tm, D), lambda i: (i, 0))],
        out_shape=[jax.ShapeDtypeStruct((s, DMIX), BF16), jax.ShapeDtypeStruct((s, D), F32)],
        scratch_shapes=[pltpu.VMEM((HC + tm, D), F32)],
        compiler_params=_cp(("arbitrary",)),
    )(z, slg, slb, wsb, sbcol, dw, dcb, dng, dnb)


def mix_odd_bwd(z, dycat, cz, slg, slb, wsb, wstb, sbcol, dw, dng, dnb, layer):
    s = z.shape[0]
    tm = min(s, 256)
    nt = s // tm

    def body(z_ref, dy_ref, cz_ref, slg_ref, slb_ref, ws_ref, wst_ref, sb_ref, dw_ref, dng_ref, dnb_ref,
             dz_ref, dslg_ref, dslb_ref, dws_ref, dsb_ref, ddw_ref, ddcb_ref, ddng_ref, ddnb_ref,
             dvn, edz):
        i = pl.program_id(0)

        @pl.when(i == 0)
        def _():
            edz[tm:tm + HC, :] = jnp.zeros((HC, D), F32)
            for ref in (dslg_ref, dslb_ref, dws_ref, dsb_ref, ddw_ref, ddcb_ref, ddng_ref, ddnb_ref):
                ref[...] = jnp.zeros_like(ref)

        vh, vrs = _ln_rows(z_ref[:, _cols(1)])
        vnb = (vh * slg_ref[...] + slb_ref[...]).astype(BF16)
        for n in range(tm // SGU_BLOCK):
            r0, r1 = n * SGU_BLOCK, (n + 1) * SGU_BLOCK
            for hd in range(HEADS):
                c0, c1 = hd * HD, (hd + 1) * HD
                vblk = vnb[r0:r1, c0:c1]
                sv = _dot(ws_ref[hd], vblk) + sb_ref[hd]
                gc = z_ref[r0:r1, 2 * D + c0:2 * D + c1]
                sg = _sig(gc)
                sil = gc * sg
                u = z_ref[r0:r1, c0:c1]
                dyc = dy_ref[r0:r1, c0:c1]
                dz_ref[r0:r1, c0:c1] = (dyc * sv * sil).astype(BF16)
                dz_ref[r0:r1, 2 * D + c0:2 * D + c1] = (dyc * u * sv * (sg * (1.0 + gc * (1.0 - sg)))).astype(BF16)
                dsv = dyc * u * sil
                dsb_ref[hd] += jnp.sum(dsv, axis=-1, keepdims=True)
                dsvb = dsv.astype(BF16)
                dws_ref[hd] += _dot_nt(dsvb, vblk)
                dvn[r0:r1, c0:c1] = _dot(wst_ref[hd], dsvb)
        dv = dvn[...]
        dslg_ref[...] += _rowsum(dv * vh)
        dslb_ref[...] += _rowsum(dv)
        dz_ref[:, _cols(1)] = _ln_rows_bwd(dv, vh, vrs, slg_ref[...]).astype(BF16)

        zh, zrs = _ln_rows(cz_ref[...])
        zn = zh * dng_ref[...] + dnb_ref[...]
        sgn = _sig(zn)
        gd = z_ref[:, _cols(5)]
        sgd = _sig(gd)
        dyd = dy_ref[:, D:2 * D]
        dz_ref[:, _cols(5)] = (dyd * (zn * sgn) * (sgd * (1.0 + gd * (1.0 - sgd)))).astype(BF16)
        dzn = dyd * (gd * sgd) * (sgn * (1.0 + zn * (1.0 - sgn)))
        ddng_ref[...] += _rowsum(dzn * zh)
        ddnb_ref[...] += _rowsum(dzn)
        dcz = _ln_rows_bwd(dzn, zh, zrs, dng_ref[...])
        ddcb_ref[...] += _rowsum(dcz)
        edz[0:tm, :] = dcz
        a = z_ref[:, _cols(3)]
        sgb = _sig(z_ref[:, _cols(4)])
        zg = a * sgb
        dzg = jnp.zeros((tm, D), F32)
        for k in range(KD):
            off = KD - 1 - k
            sh = edz[off:off + tm, :]
            dzg = dzg + dw_ref[k:k + 1, :] * sh
            ddw_ref[k:k + 1, :] += _rowsum(zg * sh)
        dz_ref[:, _cols(3)] = (dzg * sgb).astype(BF16)
        dz_ref[:, _cols(4)] = (dzg * a * sgb * (1.0 - sgb)).astype(BF16)
        edz[tm:tm + HC, :] = edz[0:HC, :]

    rev = lambda i: (nt - 1 - i, 0)
    vec = pl.BlockSpec((1, D), lambda i: (0, 0))
    wspec = pl.BlockSpec((HEADS, SGU_BLOCK, SGU_BLOCK), lambda i: (0, 0, 0))
    bspec = pl.BlockSpec((HEADS, SGU_BLOCK, 1), lambda i: (0, 0, 0))
    kspec = pl.BlockSpec((KD, D), lambda i: (0, 0))
    return pl.pallas_call(
        body, name=f"mix_odd_bwd_{layer}", grid=(nt,),
        in_specs=[pl.BlockSpec((tm, NIN), rev), pl.BlockSpec((tm, DMIX), rev), pl.BlockSpec((tm, D), rev),
                  vec, vec, wspec, wspec, bspec, kspec, vec, vec],
        out_specs=[pl.BlockSpec((tm, NIN), rev), vec, vec, wspec, bspec, kspec, vec, vec, vec],
        out_shape=[jax.ShapeDtypeStruct((s, NIN), BF16), jax.ShapeDtypeStruct((1, D), F32),
                   jax.ShapeDtypeStruct((1, D), F32),
                   jax.ShapeDtypeStruct((HEADS, SGU_BLOCK, SGU_BLOCK), F32),
                   jax.ShapeDtypeStruct((HEADS, SGU_BLOCK, 1), F32), jax.ShapeDtypeStruct((KD, D), F32),
                   jax.ShapeDtypeStruct((1, D), F32), jax.ShapeDtypeStruct((1, D), F32),
                   jax.ShapeDtypeStruct((1, D), F32)],
        scratch_shapes=[pltpu.VMEM((tm, D), F32), pltpu.VMEM((tm + HC, D), F32)],
        compiler_params=_cp(("arbitrary",)),
    )(z, dycat, cz, slg, slb, wsb, wstb, sbcol, dw, dng, dnb)


def _adamw_math(w, g, m, v):
    m = ADAM_B1 * m + (1.0 - ADAM_B1) * g
    v = ADAM_B2 * v + (1.0 - ADAM_B2) * (g * g)
    m_hat = m / (1.0 - ADAM_B1 ** ADAM_STEP)
    v_hat = v / (1.0 - ADAM_B2 ** ADAM_STEP)
    delta = -ADAM_LR * (m_hat / (jnp.sqrt(v_hat) + ADAM_EPS) + ADAM_WD * w)
    return delta, m, v


def adamw_big(w, m, v, parts0, parts1, name):
    _, r, c = w.shape
    tr = min(r, 256)
    nr = r // tr

    def body(w_ref, m_ref, v_ref, p0_ref, p1_ref, g_ref, d_ref, nm_ref, nv_ref):
        i = pl.program_id(0)

        def total(p_ref):
            acc = p_ref[0].astype(F32)
            for j in range(1, NDEV):
                acc = acc + p_ref[j].astype(F32)
            return acc

        @pl.when(i == 0)
        def _():
            g_ref[...] = total(p0_ref)

        @pl.when(i == 1)
        def _():
            g_ref[...] = total(p1_ref)

        delta, nm, nv = _adamw_math(w_ref[...], g_ref[...], m_ref[...], v_ref[...])
        d_ref[...] = delta
        nm_ref[...] = nm
        nv_ref[...] = nv

    wspec = pl.BlockSpec((None, tr, c), lambda i, j: (i, j, 0))
    p0 = pl.BlockSpec((NDEV, tr, c), lambda i, j: (0, jnp.where(i == 0, j, nr - 1), 0))
    p1 = pl.BlockSpec((NDEV, tr, c), lambda i, j: (0, jnp.where(i == 1, j, 0), 0))
    shp = jax.ShapeDtypeStruct(w.shape, F32)
    return pl.pallas_call(
        body, name=name, grid=(2, nr), in_specs=[wspec, wspec, wspec, p0, p1],
        out_specs=[wspec] * 4, out_shape=[shp] * 4,
        compiler_params=_cp(("arbitrary", "arbitrary")),
    )(w, m, v, parts0, parts1)


def sum_parts(parts, name):
    _, r, c = parts.shape

    def body(p_ref, o_ref):
        acc = p_ref[0]
        for j in range(1, NDEV):
            acc = acc + p_ref[j]
        o_ref[...] = acc

    return pl.pallas_call(body, name=name, out_shape=jax.ShapeDtypeStruct((r, c), F32),
                          compiler_params=_cp())(parts)


def adamw_small(w, g, m, v, name):
    def body(w_ref, g_ref, m_ref, v_ref, d_ref, nm_ref, nv_ref):
        delta, nm, nv = _adamw_math(w_ref[...], g_ref[...], m_ref[...], v_ref[...])
        d_ref[...] = delta
        nm_ref[...] = nm
        nv_ref[...] = nv

    shp = jax.ShapeDtypeStruct(w.shape, F32)
    return pl.pallas_call(body, name=name, out_shape=[shp] * 3, compiler_params=_cp())(w, g, m, v)


def _pack(arrays, rows):
    flat = jnp.concatenate([a.reshape(-1) for a in arrays])
    return jnp.pad(flat, (0, rows * 128 - flat.shape[0])).reshape(rows, 128)


def _unpack(flat, shapes):
    out, o = [], 0
    for shp in shapes:
        n = 1
        for d in shp:
            n *= d
        out.append(flat[o:o + n].reshape(shp))
        o += n
    return out


def _rows_for(shapes):
    n = 0
    for shp in shapes:
        k = 1
        for d in shp:
            k *= d
        n += k
    return -(-n // 1024) * 8


SHARDED_SMALL = (("pool_w", (2, 4, 256, 256), 2), ("sconv_w", (2, KS, D), 2), ("sgu_ln_g", (2, D), 1),
                 ("sgu_ln_b", (2, D), 1), ("dconv_w", (2, KD, D), 2), ("dconv_b", (2, D), 1),
                 ("dnorm_g", (2, D), 1), ("dnorm_b", (2, D), 1))
REPLICATED_SMALL = (("ln_g", (DEPTH, D)), ("ln_b", (DEPTH, D)), ("pool_scale", (2, D)), ("sconv_b", (2, D)),
                    ("sgu_w", (2, HEADS, SGU_BLOCK, SGU_BLOCK)), ("sgu_b", (2, HEADS, SGU_BLOCK)))


def _shard_shape(shape, axis):
    return tuple(d // NDEV if a == axis else d for a, d in enumerate(shape))


def _merge_gathered(g, shape, axis):
    return jnp.moveaxis(g, 0, axis).reshape(shape)


def kernel(x, ln_g, ln_b, w_in_even, w_out_even, pool_w, pool_scale, sconv_w, sconv_b, w_in_odd, w_out_odd, sgu_ln_g, sgu_ln_b, sgu_w, sgu_b, dconv_w, dconv_b, dnorm_g, dnorm_b, loss_target, m_ln_g, m_ln_b, m_w_in_even, m_w_out_even, m_pool_w, m_pool_scale, m_sconv_w, m_sconv_b, m_w_in_odd, m_w_out_odd, m_sgu_ln_g, m_sgu_ln_b, m_sgu_w, m_sgu_b, m_dconv_w, m_dconv_b, m_dnorm_g, m_dnorm_b, v_ln_g, v_ln_b, v_w_in_even, v_w_out_even, v_pool_w, v_pool_scale, v_sconv_w, v_sconv_b, v_w_in_odd, v_w_out_odd, v_sgu_ln_g, v_sgu_ln_b, v_sgu_w, v_sgu_b, v_dconv_w, v_dconv_b, v_dnorm_g, v_dnorm_b):
    given = dict(locals())
    me = 4 * lax.axis_index("x") + 2 * lax.axis_index("y") + lax.axis_index("c")
    xs = x[0]
    target = loss_target[0]

    wg_in = all_gather(cast_weights(w_in_even, w_in_odd, "cast_w_in"), "gather_w_in")
    wg_out = all_gather(cast_weights(w_out_even, w_out_odd, "cast_w_out"), "gather_w_out")

    shard_shapes = [_shard_shape(shp, ax) for _, shp, ax in SHARDED_SMALL]
    srows = _rows_for(shard_shapes)
    gathered = all_gather(_pack([given[n] for n, _, _ in SHARDED_SMALL], srows), "gather_small")
    gathered = gathered.reshape(NDEV, -1)
    full, o = {}, 0
    for (n, shp, ax), sshp in zip(SHARDED_SMALL, shard_shapes):
        cnt = 1
        for d_ in sshp:
            cnt *= d_
        full[n] = _merge_gathered(gathered[:, o:o + cnt].reshape((NDEV,) + sshp), shp, ax)
        o += cnt

    mask = (jnp.arange(SGU_BLOCK)[None, :] // 64) <= (jnp.arange(SGU_BLOCK)[:, None] // 64)
    ws = jnp.where(mask[None, None], sgu_w, 0.0)
    wsb = ws.astype(BF16)
    wstb = jnp.swapaxes(ws, -1, -2).astype(BF16)
    pwb = full["pool_w"].astype(BF16)

    ones = jnp.ones((1, D), F32)
    zeros = jnp.zeros((1, D), F32)

    xres, gp, bp = xs, ones, zeros
    xb = cast_x(xs)
    saved = []
    for layer in range(DEPTH):
        i = layer // 2
        z = proj_in(xb, wg_in, layer)
        if layer % 2 == 0:
            ycat, pooled, cv = mix_even_fwd(z, pwb[i], pool_scale[i][None], full["sconv_w"][i],
                                            sconv_b[i][None], layer)
            extra = (pooled, cv)
        else:
            ycat, cz = mix_odd_fwd(z, full["sgu_ln_g"][i][None], full["sgu_ln_b"][i][None], wsb[i],
                                   sgu_b[i][:, :, None], full["dconv_w"][i], full["dconv_b"][i][None],
                                   full["dnorm_g"][i][None], full["dnorm_b"][i][None], layer)
            extra = (cz,)
        g, b = ln_g[layer][None], ln_b[layer][None]
        outs = proj_out_ln(ycat, wg_out, layer, xres, gp, bp, g, b,
                           target=target if layer == DEPTH - 1 else None)
        xhat, rstd, xb_next = outs[0], outs[1], outs[2]
        saved.append((xb, z, ycat, extra, xhat, rstd))
        xres, gp, bp, xb = xhat, g, b, xb_next
    dxn, loss_local = outs[3], outs[4]
    loss = lax.psum(loss_local[0, 0], AXES)

    gsmall = {n: [None] * shp[0] for n, shp in REPLICATED_SMALL}
    gsmall.update({n: [None] * shp[0] for n, shp, _ in SHARDED_SMALL})
    parts_in, parts_out = [None] * DEPTH, [None] * DEPTH
    for layer in reversed(range(DEPTH)):
        i = layer // 2
        xb, z, ycat, extra, xhat, rstd = saved[layer]
        dr, drb, dycat, dg, db = ln_bwd_dycat(dxn, xhat, rstd, ln_g[layer][None], wg_out, layer)
        gsmall["ln_g"][layer], gsmall["ln_b"][layer] = dg[0], db[0]
        parts_out[layer] = exchange_blocks(dw_out(ycat, drb, layer), f"exchange_dw_out_{layer}")
        if layer % 2 == 0:
            dzb, dpw, dps, dsw, dsb = mix_even_bwd(z, dycat, extra[0], extra[1], pwb[i], pool_scale[i][None],
                                                   full["sconv_w"][i], layer)
            gsmall["pool_w"][i], gsmall["pool_scale"][i] = dpw, dps[0]
            gsmall["sconv_w"][i], gsmall["sconv_b"][i] = dsw, dsb[0]
        else:
            dzb, dslg, dslb, dws, dsbc, ddw, ddcb, ddng, ddnb = mix_odd_bwd(
                z, dycat, extra[0], full["sgu_ln_g"][i][None], full["sgu_ln_b"][i][None], wsb[i], wstb[i],
                sgu_b[i][:, :, None], full["dconv_w"][i], full["dnorm_g"][i][None], full["dnorm_b"][i][None],
                layer)
            gsmall["sgu_ln_g"][i], gsmall["sgu_ln_b"][i] = dslg[0], dslb[0]
            gsmall["sgu_w"][i], gsmall["sgu_b"][i] = jnp.where(mask[None], dws, 0.0), dsbc[:, :, 0]
            gsmall["dconv_w"][i], gsmall["dconv_b"][i] = ddw, ddcb[0]
            gsmall["dnorm_g"][i], gsmall["dnorm_b"][i] = ddng[0], ddnb[0]
        parts_in[layer] = exchange_blocks(dw_in(xb, dzb, layer), f"exchange_dw_in_{layer}")
        dxn = dx_in(dzb, wg_in, layer, dr)
    grad_x = dxn[None]

    big = {}
    big["w_in_even"] = adamw_big(w_in_even, m_w_in_even, v_w_in_even, parts_in[0], parts_in[2], "adamw_w_in_even")
    big["w_in_odd"] = adamw_big(w_in_odd, m_w_in_odd, v_w_in_odd, parts_in[1], parts_in[3], "adamw_w_in_odd")
    big["w_out_even"] = adamw_big(w_out_even, m_w_out_even, v_w_out_even, parts_out[0], parts_out[2], "adamw_w_out_even")
    big["w_out_odd"] = adamw_big(w_out_odd, m_w_out_odd, v_w_out_odd, parts_out[1], parts_out[3], "adamw_w_out_odd")

    small_names = [n for n, _ in REPLICATED_SMALL] + [n for n, _, _ in SHARDED_SMALL]
    small_shapes = [shp for _, shp in REPLICATED_SMALL] + [shp for _, shp, _ in SHARDED_SMALL]
    grows = _rows_for(small_shapes)
    partial = _pack([jnp.stack(gsmall[n]) for n in small_names], grows)
    gsum = sum_parts(all_gather(partial, "gather_small_grads"), "sum_small_grads")
    gfull = dict(zip(small_names, _unpack(gsum.reshape(-1), small_shapes)))
    own_shapes = [shp for _, shp in REPLICATED_SMALL] + shard_shapes
    gown = {n: gfull[n] for n, _ in REPLICATED_SMALL}
    for (n, shp, ax), sshp in zip(SHARDED_SMALL, shard_shapes):
        gown[n] = lax.dynamic_slice_in_dim(gfull[n], me * sshp[ax], sshp[ax], axis=ax)
    orows = _rows_for(own_shapes)
    packed = [_pack([src[n] for n in small_names], orows) for src in
              (given, gown, {n: given["m_" + n] for n in small_names}, {n: given["v_" + n] for n in small_names})]
    sd, sm, sv = adamw_small(*packed, "adamw_small")
    small = {}
    for n, d_, m_, v_ in zip(small_names, _unpack(sd.reshape(-1), own_shapes), _unpack(sm.reshape(-1), own_shapes),
                             _unpack(sv.reshape(-1), own_shapes)):
        small[n] = (gown[n], d_, m_, v_)

    weights = ['ln_g', 'ln_b', 'w_in_even', 'w_out_even', 'pool_w', 'pool_scale', 'sconv_w', 'sconv_b', 'w_in_odd',
               'w_out_odd', 'sgu_ln_g', 'sgu_ln_b', 'sgu_w', 'sgu_b', 'dconv_w', 'dconv_b', 'dnorm_g', 'dnorm_b']
    res = {n: (big[n] if n in big else small[n]) for n in weights}
    return (loss, grad_x, *[res[n][0] for n in weights], *[res[n][1] for n in weights],
            *[res[n][2] for n in weights], *[res[n][3] for n in weights])
```

```python
import jax
import jax.numpy as jnp
from jax import lax
from jax.experimental import pallas as pl
from jax.experimental.pallas import tpu as pltpu

F32 = jnp.float32
BF16 = jnp.bfloat16

D = 1024
DMIX = 2048
NIN = 6144
NDEV = 8
CW = NIN // NDEV
RW = DMIX // NDEV
DEPTH = 4
ALPHA = (2 * DEPTH) ** 0.25
LN_EPS = 1e-5
POOL_WINDOWS = (2, 4, 8, 16)
PG = 256
SGU_BLOCK = 128
HEADS = 4
HD = 256
KD = 31
KS = 3
SUB = 8
HP = 16
HS = 8
HC = 32
NSEM = 7

ADAM_LR = 0.001
ADAM_B1 = 0.9
ADAM_B2 = 0.999
ADAM_EPS = 1e-08
ADAM_WD = 0.01
ADAM_STEP = 10

VMEM_LIMIT = 56 * 1024 * 1024
MESH = pl.DeviceIdType.MESH
ANY = pl.BlockSpec(memory_space=pl.ANY)


def _cp(sem=None):
    if sem is None:
        return pltpu.CompilerParams(vmem_limit_bytes=VMEM_LIMIT)
    return pltpu.CompilerParams(dimension_semantics=sem, vmem_limit_bytes=VMEM_LIMIT)


def _sig(x):
    return 0.5 * jnp.tanh(0.5 * x) + 0.5


def _dot(a, b):
    return jnp.dot(a, b, preferred_element_type=F32)


def _dot_nt(a, b):
    return lax.dot_general(a, b, (((1,), (1,)), ((), ())), preferred_element_type=F32)


def _dot_tn(a, b):
    return lax.dot_general(a, b, (((0,), (0,)), ((), ())), preferred_element_type=F32)


def _rowsum(x):
    return jnp.sum(x, axis=0, keepdims=True)


def _comm_scratch(n):
    return [pltpu.SemaphoreType.DMA((n * NSEM,)), pltpu.SemaphoreType.DMA((n * NSEM,)),
            pltpu.SemaphoreType.DMA((n,))]


def _gather_plan(src_ref, out_ref, sems, n):
    send_sems, recv_sems, local_sems = sems
    base = n * NSEM
    x, y, c = lax.axis_index("x"), lax.axis_index("y"), lax.axis_index("c")
    me, sibling = (x, y, c), (x, y, 1 - c)
    chips = [(1 - x, y), (x, 1 - y), (1 - x, 1 - y)]

    def slot(px, py, pc):
        return out_ref.at[4 * px + 2 * py + pc]

    def copy(k, blk, to, src=None):
        return pltpu.make_async_remote_copy(
            src_ref=slot(*blk) if src is None else src, dst_ref=slot(*blk),
            send_sem=send_sems.at[base + k], recv_sem=recv_sems.at[base + k],
            device_id=to, device_id_type=MESH)

    def mine():
        return pltpu.make_async_copy(src_ref, slot(*me), local_sems.at[n])

    def start():
        mine().start()
        copy(0, me, sibling, src=src_ref).start()
        for j, chip in enumerate(chips):
            copy(1 + j, me, (*chip, c), src=src_ref).start()

    def finish():
        for j, chip in enumerate(chips):
            copy(1 + j, (*chip, c), me).wait_recv()
            copy(4 + j, (*chip, c), sibling).start()
        copy(0, sibling, me).wait_recv()
        for j, chip in enumerate(chips):
            copy(4 + j, (*chip, 1 - c), me).wait_recv()
        copy(0, me, sibling, src=src_ref).wait_send()
        for j, chip in enumerate(chips):
            copy(1 + j, me, (*chip, c), src=src_ref).wait_send()
            copy(4 + j, (*chip, c), sibling).wait_send()
        mine().wait()

    return start, finish


def _exchange_plan(p_ref, out_ref, sems, n):
    send_sems, recv_sems, local_sems = sems
    base = n * NSEM
    x, y, c = lax.axis_index("x"), lax.axis_index("y"), lax.axis_index("c")
    me = 4 * x + 2 * y + c

    def copy(r, landing):
        px, py, pc = x ^ (r >> 2), y ^ ((r >> 1) & 1), c ^ (r & 1)
        peer = 4 * px + 2 * py + pc
        return pltpu.make_async_remote_copy(
            src_ref=p_ref.at[peer], dst_ref=out_ref.at[peer if landing else me],
            send_sem=send_sems.at[base + r - 1], recv_sem=recv_sems.at[base + r - 1],
            device_id=(px, py, pc), device_id_type=MESH)

    def mine():
        return pltpu.make_async_copy(p_ref.at[me], out_ref.at[me], local_sems.at[n])

    def start():
        mine().start()
        for r in range(1, NDEV):
            copy(r, False).start()

    def finish():
        for r in range(1, NDEV):
            copy(r, True).wait_recv()
        for r in range(1, NDEV):
            copy(r, False).wait_send()
        mine().wait()

    return start, finish


def all_gather(src, name, index=None):
    shape = src.shape if index is None else src.shape[1:]

    def body(x_ref, out_ref, *sems):
        start, finish = _gather_plan(x_ref if index is None else x_ref.at[index], out_ref, sems, 0)
        start()
        finish()

    return pl.pallas_call(
        body, name=name, out_shape=jax.ShapeDtypeStruct((NDEV,) + shape, src.dtype),
        in_specs=[ANY], out_specs=ANY, scratch_shapes=_comm_scratch(1),
    )(src)


def cast_x(x):
    s = x.shape[0]
    tm = min(s, 512)

    def body(x_ref, o_ref):
        o_ref[...] = x_ref[...].astype(BF16)

    return pl.pallas_call(
        body, name="cast_x", grid=(s // tm,),
        in_specs=[pl.BlockSpec((tm, D), lambda i: (i, 0))],
        out_specs=pl.BlockSpec((tm, D), lambda i: (i, 0)),
        out_shape=jax.ShapeDtypeStruct((s, D), BF16), compiler_params=_cp(("parallel",)),
    )(x)


def cast_weights(w_even, w_odd, name):
    _, r, c = w_even.shape

    def body(e_ref, o_ref, out_ref):
        layer = pl.program_id(0)

        @pl.when(layer % 2 == 0)
        def _():
            out_ref[...] = e_ref[...].astype(BF16)

        @pl.when(layer % 2 == 1)
        def _():
            out_ref[...] = o_ref[...].astype(BF16)

    spec = pl.BlockSpec((None, r, c), lambda l: (l // 2, 0, 0))
    return pl.pallas_call(
        body, name=name, grid=(DEPTH,), in_specs=[spec, spec],
        out_specs=pl.BlockSpec((None, r, c), lambda l: (l, 0, 0)),
        out_shape=jax.ShapeDtypeStruct((DEPTH, r, c), BF16), compiler_params=_cp(("parallel",)),
    )(w_even, w_odd)


def proj_in(xb, wg, layer, wc_in, wc_out):
    s = xb.shape[0]
    tm = min(s, 512)
    nt = s // tm
    nxt = layer + 1 < DEPTH

    def body(*refs):
        if nxt:
            x_ref, w_ref, wco_ref, wci_ref, o_ref, go_ref, gi_ref = refs[:7]
        else:
            x_ref, w_ref, wco_ref, o_ref, go_ref = refs[:5]
        sems = refs[-3:]
        k, i = pl.program_id(0), pl.program_id(1)
        plans = [_gather_plan(wco_ref.at[layer], go_ref, sems, 0)]
        if nxt:
            plans.append(_gather_plan(wci_ref.at[layer + 1], gi_ref, sems, 1))

        @pl.when((k == 0) & (i == 0))
        def _():
            for start, _ in plans:
                start()

        o_ref[...] = _dot(x_ref[...], w_ref[...]).astype(BF16)

        @pl.when((k == NDEV - 1) & (i == nt - 1))
        def _():
            for _, finish in plans:
                finish()

    in_specs = [pl.BlockSpec((tm, D), lambda k, i: (i, 0)),
                pl.BlockSpec((None, D, CW), lambda k, i: (k, 0, 0)), ANY]
    out_specs = [pl.BlockSpec((tm, CW), lambda k, i: (i, k)), ANY]
    out_shape = [jax.ShapeDtypeStruct((s, NIN), BF16), jax.ShapeDtypeStruct((NDEV, RW, D), BF16)]
    args = [xb, wg, wc_out]
    if nxt:
        in_specs.append(ANY)
        args.append(wc_in)
        out_specs.append(ANY)
        out_shape.append(jax.ShapeDtypeStruct((NDEV, D, CW), BF16))
    return pl.pallas_call(
        body, name=f"proj_in_{layer}", grid=(NDEV, nt), in_specs=in_specs, out_specs=out_specs,
        out_shape=out_shape, scratch_shapes=_comm_scratch(2 if nxt else 1),
        compiler_params=_cp(("arbitrary", "arbitrary")),
    )(*args)


def proj_out_ln(ycat, wog, layer, xres, g_prev, b_prev, g, b, target=None):
    s = ycat.shape[0]
    tm = min(s, 256)
    nt = s // tm
    final = target is not None

    def body(*refs):
        if final:
            y_ref, w_ref, xr_ref, gp_ref, bp_ref, g_ref, b_ref, t_ref, xh_ref, rs_ref, xb_ref, dy_ref, loss_ref = refs
        else:
            y_ref, w_ref, xr_ref, gp_ref, bp_ref, g_ref, b_ref, xh_ref, rs_ref, xb_ref = refs
        xin = xr_ref[...] * gp_ref[...] + bp_ref[...]
        y = _dot(y_ref[...], w_ref[...].reshape(DMIX, D))
        r = ALPHA * xin + y
        mu = jnp.mean(r, axis=-1, keepdims=True)
        d = r - mu
        var = jnp.mean(d * d, axis=-1, keepdims=True)
        rstd = lax.rsqrt(var + LN_EPS)
        xh = d * rstd
        xh_ref[...] = xh
        rs_ref[...] = rstd
        xn = xh * g_ref[...] + b_ref[...]
        xb_ref[...] = xn.astype(BF16)
        if final:
            err = xn - t_ref[...]
            dy_ref[...] = err * (1.0 / D)

            @pl.when(pl.program_id(0) == 0)
            def _():
                loss_ref[...] = jnp.zeros_like(loss_ref)

            loss_ref[...] += 0.5 * jnp.sum(jnp.mean(err * err, axis=-1, keepdims=True), axis=0, keepdims=True)

    row = pl.BlockSpec((tm, D), lambda i: (i, 0))
    vec = pl.BlockSpec((1, D), lambda i: (0, 0))
    in_specs = [pl.BlockSpec((tm, DMIX), lambda i: (i, 0)),
                pl.BlockSpec((NDEV, RW, D), lambda i: (0, 0, 0)),
                row, vec, vec, vec, vec]
    out_specs = [row, pl.BlockSpec((tm, 1), lambda i: (i, 0)), row]
    out_shape = [jax.ShapeDtypeStruct((s, D), F32), jax.ShapeDtypeStruct((s, 1), F32),
                 jax.ShapeDtypeStruct((s, D), BF16)]
    args = [ycat, wog, xres, g_prev, b_prev, g, b]
    if final:
        in_specs.append(row)
        args.append(target)
        out_specs += [row, pl.BlockSpec((1, 1), lambda i: (0, 0))]
        out_shape += [jax.ShapeDtypeStruct((s, D), F32), jax.ShapeDtypeStruct((1, 1), F32)]
    return pl.pallas_call(
        body, name=f"proj_out_ln_{layer}", grid=(nt,), in_specs=in_specs, out_specs=out_specs,
        out_shape=out_shape, compiler_params=_cp(("arbitrary",)),
    )(*args)


def ln_bwd_dycat(dxn, xhat, rstd, g, wog, layer):
    s = dxn.shape[0]
    tm = min(s, 256)

    def body(dx_ref, xh_ref, rs_ref, g_ref, w_ref, dr_ref, drb_ref, dyc_ref, dg_ref, db_ref):
        @pl.when(pl.program_id(0) == 0)
        def _():
            dg_ref[...] = jnp.zeros_like(dg_ref)
            db_ref[...] = jnp.zeros_like(db_ref)

        dxo = dx_ref[...]
        xh = xh_ref[...]
        dg_ref[...] += _rowsum(dxo * xh)
        db_ref[...] += _rowsum(dxo)
        dxh = dxo * g_ref[...]
        m1 = jnp.mean(dxh, axis=-1, keepdims=True)
        m2 = jnp.mean(dxh * xh, axis=-1, keepdims=True)
        dr = rs_ref[...] * (dxh - m1 - xh * m2)
        dr_ref[...] = dr
        drb = dr.astype(BF16)
        drb_ref[...] = drb
        dyc_ref[...] = _dot_nt(drb, w_ref[...].reshape(DMIX, D))

    row = pl.BlockSpec((tm, D), lambda i: (i, 0))
    vec = pl.BlockSpec((1, D), lambda i: (0, 0))
    return pl.pallas_call(
        body, name=f"ln_bwd_dycat_{layer}", grid=(s // tm,),
        in_specs=[row, row, pl.BlockSpec((tm, 1), lambda i: (i, 0)), vec,
                  pl.BlockSpec((NDEV, RW, D), lambda i: (0, 0, 0))],
        out_specs=[row, row, pl.BlockSpec((tm, DMIX), lambda i: (i, 0)), vec, vec],
        out_shape=[jax.ShapeDtypeStruct((s, D), F32), jax.ShapeDtypeStruct((s, D), BF16),
                   jax.ShapeDtypeStruct((s, DMIX), F32), jax.ShapeDtypeStruct((1, D), F32),
                   jax.ShapeDtypeStruct((1, D), F32)],
        compiler_params=_cp(("arbitrary",)),
    )(dxn, xhat, rstd, g, wog)


def dx_in(dzb, wg, layer, dr, parts=None):
    s = dzb.shape[0]
    tm = min(s, 256)
    nt = s // tm
    carry = parts is not None

    def body(*refs):
        if carry:
            dz_ref, w_ref, dr_ref, p_ref, o_ref, land_ref = refs[:6]
            start, finish = _exchange_plan(p_ref, land_ref, refs[-3:], 0)
            pl.when(pl.program_id(0) == 0)(start)
        else:
            dz_ref, w_ref, dr_ref, o_ref = refs
        acc = ALPHA * dr_ref[...]
        for k in range(NDEV):
            acc += _dot_nt(dz_ref[:, k * CW:(k + 1) * CW], w_ref[k])
        o_ref[...] = acc
        if carry:
            pl.when(pl.program_id(0) == nt - 1)(finish)

    row = pl.BlockSpec((tm, D), lambda i: (i, 0))
    in_specs = [pl.BlockSpec((tm, NIN), lambda i: (i, 0)), pl.BlockSpec((NDEV, D, CW), lambda i: (0, 0, 0)), row]
    out_specs, out_shape, args = [row], [jax.ShapeDtypeStruct((s, D), F32)], [dzb, wg, dr]
    if carry:
        in_specs.append(ANY)
        args.append(parts)
        out_specs.append(ANY)
        out_shape.append(jax.ShapeDtypeStruct(parts.shape, parts.dtype))
    return pl.pallas_call(
        body, name=f"dx_in_{layer}", grid=(nt,), in_specs=in_specs, out_specs=out_specs, out_shape=out_shape,
        scratch_shapes=_comm_scratch(1) if carry else [], compiler_params=_cp(("arbitrary",)),
    )(*args)


def dw_in(xb, dzb, layer, parts=()):
    s = xb.shape[0]
    tm = min(s, 512)
    nt = s // tm
    n = len(parts)

    def body(*refs):
        x_ref, dz_ref = refs[:2]
        p_refs = refs[2:2 + n]
        o_ref = refs[2 + n]
        land_refs = refs[3 + n:3 + 2 * n]
        acc = refs[3 + 2 * n]
        plans = [_exchange_plan(p, land, refs[-3:], j) for j, (p, land) in enumerate(zip(p_refs, land_refs))]
        k, t = pl.program_id(0), pl.program_id(1)

        @pl.when((k == 0) & (t == 0))
        def _():
            for start, _ in plans:
                start()

        @pl.when(t == 0)
        def _():
            acc[...] = jnp.zeros_like(acc)

        acc[...] += _dot_tn(x_ref[...], dz_ref[...])

        @pl.when(t == nt - 1)
        def _():
            o_ref[...] = acc[...].astype(BF16)

        @pl.when((k == NDEV - 1) & (t == nt - 1))
        def _():
            for _, finish in plans:
                finish()

    return pl.pallas_call(
        body, name=f"dw_in_{layer}", grid=(NDEV, nt),
        in_specs=[pl.BlockSpec((tm, D), lambda k, t: (t, 0)), pl.BlockSpec((tm, CW), lambda k, t: (t, k))] + [ANY] * n,
        out_specs=[pl.BlockSpec((None, D, CW), lambda k, t: (k, 0, 0))] + [ANY] * n,
        out_shape=[jax.ShapeDtypeStruct((NDEV, D, CW), BF16)] + [jax.ShapeDtypeStruct(p.shape, p.dtype) for p in parts],
        scratch_shapes=[pltpu.VMEM((D, CW), F32)] + (_comm_scratch(n) if n else []),
        compiler_params=_cp(("arbitrary", "arbitrary")),
    )(xb, dzb, *parts)


def dw_out(ycat, drb, layer):
    s = ycat.shape[0]
    tm = min(s, 512)
    nt = s // tm

    def body(y_ref, dr_ref, o_ref, acc):
        t = pl.program_id(0)

        @pl.when(t == 0)
        def _():
            acc[...] = jnp.zeros_like(acc)

        acc[...] += _dot_tn(y_ref[...], dr_ref[...])

        @pl.when(t == nt - 1)
        def _():
            o_ref[...] = acc[...].reshape(NDEV, RW, D).astype(BF16)

    return pl.pallas_call(
        body, name=f"dw_out_{layer}", grid=(nt,),
        in_specs=[pl.BlockSpec((tm, DMIX), lambda t: (t, 0)), pl.BlockSpec((tm, D), lambda t: (t, 0))],
        out_specs=pl.BlockSpec((NDEV, RW, D), lambda t: (0, 0, 0)),
        out_shape=jax.ShapeDtypeStruct((NDEV, RW, D), BF16),
        scratch_shapes=[pltpu.VMEM((DMIX, D), F32)],
        compiler_params=_cp(("arbitrary",)),
    )(ycat, drb)


def _cols(j):
    return slice(j * D, (j + 1) * D)


def mix_even_fwd(z, pwb, pscale, sw, sb, layer):
    s = z.shape[0]
    tm = min(s, 256)

    def body(z_ref, pw_ref, ps_ref, sw_ref, sb_ref, yc_ref, pb_ref, cv_ref, exa, eq):
        i = pl.program_id(0)

        @pl.when(i == 0)
        def _():
            exa[0:HP, :] = jnp.zeros((HP, D), F32)
            eq[0:HS, :] = jnp.zeros((HS, D), F32)

        exa[HP:HP + tm, :] = z_ref[:, _cols(0)].astype(F32)
        q = z_ref[:, _cols(4)].astype(F32) * z_ref[:, _cols(2)].astype(F32)
        eq[HS:HS + tm, :] = q
        pos = (i * tm + lax.broadcasted_iota(jnp.int32, (tm, 1), 0) + 1).astype(F32)
        for gi, w in enumerate(POOL_WINDOWS):
            c0, c1 = gi * PG, (gi + 1) * PG
            acc = exa[HP:HP + tm, c0:c1]
            for j in range(1, w):
                acc = acc + exa[HP - j:HP - j + tm, c0:c1]
            pooled = acc / jnp.minimum(pos, float(w)) - exa[HP:HP + tm, c0:c1]
            pb = pooled.astype(BF16)
            pb_ref[:, c0:c1] = pb
            ga = z_ref[:, D + c0:D + c1].astype(F32)
            ya = _dot(pb, pw_ref[gi]) * ps_ref[:, c0:c1] * (ga * _sig(ga))
            yc_ref[:, c0:c1] = ya.astype(BF16)
        cv = (sw_ref[2:3, :] * q + sw_ref[1:2, :] * eq[HS - 1:HS - 1 + tm, :]
              + sw_ref[0:1, :] * eq[HS - 2:HS - 2 + tm, :] + sb_ref[...])
        cv_ref[...] = cv
        gb = z_ref[:, _cols(5)].astype(F32)
        yc_ref[:, D:2 * D] = (z_ref[:, _cols(3)].astype(F32) * cv * (gb * _sig(gb))).astype(BF16)
        exa[0:HP, :] = exa[tm:tm + HP, :]
        eq[0:HS, :] = eq[tm:tm + HS, :]

    vec = pl.BlockSpec((1, D), lambda i: (0, 0))
    return pl.pallas_call(
        body, name=f"mix_even_fwd_{layer}", grid=(s // tm,),
        in_specs=[pl.BlockSpec((tm, NIN), lambda i: (i, 0)),
                  pl.BlockSpec((4, PG, PG), lambda i: (0, 0, 0)), vec,
                  pl.BlockSpec((KS, D), lambda i: (0, 0)), vec],
        out_specs=[pl.BlockSpec((tm, DMIX), lambda i: (i, 0)), pl.BlockSpec((tm, D), lambda i: (i, 0)),
                   pl.BlockSpec((tm, D), lambda i: (i, 0))],
        out_shape=[jax.ShapeDtypeStruct((s, DMIX), BF16), jax.ShapeDtypeStruct((s, D), BF16),
                   jax.ShapeDtypeStruct((s, D), F32)],
        scratch_shapes=[pltpu.VMEM((HP + tm, D), F32), pltpu.VMEM((HS + tm, D), F32)],
        compiler_params=_cp(("arbitrary",)),
    )(z, pwb, pscale, sw, sb)


def mix_even_bwd(z, dycat, pooled, cv, pwb, pscale, sw, layer, parts=None):
    s = z.shape[0]
    tm = min(s, 256)
    nt = s // tm
    carry = parts is not None

    def body(*refs):
        z_ref, dy_ref, pb_ref, cv_ref, pw_ref, ps_ref, sw_ref = refs[:7]
        if carry:
            p_ref = refs[7]
            dz_ref, dpw_ref, dps_ref, dsw_ref, dsb_ref, land_ref, edp, edc = refs[8:16]
            start, finish = _exchange_plan(p_ref, land_ref, refs[-3:], 0)
        else:
            dz_ref, dpw_ref, dps_ref, dsw_ref, dsb_ref, edp, edc = refs[7:14]
        i = pl.program_id(0)
        if carry:
            pl.when(i == 0)(start)

        @pl.when(i == 0)
        def _():
            edp[tm:tm + HP, :] = jnp.zeros((HP, D), F32)
            edc[tm:tm + HS, :] = jnp.zeros((HS, D), F32)
            dpw_ref[...] = jnp.zeros_like(dpw_ref)
            dps_ref[...] = jnp.zeros_like(dps_ref)
            dsw_ref[...] = jnp.zeros_like(dsw_ref)
            dsb_ref[...] = jnp.zeros_like(dsb_ref)

        pos = ((nt - 1 - i) * tm + lax.broadcasted_iota(jnp.int32, (tm, 1), 0) + 1).astype(F32)
        for gi, w in enumerate(POOL_WINDOWS):
            c0, c1 = gi * PG, (gi + 1) * PG
            ga = z_ref[:, D + c0:D + c1].astype(F32)
            sg = _sig(ga)
            sil = ga * sg
            dya = dy_ref[:, c0:c1]
            pb = pb_ref[:, c0:c1]
            ya0 = _dot(pb, pw_ref[gi])
            ps = ps_ref[:, c0:c1]
            dps_ref[:, c0:c1] += _rowsum(dya * ya0 * sil)
            dz_ref[:, D + c0:D + c1] = (dya * ya0 * ps * (sg * (1.0 + ga * (1.0 - sg)))).astype(BF16)
            t1 = (dya * ps * sil).astype(BF16)
            dpool = _dot_nt(t1, pw_ref[gi])
            dpw_ref[gi] += _dot_tn(pb, t1)
            edp[0:tm, c0:c1] = dpool / jnp.minimum(pos, float(w))
            acc = -dpool
            for j in range(w):
                acc = acc + edp[j:j + tm, c0:c1]
            dz_ref[:, c0:c1] = acc.astype(BF16)
        gb = z_ref[:, _cols(5)].astype(F32)
        sgb = _sig(gb)
        silb = gb * sgb
        dyb = dy_ref[:, D:2 * D]
        cvv = cv_ref[...]
        bg = z_ref[:, _cols(3)].astype(F32)
        dz_ref[:, _cols(3)] = (dyb * cvv * silb).astype(BF16)
        dz_ref[:, _cols(5)] = (dyb * bg * cvv * (sgb * (1.0 + gb * (1.0 - sgb)))).astype(BF16)
        dcv = dyb * bg * silb
        dsb_ref[...] += _rowsum(dcv)
        edc[0:tm, :] = dcv
        h = z_ref[:, _cols(2)].astype(F32)
        cg = z_ref[:, _cols(4)].astype(F32)
        q = cg * h
        d1 = edc[1:1 + tm, :]
        d2 = edc[2:2 + tm, :]
        dq = sw_ref[2:3, :] * dcv + sw_ref[1:2, :] * d1 + sw_ref[0:1, :] * d2
        dsw_ref[2:3, :] += _rowsum(q * dcv)
        dsw_ref[1:2, :] += _rowsum(q * d1)
        dsw_ref[0:1, :] += _rowsum(q * d2)
        dz_ref[:, _cols(4)] = (dq * h).astype(BF16)
        dz_ref[:, _cols(2)] = (dq * cg).astype(BF16)
        edp[tm:tm + HP, :] = edp[0:HP, :]
        edc[tm:tm + HS, :] = edc[0:HS, :]
        if carry:
            pl.when(i == nt - 1)(finish)

    rev = lambda i: (nt - 1 - i, 0)
    vec = pl.BlockSpec((1, D), lambda i: (0, 0))
    in_specs = [pl.BlockSpec((tm, NIN), rev), pl.BlockSpec((tm, DMIX), rev),
                pl.BlockSpec((tm, D), rev), pl.BlockSpec((tm, D), rev),
                pl.BlockSpec((4, PG, PG), lambda i: (0, 0, 0)), vec,
                pl.BlockSpec((KS, D), lambda i: (0, 0))]
    out_specs = [pl.BlockSpec((tm, NIN), rev), pl.BlockSpec((4, PG, PG), lambda i: (0, 0, 0)), vec,
                 pl.BlockSpec((KS, D), lambda i: (0, 0)), vec]
    out_shape = [jax.ShapeDtypeStruct((s, NIN), BF16), jax.ShapeDtypeStruct((4, PG, PG), F32),
                 jax.ShapeDtypeStruct((1, D), F32), jax.ShapeDtypeStruct((KS, D), F32),
                 jax.ShapeDtypeStruct((1, D), F32)]
    args = [z, dycat, pooled, cv, pwb, pscale, sw]
    scratch = [pltpu.VMEM((tm + HP, D), F32), pltpu.VMEM((tm + HS, D), F32)]
    if carry:
        in_specs.append(ANY)
        args.append(parts)
        out_specs.append(ANY)
        out_shape.append(jax.ShapeDtypeStruct(parts.shape, parts.dtype))
        scratch += _comm_scratch(1)
    return pl.pallas_call(
        body, name=f"mix_even_bwd_{layer}", grid=(nt,), in_specs=in_specs, out_specs=out_specs,
        out_shape=out_shape, scratch_shapes=scratch, compiler_params=_cp(("arbitrary",)),
    )(*args)


def _ln_rows(v):
    mu = jnp.mean(v, axis=-1, keepdims=True)
    d = v - mu
    var = jnp.mean(d * d, axis=-1, keepdims=True)
    rstd = lax.rsqrt(var + LN_EPS)
    return d * rstd, rstd


def _ln_rows_bwd(dn, xh, rstd, g):
    dxh = dn * g
    m1 = jnp.mean(dxh, axis=-1, keepdims=True)
    m2 = jnp.mean(dxh * xh, axis=-1, keepdims=True)
    return rstd * (dxh - m1 - xh * m2)


def _shifted_copies(ext, shifted, rows):
    for b in range(1, SUB):
        shifted[b - 1] = ext[b:b + rows, :]


def _tap(ext, shifted, off, tm):
    a, b = off // SUB, off % SUB
    if b == 0:
        return ext[SUB * a:SUB * a + tm, :]
    return shifted[b - 1, SUB * a:SUB * a + tm, :]


def mix_odd_fwd(z, slg, slb, wsb, sbcol, dw, dcb, dng, dnb, layer):
    s = z.shape[0]
    tm = min(s, 256)

    def body(z_ref, slg_ref, slb_ref, ws_ref, sb_ref, dw_ref, dcb_ref, dng_ref, dnb_ref,
             yc_ref, cz_ref, ezg, esh):
        i = pl.program_id(0)

        @pl.when(i == 0)
        def _():
            ezg[0:HC, :] = jnp.zeros((HC, D), F32)

        vh, _ = _ln_rows(z_ref[:, _cols(1)].astype(F32))
        vnb = (vh * slg_ref[...] + slb_ref[...]).astype(BF16)
        for n in range(tm // SGU_BLOCK):
            r0, r1 = n * SGU_BLOCK, (n + 1) * SGU_BLOCK
            for hd in range(HEADS):
                c0, c1 = hd * HD, (hd + 1) * HD
                sv = _dot(ws_ref[hd], vnb[r0:r1, c0:c1]) + sb_ref[hd]
                gc = z_ref[r0:r1, 2 * D + c0:2 * D + c1].astype(F32)
                yc_ref[r0:r1, c0:c1] = (z_ref[r0:r1, c0:c1].astype(F32) * sv * (gc * _sig(gc))).astype(BF16)
        ezg[HC:HC + tm, :] = z_ref[:, _cols(3)].astype(F32) * _sig(z_ref[:, _cols(4)].astype(F32))
        _shifted_copies(ezg, esh, tm + HC - SUB)
        cz = jnp.zeros((tm, D), F32) + dcb_ref[...]
        for k in range(KD):
            cz = cz + dw_ref[k:k + 1, :] * _tap(ezg, esh, HC - (KD - 1) + k, tm)
        cz_ref[...] = cz
        zh, _ = _ln_rows(cz)
        zn = zh * dng_ref[...] + dnb_ref[...]
        gd = z_ref[:, _cols(5)].astype(F32)
        yc_ref[:, D:2 * D] = ((zn * _sig(zn)) * (gd * _sig(gd))).astype(BF16)
        ezg[0:HC, :] = ezg[tm:tm + HC, :]

    vec = pl.BlockSpec((1, D), lambda i: (0, 0))
    return pl.pallas_call(
        body, name=f"mix_odd_fwd_{layer}", grid=(s // tm,),
        in_specs=[pl.BlockSpec((tm, NIN), lambda i: (i, 0)), vec, vec,
                  pl.BlockSpec((HEADS, SGU_BLOCK, SGU_BLOCK), lambda i: (0, 0, 0)),
                  pl.BlockSpec((HEADS, SGU_BLOCK, 1), lambda i: (0, 0, 0)),
                  pl.BlockSpec((KD, D), lambda i: (0, 0)), vec, vec, vec],
        out_specs=[pl.BlockSpec((tm, DMIX), lambda i: (i, 0)), pl.BlockSpec((tm, D), lambda i: (i, 0))],
        out_shape=[jax.ShapeDtypeStruct((s, DMIX), BF16), jax.ShapeDtypeStruct((s, D), F32)],
        scratch_shapes=[pltpu.VMEM((HC + tm, D), F32), pltpu.VMEM((SUB - 1, tm + HC - SUB, D), F32)],
        compiler_params=_cp(("arbitrary",)),
    )(z, slg, slb, wsb, sbcol, dw, dcb, dng, dnb)


def mix_odd_bwd(z, dycat, cz, slg, slb, wsb, wstb, sbcol, dw, dng, dnb, layer, parts):
    s = z.shape[0]
    tm = min(s, 256)
    nt = s // tm

    def body(z_ref, dy_ref, cz_ref, slg_ref, slb_ref, ws_ref, wst_ref, sb_ref, dw_ref, dng_ref, dnb_ref, p_ref,
             dz_ref, dslg_ref, dslb_ref, dws_ref, dsb_ref, ddw_ref, ddcb_ref, ddng_ref, ddnb_ref, land_ref,
             dvn, edz, esh, *sems):
        i = pl.program_id(0)
        start, finish = _exchange_plan(p_ref, land_ref, sems, 0)
        pl.when(i == 0)(start)

        @pl.when(i == 0)
        def _():
            edz[tm:tm + HC, :] = jnp.zeros((HC, D), F32)
            for ref in (dslg_ref, dslb_ref, dws_ref, dsb_ref, ddw_ref, ddcb_ref, ddng_ref, ddnb_ref):
                ref[...] = jnp.zeros_like(ref)

        vh, vrs = _ln_rows(z_ref[:, _cols(1)].astype(F32))
        vnb = (vh * slg_ref[...] + slb_ref[...]).astype(BF16)
        for n in range(tm // SGU_BLOCK):
            r0, r1 = n * SGU_BLOCK, (n + 1) * SGU_BLOCK
            for hd in range(HEADS):
                c0, c1 = hd * HD, (hd + 1) * HD
                vblk = vnb[r0:r1, c0:c1]
                sv = _dot(ws_ref[hd], vblk) + sb_ref[hd]
                gc = z_ref[r0:r1, 2 * D + c0:2 * D + c1].astype(F32)
                sg = _sig(gc)
                sil = gc * sg
                u = z_ref[r0:r1, c0:c1].astype(F32)
                dyc = dy_ref[r0:r1, c0:c1]
                dz_ref[r0:r1, c0:c1] = (dyc * sv * sil).astype(BF16)
                dz_ref[r0:r1, 2 * D + c0:2 * D + c1] = (dyc * u * sv * (sg * (1.0 + gc * (1.0 - sg)))).astype(BF16)
                dsv = dyc * u * sil
                dsb_ref[hd] += jnp.sum(dsv, axis=-1, keepdims=True)
                dsvb = dsv.astype(BF16)
                dws_ref[hd] += _dot_nt(dsvb, vblk)
                dvn[r0:r1, c0:c1] = _dot(wst_ref[hd], dsvb)
        dv = dvn[...]
        dslg_ref[...] += _rowsum(dv * vh)
        dslb_ref[...] += _rowsum(dv)
        dz_ref[:, _cols(1)] = _ln_rows_bwd(dv, vh, vrs, slg_ref[...]).astype(BF16)

        zh, zrs = _ln_rows(cz_ref[...])
        zn = zh * dng_ref[...] + dnb_ref[...]
        sgn = _sig(zn)
        gd = z_ref[:, _cols(5)].astype(F32)
        sgd = _sig(gd)
        dyd = dy_ref[:, D:2 * D]
        dz_ref[:, _cols(5)] = (dyd * (zn * sgn) * (sgd * (1.0 + gd * (1.0 - sgd)))).astype(BF16)
        dzn = dyd * (gd * sgd) * (sgn * (1.0 + zn * (1.0 - sgn)))
        ddng_ref[...] += _rowsum(dzn * zh)
        ddnb_ref[...] += _rowsum(dzn)
        dcz = _ln_rows_bwd(dzn, zh, zrs, dng_ref[...])
        ddcb_ref[...] += _rowsum(dcz)
        edz[0:tm, :] = dcz
        _shifted_copies(edz, esh, tm + HC - SUB)
        a = z_ref[:, _cols(3)].astype(F32)
        sgb = _sig(z_ref[:, _cols(4)].astype(F32))
        zg = a * sgb
        dzg = jnp.zeros((tm, D), F32)
        for k in range(KD):
            sh = _tap(edz, esh, KD - 1 - k, tm)
            dzg = dzg + dw_ref[k:k + 1, :] * sh
            ddw_ref[k:k + 1, :] += _rowsum(zg * sh)
        dz_ref[:, _cols(3)] = (dzg * sgb).astype(BF16)
        dz_ref[:, _cols(4)] = (dzg * a * sgb * (1.0 - sgb)).astype(BF16)
        edz[tm:tm + HC, :] = edz[0:HC, :]
        pl.when(i == nt - 1)(finish)

    rev = lambda i: (nt - 1 - i, 0)
    vec = pl.BlockSpec((1, D), lambda i: (0, 0))
    wspec = pl.BlockSpec((HEADS, SGU_BLOCK, SGU_BLOCK), lambda i: (0, 0, 0))
    bspec = pl.BlockSpec((HEADS, SGU_BLOCK, 1), lambda i: (0, 0, 0))
    kspec = pl.BlockSpec((KD, D), lambda i: (0, 0))
    return pl.pallas_call(
        body, name=f"mix_odd_bwd_{layer}", grid=(nt,),
        in_specs=[pl.BlockSpec((tm, NIN), rev), pl.BlockSpec((tm, DMIX), rev), pl.BlockSpec((tm, D), rev),
                  vec, vec, wspec, wspec, bspec, kspec, vec, vec, ANY],
        out_specs=[pl.BlockSpec((tm, NIN), rev), vec, vec, wspec, bspec, kspec, vec, vec, vec, ANY],
        out_shape=[jax.ShapeDtypeStruct((s, NIN), BF16), jax.ShapeDtypeStruct((1, D), F32),
                   jax.ShapeDtypeStruct((1, D), F32),
                   jax.ShapeDtypeStruct((HEADS, SGU_BLOCK, SGU_BLOCK), F32),
                   jax.ShapeDtypeStruct((HEADS, SGU_BLOCK, 1), F32), jax.ShapeDtypeStruct((KD, D), F32),
                   jax.ShapeDtypeStruct((1, D), F32), jax.ShapeDtypeStruct((1, D), F32),
                   jax.ShapeDtypeStruct((1, D), F32), jax.ShapeDtypeStruct(parts.shape, parts.dtype)],
        scratch_shapes=[pltpu.VMEM((tm, D), F32), pltpu.VMEM((tm + HC, D), F32),
                        pltpu.VMEM((SUB - 1, tm + HC - SUB, D), F32)] + _comm_scratch(1),
        compiler_params=_cp(("arbitrary",)),
    )(z, dycat, cz, slg, slb, wsb, wstb, sbcol, dw, dng, dnb, parts)


def _adamw_math(w, g, m, v):
    m = ADAM_B1 * m + (1.0 - ADAM_B1) * g
    v = ADAM_B2 * v + (1.0 - ADAM_B2) * (g * g)
    m_hat = m / (1.0 - ADAM_B1 ** ADAM_STEP)
    v_hat = v / (1.0 - ADAM_B2 ** ADAM_STEP)
    delta = -ADAM_LR * (m_hat / (jnp.sqrt(v_hat) + ADAM_EPS) + ADAM_WD * w)
    return delta, m, v


def adamw_big(w, m, v, parts0, parts1, name):
    _, r, c = w.shape
    tr = min(r, 256)
    nr = r // tr

    def body(w_ref, m_ref, v_ref, p0_ref, p1_ref, g_ref, d_ref, nm_ref, nv_ref):
        i = pl.program_id(0)

        def total(p_ref):
            acc = p_ref[0].astype(F32)
            for j in range(1, NDEV):
                acc = acc + p_ref[j].astype(F32)
            return acc

        @pl.when(i == 0)
        def _():
            g_ref[...] = total(p0_ref)

        @pl.when(i == 1)
        def _():
            g_ref[...] = total(p1_ref)

        delta, nm, nv = _adamw_math(w_ref[...], g_ref[...], m_ref[...], v_ref[...])
        d_ref[...] = delta
        nm_ref[...] = nm
        nv_ref[...] = nv

    wspec = pl.BlockSpec((None, tr, c), lambda i, j: (i, j, 0))
    p0 = pl.BlockSpec((NDEV, tr, c), lambda i, j: (0, jnp.where(i == 0, j, nr - 1), 0))
    p1 = pl.BlockSpec((NDEV, tr, c), lambda i, j: (0, jnp.where(i == 1, j, 0), 0))
    shp = jax.ShapeDtypeStruct(w.shape, F32)
    return pl.pallas_call(
        body, name=name, grid=(2, nr), in_specs=[wspec, wspec, wspec, p0, p1],
        out_specs=[wspec] * 4, out_shape=[shp] * 4,
        compiler_params=_cp(("arbitrary", "arbitrary")),
    )(w, m, v, parts0, parts1)


def sum_parts(parts, name):
    _, r, c = parts.shape

    def body(p_ref, o_ref):
        acc = p_ref[0]
        for j in range(1, NDEV):
            acc = acc + p_ref[j]
        o_ref[...] = acc

    return pl.pallas_call(body, name=name, out_shape=jax.ShapeDtypeStruct((r, c), F32),
                          compiler_params=_cp())(parts)


def adamw_small(w, g, m, v, name):
    def body(w_ref, g_ref, m_ref, v_ref, d_ref, nm_ref, nv_ref):
        delta, nm, nv = _adamw_math(w_ref[...], g_ref[...], m_ref[...], v_ref[...])
        d_ref[...] = delta
        nm_ref[...] = nm
        nv_ref[...] = nv

    shp = jax.ShapeDtypeStruct(w.shape, F32)
    return pl.pallas_call(body, name=name, out_shape=[shp] * 3, compiler_params=_cp())(w, g, m, v)


def _size(shape):
    n = 1
    for d in shape:
        n *= d
    return n


def _pack(arrays, rows):
    flat = jnp.concatenate([a.reshape(-1) for a in arrays])
    return jnp.pad(flat, (0, rows * 128 - flat.shape[0])).reshape(rows, 128)


def _unpack(flat, shapes):
    out, o = [], 0
    for shp in shapes:
        out.append(flat[o:o + _size(shp)].reshape(shp))
        o += _size(shp)
    return out


def _rows_for(shapes):
    return -(-sum(_size(shp) for shp in shapes) // 1024) * 8


SHARDED_SMALL = (("pool_w", (2, 4, 256, 256), 2), ("sconv_w", (2, KS, D), 2), ("sgu_ln_g", (2, D), 1),
                 ("sgu_ln_b", (2, D), 1), ("dconv_w", (2, KD, D), 2), ("dconv_b", (2, D), 1),
                 ("dnorm_g", (2, D), 1), ("dnorm_b", (2, D), 1))
REPLICATED_SMALL = (("ln_g", (DEPTH, D)), ("ln_b", (DEPTH, D)), ("pool_scale", (2, D)), ("sconv_b", (2, D)),
                    ("sgu_w", (2, HEADS, SGU_BLOCK, SGU_BLOCK)), ("sgu_b", (2, HEADS, SGU_BLOCK)))


def _shard_shape(shape, axis):
    return tuple(d // NDEV if a == axis else d for a, d in enumerate(shape))


def _merge_gathered(g, shape, axis):
    return jnp.moveaxis(g, 0, axis).reshape(shape)


def kernel(x, ln_g, ln_b, w_in_even, w_out_even, pool_w, pool_scale, sconv_w, sconv_b, w_in_odd, w_out_odd, sgu_ln_g, sgu_ln_b, sgu_w, sgu_b, dconv_w, dconv_b, dnorm_g, dnorm_b, loss_target, m_ln_g, m_ln_b, m_w_in_even, m_w_out_even, m_pool_w, m_pool_scale, m_sconv_w, m_sconv_b, m_w_in_odd, m_w_out_odd, m_sgu_ln_g, m_sgu_ln_b, m_sgu_w, m_sgu_b, m_dconv_w, m_dconv_b, m_dnorm_g, m_dnorm_b, v_ln_g, v_ln_b, v_w_in_even, v_w_out_even, v_pool_w, v_pool_scale, v_sconv_w, v_sconv_b, v_w_in_odd, v_w_out_odd, v_sgu_ln_g, v_sgu_ln_b, v_sgu_w, v_sgu_b, v_dconv_w, v_dconv_b, v_dnorm_g, v_dnorm_b):
    given = dict(locals())
    me = 4 * lax.axis_index("x") + 2 * lax.axis_index("y") + lax.axis_index("c")
    xs = x[0]
    target = loss_target[0]

    wc_in = cast_weights(w_in_even, w_in_odd, "cast_w_in")
    wc_out = cast_weights(w_out_even, w_out_odd, "cast_w_out")
    wg_in = [all_gather(wc_in, "gather_w_in_0", index=0)] + [None] * (DEPTH - 1)
    wg_out = [None] * DEPTH

    shard_shapes = [_shard_shape(shp, ax) for _, shp, ax in SHARDED_SMALL]
    srows = _rows_for(shard_shapes)
    gathered = all_gather(_pack([given[n] for n, _, _ in SHARDED_SMALL], srows), "gather_small")
    gathered = gathered.reshape(NDEV, -1)
    full, o = {}, 0
    for (n, shp, ax), sshp in zip(SHARDED_SMALL, shard_shapes):
        full[n] = _merge_gathered(gathered[:, o:o + _size(sshp)].reshape((NDEV,) + sshp), shp, ax)
        o += _size(sshp)

    mask = (jnp.arange(SGU_BLOCK)[None, :] // 64) <= (jnp.arange(SGU_BLOCK)[:, None] // 64)
    ws = jnp.where(mask[None, None], sgu_w, 0.0)
    wsb = ws.astype(BF16)
    wstb = jnp.swapaxes(ws, -1, -2).astype(BF16)
    pwb = full["pool_w"].astype(BF16)

    ones = jnp.ones((1, D), F32)
    zeros = jnp.zeros((1, D), F32)

    xres, gp, bp = xs, ones, zeros
    xb = cast_x(xs)
    saved = []
    for layer in range(DEPTH):
        i = layer // 2
        outs = proj_in(xb, wg_in[layer], layer, wc_in, wc_out)
        z, wg_out[layer] = outs[0], outs[1]
        if layer + 1 < DEPTH:
            wg_in[layer + 1] = outs[2]
        if layer % 2 == 0:
            ycat, pooled, cv = mix_even_fwd(z, pwb[i], pool_scale[i][None], full["sconv_w"][i],
                                            sconv_b[i][None], layer)
            extra = (pooled, cv)
        else:
            ycat, cz = mix_odd_fwd(z, full["sgu_ln_g"][i][None], full["sgu_ln_b"][i][None], wsb[i],
                                   sgu_b[i][:, :, None], full["dconv_w"][i], full["dconv_b"][i][None],
                                   full["dnorm_g"][i][None], full["dnorm_b"][i][None], layer)
            extra = (cz,)
        g, b = ln_g[layer][None], ln_b[layer][None]
        outs = proj_out_ln(ycat, wg_out[layer], layer, xres, gp, bp, g, b,
                           target=target if layer == DEPTH - 1 else None)
        xhat, rstd, xb_next = outs[0], outs[1], outs[2]
        saved.append((xb, z, ycat, extra, xhat, rstd))
        xres, gp, bp, xb = xhat, g, b, xb_next
    dxn, loss_local = outs[3], outs[4]

    gsmall = {n: [None] * shp[0] for n, shp in REPLICATED_SMALL}
    gsmall.update({n: [None] * shp[0] for n, shp, _ in SHARDED_SMALL})
    parts_in, parts_out, parts_pw = [None] * DEPTH, [None] * DEPTH, [None] * 2
    pending = None
    for layer in reversed(range(DEPTH)):
        i = layer // 2
        xb, z, ycat, extra, xhat, rstd = saved[layer]
        dr, drb, dycat, dg, db = ln_bwd_dycat(dxn, xhat, rstd, ln_g[layer][None], wg_out[layer], layer)
        gsmall["ln_g"][layer], gsmall["ln_b"][layer] = dg[0], db[0]
        dwo = dw_out(ycat, drb, layer)
        own_out = pending is None
        riding = dwo if own_out else pending
        if layer % 2 == 0:
            dzb, dpw, dps, dsw, dsb, landed = mix_even_bwd(
                z, dycat, extra[0], extra[1], pwb[i], pool_scale[i][None], full["sconv_w"][i], layer, parts=riding)
            gsmall["pool_scale"][i] = dps[0]
            gsmall["sconv_w"][i], gsmall["sconv_b"][i] = dsw, dsb[0]
        else:
            dzb, dslg, dslb, dws, dsbc, ddw, ddcb, ddng, ddnb, landed = mix_odd_bwd(
                z, dycat, extra[0], full["sgu_ln_g"][i][None], full["sgu_ln_b"][i][None], wsb[i], wstb[i],
                sgu_b[i][:, :, None], full["dconv_w"][i], full["dnorm_g"][i][None], full["dnorm_b"][i][None],
                layer, riding)
            gsmall["sgu_ln_g"][i], gsmall["sgu_ln_b"][i] = dslg[0], dslb[0]
            gsmall["sgu_w"][i], gsmall["sgu_b"][i] = jnp.where(mask[None], dws, 0.0), dsbc[:, :, 0]
            gsmall["dconv_w"][i], gsmall["dconv_b"][i] = ddw, ddcb[0]
            gsmall["dnorm_g"][i], gsmall["dnorm_b"][i] = ddng[0], ddnb[0]
        riders = [] if own_out else [dwo]
        if layer % 2 == 0:
            riders.append(dpw.reshape(4, NDEV, PG // NDEV, PG).transpose(1, 0, 2, 3))
        outs = list(dw_in(xb, dzb, layer, riders))
        if layer % 2 == 0:
            parts_pw[i] = outs.pop()
        if own_out:
            parts_out[layer] = landed
        else:
            parts_in[layer + 1] = landed
            parts_out[layer] = outs[1]
        pending = outs[0]
        if layer == 0:
            dxn, parts_in[0] = dx_in(dzb, wg_in[layer], layer, dr, parts=pending)
        else:
            dxn = dx_in(dzb, wg_in[layer], layer, dr)[0]
    grad_x = dxn[None]

    big = {}
    big["w_in_even"] = adamw_big(w_in_even, m_w_in_even, v_w_in_even, parts_in[0], parts_in[2], "adamw_w_in_even")
    big["w_in_odd"] = adamw_big(w_in_odd, m_w_in_odd, v_w_in_odd, parts_in[1], parts_in[3], "adamw_w_in_odd")
    big["w_out_even"] = adamw_big(w_out_even, m_w_out_even, v_w_out_even, parts_out[0], parts_out[2], "adamw_w_out_even")
    big["w_out_odd"] = adamw_big(w_out_odd, m_w_out_odd, v_w_out_odd, parts_out[1], parts_out[3], "adamw_w_out_odd")

    small_names = [n for n, _ in REPLICATED_SMALL] + [n for n, _, _ in SHARDED_SMALL]
    gathered_names = [n for n in small_names if n != "pool_w"]
    gathered_shapes = [shp for n, shp in REPLICATED_SMALL] + [shp for n, shp, _ in SHARDED_SMALL if n != "pool_w"]
    grows = _rows_for(gathered_shapes + [(1,)])
    partial = _pack([jnp.stack(gsmall[n]) for n in gathered_names] + [loss_local], grows)
    gsum = sum_parts(all_gather(partial, "gather_small_grads"), "sum_small_grads")
    unpacked = _unpack(gsum.reshape(-1), gathered_shapes + [()])
    loss = unpacked.pop()
    gfull = dict(zip(gathered_names, unpacked))
    own_shapes = [shp for _, shp in REPLICATED_SMALL] + shard_shapes
    gown = {n: gfull[n] for n, _ in REPLICATED_SMALL}
    for (n, shp, ax), sshp in zip(SHARDED_SMALL, shard_shapes):
        if n == "pool_w":
            gown[n] = jnp.stack([sum_parts(p.reshape(NDEV, -1, 128), f"sum_pool_w_{j}").reshape(sshp[1:])
                                 for j, p in enumerate(parts_pw)])
        else:
            gown[n] = lax.dynamic_slice_in_dim(gfull[n], me * sshp[ax], sshp[ax], axis=ax)
    orows = _rows_for(own_shapes)
    packed = [_pack([src[n] for n in small_names], orows) for src in
              (given, gown, {n: given["m_" + n] for n in small_names}, {n: given["v_" + n] for n in small_names})]
    sd, sm, sv = adamw_small(*packed, "adamw_small")
    small = {}
    for n, d_, m_, v_ in zip(small_names, _unpack(sd.reshape(-1), own_shapes), _unpack(sm.reshape(-1), own_shapes),
                             _unpack(sv.reshape(-1), own_shapes)):
        small[n] = (gown[n], d_, m_, v_)

    weights = ['ln_g', 'ln_b', 'w_in_even', 'w_out_even', 'pool_w', 'pool_scale', 'sconv_w', 'sconv_b', 'w_in_odd',
               'w_out_odd', 'sgu_ln_g', 'sgu_ln_b', 'sgu_w', 'sgu_b', 'dconv_w', 'dconv_b', 'dnorm_g', 'dnorm_b']
    res = {n: (big[n] if n in big else small[n]) for n in weights}
    return (loss, grad_x, *[res[n][0] for n in weights], *[res[n][1] for n in weights],
            *[res[n][2] for n in weights], *[res[n][3] for n in weights])
```

```python
import jax
import jax.numpy as jnp
from jax import lax
from jax.experimental import pallas as pl
from jax.experimental.pallas import tpu as pltpu

F32 = jnp.float32
BF16 = jnp.bfloat16

D = 1024
DMIX = 2048
NIN = 6144
NDEV = 8
CW = NIN // NDEV
RW = DMIX // NDEV
DEPTH = 4
ALPHA = (2 * DEPTH) ** 0.25
LN_EPS = 1e-5
POOL_WINDOWS = (2, 4, 8, 16)
PG = 256
SGU_BLOCK = 128
HEADS = 4
HD = 256
KD = 31
KS = 3
SUB = 8
HP = 16
HS = 8
HC = 32
NSEM = 7

ADAM_LR = 0.001
ADAM_B1 = 0.9
ADAM_B2 = 0.999
ADAM_EPS = 1e-08
ADAM_WD = 0.01
ADAM_STEP = 10

VMEM_LIMIT = 56 * 1024 * 1024
MESH = pl.DeviceIdType.MESH
ANY = pl.BlockSpec(memory_space=pl.ANY)


def _cp(sem=None):
    if sem is None:
        return pltpu.CompilerParams(vmem_limit_bytes=VMEM_LIMIT)
    return pltpu.CompilerParams(dimension_semantics=sem, vmem_limit_bytes=VMEM_LIMIT)


def _sig(x):
    return 0.5 * jnp.tanh(0.5 * x) + 0.5


def _dot(a, b):
    return jnp.dot(a, b, preferred_element_type=F32)


def _dot_nt(a, b):
    return lax.dot_general(a, b, (((1,), (1,)), ((), ())), preferred_element_type=F32)


def _dot_tn(a, b):
    return lax.dot_general(a, b, (((0,), (0,)), ((), ())), preferred_element_type=F32)


def _rowsum(x):
    return jnp.sum(x, axis=0, keepdims=True)


def _comm_scratch(n):
    return [pltpu.SemaphoreType.DMA((n * NSEM,)), pltpu.SemaphoreType.DMA((n * NSEM,)),
            pltpu.SemaphoreType.DMA((n,))]


def _gather_plan(src_ref, out_ref, sems, n):
    send_sems, recv_sems, local_sems = sems
    base = n * NSEM
    x, y, c = lax.axis_index("x"), lax.axis_index("y"), lax.axis_index("c")
    me, sibling = (x, y, c), (x, y, 1 - c)
    chips = [(1 - x, y), (x, 1 - y), (1 - x, 1 - y)]

    def slot(px, py, pc):
        return out_ref.at[4 * px + 2 * py + pc]

    def copy(k, blk, to, src=None):
        return pltpu.make_async_remote_copy(
            src_ref=slot(*blk) if src is None else src, dst_ref=slot(*blk),
            send_sem=send_sems.at[base + k], recv_sem=recv_sems.at[base + k],
            device_id=to, device_id_type=MESH)

    def mine():
        return pltpu.make_async_copy(src_ref, slot(*me), local_sems.at[n])

    def start():
        mine().start()
        copy(0, me, sibling, src=src_ref).start()
        for j, chip in enumerate(chips):
            copy(1 + j, me, (*chip, c), src=src_ref).start()

    def finish():
        for j, chip in enumerate(chips):
            copy(1 + j, (*chip, c), me).wait_recv()
            copy(4 + j, (*chip, c), sibling).start()
        copy(0, sibling, me).wait_recv()
        for j, chip in enumerate(chips):
            copy(4 + j, (*chip, 1 - c), me).wait_recv()
        copy(0, me, sibling, src=src_ref).wait_send()
        for j, chip in enumerate(chips):
            copy(1 + j, me, (*chip, c), src=src_ref).wait_send()
            copy(4 + j, (*chip, c), sibling).wait_send()
        mine().wait()

    return start, finish


def _exchange_plan(p_ref, out_ref, sems, n):
    send_sems, recv_sems, local_sems = sems
    base = n * NSEM
    x, y, c = lax.axis_index("x"), lax.axis_index("y"), lax.axis_index("c")
    me = 4 * x + 2 * y + c

    def copy(r, landing):
        px, py, pc = x ^ (r >> 2), y ^ ((r >> 1) & 1), c ^ (r & 1)
        peer = 4 * px + 2 * py + pc
        return pltpu.make_async_remote_copy(
            src_ref=p_ref.at[peer], dst_ref=out_ref.at[peer if landing else me],
            send_sem=send_sems.at[base + r - 1], recv_sem=recv_sems.at[base + r - 1],
            device_id=(px, py, pc), device_id_type=MESH)

    def mine():
        return pltpu.make_async_copy(p_ref.at[me], out_ref.at[me], local_sems.at[n])

    def start():
        mine().start()
        for r in range(1, NDEV):
            copy(r, False).start()

    def finish():
        for r in range(1, NDEV):
            copy(r, True).wait_recv()
        for r in range(1, NDEV):
            copy(r, False).wait_send()
        mine().wait()

    return start, finish


def all_gather(src, name, index=None):
    shape = src.shape if index is None else src.shape[1:]

    def body(x_ref, out_ref, *sems):
        start, finish = _gather_plan(x_ref if index is None else x_ref.at[index], out_ref, sems, 0)
        start()
        finish()

    return pl.pallas_call(
        body, name=name, out_shape=jax.ShapeDtypeStruct((NDEV,) + shape, src.dtype),
        in_specs=[ANY], out_specs=ANY, scratch_shapes=_comm_scratch(1),
    )(src)


def cast_x(x, w_first, small):
    s = x.shape[0]
    tm = min(s, 512)
    nt = s // tm

    def body(x_ref, wf_ref, sm_ref, o_ref, gw_ref, gs_ref, *sems):
        i = pl.program_id(0)
        plans = [_gather_plan(wf_ref.at[0], gw_ref, sems, 0), _gather_plan(sm_ref, gs_ref, sems, 1)]

        @pl.when(i == 0)
        def _():
            for start, _ in plans:
                start()

        o_ref[...] = x_ref[...].astype(BF16)

        @pl.when(i == nt - 1)
        def _():
            for _, finish in plans:
                finish()

    return pl.pallas_call(
        body, name="cast_x", grid=(nt,),
        in_specs=[pl.BlockSpec((tm, D), lambda i: (i, 0)), ANY, ANY],
        out_specs=[pl.BlockSpec((tm, D), lambda i: (i, 0)), ANY, ANY],
        out_shape=[jax.ShapeDtypeStruct((s, D), BF16),
                   jax.ShapeDtypeStruct((NDEV,) + w_first.shape[1:], w_first.dtype),
                   jax.ShapeDtypeStruct((NDEV,) + small.shape, small.dtype)],
        scratch_shapes=_comm_scratch(2), compiler_params=_cp(("arbitrary",)),
    )(x, w_first, small)


def cast_weights(w_even, w_odd, name):
    _, r, c = w_even.shape

    def body(e_ref, o_ref, out_ref):
        layer = pl.program_id(0)

        @pl.when(layer % 2 == 0)
        def _():
            out_ref[...] = e_ref[...].astype(BF16)

        @pl.when(layer % 2 == 1)
        def _():
            out_ref[...] = o_ref[...].astype(BF16)

    spec = pl.BlockSpec((None, r, c), lambda l: (l // 2, 0, 0))
    return pl.pallas_call(
        body, name=name, grid=(DEPTH,), in_specs=[spec, spec],
        out_specs=pl.BlockSpec((None, r, c), lambda l: (l, 0, 0)),
        out_shape=jax.ShapeDtypeStruct((DEPTH, r, c), BF16), compiler_params=_cp(("parallel",)),
    )(w_even, w_odd)


def proj_in(xb, wg, layer, wc_in, wc_out):
    s = xb.shape[0]
    tm = min(s, 512)
    nt = s // tm
    nxt = layer + 1 < DEPTH

    def body(*refs):
        if nxt:
            x_ref, w_ref, wco_ref, wci_ref, o_ref, go_ref, gi_ref = refs[:7]
        else:
            x_ref, w_ref, wco_ref, o_ref, go_ref = refs[:5]
        sems = refs[-3:]
        i = pl.program_id(0)
        plans = [_gather_plan(wco_ref.at[layer], go_ref, sems, 0)]
        if nxt:
            plans.append(_gather_plan(wci_ref.at[layer + 1], gi_ref, sems, 1))

        @pl.when(i == 0)
        def _():
            for start, _ in plans:
                start()

        x = x_ref[...]
        for k in range(NDEV):
            o_ref[:, k * CW:(k + 1) * CW] = _dot(x, w_ref[k]).astype(BF16)

        @pl.when(i == nt - 1)
        def _():
            for _, finish in plans:
                finish()

    in_specs = [pl.BlockSpec((tm, D), lambda i: (i, 0)),
                pl.BlockSpec((NDEV, D, CW), lambda i: (0, 0, 0), pipeline_mode=pl.Buffered(1)), ANY]
    out_specs = [pl.BlockSpec((tm, NIN), lambda i: (i, 0)), ANY]
    out_shape = [jax.ShapeDtypeStruct((s, NIN), BF16), jax.ShapeDtypeStruct((NDEV, RW, D), BF16)]
    args = [xb, wg, wc_out]
    if nxt:
        in_specs.append(ANY)
        args.append(wc_in)
        out_specs.append(ANY)
        out_shape.append(jax.ShapeDtypeStruct((NDEV, D, CW), BF16))
    return pl.pallas_call(
        body, name=f"proj_in_{layer}", grid=(nt,), in_specs=in_specs, out_specs=out_specs,
        out_shape=out_shape, scratch_shapes=_comm_scratch(2 if nxt else 1),
        compiler_params=_cp(("arbitrary",)),
    )(*args)


def proj_out_ln(ycat, wog, layer, xres, g_prev, b_prev, g, b, target=None):
    s = ycat.shape[0]
    tm = min(s, 256)
    nt = s // tm
    final = target is not None

    def body(*refs):
        if final:
            y_ref, w_ref, xr_ref, gp_ref, bp_ref, g_ref, b_ref, t_ref, xh_ref, rs_ref, xb_ref, dy_ref, loss_ref = refs
        else:
            y_ref, w_ref, xr_ref, gp_ref, bp_ref, g_ref, b_ref, xh_ref, rs_ref, xb_ref = refs
        xin = xr_ref[...] * gp_ref[...] + bp_ref[...]
        y = _dot(y_ref[...], w_ref[...].reshape(DMIX, D))
        r = ALPHA * xin + y
        mu = jnp.mean(r, axis=-1, keepdims=True)
        d = r - mu
        var = jnp.mean(d * d, axis=-1, keepdims=True)
        rstd = lax.rsqrt(var + LN_EPS)
        xh = d * rstd
        xh_ref[...] = xh
        rs_ref[...] = rstd
        xn = xh * g_ref[...] + b_ref[...]
        xb_ref[...] = xn.astype(BF16)
        if final:
            err = xn - t_ref[...]
            dy_ref[...] = err * (1.0 / D)

            @pl.when(pl.program_id(0) == 0)
            def _():
                loss_ref[...] = jnp.zeros_like(loss_ref)

            loss_ref[...] += 0.5 * jnp.sum(jnp.mean(err * err, axis=-1, keepdims=True), axis=0, keepdims=True)

    row = pl.BlockSpec((tm, D), lambda i: (i, 0))
    vec = pl.BlockSpec((1, D), lambda i: (0, 0))
    in_specs = [pl.BlockSpec((tm, DMIX), lambda i: (i, 0)),
                pl.BlockSpec((NDEV, RW, D), lambda i: (0, 0, 0)),
                row, vec, vec, vec, vec]
    out_specs = [row, pl.BlockSpec((tm, 1), lambda i: (i, 0)), row]
    out_shape = [jax.ShapeDtypeStruct((s, D), F32), jax.ShapeDtypeStruct((s, 1), F32),
                 jax.ShapeDtypeStruct((s, D), BF16)]
    args = [ycat, wog, xres, g_prev, b_prev, g, b]
    if final:
        in_specs.append(row)
        args.append(target)
        out_specs += [row, pl.BlockSpec((1, 1), lambda i: (0, 0))]
        out_shape += [jax.ShapeDtypeStruct((s, D), F32), jax.ShapeDtypeStruct((1, 1), F32)]
    return pl.pallas_call(
        body, name=f"proj_out_ln_{layer}", grid=(nt,), in_specs=in_specs, out_specs=out_specs,
        out_shape=out_shape, compiler_params=_cp(("arbitrary",)),
    )(*args)


def ln_bwd_dycat(dxn, xhat, rstd, g, wog, layer):
    s = dxn.shape[0]
    tm = min(s, 256)

    def body(dx_ref, xh_ref, rs_ref, g_ref, w_ref, dr_ref, drb_ref, dyc_ref, dg_ref, db_ref):
        @pl.when(pl.program_id(0) == 0)
        def _():
            dg_ref[...] = jnp.zeros_like(dg_ref)
            db_ref[...] = jnp.zeros_like(db_ref)

        dxo = dx_ref[...]
        xh = xh_ref[...]
        dg_ref[...] += _rowsum(dxo * xh)
        db_ref[...] += _rowsum(dxo)
        dxh = dxo * g_ref[...]
        m1 = jnp.mean(dxh, axis=-1, keepdims=True)
        m2 = jnp.mean(dxh * xh, axis=-1, keepdims=True)
        dr = rs_ref[...] * (dxh - m1 - xh * m2)
        dr_ref[...] = dr
        drb = dr.astype(BF16)
        drb_ref[...] = drb
        dyc_ref[...] = _dot_nt(drb, w_ref[...].reshape(DMIX, D))

    row = pl.BlockSpec((tm, D), lambda i: (i, 0))
    vec = pl.BlockSpec((1, D), lambda i: (0, 0))
    return pl.pallas_call(
        body, name=f"ln_bwd_dycat_{layer}", grid=(s // tm,),
        in_specs=[row, row, pl.BlockSpec((tm, 1), lambda i: (i, 0)), vec,
                  pl.BlockSpec((NDEV, RW, D), lambda i: (0, 0, 0))],
        out_specs=[row, row, pl.BlockSpec((tm, DMIX), lambda i: (i, 0)), vec, vec],
        out_shape=[jax.ShapeDtypeStruct((s, D), F32), jax.ShapeDtypeStruct((s, D), BF16),
                   jax.ShapeDtypeStruct((s, DMIX), F32), jax.ShapeDtypeStruct((1, D), F32),
                   jax.ShapeDtypeStruct((1, D), F32)],
        compiler_params=_cp(("arbitrary",)),
    )(dxn, xhat, rstd, g, wog)


def dx_in(dzb, wg, layer, dr, parts=None):
    s = dzb.shape[0]
    tm = min(s, 256)
    nt = s // tm
    carry = parts is not None

    def body(*refs):
        if carry:
            dz_ref, w_ref, dr_ref, p_ref, o_ref, land_ref = refs[:6]
            start, finish = _exchange_plan(p_ref, land_ref, refs[-3:], 0)
            pl.when(pl.program_id(0) == 0)(start)
        else:
            dz_ref, w_ref, dr_ref, o_ref = refs
        acc = ALPHA * dr_ref[...]
        for k in range(NDEV):
            acc += _dot_nt(dz_ref[:, k * CW:(k + 1) * CW], w_ref[k])
        o_ref[...] = acc
        if carry:
            pl.when(pl.program_id(0) == nt - 1)(finish)

    row = pl.BlockSpec((tm, D), lambda i: (i, 0))
    in_specs = [pl.BlockSpec((tm, NIN), lambda i: (i, 0)), pl.BlockSpec((NDEV, D, CW), lambda i: (0, 0, 0)), row]
    out_specs, out_shape, args = [row], [jax.ShapeDtypeStruct((s, D), F32)], [dzb, wg, dr]
    if carry:
        in_specs.append(ANY)
        args.append(parts)
        out_specs.append(ANY)
        out_shape.append(jax.ShapeDtypeStruct(parts.shape, parts.dtype))
    return pl.pallas_call(
        body, name=f"dx_in_{layer}", grid=(nt,), in_specs=in_specs, out_specs=out_specs, out_shape=out_shape,
        scratch_shapes=_comm_scratch(1) if carry else [], compiler_params=_cp(("arbitrary",)),
    )(*args)


def dw_in(xb, dzb, layer, parts=()):
    s = xb.shape[0]
    tm = min(s, 2048)
    nt = s // tm
    n = len(parts)

    def body(*refs):
        x_ref, dz_ref = refs[:2]
        p_refs = refs[2:2 + n]
        o_ref = refs[2 + n]
        land_refs = refs[3 + n:3 + 2 * n]
        acc = refs[3 + 2 * n]
        plans = [_exchange_plan(p, land, refs[-3:], j) for j, (p, land) in enumerate(zip(p_refs, land_refs))]
        k, t = pl.program_id(0), pl.program_id(1)

        @pl.when((k == 0) & (t == 0))
        def _():
            for start, _ in plans:
                start()

        @pl.when(t == 0)
        def _():
            acc[...] = jnp.zeros_like(acc)

        acc[...] += _dot_tn(x_ref[...], dz_ref[...])

        @pl.when(t == nt - 1)
        def _():
            o_ref[...] = acc[...].astype(BF16)

        @pl.when((k == NDEV - 1) & (t == nt - 1))
        def _():
            for _, finish in plans:
                finish()

    return pl.pallas_call(
        body, name=f"dw_in_{layer}", grid=(NDEV, nt),
        in_specs=[pl.BlockSpec((tm, D), lambda k, t: (t, 0)), pl.BlockSpec((tm, CW), lambda k, t: (t, k))] + [ANY] * n,
        out_specs=[pl.BlockSpec((None, D, CW), lambda k, t: (k, 0, 0))] + [ANY] * n,
        out_shape=[jax.ShapeDtypeStruct((NDEV, D, CW), BF16)] + [jax.ShapeDtypeStruct(p.shape, p.dtype) for p in parts],
        scratch_shapes=[pltpu.VMEM((D, CW), F32)] + (_comm_scratch(n) if n else []),
        compiler_params=_cp(("arbitrary", "arbitrary")),
    )(xb, dzb, *parts)


def dw_out(ycat, drb, layer):
    s = ycat.shape[0]
    tm = min(s, 1024)
    nt = s // tm

    def body(y_ref, dr_ref, o_ref, acc):
        t = pl.program_id(0)

        @pl.when(t == 0)
        def _():
            acc[...] = jnp.zeros_like(acc)

        acc[...] += _dot_tn(y_ref[...], dr_ref[...])

        @pl.when(t == nt - 1)
        def _():
            o_ref[...] = acc[...].reshape(NDEV, RW, D).astype(BF16)

    return pl.pallas_call(
        body, name=f"dw_out_{layer}", grid=(nt,),
        in_specs=[pl.BlockSpec((tm, DMIX), lambda t: (t, 0)), pl.BlockSpec((tm, D), lambda t: (t, 0))],
        out_specs=pl.BlockSpec((NDEV, RW, D), lambda t: (0, 0, 0)),
        out_shape=jax.ShapeDtypeStruct((NDEV, RW, D), BF16),
        scratch_shapes=[pltpu.VMEM((DMIX, D), F32)],
        compiler_params=_cp(("arbitrary",)),
    )(ycat, drb)


def _cols(j):
    return slice(j * D, (j + 1) * D)


def mix_even_fwd(z, pwb, pscale, sw, sb, layer):
    s = z.shape[0]
    tm = min(s, 256)

    def body(z_ref, pw_ref, ps_ref, sw_ref, sb_ref, yc_ref, pb_ref, cv_ref, exa, eq):
        i = pl.program_id(0)

        @pl.when(i == 0)
        def _():
            exa[0:HP, :] = jnp.zeros((HP, D), F32)
            eq[0:HS, :] = jnp.zeros((HS, D), F32)

        exa[HP:HP + tm, :] = z_ref[:, _cols(0)].astype(F32)
        q = z_ref[:, _cols(4)].astype(F32) * z_ref[:, _cols(2)].astype(F32)
        eq[HS:HS + tm, :] = q
        pos = (i * tm + lax.broadcasted_iota(jnp.int32, (tm, 1), 0) + 1).astype(F32)
        for gi, w in enumerate(POOL_WINDOWS):
            c0, c1 = gi * PG, (gi + 1) * PG
            acc = exa[HP:HP + tm, c0:c1]
            for j in range(1, w):
                acc = acc + exa[HP - j:HP - j + tm, c0:c1]
            pooled = acc / jnp.minimum(pos, float(w)) - exa[HP:HP + tm, c0:c1]
            pb = pooled.astype(BF16)
            pb_ref[:, c0:c1] = pb
            ga = z_ref[:, D + c0:D + c1].astype(F32)
            ya = _dot(pb, pw_ref[gi]) * ps_ref[:, c0:c1] * (ga * _sig(ga))
            yc_ref[:, c0:c1] = ya.astype(BF16)
        cv = (sw_ref[2:3, :] * q + sw_ref[1:2, :] * eq[HS - 1:HS - 1 + tm, :]
              + sw_ref[0:1, :] * eq[HS - 2:HS - 2 + tm, :] + sb_ref[...])
        cv_ref[...] = cv
        gb = z_ref[:, _cols(5)].astype(F32)
        yc_ref[:, D:2 * D] = (z_ref[:, _cols(3)].astype(F32) * cv * (gb * _sig(gb))).astype(BF16)
        exa[0:HP, :] = exa[tm:tm + HP, :]
        eq[0:HS, :] = eq[tm:tm + HS, :]

    vec = pl.BlockSpec((1, D), lambda i: (0, 0))
    return pl.pallas_call(
        body, name=f"mix_even_fwd_{layer}", grid=(s // tm,),
        in_specs=[pl.BlockSpec((tm, NIN), lambda i: (i, 0)),
                  pl.BlockSpec((4, PG, PG), lambda i: (0, 0, 0)), vec,
                  pl.BlockSpec((KS, D), lambda i: (0, 0)), vec],
        out_specs=[pl.BlockSpec((tm, DMIX), lambda i: (i, 0)), pl.BlockSpec((tm, D), lambda i: (i, 0)),
                   pl.BlockSpec((tm, D), lambda i: (i, 0))],
        out_shape=[jax.ShapeDtypeStruct((s, DMIX), BF16), jax.ShapeDtypeStruct((s, D), BF16),
                   jax.ShapeDtypeStruct((s, D), F32)],
        scratch_shapes=[pltpu.VMEM((HP + tm, D), F32), pltpu.VMEM((HS + tm, D), F32)],
        compiler_params=_cp(("arbitrary",)),
    )(z, pwb, pscale, sw, sb)


def mix_even_bwd(z, dycat, pooled, cv, pwb, pscale, sw, layer, parts=None):
    s = z.shape[0]
    tm = min(s, 256)
    nt = s // tm
    carry = parts is not None

    def body(*refs):
        z_ref, dy_ref, pb_ref, cv_ref, pw_ref, ps_ref, sw_ref = refs[:7]
        if carry:
            p_ref = refs[7]
            dz_ref, dpw_ref, dps_ref, dsw_ref, dsb_ref, land_ref, edp, edc = refs[8:16]
            start, finish = _exchange_plan(p_ref, land_ref, refs[-3:], 0)
        else:
            dz_ref, dpw_ref, dps_ref, dsw_ref, dsb_ref, edp, edc = refs[7:14]
        i = pl.program_id(0)
        if carry:
            pl.when(i == 0)(start)

        @pl.when(i == 0)
        def _():
            edp[tm:tm + HP, :] = jnp.zeros((HP, D), F32)
            edc[tm:tm + HS, :] = jnp.zeros((HS, D), F32)
            dpw_ref[...] = jnp.zeros_like(dpw_ref)
            dps_ref[...] = jnp.zeros_like(dps_ref)
            dsw_ref[...] = jnp.zeros_like(dsw_ref)
            dsb_ref[...] = jnp.zeros_like(dsb_ref)

        pos = ((nt - 1 - i) * tm + lax.broadcasted_iota(jnp.int32, (tm, 1), 0) + 1).astype(F32)
        for gi, w in enumerate(POOL_WINDOWS):
            c0, c1 = gi * PG, (gi + 1) * PG
            ga = z_ref[:, D + c0:D + c1].astype(F32)
            sg = _sig(ga)
            sil = ga * sg
            dya = dy_ref[:, c0:c1]
            pb = pb_ref[:, c0:c1]
            ya0 = _dot(pb, pw_ref[gi])
            ps = ps_ref[:, c0:c1]
            dps_ref[:, c0:c1] += _rowsum(dya * ya0 * sil)
            dz_ref[:, D + c0:D + c1] = (dya * ya0 * ps * (sg * (1.0 + ga * (1.0 - sg)))).astype(BF16)
            t1 = (dya * ps * sil).astype(BF16)
            dpool = _dot_nt(t1, pw_ref[gi])
            dpw_ref[gi] += _dot_tn(pb, t1)
            edp[0:tm, c0:c1] = dpool / jnp.minimum(pos, float(w))
            acc = -dpool
            for j in range(w):
                acc = acc + edp[j:j + tm, c0:c1]
            dz_ref[:, c0:c1] = acc.astype(BF16)
        gb = z_ref[:, _cols(5)].astype(F32)
        sgb = _sig(gb)
        silb = gb * sgb
        dyb = dy_ref[:, D:2 * D]
        cvv = cv_ref[...]
        bg = z_ref[:, _cols(3)].astype(F32)
        dz_ref[:, _cols(3)] = (dyb * cvv * silb).astype(BF16)
        dz_ref[:, _cols(5)] = (dyb * bg * cvv * (sgb * (1.0 + gb * (1.0 - sgb)))).astype(BF16)
        dcv = dyb * bg * silb
        dsb_ref[...] += _rowsum(dcv)
        edc[0:tm, :] = dcv
        h = z_ref[:, _cols(2)].astype(F32)
        cg = z_ref[:, _cols(4)].astype(F32)
        q = cg * h
        d1 = edc[1:1 + tm, :]
        d2 = edc[2:2 + tm, :]
        dq = sw_ref[2:3, :] * dcv + sw_ref[1:2, :] * d1 + sw_ref[0:1, :] * d2
        dsw_ref[2:3, :] += _rowsum(q * dcv)
        dsw_ref[1:2, :] += _rowsum(q * d1)
        dsw_ref[0:1, :] += _rowsum(q * d2)
        dz_ref[:, _cols(4)] = (dq * h).astype(BF16)
        dz_ref[:, _cols(2)] = (dq * cg).astype(BF16)
        edp[tm:tm + HP, :] = edp[0:HP, :]
        edc[tm:tm + HS, :] = edc[0:HS, :]
        if carry:
            pl.when(i == nt - 1)(finish)

    rev = lambda i: (nt - 1 - i, 0)
    vec = pl.BlockSpec((1, D), lambda i: (0, 0))
    in_specs = [pl.BlockSpec((tm, NIN), rev), pl.BlockSpec((tm, DMIX), rev),
                pl.BlockSpec((tm, D), rev), pl.BlockSpec((tm, D), rev),
                pl.BlockSpec((4, PG, PG), lambda i: (0, 0, 0)), vec,
                pl.BlockSpec((KS, D), lambda i: (0, 0))]
    out_specs = [pl.BlockSpec((tm, NIN), rev), pl.BlockSpec((4, PG, PG), lambda i: (0, 0, 0)), vec,
                 pl.BlockSpec((KS, D), lambda i: (0, 0)), vec]
    out_shape = [jax.ShapeDtypeStruct((s, NIN), BF16), jax.ShapeDtypeStruct((4, PG, PG), F32),
                 jax.ShapeDtypeStruct((1, D), F32), jax.ShapeDtypeStruct((KS, D), F32),
                 jax.ShapeDtypeStruct((1, D), F32)]
    args = [z, dycat, pooled, cv, pwb, pscale, sw]
    scratch = [pltpu.VMEM((tm + HP, D), F32), pltpu.VMEM((tm + HS, D), F32)]
    if carry:
        in_specs.append(ANY)
        args.append(parts)
        out_specs.append(ANY)
        out_shape.append(jax.ShapeDtypeStruct(parts.shape, parts.dtype))
        scratch += _comm_scratch(1)
    return pl.pallas_call(
        body, name=f"mix_even_bwd_{layer}", grid=(nt,), in_specs=in_specs, out_specs=out_specs,
        out_shape=out_shape, scratch_shapes=scratch, compiler_params=_cp(("arbitrary",)),
    )(*args)


def _ln_rows(v):
    mu = jnp.mean(v, axis=-1, keepdims=True)
    d = v - mu
    var = jnp.mean(d * d, axis=-1, keepdims=True)
    rstd = lax.rsqrt(var + LN_EPS)
    return d * rstd, rstd


def _ln_rows_bwd(dn, xh, rstd, g):
    dxh = dn * g
    m1 = jnp.mean(dxh, axis=-1, keepdims=True)
    m2 = jnp.mean(dxh * xh, axis=-1, keepdims=True)
    return rstd * (dxh - m1 - xh * m2)


def _shifted_copies(ext, shifted, rows):
    for b in range(1, SUB):
        shifted[b - 1] = ext[b:b + rows, :]


def _tap(ext, shifted, off, tm):
    a, b = off // SUB, off % SUB
    if b == 0:
        return ext[SUB * a:SUB * a + tm, :]
    return shifted[b - 1, SUB * a:SUB * a + tm, :]


def mix_odd_fwd(z, slg, slb, wsb, sbcol, dw, dcb, dng, dnb, layer):
    s = z.shape[0]
    tm = min(s, 256)

    def body(z_ref, slg_ref, slb_ref, ws_ref, sb_ref, dw_ref, dcb_ref, dng_ref, dnb_ref,
             yc_ref, cz_ref, ezg, esh):
        i = pl.program_id(0)

        @pl.when(i == 0)
        def _():
            ezg[0:HC, :] = jnp.zeros((HC, D), F32)

        vh, _ = _ln_rows(z_ref[:, _cols(1)].astype(F32))
        vnb = (vh * slg_ref[...] + slb_ref[...]).astype(BF16)
        for n in range(tm // SGU_BLOCK):
            r0, r1 = n * SGU_BLOCK, (n + 1) * SGU_BLOCK
            for hd in range(HEADS):
                c0, c1 = hd * HD, (hd + 1) * HD
                sv = _dot(ws_ref[hd], vnb[r0:r1, c0:c1]) + sb_ref[hd]
                gc = z_ref[r0:r1, 2 * D + c0:2 * D + c1].astype(F32)
                yc_ref[r0:r1, c0:c1] = (z_ref[r0:r1, c0:c1].astype(F32) * sv * (gc * _sig(gc))).astype(BF16)
        ezg[HC:HC + tm, :] = z_ref[:, _cols(3)].astype(F32) * _sig(z_ref[:, _cols(4)].astype(F32))
        _shifted_copies(ezg, esh, tm + HC - SUB)
        cz = jnp.zeros((tm, D), F32) + dcb_ref[...]
        for k in range(KD):
            cz = cz + dw_ref[k:k + 1, :] * _tap(ezg, esh, HC - (KD - 1) + k, tm)
        cz_ref[...] = cz
        zh, _ = _ln_rows(cz)
        zn = zh * dng_ref[...] + dnb_ref[...]
        gd = z_ref[:, _cols(5)].astype(F32)
        yc_ref[:, D:2 * D] = ((zn * _sig(zn)) * (gd * _sig(gd))).astype(BF16)
        ezg[0:HC, :] = ezg[tm:tm + HC, :]

    vec = pl.BlockSpec((1, D), lambda i: (0, 0))
    return pl.pallas_call(
        body, name=f"mix_odd_fwd_{layer}", grid=(s // tm,),
        in_specs=[pl.BlockSpec((tm, NIN), lambda i: (i, 0)), vec, vec,
                  pl.BlockSpec((HEADS, SGU_BLOCK, SGU_BLOCK), lambda i: (0, 0, 0)),
                  pl.BlockSpec((HEADS, SGU_BLOCK, 1), lambda i: (0, 0, 0)),
                  pl.BlockSpec((KD, D), lambda i: (0, 0)), vec, vec, vec],
        out_specs=[pl.BlockSpec((tm, DMIX), lambda i: (i, 0)), pl.BlockSpec((tm, D), lambda i: (i, 0))],
        out_shape=[jax.ShapeDtypeStruct((s, DMIX), BF16), jax.ShapeDtypeStruct((s, D), F32)],
        scratch_shapes=[pltpu.VMEM((HC + tm, D), F32), pltpu.VMEM((SUB - 1, tm + HC - SUB, D), F32)],
        compiler_params=_cp(("arbitrary",)),
    )(z, slg, slb, wsb, sbcol, dw, dcb, dng, dnb)


def mix_odd_bwd(z, dycat, cz, slg, slb, wsb, wstb, sbcol, dw, dng, dnb, layer, parts):
    s = z.shape[0]
    tm = min(s, 256)
    nt = s // tm

    def body(z_ref, dy_ref, cz_ref, slg_ref, slb_ref, ws_ref, wst_ref, sb_ref, dw_ref, dng_ref, dnb_ref, p_ref,
             dz_ref, dslg_ref, dslb_ref, dws_ref, dsb_ref, ddw_ref, ddcb_ref, ddng_ref, ddnb_ref, land_ref,
             dvn, edz, esh, *sems):
        i = pl.program_id(0)
        start, finish = _exchange_plan(p_ref, land_ref, sems, 0)
        pl.when(i == 0)(start)

        @pl.when(i == 0)
        def _():
            edz[tm:tm + HC, :] = jnp.zeros((HC, D), F32)
            for ref in (dslg_ref, dslb_ref, dws_ref, dsb_ref, ddw_ref, ddcb_ref, ddng_ref, ddnb_ref):
                ref[...] = jnp.zeros_like(ref)

        vh, vrs = _ln_rows(z_ref[:, _cols(1)].astype(F32))
        vnb = (vh * slg_ref[...] + slb_ref[...]).astype(BF16)
        for n in range(tm // SGU_BLOCK):
            r0, r1 = n * SGU_BLOCK, (n + 1) * SGU_BLOCK
            for hd in range(HEADS):
                c0, c1 = hd * HD, (hd + 1) * HD
                vblk = vnb[r0:r1, c0:c1]
                sv = _dot(ws_ref[hd], vblk) + sb_ref[hd]
                gc = z_ref[r0:r1, 2 * D + c0:2 * D + c1].astype(F32)
                sg = _sig(gc)
                sil = gc * sg
                u = z_ref[r0:r1, c0:c1].astype(F32)
                dyc = dy_ref[r0:r1, c0:c1]
                dz_ref[r0:r1, c0:c1] = (dyc * sv * sil).astype(BF16)
                dz_ref[r0:r1, 2 * D + c0:2 * D + c1] = (dyc * u * sv * (sg * (1.0 + gc * (1.0 - sg)))).astype(BF16)
                dsv = dyc * u * sil
                dsb_ref[hd] += jnp.sum(dsv, axis=-1, keepdims=True)
                dsvb = dsv.astype(BF16)
                dws_ref[hd] += _dot_nt(dsvb, vblk)
                dvn[r0:r1, c0:c1] = _dot(wst_ref[hd], dsvb)
        dv = dvn[...]
        dslg_ref[...] += _rowsum(dv * vh)
        dslb_ref[...] += _rowsum(dv)
        dz_ref[:, _cols(1)] = _ln_rows_bwd(dv, vh, vrs, slg_ref[...]).astype(BF16)

        zh, zrs = _ln_rows(cz_ref[...])
        zn = zh * dng_ref[...] + dnb_ref[...]
        sgn = _sig(zn)
        gd = z_ref[:, _cols(5)].astype(F32)
        sgd = _sig(gd)
        dyd = dy_ref[:, D:2 * D]
        dz_ref[:, _cols(5)] = (dyd * (zn * sgn) * (sgd * (1.0 + gd * (1.0 - sgd)))).astype(BF16)
        dzn = dyd * (gd * sgd) * (sgn * (1.0 + zn * (1.0 - sgn)))
        ddng_ref[...] += _rowsum(dzn * zh)
        ddnb_ref[...] += _rowsum(dzn)
        dcz = _ln_rows_bwd(dzn, zh, zrs, dng_ref[...])
        ddcb_ref[...] += _rowsum(dcz)
        edz[0:tm, :] = dcz
        _shifted_copies(edz, esh, tm + HC - SUB)
        a = z_ref[:, _cols(3)].astype(F32)
        sgb = _sig(z_ref[:, _cols(4)].astype(F32))
        zg = a * sgb
        dzg = jnp.zeros((tm, D), F32)
        for k in range(KD):
            sh = _tap(edz, esh, KD - 1 - k, tm)
            dzg = dzg + dw_ref[k:k + 1, :] * sh
            ddw_ref[k:k + 1, :] += _rowsum(zg * sh)
        dz_ref[:, _cols(3)] = (dzg * sgb).astype(BF16)
        dz_ref[:, _cols(4)] = (dzg * a * sgb * (1.0 - sgb)).astype(BF16)
        edz[tm:tm + HC, :] = edz[0:HC, :]
        pl.when(i == nt - 1)(finish)

    rev = lambda i: (nt - 1 - i, 0)
    vec = pl.BlockSpec((1, D), lambda i: (0, 0))
    wspec = pl.BlockSpec((HEADS, SGU_BLOCK, SGU_BLOCK), lambda i: (0, 0, 0))
    bspec = pl.BlockSpec((HEADS, SGU_BLOCK, 1), lambda i: (0, 0, 0))
    kspec = pl.BlockSpec((KD, D), lambda i: (0, 0))
    return pl.pallas_call(
        body, name=f"mix_odd_bwd_{layer}", grid=(nt,),
        in_specs=[pl.BlockSpec((tm, NIN), rev), pl.BlockSpec((tm, DMIX), rev), pl.BlockSpec((tm, D), rev),
                  vec, vec, wspec, wspec, bspec, kspec, vec, vec, ANY],
        out_specs=[pl.BlockSpec((tm, NIN), rev), vec, vec, wspec, bspec, kspec, vec, vec, vec, ANY],
        out_shape=[jax.ShapeDtypeStruct((s, NIN), BF16), jax.ShapeDtypeStruct((1, D), F32),
                   jax.ShapeDtypeStruct((1, D), F32),
                   jax.ShapeDtypeStruct((HEADS, SGU_BLOCK, SGU_BLOCK), F32),
                   jax.ShapeDtypeStruct((HEADS, SGU_BLOCK, 1), F32), jax.ShapeDtypeStruct((KD, D), F32),
                   jax.ShapeDtypeStruct((1, D), F32), jax.ShapeDtypeStruct((1, D), F32),
                   jax.ShapeDtypeStruct((1, D), F32), jax.ShapeDtypeStruct(parts.shape, parts.dtype)],
        scratch_shapes=[pltpu.VMEM((tm, D), F32), pltpu.VMEM((tm + HC, D), F32),
                        pltpu.VMEM((SUB - 1, tm + HC - SUB, D), F32)] + _comm_scratch(1),
        compiler_params=_cp(("arbitrary",)),
    )(z, dycat, cz, slg, slb, wsb, wstb, sbcol, dw, dng, dnb, parts)


def _adamw_math(w, g, m, v):
    m = ADAM_B1 * m + (1.0 - ADAM_B1) * g
    v = ADAM_B2 * v + (1.0 - ADAM_B2) * (g * g)
    m_hat = m / (1.0 - ADAM_B1 ** ADAM_STEP)
    v_hat = v / (1.0 - ADAM_B2 ** ADAM_STEP)
    delta = -ADAM_LR * (m_hat / (jnp.sqrt(v_hat) + ADAM_EPS) + ADAM_WD * w)
    return delta, m, v


def adamw_big(w, m, v, parts0, parts1, name):
    _, r, c = w.shape
    tr = min(r, 256)
    nr = r // tr

    def body(w_ref, m_ref, v_ref, p0_ref, p1_ref, g_ref, d_ref, nm_ref, nv_ref):
        i = pl.program_id(0)

        def total(p_ref):
            acc = p_ref[0].astype(F32)
            for j in range(1, NDEV):
                acc = acc + p_ref[j].astype(F32)
            return acc

        @pl.when(i == 0)
        def _():
            g_ref[...] = total(p0_ref)

        @pl.when(i == 1)
        def _():
            g_ref[...] = total(p1_ref)

        delta, nm, nv = _adamw_math(w_ref[...], g_ref[...], m_ref[...], v_ref[...])
        d_ref[...] = delta
        nm_ref[...] = nm
        nv_ref[...] = nv

    wspec = pl.BlockSpec((None, tr, c), lambda i, j: (i, j, 0))
    p0 = pl.BlockSpec((NDEV, tr, c), lambda i, j: (0, jnp.where(i == 0, j, nr - 1), 0))
    p1 = pl.BlockSpec((NDEV, tr, c), lambda i, j: (0, jnp.where(i == 1, j, 0), 0))
    shp = jax.ShapeDtypeStruct(w.shape, F32)
    return pl.pallas_call(
        body, name=name, grid=(2, nr), in_specs=[wspec, wspec, wspec, p0, p1],
        out_specs=[wspec] * 4, out_shape=[shp] * 4,
        compiler_params=_cp(("arbitrary", "arbitrary")),
    )(w, m, v, parts0, parts1)


def sum_parts(parts, name):
    _, r, c = parts.shape

    def body(p_ref, o_ref):
        acc = p_ref[0]
        for j in range(1, NDEV):
            acc = acc + p_ref[j]
        o_ref[...] = acc

    return pl.pallas_call(body, name=name, out_shape=jax.ShapeDtypeStruct((r, c), F32),
                          compiler_params=_cp())(parts)


def adamw_small(w, g, m, v, name):
    def body(w_ref, g_ref, m_ref, v_ref, d_ref, nm_ref, nv_ref):
        delta, nm, nv = _adamw_math(w_ref[...], g_ref[...], m_ref[...], v_ref[...])
        d_ref[...] = delta
        nm_ref[...] = nm
        nv_ref[...] = nv

    shp = jax.ShapeDtypeStruct(w.shape, F32)
    return pl.pallas_call(body, name=name, out_shape=[shp] * 3, compiler_params=_cp())(w, g, m, v)


def _size(shape):
    n = 1
    for d in shape:
        n *= d
    return n


def _pack(arrays, rows):
    flat = jnp.concatenate([a.reshape(-1) for a in arrays])
    return jnp.pad(flat, (0, rows * 128 - flat.shape[0])).reshape(rows, 128)


def _unpack(flat, shapes):
    out, o = [], 0
    for shp in shapes:
        out.append(flat[o:o + _size(shp)].reshape(shp))
        o += _size(shp)
    return out


def _rows_for(shapes):
    return -(-sum(_size(shp) for shp in shapes) // 1024) * 8


SHARDED_SMALL = (("pool_w", (2, 4, 256, 256), 2), ("sconv_w", (2, KS, D), 2), ("sgu_ln_g", (2, D), 1),
                 ("sgu_ln_b", (2, D), 1), ("dconv_w", (2, KD, D), 2), ("dconv_b", (2, D), 1),
                 ("dnorm_g", (2, D), 1), ("dnorm_b", (2, D), 1))
REPLICATED_SMALL = (("ln_g", (DEPTH, D)), ("ln_b", (DEPTH, D)), ("pool_scale", (2, D)), ("sconv_b", (2, D)),
                    ("sgu_w", (2, HEADS, SGU_BLOCK, SGU_BLOCK)), ("sgu_b", (2, HEADS, SGU_BLOCK)))


def _shard_shape(shape, axis):
    return tuple(d // NDEV if a == axis else d for a, d in enumerate(shape))


def _merge_gathered(g, shape, axis):
    return jnp.moveaxis(g, 0, axis).reshape(shape)


def kernel(x, ln_g, ln_b, w_in_even, w_out_even, pool_w, pool_scale, sconv_w, sconv_b, w_in_odd, w_out_odd, sgu_ln_g, sgu_ln_b, sgu_w, sgu_b, dconv_w, dconv_b, dnorm_g, dnorm_b, loss_target, m_ln_g, m_ln_b, m_w_in_even, m_w_out_even, m_pool_w, m_pool_scale, m_sconv_w, m_sconv_b, m_w_in_odd, m_w_out_odd, m_sgu_ln_g, m_sgu_ln_b, m_sgu_w, m_sgu_b, m_dconv_w, m_dconv_b, m_dnorm_g, m_dnorm_b, v_ln_g, v_ln_b, v_w_in_even, v_w_out_even, v_pool_w, v_pool_scale, v_sconv_w, v_sconv_b, v_w_in_odd, v_w_out_odd, v_sgu_ln_g, v_sgu_ln_b, v_sgu_w, v_sgu_b, v_dconv_w, v_dconv_b, v_dnorm_g, v_dnorm_b):
    given = dict(locals())
    me = 4 * lax.axis_index("x") + 2 * lax.axis_index("y") + lax.axis_index("c")
    xs = x[0]
    target = loss_target[0]

    wc_in = cast_weights(w_in_even, w_in_odd, "cast_w_in")
    wc_out = cast_weights(w_out_even, w_out_odd, "cast_w_out")
    shard_shapes = [_shard_shape(shp, ax) for _, shp, ax in SHARDED_SMALL]
    srows = _rows_for(shard_shapes)
    xb, wg_first, gathered = cast_x(xs, wc_in, _pack([given[n] for n, _, _ in SHARDED_SMALL], srows))
    wg_in = [wg_first] + [None] * (DEPTH - 1)
    wg_out = [None] * DEPTH
    gathered = gathered.reshape(NDEV, -1)
    full, o = {}, 0
    for (n, shp, ax), sshp in zip(SHARDED_SMALL, shard_shapes):
        full[n] = _merge_gathered(gathered[:, o:o + _size(sshp)].reshape((NDEV,) + sshp), shp, ax)
        o += _size(sshp)

    mask = (jnp.arange(SGU_BLOCK)[None, :] // 64) <= (jnp.arange(SGU_BLOCK)[:, None] // 64)
    ws = jnp.where(mask[None, None], sgu_w, 0.0)
    wsb = ws.astype(BF16)
    wstb = jnp.swapaxes(ws, -1, -2).astype(BF16)
    pwb = full["pool_w"].astype(BF16)

    ones = jnp.ones((1, D), F32)
    zeros = jnp.zeros((1, D), F32)

    xres, gp, bp = xs, ones, zeros
    saved = []
    for layer in range(DEPTH):
        i = layer // 2
        outs = proj_in(xb, wg_in[layer], layer, wc_in, wc_out)
        z, wg_out[layer] = outs[0], outs[1]
        if layer + 1 < DEPTH:
            wg_in[layer + 1] = outs[2]
        if layer % 2 == 0:
            ycat, pooled, cv = mix_even_fwd(z, pwb[i], pool_scale[i][None], full["sconv_w"][i],
                                            sconv_b[i][None], layer)
            extra = (pooled, cv)
        else:
            ycat, cz = mix_odd_fwd(z, full["sgu_ln_g"][i][None], full["sgu_ln_b"][i][None], wsb[i],
                                   sgu_b[i][:, :, None], full["dconv_w"][i], full["dconv_b"][i][None],
                                   full["dnorm_g"][i][None], full["dnorm_b"][i][None], layer)
            extra = (cz,)
        g, b = ln_g[layer][None], ln_b[layer][None]
        outs = proj_out_ln(ycat, wg_out[layer], layer, xres, gp, bp, g, b,
                           target=target if layer == DEPTH - 1 else None)
        xhat, rstd, xb_next = outs[0], outs[1], outs[2]
        saved.append((xb, z, ycat, extra, xhat, rstd))
        xres, gp, bp, xb = xhat, g, b, xb_next
    dxn, loss_local = outs[3], outs[4]

    gsmall = {n: [None] * shp[0] for n, shp in REPLICATED_SMALL}
    gsmall.update({n: [None] * shp[0] for n, shp, _ in SHARDED_SMALL})
    parts_in, parts_out, parts_pw = [None] * DEPTH, [None] * DEPTH, [None] * 2
    pending = None
    for layer in reversed(range(DEPTH)):
        i = layer // 2
        xb, z, ycat, extra, xhat, rstd = saved[layer]
        dr, drb, dycat, dg, db = ln_bwd_dycat(dxn, xhat, rstd, ln_g[layer][None], wg_out[layer], layer)
        gsmall["ln_g"][layer], gsmall["ln_b"][layer] = dg[0], db[0]
        dwo = dw_out(ycat, drb, layer)
        own_out = pending is None
        riding = dwo if own_out else pending
        if layer % 2 == 0:
            dzb, dpw, dps, dsw, dsb, landed = mix_even_bwd(
                z, dycat, extra[0], extra[1], pwb[i], pool_scale[i][None], full["sconv_w"][i], layer, parts=riding)
            gsmall["pool_scale"][i] = dps[0]
            gsmall["sconv_w"][i], gsmall["sconv_b"][i] = dsw, dsb[0]
        else:
            dzb, dslg, dslb, dws, dsbc, ddw, ddcb, ddng, ddnb, landed = mix_odd_bwd(
                z, dycat, extra[0], full["sgu_ln_g"][i][None], full["sgu_ln_b"][i][None], wsb[i], wstb[i],
                sgu_b[i][:, :, None], full["dconv_w"][i], full["dnorm_g"][i][None], full["dnorm_b"][i][None],
                layer, riding)
            gsmall["sgu_ln_g"][i], gsmall["sgu_ln_b"][i] = dslg[0], dslb[0]
            gsmall["sgu_w"][i], gsmall["sgu_b"][i] = jnp.where(mask[None], dws, 0.0), dsbc[:, :, 0]
            gsmall["dconv_w"][i], gsmall["dconv_b"][i] = ddw, ddcb[0]
            gsmall["dnorm_g"][i], gsmall["dnorm_b"][i] = ddng[0], ddnb[0]
        riders = [] if own_out else [dwo]
        if layer % 2 == 0:
            riders.append(dpw.reshape(4, NDEV, PG // NDEV, PG).transpose(1, 0, 2, 3))
        outs = list(dw_in(xb, dzb, layer, riders))
        if layer % 2 == 0:
            parts_pw[i] = outs.pop()
        if own_out:
            parts_out[layer] = landed
        else:
            parts_in[layer + 1] = landed
            parts_out[layer] = outs[1]
        pending = outs[0]
        if layer == 0:
            dxn, parts_in[0] = dx_in(dzb, wg_in[layer], layer, dr, parts=pending)
        else:
            dxn = dx_in(dzb, wg_in[layer], layer, dr)[0]
    grad_x = dxn[None]

    big = {}
    big["w_in_even"] = adamw_big(w_in_even, m_w_in_even, v_w_in_even, parts_in[0], parts_in[2], "adamw_w_in_even")
    big["w_in_odd"] = adamw_big(w_in_odd, m_w_in_odd, v_w_in_odd, parts_in[1], parts_in[3], "adamw_w_in_odd")
    big["w_out_even"] = adamw_big(w_out_even, m_w_out_even, v_w_out_even, parts_out[0], parts_out[2], "adamw_w_out_even")
    big["w_out_odd"] = adamw_big(w_out_odd, m_w_out_odd, v_w_out_odd, parts_out[1], parts_out[3], "adamw_w_out_odd")

    small_names = [n for n, _ in REPLICATED_SMALL] + [n for n, _, _ in SHARDED_SMALL]
    gathered_names = [n for n in small_names if n != "pool_w"]
    gathered_shapes = [shp for n, shp in REPLICATED_SMALL] + [shp for n, shp, _ in SHARDED_SMALL if n != "pool_w"]
    grows = _rows_for(gathered_shapes + [(1,)])
    partial = _pack([jnp.stack(gsmall[n]) for n in gathered_names] + [loss_local], grows)
    gsum = sum_parts(all_gather(partial, "gather_small_grads"), "sum_small_grads")
    unpacked = _unpack(gsum.reshape(-1), gathered_shapes + [()])
    loss = unpacked.pop()
    gfull = dict(zip(gathered_names, unpacked))
    own_shapes = [shp for _, shp in REPLICATED_SMALL] + shard_shapes
    gown = {n: gfull[n] for n, _ in REPLICATED_SMALL}
    for (n, shp, ax), sshp in zip(SHARDED_SMALL, shard_shapes):
        if n == "pool_w":
            gown[n] = jnp.stack([sum_parts(p.reshape(NDEV, -1, 128), f"sum_pool_w_{j}").reshape(sshp[1:])
                                 for j, p in enumerate(parts_pw)])
        else:
            gown[n] = lax.dynamic_slice_in_dim(gfull[n], me * sshp[ax], sshp[ax], axis=ax)
    orows = _rows_for(own_shapes)
    packed = [_pack([src[n] for n in small_names], orows) for src in
              (given, gown, {n: given["m_" + n] for n in small_names}, {n: given["v_" + n] for n in small_names})]
    sd, sm, sv = adamw_small(*packed, "adamw_small")
    small = {}
    for n, d_, m_, v_ in zip(small_names, _unpack(sd.reshape(-1), own_shapes), _unpack(sm.reshape(-1), own_shapes),
                             _unpack(sv.reshape(-1), own_shapes)):
        small[n] = (gown[n], d_, m_, v_)

    weights = ['ln_g', 'ln_b', 'w_in_even', 'w_out_even', 'pool_w', 'pool_scale', 'sconv_w', 'sconv_b', 'w_in_odd',
               'w_out_odd', 'sgu_ln_g', 'sgu_ln_b', 'sgu_w', 'sgu_b', 'dconv_w', 'dconv_b', 'dnorm_g', 'dnorm_b']
    res = {n: (big[n] if n in big else small[n]) for n in weights}
    return (loss, grad_x, *[res[n][0] for n in weights], *[res[n][1] for n in weights],
            *[res[n][2] for n in weights], *[res[n][3] for n in weights])
```

```python
import jax
import jax.numpy as jnp
from jax import lax
from jax.experimental import pallas as pl
from jax.experimental.pallas import tpu as pltpu

F32 = jnp.float32
BF16 = jnp.bfloat16

D = 1024
DMIX = 2048
NIN = 6144
NDEV = 8
CW = NIN // NDEV
RW = DMIX // NDEV
DEPTH = 4
ALPHA = (2 * DEPTH) ** 0.25
LN_EPS = 1e-5
POOL_WINDOWS = (2, 4, 8, 16)
PG = 256
SGU_BLOCK = 128
HEADS = 4
HD = 256
KD = 31
KS = 3
SUB = 8
RC = 16
CB = 512
HP = 32
HS = 8
HC = 32
NSEM = 7

ADAM_LR = 0.001
ADAM_B1 = 0.9
ADAM_B2 = 0.999
ADAM_EPS = 1e-08
ADAM_WD = 0.01
ADAM_STEP = 10

VMEM_LIMIT = 56 * 1024 * 1024
MESH = pl.DeviceIdType.MESH
ANY = pl.BlockSpec(memory_space=pl.ANY)


def _cp(sem=None):
    if sem is None:
        return pltpu.CompilerParams(vmem_limit_bytes=VMEM_LIMIT)
    return pltpu.CompilerParams(dimension_semantics=sem, vmem_limit_bytes=VMEM_LIMIT)


def _sig(x):
    return 0.5 * jnp.tanh(0.5 * x) + 0.5


def _dot(a, b):
    return jnp.dot(a, b, preferred_element_type=F32)


def _dot_nt(a, b):
    return lax.dot_general(a, b, (((1,), (1,)), ((), ())), preferred_element_type=F32)


def _dot_tn(a, b):
    return lax.dot_general(a, b, (((0,), (0,)), ((), ())), preferred_element_type=F32)


def _rowsum(x):
    return jnp.sum(x, axis=0, keepdims=True)


def _comm_scratch(n):
    return [pltpu.SemaphoreType.DMA((n * NSEM,)), pltpu.SemaphoreType.DMA((n * NSEM,)),
            pltpu.SemaphoreType.DMA((n,))]


def _gather_plan(src_ref, out_ref, sems, n):
    send_sems, recv_sems, local_sems = sems
    base = n * NSEM
    x, y, c = lax.axis_index("x"), lax.axis_index("y"), lax.axis_index("c")
    me, sibling = (x, y, c), (x, y, 1 - c)
    chips = [(1 - x, y), (x, 1 - y), (1 - x, 1 - y)]

    def slot(px, py, pc):
        return out_ref.at[4 * px + 2 * py + pc]

    def copy(k, blk, to, src=None):
        return pltpu.make_async_remote_copy(
            src_ref=slot(*blk) if src is None else src, dst_ref=slot(*blk),
            send_sem=send_sems.at[base + k], recv_sem=recv_sems.at[base + k],
            device_id=to, device_id_type=MESH)

    def mine():
        return pltpu.make_async_copy(src_ref, slot(*me), local_sems.at[n])

    def start():
        mine().start()
        copy(0, me, sibling, src=src_ref).start()
        for j, chip in enumerate(chips):
            copy(1 + j, me, (*chip, c), src=src_ref).start()

    def finish():
        for j, chip in enumerate(chips):
            copy(1 + j, (*chip, c), me).wait_recv()
            copy(4 + j, (*chip, c), sibling).start()
        copy(0, sibling, me).wait_recv()
        for j, chip in enumerate(chips):
            copy(4 + j, (*chip, 1 - c), me).wait_recv()
        copy(0, me, sibling, src=src_ref).wait_send()
        for j, chip in enumerate(chips):
            copy(1 + j, me, (*chip, c), src=src_ref).wait_send()
            copy(4 + j, (*chip, c), sibling).wait_send()
        mine().wait()

    return start, finish


def _exchange_plan(p_ref, out_ref, sems, n):
    send_sems, recv_sems, local_sems = sems
    base = n * NSEM
    x, y, c = lax.axis_index("x"), lax.axis_index("y"), lax.axis_index("c")
    me = 4 * x + 2 * y + c

    def copy(r, landing):
        px, py, pc = x ^ (r >> 2), y ^ ((r >> 1) & 1), c ^ (r & 1)
        peer = 4 * px + 2 * py + pc
        return pltpu.make_async_remote_copy(
            src_ref=p_ref.at[peer], dst_ref=out_ref.at[peer if landing else me],
            send_sem=send_sems.at[base + r - 1], recv_sem=recv_sems.at[base + r - 1],
            device_id=(px, py, pc), device_id_type=MESH)

    def mine():
        return pltpu.make_async_copy(p_ref.at[me], out_ref.at[me], local_sems.at[n])

    def start():
        mine().start()
        for r in range(1, NDEV):
            copy(r, False).start()

    def finish():
        for r in range(1, NDEV):
            copy(r, True).wait_recv()
        for r in range(1, NDEV):
            copy(r, False).wait_send()
        mine().wait()

    return start, finish


def cast_x(x, w_first, small):
    s = x.shape[0]
    tm = min(s, 512)
    nt = s // tm

    def body(x_ref, wf_ref, sm_ref, o_ref, gw_ref, gs_ref, *sems):
        i = pl.program_id(0)
        plans = [_gather_plan(wf_ref.at[0], gw_ref, sems, 0), _gather_plan(sm_ref, gs_ref, sems, 1)]

        @pl.when(i == 0)
        def _():
            for start, _ in plans:
                start()

        o_ref[...] = x_ref[...].astype(BF16)

        @pl.when(i == nt - 1)
        def _():
            for _, finish in plans:
                finish()

    return pl.pallas_call(
        body, name="cast_x", grid=(nt,),
        in_specs=[pl.BlockSpec((tm, D), lambda i: (i, 0)), ANY, ANY],
        out_specs=[pl.BlockSpec((tm, D), lambda i: (i, 0)), ANY, ANY],
        out_shape=[jax.ShapeDtypeStruct((s, D), BF16),
                   jax.ShapeDtypeStruct((NDEV,) + w_first.shape[1:], w_first.dtype),
                   jax.ShapeDtypeStruct((NDEV,) + small.shape, small.dtype)],
        scratch_shapes=_comm_scratch(2), compiler_params=_cp(("arbitrary",)),
    )(x, w_first, small)


def cast_weights(w_even, w_odd, name):
    _, r, c = w_even.shape

    def body(e_ref, o_ref, out_ref):
        layer = pl.program_id(0)

        @pl.when(layer % 2 == 0)
        def _():
            out_ref[...] = e_ref[...].astype(BF16)

        @pl.when(layer % 2 == 1)
        def _():
            out_ref[...] = o_ref[...].astype(BF16)

    spec = pl.BlockSpec((None, r, c), lambda l: (l // 2, 0, 0))
    return pl.pallas_call(
        body, name=name, grid=(DEPTH,), in_specs=[spec, spec],
        out_specs=pl.BlockSpec((None, r, c), lambda l: (l, 0, 0)),
        out_shape=jax.ShapeDtypeStruct((DEPTH, r, c), BF16), compiler_params=_cp(("parallel",)),
    )(w_even, w_odd)


def proj_in(xb, wg, layer, wc_in, wc_out):
    s = xb.shape[0]
    tm = min(s, 512)
    nt = s // tm
    nxt = layer + 1 < DEPTH

    def body(*refs):
        if nxt:
            x_ref, w_ref, wco_ref, wci_ref, o_ref, go_ref, gi_ref = refs[:7]
        else:
            x_ref, w_ref, wco_ref, o_ref, go_ref = refs[:5]
        sems = refs[-3:]
        i = pl.program_id(0)
        plans = [_gather_plan(wco_ref.at[layer], go_ref, sems, 0)]
        if nxt:
            plans.append(_gather_plan(wci_ref.at[layer + 1], gi_ref, sems, 1))

        @pl.when(i == 0)
        def _():
            for start, _ in plans:
                start()

        x = x_ref[...]
        for k in range(NDEV):
            o_ref[:, k * CW:(k + 1) * CW] = _dot(x, w_ref[k]).astype(BF16)

        @pl.when(i == nt - 1)
        def _():
            for _, finish in plans:
                finish()

    in_specs = [pl.BlockSpec((tm, D), lambda i: (i, 0)),
                pl.BlockSpec((NDEV, D, CW), lambda i: (0, 0, 0), pipeline_mode=pl.Buffered(1)), ANY]
    out_specs = [pl.BlockSpec((tm, NIN), lambda i: (i, 0)), ANY]
    out_shape = [jax.ShapeDtypeStruct((s, NIN), BF16), jax.ShapeDtypeStruct((NDEV, RW, D), BF16)]
    args = [xb, wg, wc_out]
    if nxt:
        in_specs.append(ANY)
        args.append(wc_in)
        out_specs.append(ANY)
        out_shape.append(jax.ShapeDtypeStruct((NDEV, D, CW), BF16))
    return pl.pallas_call(
        body, name=f"proj_in_{layer}", grid=(nt,), in_specs=in_specs, out_specs=out_specs,
        out_shape=out_shape, scratch_shapes=_comm_scratch(2 if nxt else 1),
        compiler_params=_cp(("arbitrary",)),
    )(*args)


def mix_fwd(z, layer, params):
    s = z.shape[0]
    tm = min(s, 256)
    even = layer % 2 == 0
    npar = len(params)
    nout = 3 if even else 2

    def even_tile(zr, j, par, outs, scr):
        pw_ref, ps_ref, sw_ref, sb_ref = par
        yc_ref, pb_ref, cv_ref = outs
        exa, eq, ya0, la, lb = scr
        for c in range(tm // RC):
            r0, r1 = c * RC, (c + 1) * RC
            exa[HP + r0:HP + r1, :] = zr[r0:r1, _cols(0)].astype(F32)
            eq[HS + r0:HS + r1, :] = zr[r0:r1, _cols(4)].astype(F32) * zr[r0:r1, _cols(2)].astype(F32)

        def level(src, dst, shift, c0, start):
            for r0 in range(start, HP + tm, 2 * RC):
                r1 = min(r0 + 2 * RC, HP + tm)
                dst[r0:r1, c0:D] = src[r0:r1, c0:D] + src[r0 - shift:r1 - shift, c0:D]

        level(exa, la, 1, PG, 8)
        level(la, lb, 2, 2 * PG, 16)
        level(lb, la, 4, 3 * PG, 24)
        last = ((exa, 1), (la, 2), (lb, 4), (la, 8))
        for c in range(tm // RC):
            r0, r1 = HP + c * RC, HP + (c + 1) * RC
            pos = (j * tm + c * RC + lax.broadcasted_iota(jnp.int32, (RC, 1), 0) + 1).astype(F32)
            for gi, w in enumerate(POOL_WINDOWS):
                c0, c1 = gi * PG, (gi + 1) * PG
                src, shift = last[gi]
                acc = src[r0:r1, c0:c1] + src[r0 - shift:r1 - shift, c0:c1]
                pooled = acc / jnp.minimum(pos, float(w)) - exa[r0:r1, c0:c1]
                pb_ref[c * RC:(c + 1) * RC, c0:c1] = pooled.astype(BF16)
        for gi in range(len(POOL_WINDOWS)):
            c0, c1 = gi * PG, (gi + 1) * PG
            ya0[:, c0:c1] = _dot(pb_ref[:, c0:c1], pw_ref[gi])
        for c in range(tm // RC):
            r0, r1 = c * RC, (c + 1) * RC
            ga = zr[r0:r1, _cols(1)].astype(F32)
            yc_ref[r0:r1, 0:D] = (ya0[r0:r1, :] * ps_ref[...] * (ga * _sig(ga))).astype(BF16)
            cv = (sw_ref[2:3, :] * eq[HS + r0:HS + r1, :] + sw_ref[1:2, :] * eq[HS + r0 - 1:HS + r1 - 1, :]
                  + sw_ref[0:1, :] * eq[HS + r0 - 2:HS + r1 - 2, :] + sb_ref[...])
            cv_ref[r0:r1, :] = cv
            gb = zr[r0:r1, _cols(5)].astype(F32)
            yc_ref[r0:r1, D:2 * D] = (zr[r0:r1, _cols(3)].astype(F32) * cv * (gb * _sig(gb))).astype(BF16)
        exa[0:HP, :] = exa[tm:tm + HP, :]
        eq[0:HS, :] = eq[tm:tm + HS, :]

    def odd_tile(zr, j, par, outs, scr):
        slg_ref, slb_ref, ws_ref, sb_ref, dw_ref, dcb_ref, dng_ref, dnb_ref = par
        yc_ref, cz_ref = outs
        ezg, esh = scr
        vh, _ = _ln_rows(zr[:, _cols(1)].astype(F32))
        vnb = (vh * slg_ref[...] + slb_ref[...]).astype(BF16)
        for n in range(tm // SGU_BLOCK):
            r0, r1 = n * SGU_BLOCK, (n + 1) * SGU_BLOCK
            for hd in range(HEADS):
                c0, c1 = hd * HD, (hd + 1) * HD
                sv = _dot(ws_ref[hd], vnb[r0:r1, c0:c1]) + sb_ref[hd]
                gc = zr[r0:r1, 2 * D + c0:2 * D + c1].astype(F32)
                yc_ref[r0:r1, c0:c1] = (zr[r0:r1, c0:c1].astype(F32) * sv * (gc * _sig(gc))).astype(BF16)
        ezg[HC:HC + tm, :] = zr[:, _cols(3)].astype(F32) * _sig(zr[:, _cols(4)].astype(F32))
        _shifted_copies(ezg, esh, tm + HC - SUB)
        cz = jnp.zeros((tm, D), F32) + dcb_ref[...]
        for k in range(KD):
            cz = cz + dw_ref[k:k + 1, :] * _tap(ezg, esh, HC - (KD - 1) + k, tm)
        cz_ref[...] = cz
        zh, _ = _ln_rows(cz)
        zn = zh * dng_ref[...] + dnb_ref[...]
        gd = zr[:, _cols(5)].astype(F32)
        yc_ref[:, D:2 * D] = ((zn * _sig(zn)) * (gd * _sig(gd))).astype(BF16)
        ezg[0:HC, :] = ezg[tm:tm + HC, :]

    def body(z_ref, *refs):
        par, outs, scr = refs[:npar], refs[npar:npar + nout], refs[npar + nout:]
        j = pl.program_id(0)

        @pl.when(j == 0)
        def _():
            if even:
                scr[0][0:HP, :] = jnp.zeros((HP, D), F32)
                scr[1][0:HS, :] = jnp.zeros((HS, D), F32)
            else:
                scr[0][0:HC, :] = jnp.zeros((HC, D), F32)

        (even_tile if even else odd_tile)(z_ref, j, par, outs, scr)

    row = lambda c: pl.BlockSpec((tm, c), lambda j: (j, 0))
    vec = pl.BlockSpec((1, D), lambda j: (0, 0))
    if even:
        par_specs = [pl.BlockSpec((4, PG, PG), lambda j: (0, 0, 0)), vec, pl.BlockSpec((KS, D), lambda j: (0, 0)), vec]
        out_specs = [row(DMIX), row(D), row(D)]
        out_shape = [jax.ShapeDtypeStruct((s, DMIX), BF16), jax.ShapeDtypeStruct((s, D), BF16),
                     jax.ShapeDtypeStruct((s, D), F32)]
        scratch = [pltpu.VMEM((HP + tm, D), F32), pltpu.VMEM((HS + tm, D), F32), pltpu.VMEM((tm, D), F32),
                   pltpu.VMEM((HP + tm, D), F32), pltpu.VMEM((HP + tm, D), F32)]
    else:
        par_specs = [vec, vec, pl.BlockSpec((HEADS, SGU_BLOCK, SGU_BLOCK), lambda j: (0, 0, 0)),
                     pl.BlockSpec((HEADS, SGU_BLOCK, 1), lambda j: (0, 0, 0)),
                     pl.BlockSpec((KD, D), lambda j: (0, 0)), vec, vec, vec]
        out_specs = [row(DMIX), row(D)]
        out_shape = [jax.ShapeDtypeStruct((s, DMIX), BF16), jax.ShapeDtypeStruct((s, D), F32)]
        scratch = [pltpu.VMEM((HC + tm, D), F32), pltpu.VMEM((SUB - 1, tm + HC - SUB, D), F32)]
    return pl.pallas_call(
        body, name=f"mix_fwd_{layer}", grid=(s // tm,), in_specs=[row(NIN)] + par_specs, out_specs=out_specs,
        out_shape=out_shape, scratch_shapes=scratch, compiler_params=_cp(("arbitrary",)),
    )(z, *params)


def proj_out_ln(ycat, wog, layer, xres, g_prev, b_prev, g, b, target=None):
    s = ycat.shape[0]
    tm = min(s, 256)
    nt = s // tm
    final = target is not None

    def body(*refs):
        if final:
            y_ref, w_ref, xr_ref, gp_ref, bp_ref, g_ref, b_ref, t_ref, xh_ref, rs_ref, xb_ref, dy_ref, loss_ref = refs
        else:
            y_ref, w_ref, xr_ref, gp_ref, bp_ref, g_ref, b_ref, xh_ref, rs_ref, xb_ref = refs
        xin = xr_ref[...] * gp_ref[...] + bp_ref[...]
        y = _dot(y_ref[...], w_ref[...].reshape(DMIX, D))
        r = ALPHA * xin + y
        mu = jnp.mean(r, axis=-1, keepdims=True)
        d = r - mu
        var = jnp.mean(d * d, axis=-1, keepdims=True)
        rstd = lax.rsqrt(var + LN_EPS)
        xh = d * rstd
        xh_ref[...] = xh
        rs_ref[...] = rstd
        xn = xh * g_ref[...] + b_ref[...]
        xb_ref[...] = xn.astype(BF16)
        if final:
            err = xn - t_ref[...]
            dy_ref[...] = err * (1.0 / D)

            @pl.when(pl.program_id(0) == 0)
            def _():
                loss_ref[...] = jnp.zeros_like(loss_ref)

            loss_ref[...] += 0.5 * jnp.sum(jnp.mean(err * err, axis=-1, keepdims=True), axis=0, keepdims=True)

    row = pl.BlockSpec((tm, D), lambda i: (i, 0))
    vec = pl.BlockSpec((1, D), lambda i: (0, 0))
    in_specs = [pl.BlockSpec((tm, DMIX), lambda i: (i, 0)),
                pl.BlockSpec((NDEV, RW, D), lambda i: (0, 0, 0)),
                row, vec, vec, vec, vec]
    out_specs = [row, pl.BlockSpec((tm, 1), lambda i: (i, 0)), row]
    out_shape = [jax.ShapeDtypeStruct((s, D), F32), jax.ShapeDtypeStruct((s, 1), F32),
                 jax.ShapeDtypeStruct((s, D), BF16)]
    args = [ycat, wog, xres, g_prev, b_prev, g, b]
    if final:
        in_specs.append(row)
        args.append(target)
        out_specs += [row, pl.BlockSpec((1, 1), lambda i: (0, 0))]
        out_shape += [jax.ShapeDtypeStruct((s, D), F32), jax.ShapeDtypeStruct((1, 1), F32)]
    return pl.pallas_call(
        body, name=f"proj_out_ln_{layer}", grid=(nt,), in_specs=in_specs, out_specs=out_specs,
        out_shape=out_shape, compiler_params=_cp(("arbitrary",)),
    )(*args)


def ln_bwd_dycat(dxn, xhat, rstd, g, wog, layer):
    s = dxn.shape[0]
    tm = min(s, 256)

    def body(dx_ref, xh_ref, rs_ref, g_ref, w_ref, dr_ref, drb_ref, dyc_ref, dg_ref, db_ref):
        @pl.when(pl.program_id(0) == 0)
        def _():
            dg_ref[...] = jnp.zeros_like(dg_ref)
            db_ref[...] = jnp.zeros_like(db_ref)

        dxo = dx_ref[...]
        xh = xh_ref[...]
        dg_ref[...] += _rowsum(dxo * xh)
        db_ref[...] += _rowsum(dxo)
        dxh = dxo * g_ref[...]
        m1 = jnp.mean(dxh, axis=-1, keepdims=True)
        m2 = jnp.mean(dxh * xh, axis=-1, keepdims=True)
        dr = rs_ref[...] * (dxh - m1 - xh * m2)
        dr_ref[...] = dr
        drb = dr.astype(BF16)
        drb_ref[...] = drb
        dyc_ref[...] = _dot_nt(drb, w_ref[...].reshape(DMIX, D)).astype(BF16)

    row = pl.BlockSpec((tm, D), lambda i: (i, 0))
    vec = pl.BlockSpec((1, D), lambda i: (0, 0))
    return pl.pallas_call(
        body, name=f"ln_bwd_dycat_{layer}", grid=(s // tm,),
        in_specs=[row, row, pl.BlockSpec((tm, 1), lambda i: (i, 0)), vec,
                  pl.BlockSpec((NDEV, RW, D), lambda i: (0, 0, 0))],
        out_specs=[row, row, pl.BlockSpec((tm, DMIX), lambda i: (i, 0)), vec, vec],
        out_shape=[jax.ShapeDtypeStruct((s, D), F32), jax.ShapeDtypeStruct((s, D), BF16),
                   jax.ShapeDtypeStruct((s, DMIX), BF16), jax.ShapeDtypeStruct((1, D), F32),
                   jax.ShapeDtypeStruct((1, D), F32)],
        compiler_params=_cp(("arbitrary",)),
    )(dxn, xhat, rstd, g, wog)


def dx_in(dzb, wg, layer, dr, parts=None):
    s = dzb.shape[0]
    tm = min(s, 256)
    nt = s // tm
    carry = parts is not None

    def body(*refs):
        if carry:
            dz_ref, w_ref, dr_ref, p_ref, o_ref, land_ref = refs[:6]
            start, finish = _exchange_plan(p_ref, land_ref, refs[-3:], 0)
            pl.when(pl.program_id(0) == 0)(start)
        else:
            dz_ref, w_ref, dr_ref, o_ref = refs
        acc = ALPHA * dr_ref[...]
        for k in range(NDEV):
            acc += _dot_nt(dz_ref[:, k * CW:(k + 1) * CW], w_ref[k])
        o_ref[...] = acc
        if carry:
            pl.when(pl.program_id(0) == nt - 1)(finish)

    row = pl.BlockSpec((tm, D), lambda i: (i, 0))
    in_specs = [pl.BlockSpec((tm, NIN), lambda i: (i, 0)), pl.BlockSpec((NDEV, D, CW), lambda i: (0, 0, 0)), row]
    out_specs, out_shape, args = [row], [jax.ShapeDtypeStruct((s, D), F32)], [dzb, wg, dr]
    if carry:
        in_specs.append(ANY)
        args.append(parts)
        out_specs.append(ANY)
        out_shape.append(jax.ShapeDtypeStruct(parts.shape, parts.dtype))
    return pl.pallas_call(
        body, name=f"dx_in_{layer}", grid=(nt,), in_specs=in_specs, out_specs=out_specs, out_shape=out_shape,
        scratch_shapes=_comm_scratch(1) if carry else [], compiler_params=_cp(("arbitrary",)),
    )(*args)


def dw_in(xb, dzb, layer, parts=(), gathers=()):
    s = xb.shape[0]
    tm = min(s, 2048)
    nt = s // tm
    ne = len(parts)
    n = ne + len(gathers)

    def body(*refs):
        x_ref, dz_ref = refs[:2]
        p_refs = refs[2:2 + n]
        o_ref = refs[2 + n]
        land_refs = refs[3 + n:3 + 2 * n]
        acc = refs[3 + 2 * n]
        plans = [(_exchange_plan if j < ne else _gather_plan)(p, land, refs[-3:], j)
                 for j, (p, land) in enumerate(zip(p_refs, land_refs))]
        k, t = pl.program_id(0), pl.program_id(1)

        @pl.when((k == 0) & (t == 0))
        def _():
            for start, _ in plans:
                start()

        @pl.when(t == 0)
        def _():
            acc[...] = jnp.zeros_like(acc)

        acc[...] += _dot_tn(x_ref[...], dz_ref[...])

        @pl.when(t == nt - 1)
        def _():
            o_ref[...] = acc[...].astype(BF16)

        @pl.when((k == NDEV - 1) & (t == nt - 1))
        def _():
            for _, finish in plans:
                finish()

    return pl.pallas_call(
        body, name=f"dw_in_{layer}", grid=(NDEV, nt),
        in_specs=[pl.BlockSpec((tm, D), lambda k, t: (t, 0)), pl.BlockSpec((tm, CW), lambda k, t: (t, k))] + [ANY] * n,
        out_specs=[pl.BlockSpec((None, D, CW), lambda k, t: (k, 0, 0))] + [ANY] * n,
        out_shape=[jax.ShapeDtypeStruct((NDEV, D, CW), BF16)] + [jax.ShapeDtypeStruct(p.shape, p.dtype) for p in parts]
        + [jax.ShapeDtypeStruct((NDEV,) + g.shape, g.dtype) for g in gathers],
        scratch_shapes=[pltpu.VMEM((D, CW), F32)] + (_comm_scratch(n) if n else []),
        compiler_params=_cp(("arbitrary", "arbitrary")),
    )(xb, dzb, *parts, *gathers)


def dw_out(ycat, drb, layer):
    s = ycat.shape[0]
    tm = min(s, 1024)
    nt = s // tm

    def body(y_ref, dr_ref, o_ref, acc):
        t = pl.program_id(0)

        @pl.when(t == 0)
        def _():
            acc[...] = jnp.zeros_like(acc)

        acc[...] += _dot_tn(y_ref[...], dr_ref[...])

        @pl.when(t == nt - 1)
        def _():
            o_ref[...] = acc[...].reshape(NDEV, RW, D).astype(BF16)

    return pl.pallas_call(
        body, name=f"dw_out_{layer}", grid=(nt,),
        in_specs=[pl.BlockSpec((tm, DMIX), lambda t: (t, 0)), pl.BlockSpec((tm, D), lambda t: (t, 0))],
        out_specs=pl.BlockSpec((NDEV, RW, D), lambda t: (0, 0, 0)),
        out_shape=jax.ShapeDtypeStruct((NDEV, RW, D), BF16),
        scratch_shapes=[pltpu.VMEM((DMIX, D), F32)],
        compiler_params=_cp(("arbitrary",)),
    )(ycat, drb)


def _cols(j):
    return slice(j * D, (j + 1) * D)


def mix_even_bwd(z, dycat, pooled, cv, pwb, pscale, sw, layer, parts=None):
    s = z.shape[0]
    tm = min(s, 256)
    nt = s // tm
    carry = parts is not None

    def body(*refs):
        z_ref, dy_ref, pb_ref, cv_ref, pw_ref, ps_ref, sw_ref = refs[:7]
        if carry:
            p_ref = refs[7]
            dz_ref, dpw_ref, dps_ref, dsw_ref, dsb_ref, land_ref = refs[8:14]
            edp, edc, ya0, t1s, dpl, la, lb = refs[14:21]
            start, finish = _exchange_plan(p_ref, land_ref, refs[-3:], 0)
        else:
            dz_ref, dpw_ref, dps_ref, dsw_ref, dsb_ref = refs[7:12]
            edp, edc, ya0, t1s, dpl, la, lb = refs[12:19]
        i = pl.program_id(0)
        if carry:
            pl.when(i == 0)(start)

        @pl.when(i == 0)
        def _():
            edp[tm:tm + HP, :] = jnp.zeros((HP, D), F32)
            edc[tm:tm + HS, :] = jnp.zeros((HS, D), F32)
            dpw_ref[...] = jnp.zeros_like(dpw_ref)
            dps_ref[...] = jnp.zeros_like(dps_ref)
            dsw_ref[...] = jnp.zeros_like(dsw_ref)
            dsb_ref[...] = jnp.zeros_like(dsb_ref)

        blocks = [(r * RC, (r + 1) * RC, c * CB, (c + 1) * CB) for r in range(tm // RC) for c in range(D // CB)]
        for gi in range(len(POOL_WINDOWS)):
            c0, c1 = gi * PG, (gi + 1) * PG
            ya0[:, c0:c1] = _dot(pb_ref[:, c0:c1], pw_ref[gi])
        for r0, r1, c0, c1 in blocks:
            ga = z_ref[r0:r1, D + c0:D + c1].astype(F32)
            sg = _sig(ga)
            sil = ga * sg
            dya = dy_ref[r0:r1, c0:c1].astype(F32)
            y0 = ya0[r0:r1, c0:c1]
            ps = ps_ref[:, c0:c1]
            dps_ref[:, c0:c1] += _rowsum(dya * y0 * sil)
            dz_ref[r0:r1, D + c0:D + c1] = (dya * y0 * ps * (sg * (1.0 + ga * (1.0 - sg)))).astype(BF16)
            t1s[r0:r1, c0:c1] = (dya * ps * sil).astype(BF16)
        for gi in range(len(POOL_WINDOWS)):
            c0, c1 = gi * PG, (gi + 1) * PG
            dpl[:, c0:c1] = _dot_nt(t1s[:, c0:c1], pw_ref[gi])
            dpw_ref[gi] += _dot_tn(pb_ref[:, c0:c1], t1s[:, c0:c1])
        for r in range(tm // RC):
            r0, r1 = r * RC, (r + 1) * RC
            pos = ((nt - 1 - i) * tm + r0 + lax.broadcasted_iota(jnp.int32, (RC, 1), 0) + 1).astype(F32)
            for gi, w in enumerate(POOL_WINDOWS):
                c0, c1 = gi * PG, (gi + 1) * PG
                edp[r0:r1, c0:c1] = dpl[r0:r1, c0:c1] / jnp.minimum(pos, float(w))

        def level(src, dst, shift, c0, end):
            for r0 in range(0, end, 2 * RC):
                r1 = min(r0 + 2 * RC, end)
                dst[r0:r1, c0:D] = src[r0:r1, c0:D] + src[r0 + shift:r1 + shift, c0:D]

        level(edp, la, 1, PG, tm + 24)
        level(la, lb, 2, 2 * PG, tm + 16)
        level(lb, la, 4, 3 * PG, tm + 8)
        last = ((edp, 1), (la, 2), (lb, 4), (la, 8))
        for r in range(tm // RC):
            r0, r1 = r * RC, (r + 1) * RC
            for gi in range(len(POOL_WINDOWS)):
                c0, c1 = gi * PG, (gi + 1) * PG
                src, shift = last[gi]
                acc = src[r0:r1, c0:c1] + src[r0 + shift:r1 + shift, c0:c1] - dpl[r0:r1, c0:c1]
                dz_ref[r0:r1, c0:c1] = acc.astype(BF16)

        for r0, r1, c0, c1 in blocks:
            gb = z_ref[r0:r1, 5 * D + c0:5 * D + c1].astype(F32)
            sgb = _sig(gb)
            silb = gb * sgb
            dyb = dy_ref[r0:r1, D + c0:D + c1].astype(F32)
            cvv = cv_ref[r0:r1, c0:c1]
            bg = z_ref[r0:r1, 3 * D + c0:3 * D + c1].astype(F32)
            dz_ref[r0:r1, 3 * D + c0:3 * D + c1] = (dyb * cvv * silb).astype(BF16)
            dz_ref[r0:r1, 5 * D + c0:5 * D + c1] = (dyb * bg * cvv * (sgb * (1.0 + gb * (1.0 - sgb)))).astype(BF16)
            dcv = dyb * bg * silb
            dsb_ref[:, c0:c1] += _rowsum(dcv)
            edc[r0:r1, c0:c1] = dcv
        for r0, r1, c0, c1 in blocks:
            h = z_ref[r0:r1, 2 * D + c0:2 * D + c1].astype(F32)
            cg = z_ref[r0:r1, 4 * D + c0:4 * D + c1].astype(F32)
            q = cg * h
            d0 = edc[r0:r1, c0:c1]
            d1 = edc[r0 + 1:r1 + 1, c0:c1]
            d2 = edc[r0 + 2:r1 + 2, c0:c1]
            dq = sw_ref[2:3, c0:c1] * d0 + sw_ref[1:2, c0:c1] * d1 + sw_ref[0:1, c0:c1] * d2
            dsw_ref[2:3, c0:c1] += _rowsum(q * d0)
            dsw_ref[1:2, c0:c1] += _rowsum(q * d1)
            dsw_ref[0:1, c0:c1] += _rowsum(q * d2)
            dz_ref[r0:r1, 4 * D + c0:4 * D + c1] = (dq * h).astype(BF16)
            dz_ref[r0:r1, 2 * D + c0:2 * D + c1] = (dq * cg).astype(BF16)
        edp[tm:tm + HP, :] = edp[0:HP, :]
        edc[tm:tm + HS, :] = edc[0:HS, :]
        if carry:
            pl.when(i == nt - 1)(finish)

    rev = lambda i: (nt - 1 - i, 0)
    vec = pl.BlockSpec((1, D), lambda i: (0, 0))
    in_specs = [pl.BlockSpec((tm, NIN), rev), pl.BlockSpec((tm, DMIX), rev),
                pl.BlockSpec((tm, D), rev), pl.BlockSpec((tm, D), rev),
                pl.BlockSpec((4, PG, PG), lambda i: (0, 0, 0)), vec,
                pl.BlockSpec((KS, D), lambda i: (0, 0))]
    out_specs = [pl.BlockSpec((tm, NIN), rev), pl.BlockSpec((4, PG, PG), lambda i: (0, 0, 0)), vec,
                 pl.BlockSpec((KS, D), lambda i: (0, 0)), vec]
    out_shape = [jax.ShapeDtypeStruct((s, NIN), BF16), jax.ShapeDtypeStruct((4, PG, PG), F32),
                 jax.ShapeDtypeStruct((1, D), F32), jax.ShapeDtypeStruct((KS, D), F32),
                 jax.ShapeDtypeStruct((1, D), F32)]
    args = [z, dycat, pooled, cv, pwb, pscale, sw]
    scratch = [pltpu.VMEM((tm + HP, D), F32), pltpu.VMEM((tm + HS, D), F32), pltpu.VMEM((tm, D), F32),
               pltpu.VMEM((tm, D), BF16), pltpu.VMEM((tm, D), F32), pltpu.VMEM((tm + HP, D), F32),
               pltpu.VMEM((tm + HP, D), F32)]
    if carry:
        in_specs.append(ANY)
        args.append(parts)
        out_specs.append(ANY)
        out_shape.append(jax.ShapeDtypeStruct(parts.shape, parts.dtype))
        scratch += _comm_scratch(1)
    return pl.pallas_call(
        body, name=f"mix_even_bwd_{layer}", grid=(nt,), in_specs=in_specs, out_specs=out_specs,
        out_shape=out_shape, scratch_shapes=scratch, compiler_params=_cp(("arbitrary",)),
    )(*args)


def _ln_rows(v):
    mu = jnp.mean(v, axis=-1, keepdims=True)
    d = v - mu
    var = jnp.mean(d * d, axis=-1, keepdims=True)
    rstd = lax.rsqrt(var + LN_EPS)
    return d * rstd, rstd


def _ln_rows_bwd(dn, xh, rstd, g):
    dxh = dn * g
    m1 = jnp.mean(dxh, axis=-1, keepdims=True)
    m2 = jnp.mean(dxh * xh, axis=-1, keepdims=True)
    return rstd * (dxh - m1 - xh * m2)


def _shifted_copies(ext, shifted, rows):
    for b in range(1, SUB):
        shifted[b - 1] = ext[b:b + rows, :]


def _tap(ext, shifted, off, tm):
    a, b = off // SUB, off % SUB
    if b == 0:
        return ext[SUB * a:SUB * a + tm, :]
    return shifted[b - 1, SUB * a:SUB * a + tm, :]


def mix_odd_bwd(z, dycat, cz, slg, slb, wsb, wstb, sbcol, dw, dng, dnb, layer, parts):
    s = z.shape[0]
    tm = min(s, 256)
    nt = s // tm

    def body(z_ref, dy_ref, cz_ref, slg_ref, slb_ref, ws_ref, wst_ref, sb_ref, dw_ref, dng_ref, dnb_ref, p_ref,
             dz_ref, dslg_ref, dslb_ref, dws_ref, dsb_ref, ddw_ref, ddcb_ref, ddng_ref, ddnb_ref, land_ref,
             dvn, edz, esh, *sems):
        i = pl.program_id(0)
        start, finish = _exchange_plan(p_ref, land_ref, sems, 0)
        pl.when(i == 0)(start)

        @pl.when(i == 0)
        def _():
            edz[tm:tm + HC, :] = jnp.zeros((HC, D), F32)
            for ref in (dslg_ref, dslb_ref, dws_ref, dsb_ref, ddw_ref, ddcb_ref, ddng_ref, ddnb_ref):
                ref[...] = jnp.zeros_like(ref)

        vh, vrs = _ln_rows(z_ref[:, _cols(1)].astype(F32))
        vnb = (vh * slg_ref[...] + slb_ref[...]).astype(BF16)
        for n in range(tm // SGU_BLOCK):
            r0, r1 = n * SGU_BLOCK, (n + 1) * SGU_BLOCK
            for hd in range(HEADS):
                c0, c1 = hd * HD, (hd + 1) * HD
                vblk = vnb[r0:r1, c0:c1]
                sv = _dot(ws_ref[hd], vblk) + sb_ref[hd]
                gc = z_ref[r0:r1, 2 * D + c0:2 * D + c1].astype(F32)
                sg = _sig(gc)
                sil = gc * sg
                u = z_ref[r0:r1, c0:c1].astype(F32)
                dyc = dy_ref[r0:r1, c0:c1].astype(F32)
                dz_ref[r0:r1, c0:c1] = (dyc * sv * sil).astype(BF16)
                dz_ref[r0:r1, 2 * D + c0:2 * D + c1] = (dyc * u * sv * (sg * (1.0 + gc * (1.0 - sg)))).astype(BF16)
                dsv = dyc * u * sil
                dsb_ref[hd] += jnp.sum(dsv, axis=-1, keepdims=True)
                dsvb = dsv.astype(BF16)
                dws_ref[hd] += _dot_nt(dsvb, vblk)
                dvn[r0:r1, c0:c1] = _dot(wst_ref[hd], dsvb)
        dv = dvn[...]
        dslg_ref[...] += _rowsum(dv * vh)
        dslb_ref[...] += _rowsum(dv)
        dz_ref[:, _cols(1)] = _ln_rows_bwd(dv, vh, vrs, slg_ref[...]).astype(BF16)

        zh, zrs = _ln_rows(cz_ref[...])
        zn = zh * dng_ref[...] + dnb_ref[...]
        sgn = _sig(zn)
        gd = z_ref[:, _cols(5)].astype(F32)
        sgd = _sig(gd)
        dyd = dy_ref[:, D:2 * D].astype(F32)
        dz_ref[:, _cols(5)] = (dyd * (zn * sgn) * (sgd * (1.0 + gd * (1.0 - sgd)))).astype(BF16)
        dzn = dyd * (gd * sgd) * (sgn * (1.0 + zn * (1.0 - sgn)))
        ddng_ref[...] += _rowsum(dzn * zh)
        ddnb_ref[...] += _rowsum(dzn)
        dcz = _ln_rows_bwd(dzn, zh, zrs, dng_ref[...])
        ddcb_ref[...] += _rowsum(dcz)
        edz[0:tm, :] = dcz
        _shifted_copies(edz, esh, tm + HC - SUB)
        a = z_ref[:, _cols(3)].astype(F32)
        sgb = _sig(z_ref[:, _cols(4)].astype(F32))
        zg = a * sgb
        dzg = jnp.zeros((tm, D), F32)
        for k in range(KD):
            sh = _tap(edz, esh, KD - 1 - k, tm)
            dzg = dzg + dw_ref[k:k + 1, :] * sh
            ddw_ref[k:k + 1, :] += _rowsum(zg * sh)
        dz_ref[:, _cols(3)] = (dzg * sgb).astype(BF16)
        dz_ref[:, _cols(4)] = (dzg * a * sgb * (1.0 - sgb)).astype(BF16)
        edz[tm:tm + HC, :] = edz[0:HC, :]
        pl.when(i == nt - 1)(finish)

    rev = lambda i: (nt - 1 - i, 0)
    vec = pl.BlockSpec((1, D), lambda i: (0, 0))
    wspec = pl.BlockSpec((HEADS, SGU_BLOCK, SGU_BLOCK), lambda i: (0, 0, 0))
    bspec = pl.BlockSpec((HEADS, SGU_BLOCK, 1), lambda i: (0, 0, 0))
    kspec = pl.BlockSpec((KD, D), lambda i: (0, 0))
    return pl.pallas_call(
        body, name=f"mix_odd_bwd_{layer}", grid=(nt,),
        in_specs=[pl.BlockSpec((tm, NIN), rev), pl.BlockSpec((tm, DMIX), rev), pl.BlockSpec((tm, D), rev),
                  vec, vec, wspec, wspec, bspec, kspec, vec, vec, ANY],
        out_specs=[pl.BlockSpec((tm, NIN), rev), vec, vec, wspec, bspec, kspec, vec, vec, vec, ANY],
        out_shape=[jax.ShapeDtypeStruct((s, NIN), BF16), jax.ShapeDtypeStruct((1, D), F32),
                   jax.ShapeDtypeStruct((1, D), F32),
                   jax.ShapeDtypeStruct((HEADS, SGU_BLOCK, SGU_BLOCK), F32),
                   jax.ShapeDtypeStruct((HEADS, SGU_BLOCK, 1), F32), jax.ShapeDtypeStruct((KD, D), F32),
                   jax.ShapeDtypeStruct((1, D), F32), jax.ShapeDtypeStruct((1, D), F32),
                   jax.ShapeDtypeStruct((1, D), F32), jax.ShapeDtypeStruct(parts.shape, parts.dtype)],
        scratch_shapes=[pltpu.VMEM((tm, D), F32), pltpu.VMEM((tm + HC, D), F32),
                        pltpu.VMEM((SUB - 1, tm + HC - SUB, D), F32)] + _comm_scratch(1),
        compiler_params=_cp(("arbitrary",)),
    )(z, dycat, cz, slg, slb, wsb, wstb, sbcol, dw, dng, dnb, parts)


def _adamw_math(w, g, m, v):
    m = ADAM_B1 * m + (1.0 - ADAM_B1) * g
    v = ADAM_B2 * v + (1.0 - ADAM_B2) * (g * g)
    m_hat = m / (1.0 - ADAM_B1 ** ADAM_STEP)
    v_hat = v / (1.0 - ADAM_B2 ** ADAM_STEP)
    delta = -ADAM_LR * (m_hat / (jnp.sqrt(v_hat) + ADAM_EPS) + ADAM_WD * w)
    return delta, m, v


def adamw_big(w, m, v, parts0, parts1, name):
    _, r, c = w.shape
    tr = min(r, 256)
    nr = r // tr

    def body(w_ref, m_ref, v_ref, p0_ref, p1_ref, g_ref, d_ref, nm_ref, nv_ref):
        i = pl.program_id(0)

        def total(p_ref):
            acc = p_ref[0].astype(F32)
            for j in range(1, NDEV):
                acc = acc + p_ref[j].astype(F32)
            return acc

        @pl.when(i == 0)
        def _():
            g_ref[...] = total(p0_ref)

        @pl.when(i == 1)
        def _():
            g_ref[...] = total(p1_ref)

        delta, nm, nv = _adamw_math(w_ref[...], g_ref[...], m_ref[...], v_ref[...])
        d_ref[...] = delta
        nm_ref[...] = nm
        nv_ref[...] = nv

    wspec = pl.BlockSpec((None, tr, c), lambda i, j: (i, j, 0))
    p0 = pl.BlockSpec((NDEV, tr, c), lambda i, j: (0, jnp.where(i == 0, j, nr - 1), 0))
    p1 = pl.BlockSpec((NDEV, tr, c), lambda i, j: (0, jnp.where(i == 1, j, 0), 0))
    shp = jax.ShapeDtypeStruct(w.shape, F32)
    return pl.pallas_call(
        body, name=name, grid=(2, nr), in_specs=[wspec, wspec, wspec, p0, p1],
        out_specs=[wspec] * 4, out_shape=[shp] * 4,
        compiler_params=_cp(("arbitrary", "arbitrary")),
    )(w, m, v, parts0, parts1)


def sum_parts(parts, name):
    _, r, c = parts.shape

    def body(p_ref, o_ref):
        acc = p_ref[0]
        for j in range(1, NDEV):
            acc = acc + p_ref[j]
        o_ref[...] = acc

    return pl.pallas_call(body, name=name, out_shape=jax.ShapeDtypeStruct((r, c), F32),
                          compiler_params=_cp())(parts)


def adamw_small(w, g, m, v, name):
    def body(w_ref, g_ref, m_ref, v_ref, d_ref, nm_ref, nv_ref):
        delta, nm, nv = _adamw_math(w_ref[...], g_ref[...], m_ref[...], v_ref[...])
        d_ref[...] = delta
        nm_ref[...] = nm
        nv_ref[...] = nv

    shp = jax.ShapeDtypeStruct(w.shape, F32)
    return pl.pallas_call(body, name=name, out_shape=[shp] * 3, compiler_params=_cp())(w, g, m, v)


def _size(shape):
    n = 1
    for d in shape:
        n *= d
    return n


def _pack(arrays, rows):
    flat = jnp.concatenate([a.reshape(-1) for a in arrays])
    return jnp.pad(flat, (0, rows * 128 - flat.shape[0])).reshape(rows, 128)


def _unpack(flat, shapes):
    out, o = [], 0
    for shp in shapes:
        out.append(flat[o:o + _size(shp)].reshape(shp))
        o += _size(shp)
    return out


def _rows_for(shapes):
    return -(-sum(_size(shp) for shp in shapes) // 1024) * 8


SHARDED_SMALL = (("pool_w", (2, 4, 256, 256), 2), ("sconv_w", (2, KS, D), 2), ("sgu_ln_g", (2, D), 1),
                 ("sgu_ln_b", (2, D), 1), ("dconv_w", (2, KD, D), 2), ("dconv_b", (2, D), 1),
                 ("dnorm_g", (2, D), 1), ("dnorm_b", (2, D), 1))
REPLICATED_SMALL = (("ln_g", (DEPTH, D)), ("ln_b", (DEPTH, D)), ("pool_scale", (2, D)), ("sconv_b", (2, D)),
                    ("sgu_w", (2, HEADS, SGU_BLOCK, SGU_BLOCK)), ("sgu_b", (2, HEADS, SGU_BLOCK)))


def _shard_shape(shape, axis):
    return tuple(d // NDEV if a == axis else d for a, d in enumerate(shape))


def _merge_gathered(g, shape, axis):
    return jnp.moveaxis(g, 0, axis).reshape(shape)


def kernel(x, ln_g, ln_b, w_in_even, w_out_even, pool_w, pool_scale, sconv_w, sconv_b, w_in_odd, w_out_odd, sgu_ln_g, sgu_ln_b, sgu_w, sgu_b, dconv_w, dconv_b, dnorm_g, dnorm_b, loss_target, m_ln_g, m_ln_b, m_w_in_even, m_w_out_even, m_pool_w, m_pool_scale, m_sconv_w, m_sconv_b, m_w_in_odd, m_w_out_odd, m_sgu_ln_g, m_sgu_ln_b, m_sgu_w, m_sgu_b, m_dconv_w, m_dconv_b, m_dnorm_g, m_dnorm_b, v_ln_g, v_ln_b, v_w_in_even, v_w_out_even, v_pool_w, v_pool_scale, v_sconv_w, v_sconv_b, v_w_in_odd, v_w_out_odd, v_sgu_ln_g, v_sgu_ln_b, v_sgu_w, v_sgu_b, v_dconv_w, v_dconv_b, v_dnorm_g, v_dnorm_b):
    given = dict(locals())
    me = 4 * lax.axis_index("x") + 2 * lax.axis_index("y") + lax.axis_index("c")
    xs = x[0]
    target = loss_target[0]

    wc_in = cast_weights(w_in_even, w_in_odd, "cast_w_in")
    wc_out = cast_weights(w_out_even, w_out_odd, "cast_w_out")
    shard_shapes = [_shard_shape(shp, ax) for _, shp, ax in SHARDED_SMALL]
    srows = _rows_for(shard_shapes)
    xb, wg_first, gathered = cast_x(xs, wc_in, _pack([given[n] for n, _, _ in SHARDED_SMALL], srows))
    wg_in = [wg_first] + [None] * (DEPTH - 1)
    wg_out = [None] * DEPTH
    gathered = gathered.reshape(NDEV, -1)
    full, o = {}, 0
    for (n, shp, ax), sshp in zip(SHARDED_SMALL, shard_shapes):
        full[n] = _merge_gathered(gathered[:, o:o + _size(sshp)].reshape((NDEV,) + sshp), shp, ax)
        o += _size(sshp)

    mask = (jnp.arange(SGU_BLOCK)[None, :] // 64) <= (jnp.arange(SGU_BLOCK)[:, None] // 64)
    ws = jnp.where(mask[None, None], sgu_w, 0.0)
    wsb = ws.astype(BF16)
    wstb = jnp.swapaxes(ws, -1, -2).astype(BF16)
    pwb = full["pool_w"].astype(BF16)

    ones = jnp.ones((1, D), F32)
    zeros = jnp.zeros((1, D), F32)

    xres, gp, bp = xs, ones, zeros
    saved = []
    for layer in range(DEPTH):
        i = layer // 2
        if layer % 2 == 0:
            params = (pwb[i], pool_scale[i][None], full["sconv_w"][i], sconv_b[i][None])
        else:
            params = (full["sgu_ln_g"][i][None], full["sgu_ln_b"][i][None], wsb[i], sgu_b[i][:, :, None],
                      full["dconv_w"][i], full["dconv_b"][i][None], full["dnorm_g"][i][None],
                      full["dnorm_b"][i][None])
        outs = proj_in(xb, wg_in[layer], layer, wc_in, wc_out)
        z, wg_out[layer] = outs[0], outs[1]
        if layer + 1 < DEPTH:
            wg_in[layer + 1] = outs[2]
        outs = mix_fwd(z, layer, params)
        ycat, extra = outs[0], tuple(outs[1:])
        g, b = ln_g[layer][None], ln_b[layer][None]
        outs = proj_out_ln(ycat, wg_out[layer], layer, xres, gp, bp, g, b,
                           target=target if layer == DEPTH - 1 else None)
        xhat, rstd, xb_next = outs[0], outs[1], outs[2]
        saved.append((xb, z, ycat, extra, xhat, rstd))
        xres, gp, bp, xb = xhat, g, b, xb_next
    dxn, loss_local = outs[3], outs[4]

    gsmall = {n: [None] * shp[0] for n, shp in REPLICATED_SMALL}
    gsmall.update({n: [None] * shp[0] for n, shp, _ in SHARDED_SMALL})
    parts_in, parts_out, parts_pw = [None] * DEPTH, [None] * DEPTH, [None] * 2
    small_names = [n for n, _ in REPLICATED_SMALL] + [n for n, _, _ in SHARDED_SMALL]
    gathered_names = [n for n in small_names if n != "pool_w"]
    gathered_shapes = [shp for n, shp in REPLICATED_SMALL] + [shp for n, shp, _ in SHARDED_SMALL if n != "pool_w"]
    grows = _rows_for(gathered_shapes + [(1,)])
    pending = None
    for layer in reversed(range(DEPTH)):
        i = layer // 2
        xb, z, ycat, extra, xhat, rstd = saved[layer]
        dr, drb, dycat, dg, db = ln_bwd_dycat(dxn, xhat, rstd, ln_g[layer][None], wg_out[layer], layer)
        gsmall["ln_g"][layer], gsmall["ln_b"][layer] = dg[0], db[0]
        dwo = dw_out(ycat, drb, layer)
        own_out = pending is None
        riding = dwo if own_out else pending
        if layer % 2 == 0:
            dzb, dpw, dps, dsw, dsb, landed = mix_even_bwd(
                z, dycat, extra[0], extra[1], pwb[i], pool_scale[i][None], full["sconv_w"][i], layer, parts=riding)
            gsmall["pool_scale"][i] = dps[0]
            gsmall["sconv_w"][i], gsmall["sconv_b"][i] = dsw, dsb[0]
        else:
            dzb, dslg, dslb, dws, dsbc, ddw, ddcb, ddng, ddnb, landed = mix_odd_bwd(
                z, dycat, extra[0], full["sgu_ln_g"][i][None], full["sgu_ln_b"][i][None], wsb[i], wstb[i],
                sgu_b[i][:, :, None], full["dconv_w"][i], full["dnorm_g"][i][None], full["dnorm_b"][i][None],
                layer, riding)
            gsmall["sgu_ln_g"][i], gsmall["sgu_ln_b"][i] = dslg[0], dslb[0]
            gsmall["sgu_w"][i], gsmall["sgu_b"][i] = jnp.where(mask[None], dws, 0.0), dsbc[:, :, 0]
            gsmall["dconv_w"][i], gsmall["dconv_b"][i] = ddw, ddcb[0]
            gsmall["dnorm_g"][i], gsmall["dnorm_b"][i] = ddng[0], ddnb[0]
        riders = [] if own_out else [dwo]
        if layer % 2 == 0:
            riders.append(dpw.reshape(4, NDEV, PG // NDEV, PG).transpose(1, 0, 2, 3))
        small_partial = []
        if layer == 0:
            small_partial.append(_pack([jnp.stack(gsmall[n]) for n in gathered_names] + [loss_local], grows))
        outs = list(dw_in(xb, dzb, layer, riders, small_partial))
        if layer == 0:
            small_gathered = outs.pop()
        if layer % 2 == 0:
            parts_pw[i] = outs.pop()
        if own_out:
            parts_out[layer] = landed
        else:
            parts_in[layer + 1] = landed
            parts_out[layer] = outs[1]
        pending = outs[0]
        if layer == 0:
            dxn, parts_in[0] = dx_in(dzb, wg_in[layer], layer, dr, parts=pending)
        else:
            dxn = dx_in(dzb, wg_in[layer], layer, dr)[0]
    grad_x = dxn[None]

    big = {}
    big["w_in_even"] = adamw_big(w_in_even, m_w_in_even, v_w_in_even, parts_in[0], parts_in[2], "adamw_w_in_even")
    big["w_in_odd"] = adamw_big(w_in_odd, m_w_in_odd, v_w_in_odd, parts_in[1], parts_in[3], "adamw_w_in_odd")
    big["w_out_even"] = adamw_big(w_out_even, m_w_out_even, v_w_out_even, parts_out[0], parts_out[2], "adamw_w_out_even")
    big["w_out_odd"] = adamw_big(w_out_odd, m_w_out_odd, v_w_out_odd, parts_out[1], parts_out[3], "adamw_w_out_odd")

    gsum = sum_parts(small_gathered, "sum_small_grads")
    unpacked = _unpack(gsum.reshape(-1), gathered_shapes + [()])
    loss = unpacked.pop()
    gfull = dict(zip(gathered_names, unpacked))
    own_shapes = [shp for _, shp in REPLICATED_SMALL] + shard_shapes
    gown = {n: gfull[n] for n, _ in REPLICATED_SMALL}
    for (n, shp, ax), sshp in zip(SHARDED_SMALL, shard_shapes):
        if n == "pool_w":
            gown[n] = jnp.stack([sum_parts(p.reshape(NDEV, -1, 128), f"sum_pool_w_{j}").reshape(sshp[1:])
                                 for j, p in enumerate(parts_pw)])
        else:
            gown[n] = lax.dynamic_slice_in_dim(gfull[n], me * sshp[ax], sshp[ax], axis=ax)
    orows = _rows_for(own_shapes)
    packed = [_pack([src[n] for n in small_names], orows) for src in
              (given, gown, {n: given["m_" + n] for n in small_names}, {n: given["v_" + n] for n in small_names})]
    sd, sm, sv = adamw_small(*packed, "adamw_small")
    small = {}
    for n, d_, m_, v_ in zip(small_names, _unpack(sd.reshape(-1), own_shapes), _unpack(sm.reshape(-1), own_shapes),
                             _unpack(sv.reshape(-1), own_shapes)):
        small[n] = (gown[n], d_, m_, v_)

    weights = ['ln_g', 'ln_b', 'w_in_even', 'w_out_even', 'pool_w', 'pool_scale', 'sconv_w', 'sconv_b', 'w_in_odd',
               'w_out_odd', 'sgu_ln_g', 'sgu_ln_b', 'sgu_w', 'sgu_b', 'dconv_w', 'dconv_b', 'dnorm_g', 'dnorm_b']
    res = {n: (big[n] if n in big else small[n]) for n in weights}
    return (loss, grad_x, *[res[n][0] for n in weights], *[res[n][1] for n in weights],
            *[res[n][2] for n in weights], *[res[n][3] for n in weights])
```

```python
import jax
import jax.numpy as jnp
from jax import lax
from jax.experimental import pallas as pl
from jax.experimental.pallas import tpu as pltpu

F32 = jnp.float32
BF16 = jnp.bfloat16

D = 1024
DMIX = 2048
NIN = 6144
NDEV = 8
CW = NIN // NDEV
RW = DMIX // NDEV
DEPTH = 4
ALPHA = (2 * DEPTH) ** 0.25
LN_EPS = 1e-5
POOL_WINDOWS = (2, 4, 8, 16)
PG = 256
SGU_BLOCK = 128
HEADS = 4
HD = 256
KD = 31
KS = 3
SUB = 8
RC = 16
CB = 512
HP = 32
HS = 8
HC = 32
NSEM = 7

ADAM_LR = 0.001
ADAM_B1 = 0.9
ADAM_B2 = 0.999
ADAM_EPS = 1e-08
ADAM_WD = 0.01
ADAM_STEP = 10

VMEM_LIMIT = 56 * 1024 * 1024
MESH = pl.DeviceIdType.MESH
ANY = pl.BlockSpec(memory_space=pl.ANY)


def _cp(sem=None):
    if sem is None:
        return pltpu.CompilerParams(vmem_limit_bytes=VMEM_LIMIT)
    return pltpu.CompilerParams(dimension_semantics=sem, vmem_limit_bytes=VMEM_LIMIT)


def _sig(x):
    return 0.5 * jnp.tanh(0.5 * x) + 0.5


def _dot(a, b):
    return jnp.dot(a, b, preferred_element_type=F32)


def _dot_nt(a, b):
    return lax.dot_general(a, b, (((1,), (1,)), ((), ())), preferred_element_type=F32)


def _dot_tn(a, b):
    return lax.dot_general(a, b, (((0,), (0,)), ((), ())), preferred_element_type=F32)


def _rowsum(x):
    return jnp.sum(x, axis=0, keepdims=True)


def _comm_scratch(n):
    return [pltpu.SemaphoreType.DMA((n * NSEM,)), pltpu.SemaphoreType.DMA((n * NSEM,)),
            pltpu.SemaphoreType.DMA((n,))]


def _gather_plan(src_ref, out_ref, sems, n):
    send_sems, recv_sems, local_sems = sems
    base = n * NSEM
    x, y, c = lax.axis_index("x"), lax.axis_index("y"), lax.axis_index("c")
    me, sibling = (x, y, c), (x, y, 1 - c)
    chips = [(1 - x, y), (x, 1 - y), (1 - x, 1 - y)]

    def slot(px, py, pc):
        return out_ref.at[4 * px + 2 * py + pc]

    def copy(k, blk, to, src=None):
        return pltpu.make_async_remote_copy(
            src_ref=slot(*blk) if src is None else src, dst_ref=slot(*blk),
            send_sem=send_sems.at[base + k], recv_sem=recv_sems.at[base + k],
            device_id=to, device_id_type=MESH)

    def mine():
        return pltpu.make_async_copy(src_ref, slot(*me), local_sems.at[n])

    def start():
        mine().start()
        copy(0, me, sibling, src=src_ref).start()
        for j, chip in enumerate(chips):
            copy(1 + j, me, (*chip, c), src=src_ref).start()

    def finish():
        for j, chip in enumerate(chips):
            copy(1 + j, (*chip, c), me).wait_recv()
            copy(4 + j, (*chip, c), sibling).start()
        copy(0, sibling, me).wait_recv()
        for j, chip in enumerate(chips):
            copy(4 + j, (*chip, 1 - c), me).wait_recv()
        copy(0, me, sibling, src=src_ref).wait_send()
        for j, chip in enumerate(chips):
            copy(1 + j, me, (*chip, c), src=src_ref).wait_send()
            copy(4 + j, (*chip, c), sibling).wait_send()
        mine().wait()

    return start, finish


def _exchange_plan(p_ref, out_ref, sems, n):
    send_sems, recv_sems, local_sems = sems
    base = n * NSEM
    x, y, c = lax.axis_index("x"), lax.axis_index("y"), lax.axis_index("c")
    me = 4 * x + 2 * y + c

    def copy(r, landing):
        px, py, pc = x ^ (r >> 2), y ^ ((r >> 1) & 1), c ^ (r & 1)
        peer = 4 * px + 2 * py + pc
        return pltpu.make_async_remote_copy(
            src_ref=p_ref.at[peer], dst_ref=out_ref.at[peer if landing else me],
            send_sem=send_sems.at[base + r - 1], recv_sem=recv_sems.at[base + r - 1],
            device_id=(px, py, pc), device_id_type=MESH)

    def mine():
        return pltpu.make_async_copy(p_ref.at[me], out_ref.at[me], local_sems.at[n])

    def start():
        mine().start()
        for r in range(1, NDEV):
            copy(r, False).start()

    def finish():
        for r in range(1, NDEV):
            copy(r, True).wait_recv()
        for r in range(1, NDEV):
            copy(r, False).wait_send()
        mine().wait()

    return start, finish


def cast_x(x, w_first, small):
    s = x.shape[0]
    tm = min(s, 512)
    nt = s // tm

    def body(x_ref, wf_ref, sm_ref, o_ref, gw_ref, gs_ref, *sems):
        i = pl.program_id(0)
        plans = [_gather_plan(wf_ref.at[0], gw_ref, sems, 0), _gather_plan(sm_ref, gs_ref, sems, 1)]

        @pl.when(i == 0)
        def _():
            for start, _ in plans:
                start()

        o_ref[...] = x_ref[...].astype(BF16)

        @pl.when(i == nt - 1)
        def _():
            for _, finish in plans:
                finish()

    return pl.pallas_call(
        body, name="cast_x", grid=(nt,),
        in_specs=[pl.BlockSpec((tm, D), lambda i: (i, 0)), ANY, ANY],
        out_specs=[pl.BlockSpec((tm, D), lambda i: (i, 0)), ANY, ANY],
        out_shape=[jax.ShapeDtypeStruct((s, D), BF16),
                   jax.ShapeDtypeStruct((NDEV,) + w_first.shape[1:], w_first.dtype),
                   jax.ShapeDtypeStruct((NDEV,) + small.shape, small.dtype)],
        scratch_shapes=_comm_scratch(2), compiler_params=_cp(("arbitrary",)),
    )(x, w_first, small)


def cast_weights(w_even, w_odd, name):
    _, r, c = w_even.shape

    def body(e_ref, o_ref, out_ref):
        layer = pl.program_id(0)

        @pl.when(layer % 2 == 0)
        def _():
            out_ref[...] = e_ref[...].astype(BF16)

        @pl.when(layer % 2 == 1)
        def _():
            out_ref[...] = o_ref[...].astype(BF16)

    spec = pl.BlockSpec((None, r, c), lambda l: (l // 2, 0, 0))
    return pl.pallas_call(
        body, name=name, grid=(DEPTH,), in_specs=[spec, spec],
        out_specs=pl.BlockSpec((None, r, c), lambda l: (l, 0, 0)),
        out_shape=jax.ShapeDtypeStruct((DEPTH, r, c), BF16), compiler_params=_cp(("parallel",)),
    )(w_even, w_odd)


def _deepnorm(y_ref, wo_ref, xr_ref, gp_ref, bp_ref, g_ref, b_ref):
    xin = xr_ref[...] * gp_ref[...] + bp_ref[...]
    r = ALPHA * xin + _dot(y_ref[...], wo_ref[...].reshape(DMIX, D))
    mu = jnp.mean(r, axis=-1, keepdims=True)
    d = r - mu
    var = jnp.mean(d * d, axis=-1, keepdims=True)
    rstd = lax.rsqrt(var + LN_EPS)
    xh = d * rstd
    return xh, rstd, xh * g_ref[...] + b_ref[...]


def proj_in(wg, layer, wc_in, wc_out, xb=None, below=None):
    fused = below is not None
    s = below[0].shape[0] if fused else xb.shape[0]
    tm = min(s, 256 if fused else 512)
    nt = s // tm
    nxt = layer + 1 < DEPTH
    nhead = 7 if fused else 1

    def body(*refs):
        head, refs = refs[:nhead], refs[nhead:]
        if nxt:
            w_ref, wco_ref, wci_ref = refs[:3]
            refs = refs[3:]
        else:
            w_ref, wco_ref = refs[:2]
            refs = refs[2:]
        o_ref = refs[0]
        if fused:
            xh_ref, rs_ref, xb_ref = refs[1:4]
            refs = refs[4:]
        else:
            refs = refs[1:]
        go_ref = refs[0]
        sems = refs[-3:]
        i = pl.program_id(0)
        plans = [_gather_plan(wco_ref.at[layer], go_ref, sems, 0)]
        if nxt:
            plans.append(_gather_plan(wci_ref.at[layer + 1], refs[1], sems, 1))

        @pl.when(i == 0)
        def _():
            for start, _ in plans:
                start()

        if fused:
            xh, rstd, xn = _deepnorm(*head)
            xh_ref[...] = xh
            rs_ref[...] = rstd
            x = xn.astype(BF16)
            xb_ref[...] = x
        else:
            x = head[0][...]
        for k in range(NDEV):
            o_ref[:, k * CW:(k + 1) * CW] = _dot(x, w_ref[k]).astype(BF16)

        @pl.when(i == nt - 1)
        def _():
            for _, finish in plans:
                finish()

    row = pl.BlockSpec((tm, D), lambda i: (i, 0))
    vec = pl.BlockSpec((1, D), lambda i: (0, 0))
    if fused:
        in_specs = [pl.BlockSpec((tm, DMIX), lambda i: (i, 0)),
                    pl.BlockSpec((NDEV, RW, D), lambda i: (0, 0, 0), pipeline_mode=pl.Buffered(1)),
                    row, vec, vec, vec, vec]
        args = list(below)
    else:
        in_specs, args = [row], [xb]
    in_specs += [pl.BlockSpec((NDEV, D, CW), lambda i: (0, 0, 0), pipeline_mode=pl.Buffered(1)), ANY]
    args += [wg, wc_out]
    if nxt:
        in_specs.append(ANY)
        args.append(wc_in)
    out_specs = [pl.BlockSpec((tm, NIN), lambda i: (i, 0))]
    out_shape = [jax.ShapeDtypeStruct((s, NIN), BF16)]
    if fused:
        out_specs += [row, pl.BlockSpec((tm, 1), lambda i: (i, 0)), row]
        out_shape += [jax.ShapeDtypeStruct((s, D), F32), jax.ShapeDtypeStruct((s, 1), F32),
                      jax.ShapeDtypeStruct((s, D), BF16)]
    out_specs.append(ANY)
    out_shape.append(jax.ShapeDtypeStruct((NDEV, RW, D), BF16))
    if nxt:
        out_specs.append(ANY)
        out_shape.append(jax.ShapeDtypeStruct((NDEV, D, CW), BF16))
    return pl.pallas_call(
        body, name=f"proj_in_{layer}", grid=(nt,), in_specs=in_specs, out_specs=out_specs,
        out_shape=out_shape, scratch_shapes=_comm_scratch(2 if nxt else 1),
        compiler_params=_cp(("arbitrary",)),
    )(*args)


def mix_fwd(z, layer, params):
    s = z.shape[0]
    tm = min(s, 256)
    even = layer % 2 == 0
    npar = len(params)
    nout = 3 if even else 2

    def even_tile(zr, j, par, outs, scr):
        pw_ref, ps_ref, sw_ref, sb_ref = par
        yc_ref, pb_ref, cv_ref = outs
        exa, eq, ya0, la, lb = scr
        for c in range(tm // RC):
            r0, r1 = c * RC, (c + 1) * RC
            exa[HP + r0:HP + r1, :] = zr[r0:r1, _cols(0)].astype(F32)
            eq[HS + r0:HS + r1, :] = zr[r0:r1, _cols(4)].astype(F32) * zr[r0:r1, _cols(2)].astype(F32)

        def level(src, dst, shift, c0, start):
            for r0 in range(start, HP + tm, 2 * RC):
                r1 = min(r0 + 2 * RC, HP + tm)
                dst[r0:r1, c0:D] = src[r0:r1, c0:D] + src[r0 - shift:r1 - shift, c0:D]

        level(exa, la, 1, PG, 8)
        level(la, lb, 2, 2 * PG, 16)
        level(lb, la, 4, 3 * PG, 24)
        last = ((exa, 1), (la, 2), (lb, 4), (la, 8))
        for c in range(tm // RC):
            r0, r1 = HP + c * RC, HP + (c + 1) * RC
            pos = (j * tm + c * RC + lax.broadcasted_iota(jnp.int32, (RC, 1), 0) + 1).astype(F32)
            for gi, w in enumerate(POOL_WINDOWS):
                c0, c1 = gi * PG, (gi + 1) * PG
                src, shift = last[gi]
                acc = src[r0:r1, c0:c1] + src[r0 - shift:r1 - shift, c0:c1]
                pooled = acc / jnp.minimum(pos, float(w)) - exa[r0:r1, c0:c1]
                pb_ref[c * RC:(c + 1) * RC, c0:c1] = pooled.astype(BF16)
        for gi in range(len(POOL_WINDOWS)):
            c0, c1 = gi * PG, (gi + 1) * PG
            ya0[:, c0:c1] = _dot(pb_ref[:, c0:c1], pw_ref[gi])
        for c in range(tm // RC):
            r0, r1 = c * RC, (c + 1) * RC
            ga = zr[r0:r1, _cols(1)].astype(F32)
            yc_ref[r0:r1, 0:D] = (ya0[r0:r1, :] * ps_ref[...] * (ga * _sig(ga))).astype(BF16)
            cv = (sw_ref[2:3, :] * eq[HS + r0:HS + r1, :] + sw_ref[1:2, :] * eq[HS + r0 - 1:HS + r1 - 1, :]
                  + sw_ref[0:1, :] * eq[HS + r0 - 2:HS + r1 - 2, :] + sb_ref[...])
            cv_ref[r0:r1, :] = cv.astype(BF16)
            gb = zr[r0:r1, _cols(5)].astype(F32)
            yc_ref[r0:r1, D:2 * D] = (zr[r0:r1, _cols(3)].astype(F32) * cv * (gb * _sig(gb))).astype(BF16)
        exa[0:HP, :] = exa[tm:tm + HP, :]
        eq[0:HS, :] = eq[tm:tm + HS, :]

    def odd_tile(zr, j, par, outs, scr):
        slg_ref, slb_ref, ws_ref, sb_ref, dw_ref, dcb_ref, dng_ref, dnb_ref = par
        yc_ref, cz_ref = outs
        ezg, esh = scr
        vh, _ = _ln_rows(zr[:, _cols(1)].astype(F32))
        vnb = (vh * slg_ref[...] + slb_ref[...]).astype(BF16)
        for n in range(tm // SGU_BLOCK):
            r0, r1 = n * SGU_BLOCK, (n + 1) * SGU_BLOCK
            for hd in range(HEADS):
                c0, c1 = hd * HD, (hd + 1) * HD
                sv = _dot(ws_ref[hd], vnb[r0:r1, c0:c1]) + sb_ref[hd]
                gc = zr[r0:r1, 2 * D + c0:2 * D + c1].astype(F32)
                yc_ref[r0:r1, c0:c1] = (zr[r0:r1, c0:c1].astype(F32) * sv * (gc * _sig(gc))).astype(BF16)
        ezg[HC:HC + tm, :] = zr[:, _cols(3)].astype(F32) * _sig(zr[:, _cols(4)].astype(F32))
        _shifted_copies(ezg, esh, tm + HC - SUB)
        cz = jnp.zeros((tm, D), F32) + dcb_ref[...]
        for k in range(KD):
            cz = cz + dw_ref[k:k + 1, :] * _tap(ezg, esh, HC - (KD - 1) + k, tm)
        cz_ref[...] = cz
        zh, _ = _ln_rows(cz)
        zn = zh * dng_ref[...] + dnb_ref[...]
        gd = zr[:, _cols(5)].astype(F32)
        yc_ref[:, D:2 * D] = ((zn * _sig(zn)) * (gd * _sig(gd))).astype(BF16)
        ezg[0:HC, :] = ezg[tm:tm + HC, :]

    def body(z_ref, *refs):
        par, outs, scr = refs[:npar], refs[npar:npar + nout], refs[npar + nout:]
        j = pl.program_id(0)

        @pl.when(j == 0)
        def _():
            if even:
                scr[0][0:HP, :] = jnp.zeros((HP, D), F32)
                scr[1][0:HS, :] = jnp.zeros((HS, D), F32)
            else:
                scr[0][0:HC, :] = jnp.zeros((HC, D), F32)

        (even_tile if even else odd_tile)(z_ref, j, par, outs, scr)

    row = lambda c: pl.BlockSpec((tm, c), lambda j: (j, 0))
    vec = pl.BlockSpec((1, D), lambda j: (0, 0))
    if even:
        par_specs = [pl.BlockSpec((4, PG, PG), lambda j: (0, 0, 0)), vec, pl.BlockSpec((KS, D), lambda j: (0, 0)), vec]
        out_specs = [row(DMIX), row(D), row(D)]
        out_shape = [jax.ShapeDtypeStruct((s, DMIX), BF16), jax.ShapeDtypeStruct((s, D), BF16),
                     jax.ShapeDtypeStruct((s, D), BF16)]
        scratch = [pltpu.VMEM((HP + tm, D), F32), pltpu.VMEM((HS + tm, D), F32), pltpu.VMEM((tm, D), F32),
                   pltpu.VMEM((HP + tm, D), F32), pltpu.VMEM((HP + tm, D), F32)]
    else:
        par_specs = [vec, vec, pl.BlockSpec((HEADS, SGU_BLOCK, SGU_BLOCK), lambda j: (0, 0, 0)),
                     pl.BlockSpec((HEADS, SGU_BLOCK, 1), lambda j: (0, 0, 0)),
                     pl.BlockSpec((KD, D), lambda j: (0, 0)), vec, vec, vec]
        out_specs = [row(DMIX), row(D)]
        out_shape = [jax.ShapeDtypeStruct((s, DMIX), BF16), jax.ShapeDtypeStruct((s, D), F32)]
        scratch = [pltpu.VMEM((HC + tm, D), F32), pltpu.VMEM((SUB - 1, tm + HC - SUB, D), F32)]
    return pl.pallas_call(
        body, name=f"mix_fwd_{layer}", grid=(s // tm,), in_specs=[row(NIN)] + par_specs, out_specs=out_specs,
        out_shape=out_shape, scratch_shapes=scratch, compiler_params=_cp(("arbitrary",)),
    )(z, *params)


def head_loss(below, target):
    s = target.shape[0]
    tm = min(s, 256)

    def body(*refs):
        t_ref, xh_ref, rs_ref, dy_ref, loss_ref = refs[7:]
        xh, rstd, xn = _deepnorm(*refs[:7])
        xh_ref[...] = xh
        rs_ref[...] = rstd
        err = xn - t_ref[...]
        dy_ref[...] = err * (1.0 / D)

        @pl.when(pl.program_id(0) == 0)
        def _():
            loss_ref[...] = jnp.zeros_like(loss_ref)

        loss_ref[...] += 0.5 * jnp.sum(jnp.mean(err * err, axis=-1, keepdims=True), axis=0, keepdims=True)

    row = pl.BlockSpec((tm, D), lambda i: (i, 0))
    vec = pl.BlockSpec((1, D), lambda i: (0, 0))
    return pl.pallas_call(
        body, name="head_loss", grid=(s // tm,),
        in_specs=[pl.BlockSpec((tm, DMIX), lambda i: (i, 0)), pl.BlockSpec((NDEV, RW, D), lambda i: (0, 0, 0)),
                  row, vec, vec, vec, vec, row],
        out_specs=[row, pl.BlockSpec((tm, 1), lambda i: (i, 0)), row, pl.BlockSpec((1, 1), lambda i: (0, 0))],
        out_shape=[jax.ShapeDtypeStruct((s, D), F32), jax.ShapeDtypeStruct((s, 1), F32),
                   jax.ShapeDtypeStruct((s, D), F32), jax.ShapeDtypeStruct((1, 1), F32)],
        compiler_params=_cp(("arbitrary",)),
    )(*below, target)


def ln_bwd_dycat(xhat, rstd, g, wog, layer, dxn=None, upstream=None):
    s = xhat.shape[0]
    tm = min(s, 256)
    fused = upstream is not None

    def body(*refs):
        if fused:
            dz_ref, wi_ref, dru_ref = refs[:3]
            refs = refs[3:]
        else:
            dx_ref = refs[0]
            refs = refs[1:]
        xh_ref, rs_ref, g_ref, w_ref, dr_ref, drb_ref, dyc_ref, dg_ref, db_ref = refs

        @pl.when(pl.program_id(0) == 0)
        def _():
            dg_ref[...] = jnp.zeros_like(dg_ref)
            db_ref[...] = jnp.zeros_like(db_ref)

        if fused:
            dxo = ALPHA * dru_ref[...]
            for k in range(NDEV):
                dxo += _dot_nt(dz_ref[:, k * CW:(k + 1) * CW], wi_ref[k])
        else:
            dxo = dx_ref[...]
        xh = xh_ref[...]
        dg_ref[...] += _rowsum(dxo * xh)
        db_ref[...] += _rowsum(dxo)
        dxh = dxo * g_ref[...]
        m1 = jnp.mean(dxh, axis=-1, keepdims=True)
        m2 = jnp.mean(dxh * xh, axis=-1, keepdims=True)
        dr = rs_ref[...] * (dxh - m1 - xh * m2)
        dr_ref[...] = dr
        drb = dr.astype(BF16)
        drb_ref[...] = drb
        dyc_ref[...] = _dot_nt(drb, w_ref[...].reshape(DMIX, D)).astype(BF16)

    row = pl.BlockSpec((tm, D), lambda i: (i, 0))
    vec = pl.BlockSpec((1, D), lambda i: (0, 0))
    if fused:
        head_specs = [pl.BlockSpec((tm, NIN), lambda i: (i, 0)),
                      pl.BlockSpec((NDEV, D, CW), lambda i: (0, 0, 0), pipeline_mode=pl.Buffered(1)), row]
        head = list(upstream)
    else:
        head_specs, head = [row], [dxn]
    return pl.pallas_call(
        body, name=f"ln_bwd_dycat_{layer}", grid=(s // tm,),
        in_specs=head_specs + [row, pl.BlockSpec((tm, 1), lambda i: (i, 0)), vec,
                               pl.BlockSpec((NDEV, RW, D), lambda i: (0, 0, 0))],
        out_specs=[row, row, pl.BlockSpec((tm, DMIX), lambda i: (i, 0)), vec, vec],
        out_shape=[jax.ShapeDtypeStruct((s, D), F32), jax.ShapeDtypeStruct((s, D), BF16),
                   jax.ShapeDtypeStruct((s, DMIX), BF16), jax.ShapeDtypeStruct((1, D), F32),
                   jax.ShapeDtypeStruct((1, D), F32)],
        compiler_params=_cp(("arbitrary",)),
    )(*head, xhat, rstd, g, wog)


def dx_in(dzb, wg, dr, parts):
    s = dzb.shape[0]
    tm = min(s, 256)
    nt = s // tm

    def body(dz_ref, w_ref, dr_ref, p_ref, o_ref, land_ref, *sems):
        start, finish = _exchange_plan(p_ref, land_ref, sems, 0)
        pl.when(pl.program_id(0) == 0)(start)
        acc = ALPHA * dr_ref[...]
        for k in range(NDEV):
            acc += _dot_nt(dz_ref[:, k * CW:(k + 1) * CW], w_ref[k])
        o_ref[...] = acc
        pl.when(pl.program_id(0) == nt - 1)(finish)

    row = pl.BlockSpec((tm, D), lambda i: (i, 0))
    return pl.pallas_call(
        body, name="dx_in_0", grid=(nt,),
        in_specs=[pl.BlockSpec((tm, NIN), lambda i: (i, 0)),
                  pl.BlockSpec((NDEV, D, CW), lambda i: (0, 0, 0), pipeline_mode=pl.Buffered(1)), row, ANY],
        out_specs=[row, ANY],
        out_shape=[jax.ShapeDtypeStruct((s, D), F32), jax.ShapeDtypeStruct(parts.shape, parts.dtype)],
        scratch_shapes=_comm_scratch(1), compiler_params=_cp(("arbitrary",)),
    )(dzb, wg, dr, parts)


def dw_in(xb, dzb, layer, parts=(), gathers=()):
    s = xb.shape[0]
    tm = min(s, 2048)
    nt = s // tm
    ne = len(parts)
    n = ne + len(gathers)

    def body(*refs):
        x_ref, dz_ref = refs[:2]
        p_refs = refs[2:2 + n]
        o_ref = refs[2 + n]
        land_refs = refs[3 + n:3 + 2 * n]
        acc = refs[3 + 2 * n]
        plans = [(_exchange_plan if j < ne else _gather_plan)(p, land, refs[-3:], j)
                 for j, (p, land) in enumerate(zip(p_refs, land_refs))]
        k, t = pl.program_id(0), pl.program_id(1)

        @pl.when((k == 0) & (t == 0))
        def _():
            for start, _ in plans:
                start()

        @pl.when(t == 0)
        def _():
            acc[...] = jnp.zeros_like(acc)

        acc[...] += _dot_tn(x_ref[...], dz_ref[...])

        @pl.when(t == nt - 1)
        def _():
            o_ref[...] = acc[...].astype(BF16)

        @pl.when((k == NDEV - 1) & (t == nt - 1))
        def _():
            for _, finish in plans:
                finish()

    return pl.pallas_call(
        body, name=f"dw_in_{layer}", grid=(NDEV, nt),
        in_specs=[pl.BlockSpec((tm, D), lambda k, t: (t, 0)), pl.BlockSpec((tm, CW), lambda k, t: (t, k))] + [ANY] * n,
        out_specs=[pl.BlockSpec((None, D, CW), lambda k, t: (k, 0, 0))] + [ANY] * n,
        out_shape=[jax.ShapeDtypeStruct((NDEV, D, CW), BF16)] + [jax.ShapeDtypeStruct(p.shape, p.dtype) for p in parts]
        + [jax.ShapeDtypeStruct((NDEV,) + g.shape, g.dtype) for g in gathers],
        scratch_shapes=[pltpu.VMEM((D, CW), F32)] + (_comm_scratch(n) if n else []),
        compiler_params=_cp(("arbitrary", "arbitrary")),
    )(xb, dzb, *parts, *gathers)


def dw_out(ycat, drb, layer):
    s = ycat.shape[0]
    tm = min(s, 1024)
    nt = s // tm

    def body(y_ref, dr_ref, o_ref, acc):
        t = pl.program_id(0)

        @pl.when(t == 0)
        def _():
            acc[...] = jnp.zeros_like(acc)

        acc[...] += _dot_tn(y_ref[...], dr_ref[...])

        @pl.when(t == nt - 1)
        def _():
            o_ref[...] = acc[...].reshape(NDEV, RW, D).astype(BF16)

    return pl.pallas_call(
        body, name=f"dw_out_{layer}", grid=(nt,),
        in_specs=[pl.BlockSpec((tm, DMIX), lambda t: (t, 0)), pl.BlockSpec((tm, D), lambda t: (t, 0))],
        out_specs=pl.BlockSpec((NDEV, RW, D), lambda t: (0, 0, 0)),
        out_shape=jax.ShapeDtypeStruct((NDEV, RW, D), BF16),
        scratch_shapes=[pltpu.VMEM((DMIX, D), F32)],
        compiler_params=_cp(("arbitrary",)),
    )(ycat, drb)


def _cols(j):
    return slice(j * D, (j + 1) * D)


def mix_even_bwd(z, dycat, pooled, cv, pwb, pscale, sw, layer, parts=None):
    s = z.shape[0]
    tm = min(s, 256)
    nt = s // tm
    carry = parts is not None

    def body(*refs):
        z_ref, dy_ref, pb_ref, cv_ref, pw_ref, ps_ref, sw_ref = refs[:7]
        if carry:
            p_ref = refs[7]
            dz_ref, dpw_ref, dps_ref, dsw_ref, dsb_ref, land_ref = refs[8:14]
            edp, edc, ya0, t1s, dpl, la, lb = refs[14:21]
            start, finish = _exchange_plan(p_ref, land_ref, refs[-3:], 0)
        else:
            dz_ref, dpw_ref, dps_ref, dsw_ref, dsb_ref = refs[7:12]
            edp, edc, ya0, t1s, dpl, la, lb = refs[12:19]
        i = pl.program_id(0)
        if carry:
            pl.when(i == 0)(start)

        @pl.when(i == 0)
        def _():
            edp[tm:tm + HP, :] = jnp.zeros((HP, D), F32)
            edc[tm:tm + HS, :] = jnp.zeros((HS, D), F32)
            dpw_ref[...] = jnp.zeros_like(dpw_ref)
            dps_ref[...] = jnp.zeros_like(dps_ref)
            dsw_ref[...] = jnp.zeros_like(dsw_ref)
            dsb_ref[...] = jnp.zeros_like(dsb_ref)

        blocks = [(r * RC, (r + 1) * RC, c * CB, (c + 1) * CB) for r in range(tm // RC) for c in range(D // CB)]
        for gi in range(len(POOL_WINDOWS)):
            c0, c1 = gi * PG, (gi + 1) * PG
            ya0[:, c0:c1] = _dot(pb_ref[:, c0:c1], pw_ref[gi])
        for r0, r1, c0, c1 in blocks:
            ga = z_ref[r0:r1, D + c0:D + c1].astype(F32)
            sg = _sig(ga)
            sil = ga * sg
            dya = dy_ref[r0:r1, c0:c1].astype(F32)
            y0 = ya0[r0:r1, c0:c1]
            ps = ps_ref[:, c0:c1]
            dps_ref[:, c0:c1] += _rowsum(dya * y0 * sil)
            dz_ref[r0:r1, D + c0:D + c1] = (dya * y0 * ps * (sg * (1.0 + ga * (1.0 - sg)))).astype(BF16)
            t1s[r0:r1, c0:c1] = (dya * ps * sil).astype(BF16)
        for gi in range(len(POOL_WINDOWS)):
            c0, c1 = gi * PG, (gi + 1) * PG
            dpl[:, c0:c1] = _dot_nt(t1s[:, c0:c1], pw_ref[gi])
            dpw_ref[gi] += _dot_tn(pb_ref[:, c0:c1], t1s[:, c0:c1])
        for r in range(tm // RC):
            r0, r1 = r * RC, (r + 1) * RC
            pos = ((nt - 1 - i) * tm + r0 + lax.broadcasted_iota(jnp.int32, (RC, 1), 0) + 1).astype(F32)
            for gi, w in enumerate(POOL_WINDOWS):
                c0, c1 = gi * PG, (gi + 1) * PG
                edp[r0:r1, c0:c1] = dpl[r0:r1, c0:c1] / jnp.minimum(pos, float(w))

        def level(src, dst, shift, c0, end):
            for r0 in range(0, end, 2 * RC):
                r1 = min(r0 + 2 * RC, end)
                dst[r0:r1, c0:D] = src[r0:r1, c0:D] + src[r0 + shift:r1 + shift, c0:D]

        level(edp, la, 1, PG, tm + 24)
        level(la, lb, 2, 2 * PG, tm + 16)
        level(lb, la, 4, 3 * PG, tm + 8)
        last = ((edp, 1), (la, 2), (lb, 4), (la, 8))
        for r in range(tm // RC):
            r0, r1 = r * RC, (r + 1) * RC
            for gi in range(len(POOL_WINDOWS)):
                c0, c1 = gi * PG, (gi + 1) * PG
                src, shift = last[gi]
                acc = src[r0:r1, c0:c1] + src[r0 + shift:r1 + shift, c0:c1] - dpl[r0:r1, c0:c1]
                dz_ref[r0:r1, c0:c1] = acc.astype(BF16)

        for r0, r1, c0, c1 in blocks:
            gb = z_ref[r0:r1, 5 * D + c0:5 * D + c1].astype(F32)
            sgb = _sig(gb)
            silb = gb * sgb
            dyb = dy_ref[r0:r1, D + c0:D + c1].astype(F32)
            cvv = cv_ref[r0:r1, c0:c1].astype(F32)
            bg = z_ref[r0:r1, 3 * D + c0:3 * D + c1].astype(F32)
            dz_ref[r0:r1, 3 * D + c0:3 * D + c1] = (dyb * cvv * silb).astype(BF16)
            dz_ref[r0:r1, 5 * D + c0:5 * D + c1] = (dyb * bg * cvv * (sgb * (1.0 + gb * (1.0 - sgb)))).astype(BF16)
            dcv = dyb * bg * silb
            dsb_ref[:, c0:c1] += _rowsum(dcv)
            edc[r0:r1, c0:c1] = dcv
        for r0, r1, c0, c1 in blocks:
            h = z_ref[r0:r1, 2 * D + c0:2 * D + c1].astype(F32)
            cg = z_ref[r0:r1, 4 * D + c0:4 * D + c1].astype(F32)
            q = cg * h
            d0 = edc[r0:r1, c0:c1]
            d1 = edc[r0 + 1:r1 + 1, c0:c1]
            d2 = edc[r0 + 2:r1 + 2, c0:c1]
            dq = sw_ref[2:3, c0:c1] * d0 + sw_ref[1:2, c0:c1] * d1 + sw_ref[0:1, c0:c1] * d2
            dsw_ref[2:3, c0:c1] += _rowsum(q * d0)
            dsw_ref[1:2, c0:c1] += _rowsum(q * d1)
            dsw_ref[0:1, c0:c1] += _rowsum(q * d2)
            dz_ref[r0:r1, 4 * D + c0:4 * D + c1] = (dq * h).astype(BF16)
            dz_ref[r0:r1, 2 * D + c0:2 * D + c1] = (dq * cg).astype(BF16)
        edp[tm:tm + HP, :] = edp[0:HP, :]
        edc[tm:tm + HS, :] = edc[0:HS, :]
        if carry:
            pl.when(i == nt - 1)(finish)

    rev = lambda i: (nt - 1 - i, 0)
    vec = pl.BlockSpec((1, D), lambda i: (0, 0))
    in_specs = [pl.BlockSpec((tm, NIN), rev), pl.BlockSpec((tm, DMIX), rev),
                pl.BlockSpec((tm, D), rev), pl.BlockSpec((tm, D), rev),
                pl.BlockSpec((4, PG, PG), lambda i: (0, 0, 0)), vec,
                pl.BlockSpec((KS, D), lambda i: (0, 0))]
    out_specs = [pl.BlockSpec((tm, NIN), rev), pl.BlockSpec((4, PG, PG), lambda i: (0, 0, 0)), vec,
                 pl.BlockSpec((KS, D), lambda i: (0, 0)), vec]
    out_shape = [jax.ShapeDtypeStruct((s, NIN), BF16), jax.ShapeDtypeStruct((4, PG, PG), F32),
                 jax.ShapeDtypeStruct((1, D), F32), jax.ShapeDtypeStruct((KS, D), F32),
                 jax.ShapeDtypeStruct((1, D), F32)]
    args = [z, dycat, pooled, cv, pwb, pscale, sw]
    scratch = [pltpu.VMEM((tm + HP, D), F32), pltpu.VMEM((tm + HS, D), F32), pltpu.VMEM((tm, D), F32),
               pltpu.VMEM((tm, D), BF16), pltpu.VMEM((tm, D), F32), pltpu.VMEM((tm + HP, D), F32),
               pltpu.VMEM((tm + HP, D), F32)]
    if carry:
        in_specs.append(ANY)
        args.append(parts)
        out_specs.append(ANY)
        out_shape.append(jax.ShapeDtypeStruct(parts.shape, parts.dtype))
        scratch += _comm_scratch(1)
    return pl.pallas_call(
        body, name=f"mix_even_bwd_{layer}", grid=(nt,), in_specs=in_specs, out_specs=out_specs,
        out_shape=out_shape, scratch_shapes=scratch, compiler_params=_cp(("arbitrary",)),
    )(*args)


def _ln_rows(v):
    mu = jnp.mean(v, axis=-1, keepdims=True)
    d = v - mu
    var = jnp.mean(d * d, axis=-1, keepdims=True)
    rstd = lax.rsqrt(var + LN_EPS)
    return d * rstd, rstd


def _ln_rows_bwd(dn, xh, rstd, g):
    dxh = dn * g
    m1 = jnp.mean(dxh, axis=-1, keepdims=True)
    m2 = jnp.mean(dxh * xh, axis=-1, keepdims=True)
    return rstd * (dxh - m1 - xh * m2)


def _shifted_copies(ext, shifted, rows):
    for b in range(1, SUB):
        shifted[b - 1] = ext[b:b + rows, :]


def _tap(ext, shifted, off, tm):
    a, b = off // SUB, off % SUB
    if b == 0:
        return ext[SUB * a:SUB * a + tm, :]
    return shifted[b - 1, SUB * a:SUB * a + tm, :]


def mix_odd_bwd(z, dycat, cz, slg, slb, wsb, wstb, sbcol, dw, dng, dnb, layer, parts):
    s = z.shape[0]
    tm = min(s, 256)
    nt = s // tm

    def body(z_ref, dy_ref, cz_ref, slg_ref, slb_ref, ws_ref, wst_ref, sb_ref, dw_ref, dng_ref, dnb_ref, p_ref,
             dz_ref, dslg_ref, dslb_ref, dws_ref, dsb_ref, ddw_ref, ddcb_ref, ddng_ref, ddnb_ref, land_ref,
             dvn, edz, esh, *sems):
        i = pl.program_id(0)
        start, finish = _exchange_plan(p_ref, land_ref, sems, 0)
        pl.when(i == 0)(start)

        @pl.when(i == 0)
        def _():
            edz[tm:tm + HC, :] = jnp.zeros((HC, D), F32)
            for ref in (dslg_ref, dslb_ref, dws_ref, dsb_ref, ddw_ref, ddcb_ref, ddng_ref, ddnb_ref):
                ref[...] = jnp.zeros_like(ref)

        vh, vrs = _ln_rows(z_ref[:, _cols(1)].astype(F32))
        vnb = (vh * slg_ref[...] + slb_ref[...]).astype(BF16)
        for n in range(tm // SGU_BLOCK):
            r0, r1 = n * SGU_BLOCK, (n + 1) * SGU_BLOCK
            for hd in range(HEADS):
                c0, c1 = hd * HD, (hd + 1) * HD
                vblk = vnb[r0:r1, c0:c1]
                sv = _dot(ws_ref[hd], vblk) + sb_ref[hd]
                gc = z_ref[r0:r1, 2 * D + c0:2 * D + c1].astype(F32)
                sg = _sig(gc)
                sil = gc * sg
                u = z_ref[r0:r1, c0:c1].astype(F32)
                dyc = dy_ref[r0:r1, c0:c1].astype(F32)
                dz_ref[r0:r1, c0:c1] = (dyc * sv * sil).astype(BF16)
                dz_ref[r0:r1, 2 * D + c0:2 * D + c1] = (dyc * u * sv * (sg * (1.0 + gc * (1.0 - sg)))).astype(BF16)
                dsv = dyc * u * sil
                dsb_ref[hd] += jnp.sum(dsv, axis=-1, keepdims=True)
                dsvb = dsv.astype(BF16)
                dws_ref[hd] += _dot_nt(dsvb, vblk)
                dvn[r0:r1, c0:c1] = _dot(wst_ref[hd], dsvb)
        dv = dvn[...]
        dslg_ref[...] += _rowsum(dv * vh)
        dslb_ref[...] += _rowsum(dv)
        dz_ref[:, _cols(1)] = _ln_rows_bwd(dv, vh, vrs, slg_ref[...]).astype(BF16)

        zh, zrs = _ln_rows(cz_ref[...])
        zn = zh * dng_ref[...] + dnb_ref[...]
        sgn = _sig(zn)
        gd = z_ref[:, _cols(5)].astype(F32)
        sgd = _sig(gd)
        dyd = dy_ref[:, D:2 * D].astype(F32)
        dz_ref[:, _cols(5)] = (dyd * (zn * sgn) * (sgd * (1.0 + gd * (1.0 - sgd)))).astype(BF16)
        dzn = dyd * (gd * sgd) * (sgn * (1.0 + zn * (1.0 - sgn)))
        ddng_ref[...] += _rowsum(dzn * zh)
        ddnb_ref[...] += _rowsum(dzn)
        dcz = _ln_rows_bwd(dzn, zh, zrs, dng_ref[...])
        ddcb_ref[...] += _rowsum(dcz)
        edz[0:tm, :] = dcz
        _shifted_copies(edz, esh, tm + HC - SUB)
        a = z_ref[:, _cols(3)].astype(F32)
        sgb = _sig(z_ref[:, _cols(4)].astype(F32))
        zg = a * sgb
        dzg = jnp.zeros((tm, D), F32)
        for k in range(KD):
            sh = _tap(edz, esh, KD - 1 - k, tm)
            dzg = dzg + dw_ref[k:k + 1, :] * sh
            ddw_ref[k:k + 1, :] += _rowsum(zg * sh)
        dz_ref[:, _cols(3)] = (dzg * sgb).astype(BF16)
        dz_ref[:, _cols(4)] = (dzg * a * sgb * (1.0 - sgb)).astype(BF16)
        edz[tm:tm + HC, :] = edz[0:HC, :]
        pl.when(i == nt - 1)(finish)

    rev = lambda i: (nt - 1 - i, 0)
    vec = pl.BlockSpec((1, D), lambda i: (0, 0))
    wspec = pl.BlockSpec((HEADS, SGU_BLOCK, SGU_BLOCK), lambda i: (0, 0, 0))
    bspec = pl.BlockSpec((HEADS, SGU_BLOCK, 1), lambda i: (0, 0, 0))
    kspec = pl.BlockSpec((KD, D), lambda i: (0, 0))
    return pl.pallas_call(
        body, name=f"mix_odd_bwd_{layer}", grid=(nt,),
        in_specs=[pl.BlockSpec((tm, NIN), rev), pl.BlockSpec((tm, DMIX), rev), pl.BlockSpec((tm, D), rev),
                  vec, vec, wspec, wspec, bspec, kspec, vec, vec, ANY],
        out_specs=[pl.BlockSpec((tm, NIN), rev), vec, vec, wspec, bspec, kspec, vec, vec, vec, ANY],
        out_shape=[jax.ShapeDtypeStruct((s, NIN), BF16), jax.ShapeDtypeStruct((1, D), F32),
                   jax.ShapeDtypeStruct((1, D), F32),
                   jax.ShapeDtypeStruct((HEADS, SGU_BLOCK, SGU_BLOCK), F32),
                   jax.ShapeDtypeStruct((HEADS, SGU_BLOCK, 1), F32), jax.ShapeDtypeStruct((KD, D), F32),
                   jax.ShapeDtypeStruct((1, D), F32), jax.ShapeDtypeStruct((1, D), F32),
                   jax.ShapeDtypeStruct((1, D), F32), jax.ShapeDtypeStruct(parts.shape, parts.dtype)],
        scratch_shapes=[pltpu.VMEM((tm, D), F32), pltpu.VMEM((tm + HC, D), F32),
                        pltpu.VMEM((SUB - 1, tm + HC - SUB, D), F32)] + _comm_scratch(1),
        compiler_params=_cp(("arbitrary",)),
    )(z, dycat, cz, slg, slb, wsb, wstb, sbcol, dw, dng, dnb, parts)


def _adamw_math(w, g, m, v):
    m = ADAM_B1 * m + (1.0 - ADAM_B1) * g
    v = ADAM_B2 * v + (1.0 - ADAM_B2) * (g * g)
    m_hat = m / (1.0 - ADAM_B1 ** ADAM_STEP)
    v_hat = v / (1.0 - ADAM_B2 ** ADAM_STEP)
    delta = -ADAM_LR * (m_hat / (jnp.sqrt(v_hat) + ADAM_EPS) + ADAM_WD * w)
    return delta, m, v


def adamw_big(w, m, v, parts0, parts1, name):
    _, r, c = w.shape
    tr = min(r, 256)
    nr = r // tr

    def body(w_ref, m_ref, v_ref, p0_ref, p1_ref, g_ref, d_ref, nm_ref, nv_ref):
        i = pl.program_id(0)

        def total(p_ref):
            acc = p_ref[0].astype(F32)
            for j in range(1, NDEV):
                acc = acc + p_ref[j].astype(F32)
            return acc

        @pl.when(i == 0)
        def _():
            g_ref[...] = total(p0_ref)

        @pl.when(i == 1)
        def _():
            g_ref[...] = total(p1_ref)

        delta, nm, nv = _adamw_math(w_ref[...], g_ref[...], m_ref[...], v_ref[...])
        d_ref[...] = delta
        nm_ref[...] = nm
        nv_ref[...] = nv

    wspec = pl.BlockSpec((None, tr, c), lambda i, j: (i, j, 0))
    p0 = pl.BlockSpec((NDEV, tr, c), lambda i, j: (0, jnp.where(i == 0, j, nr - 1), 0))
    p1 = pl.BlockSpec((NDEV, tr, c), lambda i, j: (0, jnp.where(i == 1, j, 0), 0))
    shp = jax.ShapeDtypeStruct(w.shape, F32)
    return pl.pallas_call(
        body, name=name, grid=(2, nr), in_specs=[wspec, wspec, wspec, p0, p1],
        out_specs=[wspec] * 4, out_shape=[shp] * 4,
        compiler_params=_cp(("arbitrary", "arbitrary")),
    )(w, m, v, parts0, parts1)


def sum_parts(parts, name):
    _, r, c = parts.shape

    def body(p_ref, o_ref):
        acc = p_ref[0]
        for j in range(1, NDEV):
            acc = acc + p_ref[j]
        o_ref[...] = acc

    return pl.pallas_call(body, name=name, out_shape=jax.ShapeDtypeStruct((r, c), F32),
                          compiler_params=_cp())(parts)


def adamw_small(w, g, m, v, name):
    def body(w_ref, g_ref, m_ref, v_ref, d_ref, nm_ref, nv_ref):
        delta, nm, nv = _adamw_math(w_ref[...], g_ref[...], m_ref[...], v_ref[...])
        d_ref[...] = delta
        nm_ref[...] = nm
        nv_ref[...] = nv

    shp = jax.ShapeDtypeStruct(w.shape, F32)
    return pl.pallas_call(body, name=name, out_shape=[shp] * 3, compiler_params=_cp())(w, g, m, v)


def _size(shape):
    n = 1
    for d in shape:
        n *= d
    return n


def _pack(arrays, rows):
    flat = jnp.concatenate([a.reshape(-1) for a in arrays])
    return jnp.pad(flat, (0, rows * 128 - flat.shape[0])).reshape(rows, 128)


def _unpack(flat, shapes):
    out, o = [], 0
    for shp in shapes:
        out.append(flat[o:o + _size(shp)].reshape(shp))
        o += _size(shp)
    return out


def _rows_for(shapes):
    return -(-sum(_size(shp) for shp in shapes) // 1024) * 8


SHARDED_SMALL = (("pool_w", (2, 4, 256, 256), 2), ("sconv_w", (2, KS, D), 2), ("sgu_ln_g", (2, D), 1),
                 ("sgu_ln_b", (2, D), 1), ("dconv_w", (2, KD, D), 2), ("dconv_b", (2, D), 1),
                 ("dnorm_g", (2, D), 1), ("dnorm_b", (2, D), 1))
REPLICATED_SMALL = (("ln_g", (DEPTH, D)), ("ln_b", (DEPTH, D)), ("pool_scale", (2, D)), ("sconv_b", (2, D)),
                    ("sgu_w", (2, HEADS, SGU_BLOCK, SGU_BLOCK)), ("sgu_b", (2, HEADS, SGU_BLOCK)))


def _shard_shape(shape, axis):
    return tuple(d // NDEV if a == axis else d for a, d in enumerate(shape))


def _merge_gathered(g, shape, axis):
    return jnp.moveaxis(g, 0, axis).reshape(shape)


def kernel(x, ln_g, ln_b, w_in_even, w_out_even, pool_w, pool_scale, sconv_w, sconv_b, w_in_odd, w_out_odd, sgu_ln_g, sgu_ln_b, sgu_w, sgu_b, dconv_w, dconv_b, dnorm_g, dnorm_b, loss_target, m_ln_g, m_ln_b, m_w_in_even, m_w_out_even, m_pool_w, m_pool_scale, m_sconv_w, m_sconv_b, m_w_in_odd, m_w_out_odd, m_sgu_ln_g, m_sgu_ln_b, m_sgu_w, m_sgu_b, m_dconv_w, m_dconv_b, m_dnorm_g, m_dnorm_b, v_ln_g, v_ln_b, v_w_in_even, v_w_out_even, v_pool_w, v_pool_scale, v_sconv_w, v_sconv_b, v_w_in_odd, v_w_out_odd, v_sgu_ln_g, v_sgu_ln_b, v_sgu_w, v_sgu_b, v_dconv_w, v_dconv_b, v_dnorm_g, v_dnorm_b):
    given = dict(locals())
    me = 4 * lax.axis_index("x") + 2 * lax.axis_index("y") + lax.axis_index("c")
    xs = x[0]
    target = loss_target[0]

    wc_in = cast_weights(w_in_even, w_in_odd, "cast_w_in")
    wc_out = cast_weights(w_out_even, w_out_odd, "cast_w_out")
    shard_shapes = [_shard_shape(shp, ax) for _, shp, ax in SHARDED_SMALL]
    srows = _rows_for(shard_shapes)
    xb, wg_first, gathered = cast_x(xs, wc_in, _pack([given[n] for n, _, _ in SHARDED_SMALL], srows))
    wg_in = [wg_first] + [None] * (DEPTH - 1)
    wg_out = [None] * DEPTH
    gathered = gathered.reshape(NDEV, -1)
    full, o = {}, 0
    for (n, shp, ax), sshp in zip(SHARDED_SMALL, shard_shapes):
        full[n] = _merge_gathered(gathered[:, o:o + _size(sshp)].reshape((NDEV,) + sshp), shp, ax)
        o += _size(sshp)

    mask = (jnp.arange(SGU_BLOCK)[None, :] // 64) <= (jnp.arange(SGU_BLOCK)[:, None] // 64)
    ws = jnp.where(mask[None, None], sgu_w, 0.0)
    wsb = ws.astype(BF16)
    wstb = jnp.swapaxes(ws, -1, -2).astype(BF16)
    pwb = full["pool_w"].astype(BF16)

    ones = jnp.ones((1, D), F32)
    zeros = jnp.zeros((1, D), F32)

    xres, gp, bp = xs, ones, zeros
    saved, below = [], None
    for layer in range(DEPTH):
        i = layer // 2
        if layer % 2 == 0:
            params = (pwb[i], pool_scale[i][None], full["sconv_w"][i], sconv_b[i][None])
        else:
            params = (full["sgu_ln_g"][i][None], full["sgu_ln_b"][i][None], wsb[i], sgu_b[i][:, :, None],
                      full["dconv_w"][i], full["dconv_b"][i][None], full["dnorm_g"][i][None],
                      full["dnorm_b"][i][None])
        outs = list(proj_in(wg_in[layer], layer, wc_in, wc_out, xb=xb if below is None else None, below=below))
        if layer + 1 < DEPTH:
            wg_in[layer + 1] = outs.pop()
        wg_out[layer] = outs.pop()
        z = outs[0]
        if below is not None:
            xhat, rstd, xb = outs[1:4]
            saved[-1] += [xhat, rstd]
            xres, gp, bp = xhat, below[5], below[6]
        outs = mix_fwd(z, layer, params)
        ycat, extra = outs[0], tuple(outs[1:])
        saved.append([xb, z, ycat, extra])
        below = (ycat, wg_out[layer], xres, gp, bp, ln_g[layer][None], ln_b[layer][None])
    xhat, rstd, dxn, loss_local = head_loss(below, target)
    saved[-1] += [xhat, rstd]

    gsmall = {n: [None] * shp[0] for n, shp in REPLICATED_SMALL}
    gsmall.update({n: [None] * shp[0] for n, shp, _ in SHARDED_SMALL})
    parts_in, parts_out, parts_pw = [None] * DEPTH, [None] * DEPTH, [None] * 2
    small_names = [n for n, _ in REPLICATED_SMALL] + [n for n, _, _ in SHARDED_SMALL]
    gathered_names = [n for n in small_names if n != "pool_w"]
    gathered_shapes = [shp for n, shp in REPLICATED_SMALL] + [shp for n, shp, _ in SHARDED_SMALL if n != "pool_w"]
    grows = _rows_for(gathered_shapes + [(1,)])
    pending, above = None, None
    for layer in reversed(range(DEPTH)):
        i = layer // 2
        xb, z, ycat, extra, xhat, rstd = saved[layer]
        dr, drb, dycat, dg, db = ln_bwd_dycat(xhat, rstd, ln_g[layer][None], wg_out[layer], layer,
                                              dxn=dxn, upstream=above)
        gsmall["ln_g"][layer], gsmall["ln_b"][layer] = dg[0], db[0]
        dwo = dw_out(ycat, drb, layer)
        own_out = pending is None
        riding = dwo if own_out else pending
        if layer % 2 == 0:
            dzb, dpw, dps, dsw, dsb, landed = mix_even_bwd(
                z, dycat, extra[0], extra[1], pwb[i], pool_scale[i][None], full["sconv_w"][i], layer, parts=riding)
            gsmall["pool_scale"][i] = dps[0]
            gsmall["sconv_w"][i], gsmall["sconv_b"][i] = dsw, dsb[0]
        else:
            dzb, dslg, dslb, dws, dsbc, ddw, ddcb, ddng, ddnb, landed = mix_odd_bwd(
                z, dycat, extra[0], full["sgu_ln_g"][i][None], full["sgu_ln_b"][i][None], wsb[i], wstb[i],
                sgu_b[i][:, :, None], full["dconv_w"][i], full["dnorm_g"][i][None], full["dnorm_b"][i][None],
                layer, riding)
            gsmall["sgu_ln_g"][i], gsmall["sgu_ln_b"][i] = dslg[0], dslb[0]
            gsmall["sgu_w"][i], gsmall["sgu_b"][i] = jnp.where(mask[None], dws, 0.0), dsbc[:, :, 0]
            gsmall["dconv_w"][i], gsmall["dconv_b"][i] = ddw, ddcb[0]
            gsmall["dnorm_g"][i], gsmall["dnorm_b"][i] = ddng[0], ddnb[0]
        riders = [] if own_out else [dwo]
        if layer % 2 == 0:
            riders.append(dpw.reshape(4, NDEV, PG // NDEV, PG).transpose(1, 0, 2, 3))
        small_partial = []
        if layer == 0:
            small_partial.append(_pack([jnp.stack(gsmall[n]) for n in gathered_names] + [loss_local], grows))
        outs = list(dw_in(xb, dzb, layer, riders, small_partial))
        if layer == 0:
            small_gathered = outs.pop()
        if layer % 2 == 0:
            parts_pw[i] = outs.pop()
        if own_out:
            parts_out[layer] = landed
        else:
            parts_in[layer + 1] = landed
            parts_out[layer] = outs[1]
        pending = outs[0]
        above = (dzb, wg_in[layer], dr)
    dxn, parts_in[0] = dx_in(dzb, wg_in[0], dr, pending)
    grad_x = dxn[None]

    big = {}
    big["w_in_even"] = adamw_big(w_in_even, m_w_in_even, v_w_in_even, parts_in[0], parts_in[2], "adamw_w_in_even")
    big["w_in_odd"] = adamw_big(w_in_odd, m_w_in_odd, v_w_in_odd, parts_in[1], parts_in[3], "adamw_w_in_odd")
    big["w_out_even"] = adamw_big(w_out_even, m_w_out_even, v_w_out_even, parts_out[0], parts_out[2], "adamw_w_out_even")
    big["w_out_odd"] = adamw_big(w_out_odd, m_w_out_odd, v_w_out_odd, parts_out[1], parts_out[3], "adamw_w_out_odd")

    gsum = sum_parts(small_gathered, "sum_small_grads")
    unpacked = _unpack(gsum.reshape(-1), gathered_shapes + [()])
    loss = unpacked.pop()
    gfull = dict(zip(gathered_names, unpacked))
    own_shapes = [shp for _, shp in REPLICATED_SMALL] + shard_shapes
    gown = {n: gfull[n] for n, _ in REPLICATED_SMALL}
    for (n, shp, ax), sshp in zip(SHARDED_SMALL, shard_shapes):
        if n == "pool_w":
            gown[n] = jnp.stack([sum_parts(p.reshape(NDEV, -1, 128), f"sum_pool_w_{j}").reshape(sshp[1:])
                                 for j, p in enumerate(parts_pw)])
        else:
            gown[n] = lax.dynamic_slice_in_dim(gfull[n], me * sshp[ax], sshp[ax], axis=ax)
    orows = _rows_for(own_shapes)
    packed = [_pack([src[n] for n in small_names], orows) for src in
              (given, gown, {n: given["m_" + n] for n in small_names}, {n: given["v_" + n] for n in small_names})]
    sd, sm, sv = adamw_small(*packed, "adamw_small")
    small = {}
    for n, d_, m_, v_ in zip(small_names, _unpack(sd.reshape(-1), own_shapes), _unpack(sm.reshape(-1), own_shapes),
                             _unpack(sv.reshape(-1), own_shapes)):
        small[n] = (gown[n], d_, m_, v_)

    weights = ['ln_g', 'ln_b', 'w_in_even', 'w_out_even', 'pool_w', 'pool_scale', 'sconv_w', 'sconv_b', 'w_in_odd',
               'w_out_odd', 'sgu_ln_g', 'sgu_ln_b', 'sgu_w', 'sgu_b', 'dconv_w', 'dconv_b', 'dnorm_g', 'dnorm_b']
    res = {n: (big[n] if n in big else small[n]) for n in weights}
    return (loss, grad_x, *[res[n][0] for n in weights], *[res[n][1] for n in weights],
            *[res[n][2] for n in weights], *[res[n][3] for n in weights])
```

```python
import jax
import jax.numpy as jnp
from jax import lax
from jax.experimental import pallas as pl
from jax.experimental.pallas import tpu as pltpu

F32 = jnp.float32
BF16 = jnp.bfloat16

D = 1024
DMIX = 2048
NIN = 6144
NDEV = 8
CW = NIN // NDEV
RW = DMIX // NDEV
DEPTH = 4
ALPHA = (2 * DEPTH) ** 0.25
LN_EPS = 1e-5
POOL_WINDOWS = (2, 4, 8, 16)
PG = 256
SGU_BLOCK = 128
HEADS = 4
HD = 256
KD = 31
KS = 3
SUB = 8
RC = 16
CB = 512
HP = 32
HS = 8
HC = 32
NSEM = 7

ADAM_LR = 0.001
ADAM_B1 = 0.9
ADAM_B2 = 0.999
ADAM_EPS = 1e-08
ADAM_WD = 0.01
ADAM_STEP = 10

VMEM_LIMIT = 56 * 1024 * 1024
MESH = pl.DeviceIdType.MESH
ANY = pl.BlockSpec(memory_space=pl.ANY)


def _cp(sem=None):
    if sem is None:
        return pltpu.CompilerParams(vmem_limit_bytes=VMEM_LIMIT)
    return pltpu.CompilerParams(dimension_semantics=sem, vmem_limit_bytes=VMEM_LIMIT)


def _sig(x):
    return 0.5 * jnp.tanh(0.5 * x) + 0.5


def _dot(a, b):
    return jnp.dot(a, b, preferred_element_type=F32)


def _dot_nt(a, b):
    return lax.dot_general(a, b, (((1,), (1,)), ((), ())), preferred_element_type=F32)


def _dot_tn(a, b):
    return lax.dot_general(a, b, (((0,), (0,)), ((), ())), preferred_element_type=F32)


def _rowsum(x):
    return jnp.sum(x, axis=0, keepdims=True)


def _comm_scratch(n):
    return [pltpu.SemaphoreType.DMA((n * NSEM,)), pltpu.SemaphoreType.DMA((n * NSEM,)),
            pltpu.SemaphoreType.DMA((n,))]


def _gather_plan(src_ref, out_ref, sems, n):
    send_sems, recv_sems, local_sems = sems
    base = n * NSEM
    x, y, c = lax.axis_index("x"), lax.axis_index("y"), lax.axis_index("c")
    me, sibling = (x, y, c), (x, y, 1 - c)
    chips = [(1 - x, y), (x, 1 - y), (1 - x, 1 - y)]

    def slot(px, py, pc):
        return out_ref.at[4 * px + 2 * py + pc]

    def copy(k, blk, to, src=None):
        return pltpu.make_async_remote_copy(
            src_ref=slot(*blk) if src is None else src, dst_ref=slot(*blk),
            send_sem=send_sems.at[base + k], recv_sem=recv_sems.at[base + k],
            device_id=to, device_id_type=MESH)

    def mine():
        return pltpu.make_async_copy(src_ref, slot(*me), local_sems.at[n])

    def start():
        mine().start()
        copy(0, me, sibling, src=src_ref).start()
        for j, chip in enumerate(chips):
            copy(1 + j, me, (*chip, c), src=src_ref).start()

    def finish():
        for j, chip in enumerate(chips):
            copy(1 + j, (*chip, c), me).wait_recv()
            copy(4 + j, (*chip, c), sibling).start()
        copy(0, sibling, me).wait_recv()
        for j, chip in enumerate(chips):
            copy(4 + j, (*chip, 1 - c), me).wait_recv()
        copy(0, me, sibling, src=src_ref).wait_send()
        for j, chip in enumerate(chips):
            copy(1 + j, me, (*chip, c), src=src_ref).wait_send()
            copy(4 + j, (*chip, c), sibling).wait_send()
        mine().wait()

    return start, finish


def _exchange_plan(p_ref, out_ref, sems, n):
    send_sems, recv_sems, local_sems = sems
    base = n * NSEM
    x, y, c = lax.axis_index("x"), lax.axis_index("y"), lax.axis_index("c")
    me = 4 * x + 2 * y + c

    def copy(r, landing):
        px, py, pc = x ^ (r >> 2), y ^ ((r >> 1) & 1), c ^ (r & 1)
        peer = 4 * px + 2 * py + pc
        return pltpu.make_async_remote_copy(
            src_ref=p_ref.at[peer], dst_ref=out_ref.at[peer if landing else me],
            send_sem=send_sems.at[base + r - 1], recv_sem=recv_sems.at[base + r - 1],
            device_id=(px, py, pc), device_id_type=MESH)

    def mine():
        return pltpu.make_async_copy(p_ref.at[me], out_ref.at[me], local_sems.at[n])

    def start():
        mine().start()
        for r in range(1, NDEV):
            copy(r, False).start()

    def finish():
        for r in range(1, NDEV):
            copy(r, True).wait_recv()
        for r in range(1, NDEV):
            copy(r, False).wait_send()
        mine().wait()

    return start, finish


def cast_x(x, w_first, small):
    s = x.shape[0]
    tm = min(s, 512)
    nt = s // tm

    def body(x_ref, wf_ref, sm_ref, o_ref, gw_ref, gs_ref, *sems):
        i = pl.program_id(0)
        plans = [_gather_plan(wf_ref.at[0], gw_ref, sems, 0), _gather_plan(sm_ref, gs_ref, sems, 1)]

        @pl.when(i == 0)
        def _():
            for start, _ in plans:
                start()

        o_ref[...] = x_ref[...].astype(BF16)

        @pl.when(i == nt - 1)
        def _():
            for _, finish in plans:
                finish()

    return pl.pallas_call(
        body, name="cast_x", grid=(nt,),
        in_specs=[pl.BlockSpec((tm, D), lambda i: (i, 0)), ANY, ANY],
        out_specs=[pl.BlockSpec((tm, D), lambda i: (i, 0)), ANY, ANY],
        out_shape=[jax.ShapeDtypeStruct((s, D), BF16),
                   jax.ShapeDtypeStruct((NDEV,) + w_first.shape[1:], w_first.dtype),
                   jax.ShapeDtypeStruct((NDEV,) + small.shape, small.dtype)],
        scratch_shapes=_comm_scratch(2), compiler_params=_cp(("arbitrary",)),
    )(x, w_first, small)


def cast_weights(w_even, w_odd, name):
    _, r, c = w_even.shape

    def body(e_ref, o_ref, out_ref):
        layer = pl.program_id(0)

        @pl.when(layer % 2 == 0)
        def _():
            out_ref[...] = e_ref[...].astype(BF16)

        @pl.when(layer % 2 == 1)
        def _():
            out_ref[...] = o_ref[...].astype(BF16)

    spec = pl.BlockSpec((None, r, c), lambda l: (l // 2, 0, 0))
    return pl.pallas_call(
        body, name=name, grid=(DEPTH,), in_specs=[spec, spec],
        out_specs=pl.BlockSpec((None, r, c), lambda l: (l, 0, 0)),
        out_shape=jax.ShapeDtypeStruct((DEPTH, r, c), BF16), compiler_params=_cp(("parallel",)),
    )(w_even, w_odd)


def _deepnorm(y_ref, wo_ref, xr_ref, gp_ref, bp_ref, g_ref, b_ref):
    xin = xr_ref[...] * gp_ref[...] + bp_ref[...]
    r = ALPHA * xin + _dot(y_ref[...], wo_ref[...].reshape(DMIX, D))
    mu = jnp.mean(r, axis=-1, keepdims=True)
    d = r - mu
    var = jnp.mean(d * d, axis=-1, keepdims=True)
    rstd = lax.rsqrt(var + LN_EPS)
    xh = d * rstd
    return xh, rstd, xh * g_ref[...] + b_ref[...]


def proj_in(wg, layer, wc_in, wc_out, xb=None, below=None):
    fused = below is not None
    s = below[0].shape[0] if fused else xb.shape[0]
    tm = min(s, 256 if fused else 512)
    nt = s // tm
    nxt = layer + 1 < DEPTH
    nhead = 7 if fused else 1

    def body(*refs):
        head, refs = refs[:nhead], refs[nhead:]
        if nxt:
            w_ref, wco_ref, wci_ref = refs[:3]
            refs = refs[3:]
        else:
            w_ref, wco_ref = refs[:2]
            refs = refs[2:]
        o_ref = refs[0]
        if fused:
            xh_ref, rs_ref, xb_ref = refs[1:4]
            refs = refs[4:]
        else:
            refs = refs[1:]
        go_ref = refs[0]
        sems = refs[-3:]
        i = pl.program_id(0)
        plans = [_gather_plan(wco_ref.at[layer], go_ref, sems, 0)]
        if nxt:
            plans.append(_gather_plan(wci_ref.at[layer + 1], refs[1], sems, 1))

        @pl.when(i == 0)
        def _():
            for start, _ in plans:
                start()

        if fused:
            xh, rstd, xn = _deepnorm(*head)
            xh_ref[...] = xh
            rs_ref[...] = rstd
            x = xn.astype(BF16)
            xb_ref[...] = x
        else:
            x = head[0][...]
        for k in range(NDEV):
            o_ref[:, k * CW:(k + 1) * CW] = _dot(x, w_ref[k]).astype(BF16)

        @pl.when(i == nt - 1)
        def _():
            for _, finish in plans:
                finish()

    row = pl.BlockSpec((tm, D), lambda i: (i, 0))
    vec = pl.BlockSpec((1, D), lambda i: (0, 0))
    if fused:
        in_specs = [pl.BlockSpec((tm, DMIX), lambda i: (i, 0)),
                    pl.BlockSpec((NDEV, RW, D), lambda i: (0, 0, 0), pipeline_mode=pl.Buffered(1)),
                    row, vec, vec, vec, vec]
        args = list(below)
    else:
        in_specs, args = [row], [xb]
    in_specs += [pl.BlockSpec((NDEV, D, CW), lambda i: (0, 0, 0), pipeline_mode=pl.Buffered(1)), ANY]
    args += [wg, wc_out]
    if nxt:
        in_specs.append(ANY)
        args.append(wc_in)
    out_specs = [pl.BlockSpec((tm, NIN), lambda i: (i, 0))]
    out_shape = [jax.ShapeDtypeStruct((s, NIN), BF16)]
    if fused:
        out_specs += [row, pl.BlockSpec((tm, 1), lambda i: (i, 0)), row]
        out_shape += [jax.ShapeDtypeStruct((s, D), F32), jax.ShapeDtypeStruct((s, 1), F32),
                      jax.ShapeDtypeStruct((s, D), BF16)]
    out_specs.append(ANY)
    out_shape.append(jax.ShapeDtypeStruct((NDEV, RW, D), BF16))
    if nxt:
        out_specs.append(ANY)
        out_shape.append(jax.ShapeDtypeStruct((NDEV, D, CW), BF16))
    return pl.pallas_call(
        body, name=f"proj_in_{layer}", grid=(nt,), in_specs=in_specs, out_specs=out_specs,
        out_shape=out_shape, scratch_shapes=_comm_scratch(2 if nxt else 1),
        compiler_params=_cp(("arbitrary",)),
    )(*args)


def mix_fwd(z, layer, params):
    s = z.shape[0]
    tm = min(s, 256)
    even = layer % 2 == 0
    npar = len(params)
    nout = 3 if even else 2

    def even_tile(zr, j, par, outs, scr):
        pw_ref, ps_ref, sw_ref, sb_ref = par
        yc_ref, pb_ref, cv_ref = outs
        exa, eq, ya0, la, lb = scr
        for c in range(tm // RC):
            r0, r1 = c * RC, (c + 1) * RC
            exa[HP + r0:HP + r1, :] = zr[r0:r1, _cols(0)].astype(F32)
            eq[HS + r0:HS + r1, :] = zr[r0:r1, _cols(4)].astype(F32) * zr[r0:r1, _cols(2)].astype(F32)

        def level(src, dst, shift, c0, start):
            for r0 in range(start, HP + tm, 2 * RC):
                r1 = min(r0 + 2 * RC, HP + tm)
                dst[r0:r1, c0:D] = src[r0:r1, c0:D] + src[r0 - shift:r1 - shift, c0:D]

        level(exa, la, 1, PG, 8)
        level(la, lb, 2, 2 * PG, 16)
        level(lb, la, 4, 3 * PG, 24)
        last = ((exa, 1), (la, 2), (lb, 4), (la, 8))
        for c in range(tm // RC):
            r0, r1 = HP + c * RC, HP + (c + 1) * RC
            pos = (j * tm + c * RC + lax.broadcasted_iota(jnp.int32, (RC, 1), 0) + 1).astype(F32)
            for gi, w in enumerate(POOL_WINDOWS):
                c0, c1 = gi * PG, (gi + 1) * PG
                src, shift = last[gi]
                acc = src[r0:r1, c0:c1] + src[r0 - shift:r1 - shift, c0:c1]
                pooled = acc / jnp.minimum(pos, float(w)) - exa[r0:r1, c0:c1]
                pb_ref[c * RC:(c + 1) * RC, c0:c1] = pooled.astype(BF16)
        for gi in range(len(POOL_WINDOWS)):
            c0, c1 = gi * PG, (gi + 1) * PG
            ya0[:, c0:c1] = _dot(pb_ref[:, c0:c1], pw_ref[gi])
        for c in range(tm // RC):
            r0, r1 = c * RC, (c + 1) * RC
            ga = zr[r0:r1, _cols(1)].astype(F32)
            yc_ref[r0:r1, 0:D] = (ya0[r0:r1, :] * ps_ref[...] * (ga * _sig(ga))).astype(BF16)
            cv = (sw_ref[2:3, :] * eq[HS + r0:HS + r1, :] + sw_ref[1:2, :] * eq[HS + r0 - 1:HS + r1 - 1, :]
                  + sw_ref[0:1, :] * eq[HS + r0 - 2:HS + r1 - 2, :] + sb_ref[...])
            cv_ref[r0:r1, :] = cv.astype(BF16)
            gb = zr[r0:r1, _cols(5)].astype(F32)
            yc_ref[r0:r1, D:2 * D] = (zr[r0:r1, _cols(3)].astype(F32) * cv * (gb * _sig(gb))).astype(BF16)
        exa[0:HP, :] = exa[tm:tm + HP, :]
        eq[0:HS, :] = eq[tm:tm + HS, :]

    def odd_tile(zr, j, par, outs, scr):
        slg_ref, slb_ref, ws_ref, sb_ref, dw_ref, dcb_ref, dng_ref, dnb_ref = par
        yc_ref, cz_ref = outs
        ezg, esh = scr
        vh, _ = _ln_rows(zr[:, _cols(1)].astype(F32))
        vnb = (vh * slg_ref[...] + slb_ref[...]).astype(BF16)
        for n in range(tm // SGU_BLOCK):
            r0, r1 = n * SGU_BLOCK, (n + 1) * SGU_BLOCK
            for hd in range(HEADS):
                c0, c1 = hd * HD, (hd + 1) * HD
                sv = _dot(ws_ref[hd], vnb[r0:r1, c0:c1]) + sb_ref[hd]
                gc = zr[r0:r1, 2 * D + c0:2 * D + c1].astype(F32)
                yc_ref[r0:r1, c0:c1] = (zr[r0:r1, c0:c1].astype(F32) * sv * (gc * _sig(gc))).astype(BF16)
        ezg[HC:HC + tm, :] = zr[:, _cols(3)].astype(F32) * _sig(zr[:, _cols(4)].astype(F32))
        _shifted_copies(ezg, esh, tm + HC - SUB)
        cz = jnp.zeros((tm, D), F32) + dcb_ref[...]
        for k in range(KD):
            cz = cz + dw_ref[k:k + 1, :] * _tap(ezg, esh, HC - (KD - 1) + k, 0, tm)
        cz_ref[...] = cz
        zh, _ = _ln_rows(cz)
        zn = zh * dng_ref[...] + dnb_ref[...]
        gd = zr[:, _cols(5)].astype(F32)
        yc_ref[:, D:2 * D] = ((zn * _sig(zn)) * (gd * _sig(gd))).astype(BF16)
        ezg[0:HC, :] = ezg[tm:tm + HC, :]

    def body(z_ref, *refs):
        par, outs, scr = refs[:npar], refs[npar:npar + nout], refs[npar + nout:]
        j = pl.program_id(0)

        @pl.when(j == 0)
        def _():
            if even:
                scr[0][0:HP, :] = jnp.zeros((HP, D), F32)
                scr[1][0:HS, :] = jnp.zeros((HS, D), F32)
            else:
                scr[0][0:HC, :] = jnp.zeros((HC, D), F32)

        (even_tile if even else odd_tile)(z_ref, j, par, outs, scr)

    row = lambda c: pl.BlockSpec((tm, c), lambda j: (j, 0))
    vec = pl.BlockSpec((1, D), lambda j: (0, 0))
    if even:
        par_specs = [pl.BlockSpec((4, PG, PG), lambda j: (0, 0, 0)), vec, pl.BlockSpec((KS, D), lambda j: (0, 0)), vec]
        out_specs = [row(DMIX), row(D), row(D)]
        out_shape = [jax.ShapeDtypeStruct((s, DMIX), BF16), jax.ShapeDtypeStruct((s, D), BF16),
                     jax.ShapeDtypeStruct((s, D), BF16)]
        scratch = [pltpu.VMEM((HP + tm, D), F32), pltpu.VMEM((HS + tm, D), F32), pltpu.VMEM((tm, D), F32),
                   pltpu.VMEM((HP + tm, D), F32), pltpu.VMEM((HP + tm, D), F32)]
    else:
        par_specs = [vec, vec, pl.BlockSpec((HEADS, SGU_BLOCK, SGU_BLOCK), lambda j: (0, 0, 0)),
                     pl.BlockSpec((HEADS, SGU_BLOCK, 1), lambda j: (0, 0, 0)),
                     pl.BlockSpec((KD, D), lambda j: (0, 0)), vec, vec, vec]
        out_specs = [row(DMIX), row(D)]
        out_shape = [jax.ShapeDtypeStruct((s, DMIX), BF16), jax.ShapeDtypeStruct((s, D), F32)]
        scratch = [pltpu.VMEM((HC + tm, D), F32), pltpu.VMEM((SUB - 1, tm + HC - SUB, D), F32)]
    return pl.pallas_call(
        body, name=f"mix_fwd_{layer}", grid=(s // tm,), in_specs=[row(NIN)] + par_specs, out_specs=out_specs,
        out_shape=out_shape, scratch_shapes=scratch, compiler_params=_cp(("arbitrary",)),
    )(z, *params)


def head_loss(below, target):
    s = target.shape[0]
    tm = min(s, 256)

    def body(*refs):
        t_ref, xh_ref, rs_ref, dy_ref, loss_ref = refs[7:]
        xh, rstd, xn = _deepnorm(*refs[:7])
        xh_ref[...] = xh
        rs_ref[...] = rstd
        err = xn - t_ref[...]
        dy_ref[...] = err * (1.0 / D)

        @pl.when(pl.program_id(0) == 0)
        def _():
            loss_ref[...] = jnp.zeros_like(loss_ref)

        loss_ref[...] += 0.5 * jnp.sum(jnp.mean(err * err, axis=-1, keepdims=True), axis=0, keepdims=True)

    row = pl.BlockSpec((tm, D), lambda i: (i, 0))
    vec = pl.BlockSpec((1, D), lambda i: (0, 0))
    return pl.pallas_call(
        body, name="head_loss", grid=(s // tm,),
        in_specs=[pl.BlockSpec((tm, DMIX), lambda i: (i, 0)), pl.BlockSpec((NDEV, RW, D), lambda i: (0, 0, 0)),
                  row, vec, vec, vec, vec, row],
        out_specs=[row, pl.BlockSpec((tm, 1), lambda i: (i, 0)), row, pl.BlockSpec((1, 1), lambda i: (0, 0))],
        out_shape=[jax.ShapeDtypeStruct((s, D), F32), jax.ShapeDtypeStruct((s, 1), F32),
                   jax.ShapeDtypeStruct((s, D), F32), jax.ShapeDtypeStruct((1, 1), F32)],
        compiler_params=_cp(("arbitrary",)),
    )(*below, target)


def ln_bwd_dycat(xhat, rstd, g, wog, layer, dxn=None, upstream=None):
    s = xhat.shape[0]
    tm = min(s, 256)
    fused = upstream is not None

    def body(*refs):
        if fused:
            dz_ref, wi_ref, dru_ref = refs[:3]
            refs = refs[3:]
        else:
            dx_ref = refs[0]
            refs = refs[1:]
        xh_ref, rs_ref, g_ref, w_ref, dr_ref, drb_ref, dyc_ref, dg_ref, db_ref = refs

        @pl.when(pl.program_id(0) == 0)
        def _():
            dg_ref[...] = jnp.zeros_like(dg_ref)
            db_ref[...] = jnp.zeros_like(db_ref)

        if fused:
            dxo = ALPHA * dru_ref[...]
            for k in range(NDEV):
                dxo += _dot_nt(dz_ref[:, k * CW:(k + 1) * CW], wi_ref[k])
        else:
            dxo = dx_ref[...]
        xh = xh_ref[...]
        dg_ref[...] += _rowsum(dxo * xh)
        db_ref[...] += _rowsum(dxo)
        dxh = dxo * g_ref[...]
        m1 = jnp.mean(dxh, axis=-1, keepdims=True)
        m2 = jnp.mean(dxh * xh, axis=-1, keepdims=True)
        dr = rs_ref[...] * (dxh - m1 - xh * m2)
        dr_ref[...] = dr
        drb = dr.astype(BF16)
        drb_ref[...] = drb
        dyc_ref[...] = _dot_nt(drb, w_ref[...].reshape(DMIX, D)).astype(BF16)

    row = pl.BlockSpec((tm, D), lambda i: (i, 0))
    vec = pl.BlockSpec((1, D), lambda i: (0, 0))
    if fused:
        head_specs = [pl.BlockSpec((tm, NIN), lambda i: (i, 0)),
                      pl.BlockSpec((NDEV, D, CW), lambda i: (0, 0, 0), pipeline_mode=pl.Buffered(1)), row]
        head = list(upstream)
    else:
        head_specs, head = [row], [dxn]
    return pl.pallas_call(
        body, name=f"ln_bwd_dycat_{layer}", grid=(s // tm,),
        in_specs=head_specs + [row, pl.BlockSpec((tm, 1), lambda i: (i, 0)), vec,
                               pl.BlockSpec((NDEV, RW, D), lambda i: (0, 0, 0))],
        out_specs=[row, row, pl.BlockSpec((tm, DMIX), lambda i: (i, 0)), vec, vec],
        out_shape=[jax.ShapeDtypeStruct((s, D), F32), jax.ShapeDtypeStruct((s, D), BF16),
                   jax.ShapeDtypeStruct((s, DMIX), BF16), jax.ShapeDtypeStruct((1, D), F32),
                   jax.ShapeDtypeStruct((1, D), F32)],
        compiler_params=_cp(("arbitrary",)),
    )(*head, xhat, rstd, g, wog)


def dx_in(dzb, wg, dr, parts):
    s = dzb.shape[0]
    tm = min(s, 256)
    nt = s // tm

    def body(dz_ref, w_ref, dr_ref, p_ref, o_ref, land_ref, *sems):
        start, finish = _exchange_plan(p_ref, land_ref, sems, 0)
        pl.when(pl.program_id(0) == 0)(start)
        acc = ALPHA * dr_ref[...]
        for k in range(NDEV):
            acc += _dot_nt(dz_ref[:, k * CW:(k + 1) * CW], w_ref[k])
        o_ref[...] = acc
        pl.when(pl.program_id(0) == nt - 1)(finish)

    row = pl.BlockSpec((tm, D), lambda i: (i, 0))
    return pl.pallas_call(
        body, name="dx_in_0", grid=(nt,),
        in_specs=[pl.BlockSpec((tm, NIN), lambda i: (i, 0)),
                  pl.BlockSpec((NDEV, D, CW), lambda i: (0, 0, 0), pipeline_mode=pl.Buffered(1)), row, ANY],
        out_specs=[row, ANY],
        out_shape=[jax.ShapeDtypeStruct((s, D), F32), jax.ShapeDtypeStruct(parts.shape, parts.dtype)],
        scratch_shapes=_comm_scratch(1), compiler_params=_cp(("arbitrary",)),
    )(dzb, wg, dr, parts)


def dw_in(xb, dzb, layer, parts=(), gathers=()):
    s = xb.shape[0]
    tm = min(s, 2048)
    nt = s // tm
    ne = len(parts)
    n = ne + len(gathers)

    def body(*refs):
        x_ref, dz_ref = refs[:2]
        p_refs = refs[2:2 + n]
        o_ref = refs[2 + n]
        land_refs = refs[3 + n:3 + 2 * n]
        acc = refs[3 + 2 * n]
        plans = [(_exchange_plan if j < ne else _gather_plan)(p, land, refs[-3:], j)
                 for j, (p, land) in enumerate(zip(p_refs, land_refs))]
        k, t = pl.program_id(0), pl.program_id(1)

        @pl.when((k == 0) & (t == 0))
        def _():
            for start, _ in plans:
                start()

        @pl.when(t == 0)
        def _():
            acc[...] = jnp.zeros_like(acc)

        acc[...] += _dot_tn(x_ref[...], dz_ref[...])

        @pl.when(t == nt - 1)
        def _():
            o_ref[...] = acc[...].astype(BF16)

        @pl.when((k == NDEV - 1) & (t == nt - 1))
        def _():
            for _, finish in plans:
                finish()

    return pl.pallas_call(
        body, name=f"dw_in_{layer}", grid=(NDEV, nt),
        in_specs=[pl.BlockSpec((tm, D), lambda k, t: (t, 0)), pl.BlockSpec((tm, CW), lambda k, t: (t, k))] + [ANY] * n,
        out_specs=[pl.BlockSpec((None, D, CW), lambda k, t: (k, 0, 0))] + [ANY] * n,
        out_shape=[jax.ShapeDtypeStruct((NDEV, D, CW), BF16)] + [jax.ShapeDtypeStruct(p.shape, p.dtype) for p in parts]
        + [jax.ShapeDtypeStruct((NDEV,) + g.shape, g.dtype) for g in gathers],
        scratch_shapes=[pltpu.VMEM((D, CW), F32)] + (_comm_scratch(n) if n else []),
        compiler_params=_cp(("arbitrary", "arbitrary")),
    )(xb, dzb, *parts, *gathers)


def dw_out(ycat, drb, layer):
    s = ycat.shape[0]
    tm = min(s, 1024)
    nt = s // tm

    def body(y_ref, dr_ref, o_ref, acc):
        t = pl.program_id(0)

        @pl.when(t == 0)
        def _():
            acc[...] = jnp.zeros_like(acc)

        acc[...] += _dot_tn(y_ref[...], dr_ref[...])

        @pl.when(t == nt - 1)
        def _():
            o_ref[...] = acc[...].reshape(NDEV, RW, D).astype(BF16)

    return pl.pallas_call(
        body, name=f"dw_out_{layer}", grid=(nt,),
        in_specs=[pl.BlockSpec((tm, DMIX), lambda t: (t, 0)), pl.BlockSpec((tm, D), lambda t: (t, 0))],
        out_specs=pl.BlockSpec((NDEV, RW, D), lambda t: (0, 0, 0)),
        out_shape=jax.ShapeDtypeStruct((NDEV, RW, D), BF16),
        scratch_shapes=[pltpu.VMEM((DMIX, D), F32)],
        compiler_params=_cp(("arbitrary",)),
    )(ycat, drb)


def _cols(j):
    return slice(j * D, (j + 1) * D)


def mix_even_bwd(z, dycat, pooled, cv, pwb, pscale, sw, layer, parts=None):
    s = z.shape[0]
    tm = min(s, 256)
    nt = s // tm
    carry = parts is not None

    def body(*refs):
        z_ref, dy_ref, pb_ref, cv_ref, pw_ref, ps_ref, sw_ref = refs[:7]
        if carry:
            p_ref = refs[7]
            dz_ref, dpw_ref, dps_ref, dsw_ref, dsb_ref, land_ref = refs[8:14]
            edp, edc, ya0, t1s, dpl, la, lb = refs[14:21]
            start, finish = _exchange_plan(p_ref, land_ref, refs[-3:], 0)
        else:
            dz_ref, dpw_ref, dps_ref, dsw_ref, dsb_ref = refs[7:12]
            edp, edc, ya0, t1s, dpl, la, lb = refs[12:19]
        i = pl.program_id(0)
        if carry:
            pl.when(i == 0)(start)

        @pl.when(i == 0)
        def _():
            edp[tm:tm + HP, :] = jnp.zeros((HP, D), F32)
            edc[tm:tm + HS, :] = jnp.zeros((HS, D), F32)
            dpw_ref[...] = jnp.zeros_like(dpw_ref)
            dps_ref[...] = jnp.zeros_like(dps_ref)
            dsw_ref[...] = jnp.zeros_like(dsw_ref)
            dsb_ref[...] = jnp.zeros_like(dsb_ref)

        blocks = [(r * RC, (r + 1) * RC, c * CB, (c + 1) * CB) for r in range(tm // RC) for c in range(D // CB)]
        for gi in range(len(POOL_WINDOWS)):
            c0, c1 = gi * PG, (gi + 1) * PG
            ya0[:, c0:c1] = _dot(pb_ref[:, c0:c1], pw_ref[gi])
        for r0, r1, c0, c1 in blocks:
            ga = z_ref[r0:r1, D + c0:D + c1].astype(F32)
            sg = _sig(ga)
            sil = ga * sg
            dya = dy_ref[r0:r1, c0:c1].astype(F32)
            y0 = ya0[r0:r1, c0:c1]
            ps = ps_ref[:, c0:c1]
            dps_ref[:, c0:c1] += _rowsum(dya * y0 * sil)
            dz_ref[r0:r1, D + c0:D + c1] = (dya * y0 * ps * (sg * (1.0 + ga * (1.0 - sg)))).astype(BF16)
            t1s[r0:r1, c0:c1] = (dya * ps * sil).astype(BF16)
        for gi in range(len(POOL_WINDOWS)):
            c0, c1 = gi * PG, (gi + 1) * PG
            dpl[:, c0:c1] = _dot_nt(t1s[:, c0:c1], pw_ref[gi])
            dpw_ref[gi] += _dot_tn(pb_ref[:, c0:c1], t1s[:, c0:c1])
        for r in range(tm // RC):
            r0, r1 = r * RC, (r + 1) * RC
            pos = ((nt - 1 - i) * tm + r0 + lax.broadcasted_iota(jnp.int32, (RC, 1), 0) + 1).astype(F32)
            for gi, w in enumerate(POOL_WINDOWS):
                c0, c1 = gi * PG, (gi + 1) * PG
                edp[r0:r1, c0:c1] = dpl[r0:r1, c0:c1] / jnp.minimum(pos, float(w))

        def level(src, dst, shift, c0, end):
            for r0 in range(0, end, 2 * RC):
                r1 = min(r0 + 2 * RC, end)
                dst[r0:r1, c0:D] = src[r0:r1, c0:D] + src[r0 + shift:r1 + shift, c0:D]

        level(edp, la, 1, PG, tm + 24)
        level(la, lb, 2, 2 * PG, tm + 16)
        level(lb, la, 4, 3 * PG, tm + 8)
        last = ((edp, 1), (la, 2), (lb, 4), (la, 8))
        for r in range(tm // RC):
            r0, r1 = r * RC, (r + 1) * RC
            for gi in range(len(POOL_WINDOWS)):
                c0, c1 = gi * PG, (gi + 1) * PG
                src, shift = last[gi]
                acc = src[r0:r1, c0:c1] + src[r0 + shift:r1 + shift, c0:c1] - dpl[r0:r1, c0:c1]
                dz_ref[r0:r1, c0:c1] = acc.astype(BF16)

        for r0, r1, c0, c1 in blocks:
            gb = z_ref[r0:r1, 5 * D + c0:5 * D + c1].astype(F32)
            sgb = _sig(gb)
            silb = gb * sgb
            dyb = dy_ref[r0:r1, D + c0:D + c1].astype(F32)
            cvv = cv_ref[r0:r1, c0:c1].astype(F32)
            bg = z_ref[r0:r1, 3 * D + c0:3 * D + c1].astype(F32)
            dz_ref[r0:r1, 3 * D + c0:3 * D + c1] = (dyb * cvv * silb).astype(BF16)
            dz_ref[r0:r1, 5 * D + c0:5 * D + c1] = (dyb * bg * cvv * (sgb * (1.0 + gb * (1.0 - sgb)))).astype(BF16)
            dcv = dyb * bg * silb
            dsb_ref[:, c0:c1] += _rowsum(dcv)
            edc[r0:r1, c0:c1] = dcv
        for r0, r1, c0, c1 in blocks:
            h = z_ref[r0:r1, 2 * D + c0:2 * D + c1].astype(F32)
            cg = z_ref[r0:r1, 4 * D + c0:4 * D + c1].astype(F32)
            q = cg * h
            d0 = edc[r0:r1, c0:c1]
            d1 = edc[r0 + 1:r1 + 1, c0:c1]
            d2 = edc[r0 + 2:r1 + 2, c0:c1]
            dq = sw_ref[2:3, c0:c1] * d0 + sw_ref[1:2, c0:c1] * d1 + sw_ref[0:1, c0:c1] * d2
            dsw_ref[2:3, c0:c1] += _rowsum(q * d0)
            dsw_ref[1:2, c0:c1] += _rowsum(q * d1)
            dsw_ref[0:1, c0:c1] += _rowsum(q * d2)
            dz_ref[r0:r1, 4 * D + c0:4 * D + c1] = (dq * h).astype(BF16)
            dz_ref[r0:r1, 2 * D + c0:2 * D + c1] = (dq * cg).astype(BF16)
        edp[tm:tm + HP, :] = edp[0:HP, :]
        edc[tm:tm + HS, :] = edc[0:HS, :]
        if carry:
            pl.when(i == nt - 1)(finish)

    rev = lambda i: (nt - 1 - i, 0)
    vec = pl.BlockSpec((1, D), lambda i: (0, 0))
    in_specs = [pl.BlockSpec((tm, NIN), rev), pl.BlockSpec((tm, DMIX), rev),
                pl.BlockSpec((tm, D), rev), pl.BlockSpec((tm, D), rev),
                pl.BlockSpec((4, PG, PG), lambda i: (0, 0, 0)), vec,
                pl.BlockSpec((KS, D), lambda i: (0, 0))]
    out_specs = [pl.BlockSpec((tm, NIN), rev), pl.BlockSpec((4, PG, PG), lambda i: (0, 0, 0)), vec,
                 pl.BlockSpec((KS, D), lambda i: (0, 0)), vec]
    out_shape = [jax.ShapeDtypeStruct((s, NIN), BF16), jax.ShapeDtypeStruct((4, PG, PG), F32),
                 jax.ShapeDtypeStruct((1, D), F32), jax.ShapeDtypeStruct((KS, D), F32),
                 jax.ShapeDtypeStruct((1, D), F32)]
    args = [z, dycat, pooled, cv, pwb, pscale, sw]
    scratch = [pltpu.VMEM((tm + HP, D), F32), pltpu.VMEM((tm + HS, D), F32), pltpu.VMEM((tm, D), F32),
               pltpu.VMEM((tm, D), BF16), pltpu.VMEM((tm, D), F32), pltpu.VMEM((tm + HP, D), F32),
               pltpu.VMEM((tm + HP, D), F32)]
    if carry:
        in_specs.append(ANY)
        args.append(parts)
        out_specs.append(ANY)
        out_shape.append(jax.ShapeDtypeStruct(parts.shape, parts.dtype))
        scratch += _comm_scratch(1)
    return pl.pallas_call(
        body, name=f"mix_even_bwd_{layer}", grid=(nt,), in_specs=in_specs, out_specs=out_specs,
        out_shape=out_shape, scratch_shapes=scratch, compiler_params=_cp(("arbitrary",)),
    )(*args)


def _ln_rows(v):
    mu = jnp.mean(v, axis=-1, keepdims=True)
    d = v - mu
    var = jnp.mean(d * d, axis=-1, keepdims=True)
    rstd = lax.rsqrt(var + LN_EPS)
    return d * rstd, rstd


def _ln_rows_bwd(dn, xh, rstd, g):
    dxh = dn * g
    m1 = jnp.mean(dxh, axis=-1, keepdims=True)
    m2 = jnp.mean(dxh * xh, axis=-1, keepdims=True)
    return rstd * (dxh - m1 - xh * m2)


def _shifted_copies(ext, shifted, rows):
    for b in range(1, SUB):
        shifted[b - 1] = ext[b:b + rows, :]


def _tap(ext, shifted, off, r0, r1, cols=slice(None)):
    a, b = off // SUB, off % SUB
    if b == 0:
        return ext[SUB * a + r0:SUB * a + r1, cols]
    return shifted[b - 1, SUB * a + r0:SUB * a + r1, cols]


def mix_odd_bwd(z, dycat, cz, slg, slb, wsb, wstb, sbcol, dw, dng, dnb, layer, parts):
    s = z.shape[0]
    tm = min(s, 256)
    nt = s // tm

    def body(z_ref, dy_ref, cz_ref, slg_ref, slb_ref, ws_ref, wst_ref, sb_ref, dw_ref, dng_ref, dnb_ref, p_ref,
             dz_ref, dslg_ref, dslb_ref, dws_ref, dsb_ref, ddw_ref, ddcb_ref, ddng_ref, ddnb_ref, land_ref,
             dvn, edz, esh, ddw8, *sems):
        i = pl.program_id(0)
        start, finish = _exchange_plan(p_ref, land_ref, sems, 0)
        pl.when(i == 0)(start)

        @pl.when(i == 0)
        def _():
            edz[tm:tm + HC, :] = jnp.zeros((HC, D), F32)
            for ref in (dslg_ref, dslb_ref, dws_ref, dsb_ref, ddw8, ddcb_ref, ddng_ref, ddnb_ref):
                ref[...] = jnp.zeros_like(ref)

        vh, vrs = _ln_rows(z_ref[:, _cols(1)].astype(F32))
        vnb = (vh * slg_ref[...] + slb_ref[...]).astype(BF16)
        for n in range(tm // SGU_BLOCK):
            r0, r1 = n * SGU_BLOCK, (n + 1) * SGU_BLOCK
            for hd in range(HEADS):
                c0, c1 = hd * HD, (hd + 1) * HD
                vblk = vnb[r0:r1, c0:c1]
                sv = _dot(ws_ref[hd], vblk) + sb_ref[hd]
                gc = z_ref[r0:r1, 2 * D + c0:2 * D + c1].astype(F32)
                sg = _sig(gc)
                sil = gc * sg
                u = z_ref[r0:r1, c0:c1].astype(F32)
                dyc = dy_ref[r0:r1, c0:c1].astype(F32)
                dz_ref[r0:r1, c0:c1] = (dyc * sv * sil).astype(BF16)
                dz_ref[r0:r1, 2 * D + c0:2 * D + c1] = (dyc * u * sv * (sg * (1.0 + gc * (1.0 - sg)))).astype(BF16)
                dsv = dyc * u * sil
                dsb_ref[hd] += jnp.sum(dsv, axis=-1, keepdims=True)
                dsvb = dsv.astype(BF16)
                dws_ref[hd] += _dot_nt(dsvb, vblk)
                dvn[r0:r1, c0:c1] = _dot(wst_ref[hd], dsvb)
        dv = dvn[...]
        dslg_ref[...] += _rowsum(dv * vh)
        dslb_ref[...] += _rowsum(dv)
        dz_ref[:, _cols(1)] = _ln_rows_bwd(dv, vh, vrs, slg_ref[...]).astype(BF16)

        zh, zrs = _ln_rows(cz_ref[...])
        zn = zh * dng_ref[...] + dnb_ref[...]
        sgn = _sig(zn)
        gd = z_ref[:, _cols(5)].astype(F32)
        sgd = _sig(gd)
        dyd = dy_ref[:, D:2 * D].astype(F32)
        dz_ref[:, _cols(5)] = (dyd * (zn * sgn) * (sgd * (1.0 + gd * (1.0 - sgd)))).astype(BF16)
        dzn = dyd * (gd * sgd) * (sgn * (1.0 + zn * (1.0 - sgn)))
        ddng_ref[...] += _rowsum(dzn * zh)
        ddnb_ref[...] += _rowsum(dzn)
        dcz = _ln_rows_bwd(dzn, zh, zrs, dng_ref[...])
        ddcb_ref[...] += _rowsum(dcz)
        edz[0:tm, :] = dcz
        _shifted_copies(edz, esh, tm + HC - SUB)
        for r0 in range(0, tm, RC):
            for c0 in range(0, D, CB):
                r1, c1 = r0 + RC, c0 + CB
                a = z_ref[r0:r1, 3 * D + c0:3 * D + c1].astype(F32)
                sgb = _sig(z_ref[r0:r1, 4 * D + c0:4 * D + c1].astype(F32))
                zg = a * sgb
                dzg = jnp.zeros((RC, CB), F32)
                for k in range(KD):
                    sh = _tap(edz, esh, KD - 1 - k, r0, r1, slice(c0, c1))
                    dzg = dzg + dw_ref[k:k + 1, c0:c1] * sh
                    prod = zg * sh
                    ddw8[k, :, c0:c1] += prod[0:SUB] + prod[SUB:2 * SUB]
                dz_ref[r0:r1, 3 * D + c0:3 * D + c1] = (dzg * sgb).astype(BF16)
                dz_ref[r0:r1, 4 * D + c0:4 * D + c1] = (dzg * a * sgb * (1.0 - sgb)).astype(BF16)
        edz[tm:tm + HC, :] = edz[0:HC, :]

        @pl.when(i == nt - 1)
        def _():
            ddw_ref[...] = jnp.sum(ddw8[...], axis=1)

        pl.when(i == nt - 1)(finish)

    rev = lambda i: (nt - 1 - i, 0)
    vec = pl.BlockSpec((1, D), lambda i: (0, 0))
    wspec = pl.BlockSpec((HEADS, SGU_BLOCK, SGU_BLOCK), lambda i: (0, 0, 0))
    bspec = pl.BlockSpec((HEADS, SGU_BLOCK, 1), lambda i: (0, 0, 0))
    kspec = pl.BlockSpec((KD, D), lambda i: (0, 0))
    return pl.pallas_call(
        body, name=f"mix_odd_bwd_{layer}", grid=(nt,),
        in_specs=[pl.BlockSpec((tm, NIN), rev), pl.BlockSpec((tm, DMIX), rev), pl.BlockSpec((tm, D), rev),
                  vec, vec, wspec, wspec, bspec, kspec, vec, vec, ANY],
        out_specs=[pl.BlockSpec((tm, NIN), rev), vec, vec, wspec, bspec, kspec, vec, vec, vec, ANY],
        out_shape=[jax.ShapeDtypeStruct((s, NIN), BF16), jax.ShapeDtypeStruct((1, D), F32),
                   jax.ShapeDtypeStruct((1, D), F32),
                   jax.ShapeDtypeStruct((HEADS, SGU_BLOCK, SGU_BLOCK), F32),
                   jax.ShapeDtypeStruct((HEADS, SGU_BLOCK, 1), F32), jax.ShapeDtypeStruct((KD, D), F32),
                   jax.ShapeDtypeStruct((1, D), F32), jax.ShapeDtypeStruct((1, D), F32),
                   jax.ShapeDtypeStruct((1, D), F32), jax.ShapeDtypeStruct(parts.shape, parts.dtype)],
        scratch_shapes=[pltpu.VMEM((tm, D), F32), pltpu.VMEM((tm + HC, D), F32),
                        pltpu.VMEM((SUB - 1, tm + HC - SUB, D), F32), pltpu.VMEM((KD, SUB, D), F32)]
        + _comm_scratch(1),
        compiler_params=_cp(("arbitrary",)),
    )(z, dycat, cz, slg, slb, wsb, wstb, sbcol, dw, dng, dnb, parts)


def _adamw_math(w, g, m, v):
    m = ADAM_B1 * m + (1.0 - ADAM_B1) * g
    v = ADAM_B2 * v + (1.0 - ADAM_B2) * (g * g)
    m_hat = m / (1.0 - ADAM_B1 ** ADAM_STEP)
    v_hat = v / (1.0 - ADAM_B2 ** ADAM_STEP)
    delta = -ADAM_LR * (m_hat / (jnp.sqrt(v_hat) + ADAM_EPS) + ADAM_WD * w)
    return delta, m, v


def adamw_big(w, m, v, parts0, parts1, name):
    _, r, c = w.shape
    tr = min(r, 256)
    nr = r // tr

    def body(w_ref, m_ref, v_ref, p0_ref, p1_ref, g_ref, d_ref, nm_ref, nv_ref):
        i = pl.program_id(0)

        def total(p_ref):
            acc = p_ref[0].astype(F32)
            for j in range(1, NDEV):
                acc = acc + p_ref[j].astype(F32)
            return acc

        @pl.when(i == 0)
        def _():
            g_ref[...] = total(p0_ref)

        @pl.when(i == 1)
        def _():
            g_ref[...] = total(p1_ref)

        delta, nm, nv = _adamw_math(w_ref[...], g_ref[...], m_ref[...], v_ref[...])
        d_ref[...] = delta
        nm_ref[...] = nm
        nv_ref[...] = nv

    wspec = pl.BlockSpec((None, tr, c), lambda i, j: (i, j, 0))
    p0 = pl.BlockSpec((NDEV, tr, c), lambda i, j: (0, jnp.where(i == 0, j, nr - 1), 0))
    p1 = pl.BlockSpec((NDEV, tr, c), lambda i, j: (0, jnp.where(i == 1, j, 0), 0))
    shp = jax.ShapeDtypeStruct(w.shape, F32)
    return pl.pallas_call(
        body, name=name, grid=(2, nr), in_specs=[wspec, wspec, wspec, p0, p1],
        out_specs=[wspec] * 4, out_shape=[shp] * 4,
        compiler_params=_cp(("arbitrary", "arbitrary")),
    )(w, m, v, parts0, parts1)


def sum_parts(parts, name):
    _, r, c = parts.shape

    def body(p_ref, o_ref):
        acc = p_ref[0]
        for j in range(1, NDEV):
            acc = acc + p_ref[j]
        o_ref[...] = acc

    return pl.pallas_call(body, name=name, out_shape=jax.ShapeDtypeStruct((r, c), F32),
                          compiler_params=_cp())(parts)


def adamw_small(w, g, m, v, name):
    def body(w_ref, g_ref, m_ref, v_ref, d_ref, nm_ref, nv_ref):
        delta, nm, nv = _adamw_math(w_ref[...], g_ref[...], m_ref[...], v_ref[...])
        d_ref[...] = delta
        nm_ref[...] = nm
        nv_ref[...] = nv

    shp = jax.ShapeDtypeStruct(w.shape, F32)
    return pl.pallas_call(body, name=name, out_shape=[shp] * 3, compiler_params=_cp())(w, g, m, v)


def _size(shape):
    n = 1
    for d in shape:
        n *= d
    return n


def _pack(arrays, rows):
    flat = jnp.concatenate([a.reshape(-1) for a in arrays])
    return jnp.pad(flat, (0, rows * 128 - flat.shape[0])).reshape(rows, 128)


def _unpack(flat, shapes):
    out, o = [], 0
    for shp in shapes:
        out.append(flat[o:o + _size(shp)].reshape(shp))
        o += _size(shp)
    return out


def _rows_for(shapes):
    return -(-sum(_size(shp) for shp in shapes) // 1024) * 8


SHARDED_SMALL = (("pool_w", (2, 4, 256, 256), 2), ("sconv_w", (2, KS, D), 2), ("sgu_ln_g", (2, D), 1),
                 ("sgu_ln_b", (2, D), 1), ("dconv_w", (2, KD, D), 2), ("dconv_b", (2, D), 1),
                 ("dnorm_g", (2, D), 1), ("dnorm_b", (2, D), 1))
REPLICATED_SMALL = (("ln_g", (DEPTH, D)), ("ln_b", (DEPTH, D)), ("pool_scale", (2, D)), ("sconv_b", (2, D)),
                    ("sgu_w", (2, HEADS, SGU_BLOCK, SGU_BLOCK)), ("sgu_b", (2, HEADS, SGU_BLOCK)))


def _shard_shape(shape, axis):
    return tuple(d // NDEV if a == axis else d for a, d in enumerate(shape))


def _merge_gathered(g, shape, axis):
    return jnp.moveaxis(g, 0, axis).reshape(shape)


def kernel(x, ln_g, ln_b, w_in_even, w_out_even, pool_w, pool_scale, sconv_w, sconv_b, w_in_odd, w_out_odd, sgu_ln_g, sgu_ln_b, sgu_w, sgu_b, dconv_w, dconv_b, dnorm_g, dnorm_b, loss_target, m_ln_g, m_ln_b, m_w_in_even, m_w_out_even, m_pool_w, m_pool_scale, m_sconv_w, m_sconv_b, m_w_in_odd, m_w_out_odd, m_sgu_ln_g, m_sgu_ln_b, m_sgu_w, m_sgu_b, m_dconv_w, m_dconv_b, m_dnorm_g, m_dnorm_b, v_ln_g, v_ln_b, v_w_in_even, v_w_out_even, v_pool_w, v_pool_scale, v_sconv_w, v_sconv_b, v_w_in_odd, v_w_out_odd, v_sgu_ln_g, v_sgu_ln_b, v_sgu_w, v_sgu_b, v_dconv_w, v_dconv_b, v_dnorm_g, v_dnorm_b):
    given = dict(locals())
    me = 4 * lax.axis_index("x") + 2 * lax.axis_index("y") + lax.axis_index("c")
    xs = x[0]
    target = loss_target[0]

    wc_in = cast_weights(w_in_even, w_in_odd, "cast_w_in")
    wc_out = cast_weights(w_out_even, w_out_odd, "cast_w_out")
    shard_shapes = [_shard_shape(shp, ax) for _, shp, ax in SHARDED_SMALL]
    srows = _rows_for(shard_shapes)
    xb, wg_first, gathered = cast_x(xs, wc_in, _pack([given[n] for n, _, _ in SHARDED_SMALL], srows))
    wg_in = [wg_first] + [None] * (DEPTH - 1)
    wg_out = [None] * DEPTH
    gathered = gathered.reshape(NDEV, -1)
    full, o = {}, 0
    for (n, shp, ax), sshp in zip(SHARDED_SMALL, shard_shapes):
        full[n] = _merge_gathered(gathered[:, o:o + _size(sshp)].reshape((NDEV,) + sshp), shp, ax)
        o += _size(sshp)

    mask = (jnp.arange(SGU_BLOCK)[None, :] // 64) <= (jnp.arange(SGU_BLOCK)[:, None] // 64)
    ws = jnp.where(mask[None, None], sgu_w, 0.0)
    wsb = ws.astype(BF16)
    wstb = jnp.swapaxes(ws, -1, -2).astype(BF16)
    pwb = full["pool_w"].astype(BF16)

    ones = jnp.ones((1, D), F32)
    zeros = jnp.zeros((1, D), F32)

    xres, gp, bp = xs, ones, zeros
    saved, below = [], None
    for layer in range(DEPTH):
        i = layer // 2
        if layer % 2 == 0:
            params = (pwb[i], pool_scale[i][None], full["sconv_w"][i], sconv_b[i][None])
        else:
            params = (full["sgu_ln_g"][i][None], full["sgu_ln_b"][i][None], wsb[i], sgu_b[i][:, :, None],
                      full["dconv_w"][i], full["dconv_b"][i][None], full["dnorm_g"][i][None],
                      full["dnorm_b"][i][None])
        outs = list(proj_in(wg_in[layer], layer, wc_in, wc_out, xb=xb if below is None else None, below=below))
        if layer + 1 < DEPTH:
            wg_in[layer + 1] = outs.pop()
        wg_out[layer] = outs.pop()
        z = outs[0]
        if below is not None:
            xhat, rstd, xb = outs[1:4]
            saved[-1] += [xhat, rstd]
            xres, gp, bp = xhat, below[5], below[6]
        outs = mix_fwd(z, layer, params)
        ycat, extra = outs[0], tuple(outs[1:])
        saved.append([xb, z, ycat, extra])
        below = (ycat, wg_out[layer], xres, gp, bp, ln_g[layer][None], ln_b[layer][None])
    xhat, rstd, dxn, loss_local = head_loss(below, target)
    saved[-1] += [xhat, rstd]

    gsmall = {n: [None] * shp[0] for n, shp in REPLICATED_SMALL}
    gsmall.update({n: [None] * shp[0] for n, shp, _ in SHARDED_SMALL})
    parts_in, parts_out, parts_pw = [None] * DEPTH, [None] * DEPTH, [None] * 2
    small_names = [n for n, _ in REPLICATED_SMALL] + [n for n, _, _ in SHARDED_SMALL]
    gathered_names = [n for n in small_names if n != "pool_w"]
    gathered_shapes = [shp for n, shp in REPLICATED_SMALL] + [shp for n, shp, _ in SHARDED_SMALL if n != "pool_w"]
    grows = _rows_for(gathered_shapes + [(1,)])
    pending, above = None, None
    for layer in reversed(range(DEPTH)):
        i = layer // 2
        xb, z, ycat, extra, xhat, rstd = saved[layer]
        dr, drb, dycat, dg, db = ln_bwd_dycat(xhat, rstd, ln_g[layer][None], wg_out[layer], layer,
                                              dxn=dxn, upstream=above)
        gsmall["ln_g"][layer], gsmall["ln_b"][layer] = dg[0], db[0]
        dwo = dw_out(ycat, drb, layer)
        own_out = pending is None
        riding = dwo if own_out else pending
        if layer % 2 == 0:
            dzb, dpw, dps, dsw, dsb, landed = mix_even_bwd(
                z, dycat, extra[0], extra[1], pwb[i], pool_scale[i][None], full["sconv_w"][i], layer, parts=riding)
            gsmall["pool_scale"][i] = dps[0]
            gsmall["sconv_w"][i], gsmall["sconv_b"][i] = dsw, dsb[0]
        else:
            dzb, dslg, dslb, dws, dsbc, ddw, ddcb, ddng, ddnb, landed = mix_odd_bwd(
                z, dycat, extra[0], full["sgu_ln_g"][i][None], full["sgu_ln_b"][i][None], wsb[i], wstb[i],
                sgu_b[i][:, :, None], full["dconv_w"][i], full["dnorm_g"][i][None], full["dnorm_b"][i][None],
                layer, riding)
            gsmall["sgu_ln_g"][i], gsmall["sgu_ln_b"][i] = dslg[0], dslb[0]
            gsmall["sgu_w"][i], gsmall["sgu_b"][i] = jnp.where(mask[None], dws, 0.0), dsbc[:, :, 0]
            gsmall["dconv_w"][i], gsmall["dconv_b"][i] = ddw, ddcb[0]
            gsmall["dnorm_g"][i], gsmall["dnorm_b"][i] = ddng[0], ddnb[0]
        riders = [] if own_out else [dwo]
        if layer % 2 == 0:
            riders.append(dpw.reshape(4, NDEV, PG // NDEV, PG).transpose(1, 0, 2, 3))
        small_partial = []
        if layer == 0:
            small_partial.append(_pack([jnp.stack(gsmall[n]) for n in gathered_names] + [loss_local], grows))
        outs = list(dw_in(xb, dzb, layer, riders, small_partial))
        if layer == 0:
            small_gathered = outs.pop()
        if layer % 2 == 0:
            parts_pw[i] = outs.pop()
        if own_out:
            parts_out[layer] = landed
        else:
            parts_in[layer + 1] = landed
            parts_out[layer] = outs[1]
        pending = outs[0]
        above = (dzb, wg_in[layer], dr)
    dxn, parts_in[0] = dx_in(dzb, wg_in[0], dr, pending)
    grad_x = dxn[None]

    big = {}
    big["w_in_even"] = adamw_big(w_in_even, m_w_in_even, v_w_in_even, parts_in[0], parts_in[2], "adamw_w_in_even")
    big["w_in_odd"] = adamw_big(w_in_odd, m_w_in_odd, v_w_in_odd, parts_in[1], parts_in[3], "adamw_w_in_odd")
    big["w_out_even"] = adamw_big(w_out_even, m_w_out_even, v_w_out_even, parts_out[0], parts_out[2], "adamw_w_out_even")
    big["w_out_odd"] = adamw_big(w_out_odd, m_w_out_odd, v_w_out_odd, parts_out[1], parts_out[3], "adamw_w_out_odd")

    gsum = sum_parts(small_gathered, "sum_small_grads")
    unpacked = _unpack(gsum.reshape(-1), gathered_shapes + [()])
    loss = unpacked.pop()
    gfull = dict(zip(gathered_names, unpacked))
    own_shapes = [shp for _, shp in REPLICATED_SMALL] + shard_shapes
    gown = {n: gfull[n] for n, _ in REPLICATED_SMALL}
    for (n, shp, ax), sshp in zip(SHARDED_SMALL, shard_shapes):
        if n == "pool_w":
            gown[n] = jnp.stack([sum_parts(p.reshape(NDEV, -1, 128), f"sum_pool_w_{j}").reshape(sshp[1:])
                                 for j, p in enumerate(parts_pw)])
        else:
            gown[n] = lax.dynamic_slice_in_dim(gfull[n], me * sshp[ax], sshp[ax], axis=ax)
    orows = _rows_for(own_shapes)
    packed = [_pack([src[n] for n in small_names], orows) for src in
              (given, gown, {n: given["m_" + n] for n in small_names}, {n: given["v_" + n] for n in small_names})]
    sd, sm, sv = adamw_small(*packed, "adamw_small")
    small = {}
    for n, d_, m_, v_ in zip(small_names, _unpack(sd.reshape(-1), own_shapes), _unpack(sm.reshape(-1), own_shapes),
                             _unpack(sv.reshape(-1), own_shapes)):
        small[n] = (gown[n], d_, m_, v_)

    weights = ['ln_g', 'ln_b', 'w_in_even', 'w_out_even', 'pool_w', 'pool_scale', 'sconv_w', 'sconv_b', 'w_in_odd',
               'w_out_odd', 'sgu_ln_g', 'sgu_ln_b', 'sgu_w', 'sgu_b', 'dconv_w', 'dconv_b', 'dnorm_g', 'dnorm_b']
    res = {n: (big[n] if n in big else small[n]) for n in weights}
    return (loss, grad_x, *[res[n][0] for n in weights], *[res[n][1] for n in weights],
            *[res[n][2] for n in weights], *[res[n][3] for n in weights])
```

```python
import jax
import jax.numpy as jnp
from jax import lax
from jax.experimental import pallas as pl
from jax.experimental.pallas import tpu as pltpu

F32 = jnp.float32
BF16 = jnp.bfloat16

D = 1024
DMIX = 2048
NIN = 6144
NDEV = 8
CW = NIN // NDEV
RW = DMIX // NDEV
DEPTH = 4
ALPHA = (2 * DEPTH) ** 0.25
LN_EPS = 1e-5
POOL_WINDOWS = (2, 4, 8, 16)
PG = 256
SGU_BLOCK = 128
HEADS = 4
HD = 256
KD = 31
KS = 3
SUB = 8
RC = 16
CB = 512
HP = 32
HS = 8
HC = 32
NSEM = 7

ADAM_LR = 0.001
ADAM_B1 = 0.9
ADAM_B2 = 0.999
ADAM_EPS = 1e-08
ADAM_WD = 0.01
ADAM_STEP = 10

VMEM_LIMIT = 56 * 1024 * 1024
MESH = pl.DeviceIdType.MESH
ANY = pl.BlockSpec(memory_space=pl.ANY)


def _cp(sem=None):
    if sem is None:
        return pltpu.CompilerParams(vmem_limit_bytes=VMEM_LIMIT)
    return pltpu.CompilerParams(dimension_semantics=sem, vmem_limit_bytes=VMEM_LIMIT)


def _sig(x):
    return 0.5 * jnp.tanh(0.5 * x) + 0.5


def _dot(a, b):
    return jnp.dot(a, b, preferred_element_type=F32)


def _dot_nt(a, b):
    return lax.dot_general(a, b, (((1,), (1,)), ((), ())), preferred_element_type=F32)


def _dot_tn(a, b):
    return lax.dot_general(a, b, (((0,), (0,)), ((), ())), preferred_element_type=F32)


def _rowsum(x):
    return jnp.sum(x, axis=0, keepdims=True)


def _comm_scratch(n):
    return [pltpu.SemaphoreType.DMA((n * NSEM,)), pltpu.SemaphoreType.DMA((n * NSEM,)),
            pltpu.SemaphoreType.DMA((n,))]


def _gather_plan(src_ref, out_ref, sems, n):
    send_sems, recv_sems, local_sems = sems
    base = n * NSEM
    x, y, c = lax.axis_index("x"), lax.axis_index("y"), lax.axis_index("c")
    me, sibling = (x, y, c), (x, y, 1 - c)
    chips = [(1 - x, y), (x, 1 - y), (1 - x, 1 - y)]

    def slot(px, py, pc):
        return out_ref.at[4 * px + 2 * py + pc]

    def copy(k, blk, to, src=None):
        return pltpu.make_async_remote_copy(
            src_ref=slot(*blk) if src is None else src, dst_ref=slot(*blk),
            send_sem=send_sems.at[base + k], recv_sem=recv_sems.at[base + k],
            device_id=to, device_id_type=MESH)

    def mine():
        return pltpu.make_async_copy(src_ref, slot(*me), local_sems.at[n])

    def start():
        mine().start()
        copy(0, me, sibling, src=src_ref).start()
        for j, chip in enumerate(chips):
            copy(1 + j, me, (*chip, c), src=src_ref).start()

    def finish():
        for j, chip in enumerate(chips):
            copy(1 + j, (*chip, c), me).wait_recv()
            copy(4 + j, (*chip, c), sibling).start()
        copy(0, sibling, me).wait_recv()
        for j, chip in enumerate(chips):
            copy(4 + j, (*chip, 1 - c), me).wait_recv()
        copy(0, me, sibling, src=src_ref).wait_send()
        for j, chip in enumerate(chips):
            copy(1 + j, me, (*chip, c), src=src_ref).wait_send()
            copy(4 + j, (*chip, c), sibling).wait_send()
        mine().wait()

    return start, finish


def _exchange_plan(p_ref, out_ref, sems, n):
    send_sems, recv_sems, local_sems = sems
    base = n * NSEM
    x, y, c = lax.axis_index("x"), lax.axis_index("y"), lax.axis_index("c")
    me = 4 * x + 2 * y + c

    def copy(r, landing):
        px, py, pc = x ^ (r >> 2), y ^ ((r >> 1) & 1), c ^ (r & 1)
        peer = 4 * px + 2 * py + pc
        return pltpu.make_async_remote_copy(
            src_ref=p_ref.at[peer], dst_ref=out_ref.at[peer if landing else me],
            send_sem=send_sems.at[base + r - 1], recv_sem=recv_sems.at[base + r - 1],
            device_id=(px, py, pc), device_id_type=MESH)

    def mine():
        return pltpu.make_async_copy(p_ref.at[me], out_ref.at[me], local_sems.at[n])

    def start():
        mine().start()
        for r in range(1, NDEV):
            copy(r, False).start()

    def finish():
        for r in range(1, NDEV):
            copy(r, True).wait_recv()
        for r in range(1, NDEV):
            copy(r, False).wait_send()
        mine().wait()

    return start, finish


def cast_x(x, w_first, small):
    s = x.shape[0]
    tm = min(s, 512)
    nt = s // tm

    def body(x_ref, wf_ref, sm_ref, o_ref, gw_ref, gs_ref, *sems):
        i = pl.program_id(0)
        plans = [_gather_plan(wf_ref.at[0], gw_ref, sems, 0), _gather_plan(sm_ref, gs_ref, sems, 1)]

        @pl.when(i == 0)
        def _():
            for start, _ in plans:
                start()

        o_ref[...] = x_ref[...].astype(BF16)

        @pl.when(i == nt - 1)
        def _():
            for _, finish in plans:
                finish()

    return pl.pallas_call(
        body, name="cast_x", grid=(nt,),
        in_specs=[pl.BlockSpec((tm, D), lambda i: (i, 0)), ANY, ANY],
        out_specs=[pl.BlockSpec((tm, D), lambda i: (i, 0)), ANY, ANY],
        out_shape=[jax.ShapeDtypeStruct((s, D), BF16),
                   jax.ShapeDtypeStruct((NDEV,) + w_first.shape[1:], w_first.dtype),
                   jax.ShapeDtypeStruct((NDEV,) + small.shape, small.dtype)],
        scratch_shapes=_comm_scratch(2), compiler_params=_cp(("arbitrary",)),
    )(x, w_first, small)


def cast_weights(w_even, w_odd, name):
    _, r, c = w_even.shape

    def body(e_ref, o_ref, out_ref):
        layer = pl.program_id(0)

        @pl.when(layer % 2 == 0)
        def _():
            out_ref[...] = e_ref[...].astype(BF16)

        @pl.when(layer % 2 == 1)
        def _():
            out_ref[...] = o_ref[...].astype(BF16)

    spec = pl.BlockSpec((None, r, c), lambda l: (l // 2, 0, 0))
    return pl.pallas_call(
        body, name=name, grid=(DEPTH,), in_specs=[spec, spec],
        out_specs=pl.BlockSpec((None, r, c), lambda l: (l, 0, 0)),
        out_shape=jax.ShapeDtypeStruct((DEPTH, r, c), BF16), compiler_params=_cp(("parallel",)),
    )(w_even, w_odd)


def _deepnorm(y_ref, wo_ref, xr_ref, gp_ref, bp_ref, g_ref, b_ref):
    xin = xr_ref[...] * gp_ref[...] + bp_ref[...]
    r = ALPHA * xin + _dot(y_ref[...], wo_ref[...].reshape(DMIX, D))
    mu = jnp.mean(r, axis=-1, keepdims=True)
    d = r - mu
    var = jnp.mean(d * d, axis=-1, keepdims=True)
    rstd = lax.rsqrt(var + LN_EPS)
    xh = d * rstd
    return xh, rstd, xh * g_ref[...] + b_ref[...]


def proj_in(wg, layer, wc_in, wc_out, xb=None, below=None):
    fused = below is not None
    s = below[0].shape[0] if fused else xb.shape[0]
    tm = min(s, 256 if fused else 512)
    nt = s // tm
    nxt = layer + 1 < DEPTH
    nhead = 7 if fused else 1

    def body(*refs):
        head, refs = refs[:nhead], refs[nhead:]
        if nxt:
            w_ref, wco_ref, wci_ref = refs[:3]
            refs = refs[3:]
        else:
            w_ref, wco_ref = refs[:2]
            refs = refs[2:]
        o_ref = refs[0]
        if fused:
            xh_ref, rs_ref, xb_ref = refs[1:4]
            refs = refs[4:]
        else:
            refs = refs[1:]
        go_ref = refs[0]
        sems = refs[-3:]
        i = pl.program_id(0)
        plans = [_gather_plan(wco_ref.at[layer], go_ref, sems, 0)]
        if nxt:
            plans.append(_gather_plan(wci_ref.at[layer + 1], refs[1], sems, 1))

        @pl.when(i == 0)
        def _():
            for start, _ in plans:
                start()

        if fused:
            xh, rstd, xn = _deepnorm(*head)
            xh_ref[...] = xh
            rs_ref[...] = rstd
            x = xn.astype(BF16)
            xb_ref[...] = x
        else:
            x = head[0][...]
        for k in range(NDEV):
            o_ref[:, k * CW:(k + 1) * CW] = _dot(x, w_ref[k]).astype(BF16)

        @pl.when(i == nt - 1)
        def _():
            for _, finish in plans:
                finish()

    row = pl.BlockSpec((tm, D), lambda i: (i, 0))
    vec = pl.BlockSpec((1, D), lambda i: (0, 0))
    if fused:
        in_specs = [pl.BlockSpec((tm, DMIX), lambda i: (i, 0)),
                    pl.BlockSpec((NDEV, RW, D), lambda i: (0, 0, 0), pipeline_mode=pl.Buffered(1)),
                    row, vec, vec, vec, vec]
        args = list(below)
    else:
        in_specs, args = [row], [xb]
    in_specs += [pl.BlockSpec((NDEV, D, CW), lambda i: (0, 0, 0), pipeline_mode=pl.Buffered(1)), ANY]
    args += [wg, wc_out]
    if nxt:
        in_specs.append(ANY)
        args.append(wc_in)
    out_specs = [pl.BlockSpec((tm, NIN), lambda i: (i, 0))]
    out_shape = [jax.ShapeDtypeStruct((s, NIN), BF16)]
    if fused:
        out_specs += [row, pl.BlockSpec((tm, 1), lambda i: (i, 0)), row]
        out_shape += [jax.ShapeDtypeStruct((s, D), F32), jax.ShapeDtypeStruct((s, 1), F32),
                      jax.ShapeDtypeStruct((s, D), BF16)]
    out_specs.append(ANY)
    out_shape.append(jax.ShapeDtypeStruct((NDEV, RW, D), BF16))
    if nxt:
        out_specs.append(ANY)
        out_shape.append(jax.ShapeDtypeStruct((NDEV, D, CW), BF16))
    return pl.pallas_call(
        body, name=f"proj_in_{layer}", grid=(nt,), in_specs=in_specs, out_specs=out_specs,
        out_shape=out_shape, scratch_shapes=_comm_scratch(2 if nxt else 1),
        compiler_params=_cp(("arbitrary",)),
    )(*args)


def mix_fwd(z, layer, params):
    s = z.shape[0]
    tm = min(s, 256)
    even = layer % 2 == 0
    npar = len(params)
    nout = 3 if even else 2

    def even_tile(zr, j, par, outs, scr):
        pw_ref, ps_ref, sw_ref, sb_ref = par
        yc_ref, pb_ref, cv_ref = outs
        exa, eq, ya0, la, lb = scr
        for c in range(tm // RC):
            r0, r1 = c * RC, (c + 1) * RC
            exa[HP + r0:HP + r1, :] = zr[r0:r1, _cols(0)].astype(F32)
            eq[HS + r0:HS + r1, :] = zr[r0:r1, _cols(4)].astype(F32) * zr[r0:r1, _cols(2)].astype(F32)

        def level(src, dst, shift, c0, start):
            for r0 in range(start, HP + tm, 2 * RC):
                r1 = min(r0 + 2 * RC, HP + tm)
                dst[r0:r1, c0:D] = src[r0:r1, c0:D] + src[r0 - shift:r1 - shift, c0:D]

        level(exa, la, 1, PG, 8)
        level(la, lb, 2, 2 * PG, 16)
        level(lb, la, 4, 3 * PG, 24)
        last = ((exa, 1), (la, 2), (lb, 4), (la, 8))
        for c in range(tm // RC):
            r0, r1 = HP + c * RC, HP + (c + 1) * RC
            pos = (j * tm + c * RC + lax.broadcasted_iota(jnp.int32, (RC, 1), 0) + 1).astype(F32)
            for gi, w in enumerate(POOL_WINDOWS):
                c0, c1 = gi * PG, (gi + 1) * PG
                src, shift = last[gi]
                acc = src[r0:r1, c0:c1] + src[r0 - shift:r1 - shift, c0:c1]
                pooled = acc / jnp.minimum(pos, float(w)) - exa[r0:r1, c0:c1]
                pb_ref[c * RC:(c + 1) * RC, c0:c1] = pooled.astype(BF16)
        for gi in range(len(POOL_WINDOWS)):
            c0, c1 = gi * PG, (gi + 1) * PG
            ya0[:, c0:c1] = _dot(pb_ref[:, c0:c1], pw_ref[gi])
        for c in range(tm // RC):
            r0, r1 = c * RC, (c + 1) * RC
            ga = zr[r0:r1, _cols(1)].astype(F32)
            yc_ref[r0:r1, 0:D] = (ya0[r0:r1, :] * ps_ref[...] * (ga * _sig(ga))).astype(BF16)
            cv = (sw_ref[2:3, :] * eq[HS + r0:HS + r1, :] + sw_ref[1:2, :] * eq[HS + r0 - 1:HS + r1 - 1, :]
                  + sw_ref[0:1, :] * eq[HS + r0 - 2:HS + r1 - 2, :] + sb_ref[...])
            cv_ref[r0:r1, :] = cv.astype(BF16)
            gb = zr[r0:r1, _cols(5)].astype(F32)
            yc_ref[r0:r1, D:2 * D] = (zr[r0:r1, _cols(3)].astype(F32) * cv * (gb * _sig(gb))).astype(BF16)
        exa[0:HP, :] = exa[tm:tm + HP, :]
        eq[0:HS, :] = eq[tm:tm + HS, :]

    def odd_tile(zr, j, par, outs, scr):
        slg_ref, slb_ref, ws_ref, sb_ref, dw_ref, dcb_ref, dng_ref, dnb_ref = par
        yc_ref, cz_ref = outs
        ezg, esh = scr
        vh, _ = _ln_rows(zr[:, _cols(1)].astype(F32))
        vnb = (vh * slg_ref[...] + slb_ref[...]).astype(BF16)
        for n in range(tm // SGU_BLOCK):
            r0, r1 = n * SGU_BLOCK, (n + 1) * SGU_BLOCK
            for hd in range(HEADS):
                c0, c1 = hd * HD, (hd + 1) * HD
                sv = _dot(ws_ref[hd], vnb[r0:r1, c0:c1]) + sb_ref[hd]
                gc = zr[r0:r1, 2 * D + c0:2 * D + c1].astype(F32)
                yc_ref[r0:r1, c0:c1] = (zr[r0:r1, c0:c1].astype(F32) * sv * (gc * _sig(gc))).astype(BF16)
        ezg[HC:HC + tm, :] = zr[:, _cols(3)].astype(F32) * _sig(zr[:, _cols(4)].astype(F32))
        _shifted_copies(ezg, esh, tm + HC - SUB)
        cz = jnp.zeros((tm, D), F32) + dcb_ref[...]
        for k in range(KD):
            cz = cz + dw_ref[k:k + 1, :] * _tap(ezg, esh, HC - (KD - 1) + k, 0, tm)
        cz_ref[...] = cz
        zh, _ = _ln_rows(cz)
        zn = zh * dng_ref[...] + dnb_ref[...]
        gd = zr[:, _cols(5)].astype(F32)
        yc_ref[:, D:2 * D] = ((zn * _sig(zn)) * (gd * _sig(gd))).astype(BF16)
        ezg[0:HC, :] = ezg[tm:tm + HC, :]

    def body(z_ref, *refs):
        par, outs, scr = refs[:npar], refs[npar:npar + nout], refs[npar + nout:]
        j = pl.program_id(0)

        @pl.when(j == 0)
        def _():
            if even:
                scr[0][0:HP, :] = jnp.zeros((HP, D), F32)
                scr[1][0:HS, :] = jnp.zeros((HS, D), F32)
            else:
                scr[0][0:HC, :] = jnp.zeros((HC, D), F32)

        (even_tile if even else odd_tile)(z_ref, j, par, outs, scr)

    row = lambda c: pl.BlockSpec((tm, c), lambda j: (j, 0))
    vec = pl.BlockSpec((1, D), lambda j: (0, 0))
    if even:
        par_specs = [pl.BlockSpec((4, PG, PG), lambda j: (0, 0, 0)), vec, pl.BlockSpec((KS, D), lambda j: (0, 0)), vec]
        out_specs = [row(DMIX), row(D), row(D)]
        out_shape = [jax.ShapeDtypeStruct((s, DMIX), BF16), jax.ShapeDtypeStruct((s, D), BF16),
                     jax.ShapeDtypeStruct((s, D), BF16)]
        scratch = [pltpu.VMEM((HP + tm, D), F32), pltpu.VMEM((HS + tm, D), F32), pltpu.VMEM((tm, D), F32),
                   pltpu.VMEM((HP + tm, D), F32), pltpu.VMEM((HP + tm, D), F32)]
    else:
        par_specs = [vec, vec, pl.BlockSpec((HEADS, SGU_BLOCK, SGU_BLOCK), lambda j: (0, 0, 0)),
                     pl.BlockSpec((HEADS, SGU_BLOCK, 1), lambda j: (0, 0, 0)),
                     pl.BlockSpec((KD, D), lambda j: (0, 0)), vec, vec, vec]
        out_specs = [row(DMIX), row(D)]
        out_shape = [jax.ShapeDtypeStruct((s, DMIX), BF16), jax.ShapeDtypeStruct((s, D), F32)]
        scratch = [pltpu.VMEM((HC + tm, D), F32), pltpu.VMEM((SUB - 1, tm + HC - SUB, D), F32)]
    return pl.pallas_call(
        body, name=f"mix_fwd_{layer}", grid=(s // tm,), in_specs=[row(NIN)] + par_specs, out_specs=out_specs,
        out_shape=out_shape, scratch_shapes=scratch, compiler_params=_cp(("arbitrary",)),
    )(z, *params)


def _ln_bwd_tile(dxo, xh, rstd, g_ref, wo_ref, dr_ref, drb_ref, dyc_ref, dg_ref, db_ref):
    @pl.when(pl.program_id(0) == 0)
    def _():
        dg_ref[...] = jnp.zeros_like(dg_ref)
        db_ref[...] = jnp.zeros_like(db_ref)

    dg_ref[...] += _rowsum(dxo * xh)
    db_ref[...] += _rowsum(dxo)
    dr = _ln_rows_bwd(dxo, xh, rstd, g_ref[...])
    dr_ref[...] = dr
    drb = dr.astype(BF16)
    drb_ref[...] = drb
    dyc_ref[...] = _dot_nt(drb, wo_ref[...].reshape(DMIX, D)).astype(BF16)


def _ln_bwd_outs(s, tm):
    row = pl.BlockSpec((tm, D), lambda i: (i, 0))
    vec = pl.BlockSpec((1, D), lambda i: (0, 0))
    return ([row, row, pl.BlockSpec((tm, DMIX), lambda i: (i, 0)), vec, vec],
            [jax.ShapeDtypeStruct((s, D), F32), jax.ShapeDtypeStruct((s, D), BF16),
             jax.ShapeDtypeStruct((s, DMIX), BF16), jax.ShapeDtypeStruct((1, D), F32),
             jax.ShapeDtypeStruct((1, D), F32)])


def head_loss_bwd(below, target):
    s = target.shape[0]
    tm = min(s, 256)

    def body(*refs):
        t_ref = refs[7]
        loss_ref = refs[-1]
        xh, rstd, xn = _deepnorm(*refs[:7])
        err = xn - t_ref[...]

        @pl.when(pl.program_id(0) == 0)
        def _():
            loss_ref[...] = jnp.zeros_like(loss_ref)

        loss_ref[...] += 0.5 * jnp.sum(jnp.mean(err * err, axis=-1, keepdims=True), axis=0, keepdims=True)
        _ln_bwd_tile(err * (1.0 / D), xh, rstd, refs[5], refs[1], *refs[8:13])

    row = pl.BlockSpec((tm, D), lambda i: (i, 0))
    vec = pl.BlockSpec((1, D), lambda i: (0, 0))
    out_specs, out_shape = _ln_bwd_outs(s, tm)
    return pl.pallas_call(
        body, name="head_loss_bwd", grid=(s // tm,),
        in_specs=[pl.BlockSpec((tm, DMIX), lambda i: (i, 0)), pl.BlockSpec((NDEV, RW, D), lambda i: (0, 0, 0)),
                  row, vec, vec, vec, vec, row],
        out_specs=out_specs + [pl.BlockSpec((1, 1), lambda i: (0, 0))],
        out_shape=out_shape + [jax.ShapeDtypeStruct((1, 1), F32)],
        compiler_params=_cp(("arbitrary",)),
    )(*below, target)


def ln_bwd_dycat(xhat, rstd, g, wog, layer, upstream):
    s = xhat.shape[0]
    tm = min(s, 256)

    def body(dz_ref, wi_ref, dru_ref, xh_ref, rs_ref, g_ref, w_ref, *outs):
        dxo = ALPHA * dru_ref[...]
        for k in range(NDEV):
            dxo += _dot_nt(dz_ref[:, k * CW:(k + 1) * CW], wi_ref[k])
        _ln_bwd_tile(dxo, xh_ref[...], rs_ref[...], g_ref, w_ref, *outs)

    row = pl.BlockSpec((tm, D), lambda i: (i, 0))
    vec = pl.BlockSpec((1, D), lambda i: (0, 0))
    out_specs, out_shape = _ln_bwd_outs(s, tm)
    return pl.pallas_call(
        body, name=f"ln_bwd_dycat_{layer}", grid=(s // tm,),
        in_specs=[pl.BlockSpec((tm, NIN), lambda i: (i, 0)),
                  pl.BlockSpec((NDEV, D, CW), lambda i: (0, 0, 0), pipeline_mode=pl.Buffered(1)), row,
                  row, pl.BlockSpec((tm, 1), lambda i: (i, 0)), vec,
                  pl.BlockSpec((NDEV, RW, D), lambda i: (0, 0, 0))],
        out_specs=out_specs, out_shape=out_shape, compiler_params=_cp(("arbitrary",)),
    )(*upstream, xhat, rstd, g, wog)


def dx_in(dzb, wg, dr, parts):
    s = dzb.shape[0]
    tm = min(s, 256)
    nt = s // tm

    def body(dz_ref, w_ref, dr_ref, p_ref, o_ref, land_ref, *sems):
        start, finish = _exchange_plan(p_ref, land_ref, sems, 0)
        pl.when(pl.program_id(0) == 0)(start)
        acc = ALPHA * dr_ref[...]
        for k in range(NDEV):
            acc += _dot_nt(dz_ref[:, k * CW:(k + 1) * CW], w_ref[k])
        o_ref[...] = acc
        pl.when(pl.program_id(0) == nt - 1)(finish)

    row = pl.BlockSpec((tm, D), lambda i: (i, 0))
    return pl.pallas_call(
        body, name="dx_in_0", grid=(nt,),
        in_specs=[pl.BlockSpec((tm, NIN), lambda i: (i, 0)),
                  pl.BlockSpec((NDEV, D, CW), lambda i: (0, 0, 0), pipeline_mode=pl.Buffered(1)), row, ANY],
        out_specs=[row, ANY],
        out_shape=[jax.ShapeDtypeStruct((s, D), F32), jax.ShapeDtypeStruct(parts.shape, parts.dtype)],
        scratch_shapes=_comm_scratch(1), compiler_params=_cp(("arbitrary",)),
    )(dzb, wg, dr, parts)


def dw_in(xb, dzb, layer, parts=(), gathers=()):
    s = xb.shape[0]
    tm = min(s, 2048)
    nt = s // tm
    ne = len(parts)
    n = ne + len(gathers)

    def body(*refs):
        x_ref, dz_ref = refs[:2]
        p_refs = refs[2:2 + n]
        o_ref = refs[2 + n]
        land_refs = refs[3 + n:3 + 2 * n]
        acc = refs[3 + 2 * n]
        plans = [(_exchange_plan if j < ne else _gather_plan)(p, land, refs[-3:], j)
                 for j, (p, land) in enumerate(zip(p_refs, land_refs))]
        k, t = pl.program_id(0), pl.program_id(1)

        @pl.when((k == 0) & (t == 0))
        def _():
            for start, _ in plans:
                start()

        @pl.when(t == 0)
        def _():
            acc[...] = jnp.zeros_like(acc)

        acc[...] += _dot_tn(x_ref[...], dz_ref[...])

        @pl.when(t == nt - 1)
        def _():
            o_ref[...] = acc[...].astype(BF16)

        @pl.when((k == NDEV - 1) & (t == nt - 1))
        def _():
            for _, finish in plans:
                finish()

    return pl.pallas_call(
        body, name=f"dw_in_{layer}", grid=(NDEV, nt),
        in_specs=[pl.BlockSpec((tm, D), lambda k, t: (t, 0)), pl.BlockSpec((tm, CW), lambda k, t: (t, k))] + [ANY] * n,
        out_specs=[pl.BlockSpec((None, D, CW), lambda k, t: (k, 0, 0))] + [ANY] * n,
        out_shape=[jax.ShapeDtypeStruct((NDEV, D, CW), BF16)] + [jax.ShapeDtypeStruct(p.shape, p.dtype) for p in parts]
        + [jax.ShapeDtypeStruct((NDEV,) + g.shape, g.dtype) for g in gathers],
        scratch_shapes=[pltpu.VMEM((D, CW), F32)] + (_comm_scratch(n) if n else []),
        compiler_params=_cp(("arbitrary", "arbitrary")),
    )(xb, dzb, *parts, *gathers)


def dw_out(ycat, drb, layer):
    s = ycat.shape[0]
    tm = min(s, 1024)
    nt = s // tm

    def body(y_ref, dr_ref, o_ref, acc):
        t = pl.program_id(0)

        @pl.when(t == 0)
        def _():
            acc[...] = jnp.zeros_like(acc)

        acc[...] += _dot_tn(y_ref[...], dr_ref[...])

        @pl.when(t == nt - 1)
        def _():
            o_ref[...] = acc[...].reshape(NDEV, RW, D).astype(BF16)

    return pl.pallas_call(
        body, name=f"dw_out_{layer}", grid=(nt,),
        in_specs=[pl.BlockSpec((tm, DMIX), lambda t: (t, 0)), pl.BlockSpec((tm, D), lambda t: (t, 0))],
        out_specs=pl.BlockSpec((NDEV, RW, D), lambda t: (0, 0, 0)),
        out_shape=jax.ShapeDtypeStruct((NDEV, RW, D), BF16),
        scratch_shapes=[pltpu.VMEM((DMIX, D), F32)],
        compiler_params=_cp(("arbitrary",)),
    )(ycat, drb)


def _cols(j):
    return slice(j * D, (j + 1) * D)


def mix_even_bwd(z, dycat, pooled, cv, pwb, pscale, sw, layer, parts=None):
    s = z.shape[0]
    tm = min(s, 256)
    nt = s // tm
    carry = parts is not None

    def body(*refs):
        z_ref, dy_ref, pb_ref, cv_ref, pw_ref, ps_ref, sw_ref = refs[:7]
        if carry:
            p_ref = refs[7]
            dz_ref, dpw_ref, dps_ref, dsw_ref, dsb_ref, land_ref = refs[8:14]
            edp, edc, ya0, t1s, dpl, la, lb = refs[14:21]
            start, finish = _exchange_plan(p_ref, land_ref, refs[-3:], 0)
        else:
            dz_ref, dpw_ref, dps_ref, dsw_ref, dsb_ref = refs[7:12]
            edp, edc, ya0, t1s, dpl, la, lb = refs[12:19]
        i = pl.program_id(0)
        if carry:
            pl.when(i == 0)(start)

        @pl.when(i == 0)
        def _():
            edp[tm:tm + HP, :] = jnp.zeros((HP, D), F32)
            edc[tm:tm + HS, :] = jnp.zeros((HS, D), F32)
            dpw_ref[...] = jnp.zeros_like(dpw_ref)
            dps_ref[...] = jnp.zeros_like(dps_ref)
            dsw_ref[...] = jnp.zeros_like(dsw_ref)
            dsb_ref[...] = jnp.zeros_like(dsb_ref)

        blocks = [(r * RC, (r + 1) * RC, c * CB, (c + 1) * CB) for r in range(tm // RC) for c in range(D // CB)]
        for gi in range(len(POOL_WINDOWS)):
            c0, c1 = gi * PG, (gi + 1) * PG
            ya0[:, c0:c1] = _dot(pb_ref[:, c0:c1], pw_ref[gi])
        for r0, r1, c0, c1 in blocks:
            ga = z_ref[r0:r1, D + c0:D + c1].astype(F32)
            sg = _sig(ga)
            sil = ga * sg
            dya = dy_ref[r0:r1, c0:c1].astype(F32)
            y0 = ya0[r0:r1, c0:c1]
            ps = ps_ref[:, c0:c1]
            dps_ref[:, c0:c1] += _rowsum(dya * y0 * sil)
            dz_ref[r0:r1, D + c0:D + c1] = (dya * y0 * ps * (sg * (1.0 + ga * (1.0 - sg)))).astype(BF16)
            t1s[r0:r1, c0:c1] = (dya * ps * sil).astype(BF16)
        for gi in range(len(POOL_WINDOWS)):
            c0, c1 = gi * PG, (gi + 1) * PG
            dpl[:, c0:c1] = _dot_nt(t1s[:, c0:c1], pw_ref[gi])
            dpw_ref[gi] += _dot_tn(pb_ref[:, c0:c1], t1s[:, c0:c1])
        for r in range(tm // RC):
            r0, r1 = r * RC, (r + 1) * RC
            pos = ((nt - 1 - i) * tm + r0 + lax.broadcasted_iota(jnp.int32, (RC, 1), 0) + 1).astype(F32)
            for gi, w in enumerate(POOL_WINDOWS):
                c0, c1 = gi * PG, (gi + 1) * PG
                edp[r0:r1, c0:c1] = dpl[r0:r1, c0:c1] / jnp.minimum(pos, float(w))

        def level(src, dst, shift, c0, end):
            for r0 in range(0, end, 2 * RC):
                r1 = min(r0 + 2 * RC, end)
                dst[r0:r1, c0:D] = src[r0:r1, c0:D] + src[r0 + shift:r1 + shift, c0:D]

        level(edp, la, 1, PG, tm + 24)
        level(la, lb, 2, 2 * PG, tm + 16)
        level(lb, la, 4, 3 * PG, tm + 8)
        last = ((edp, 1), (la, 2), (lb, 4), (la, 8))
        for r in range(tm // RC):
            r0, r1 = r * RC, (r + 1) * RC
            for gi in range(len(POOL_WINDOWS)):
                c0, c1 = gi * PG, (gi + 1) * PG
                src, shift = last[gi]
                acc = src[r0:r1, c0:c1] + src[r0 + shift:r1 + shift, c0:c1] - dpl[r0:r1, c0:c1]
                dz_ref[r0:r1, c0:c1] = acc.astype(BF16)

        for r0, r1, c0, c1 in blocks:
            gb = z_ref[r0:r1, 5 * D + c0:5 * D + c1].astype(F32)
            sgb = _sig(gb)
            silb = gb * sgb
            dyb = dy_ref[r0:r1, D + c0:D + c1].astype(F32)
            cvv = cv_ref[r0:r1, c0:c1].astype(F32)
            bg = z_ref[r0:r1, 3 * D + c0:3 * D + c1].astype(F32)
            dz_ref[r0:r1, 3 * D + c0:3 * D + c1] = (dyb * cvv * silb).astype(BF16)
            dz_ref[r0:r1, 5 * D + c0:5 * D + c1] = (dyb * bg * cvv * (sgb * (1.0 + gb * (1.0 - sgb)))).astype(BF16)
            dcv = dyb * bg * silb
            dsb_ref[:, c0:c1] += _rowsum(dcv)
            edc[r0:r1, c0:c1] = dcv
        for r0, r1, c0, c1 in blocks:
            h = z_ref[r0:r1, 2 * D + c0:2 * D + c1].astype(F32)
            cg = z_ref[r0:r1, 4 * D + c0:4 * D + c1].astype(F32)
            q = cg * h
            d0 = edc[r0:r1, c0:c1]
            d1 = edc[r0 + 1:r1 + 1, c0:c1]
            d2 = edc[r0 + 2:r1 + 2, c0:c1]
            dq = sw_ref[2:3, c0:c1] * d0 + sw_ref[1:2, c0:c1] * d1 + sw_ref[0:1, c0:c1] * d2
            dsw_ref[2:3, c0:c1] += _rowsum(q * d0)
            dsw_ref[1:2, c0:c1] += _rowsum(q * d1)
            dsw_ref[0:1, c0:c1] += _rowsum(q * d2)
            dz_ref[r0:r1, 4 * D + c0:4 * D + c1] = (dq * h).astype(BF16)
            dz_ref[r0:r1, 2 * D + c0:2 * D + c1] = (dq * cg).astype(BF16)
        edp[tm:tm + HP, :] = edp[0:HP, :]
        edc[tm:tm + HS, :] = edc[0:HS, :]
        if carry:
            pl.when(i == nt - 1)(finish)

    rev = lambda i: (nt - 1 - i, 0)
    vec = pl.BlockSpec((1, D), lambda i: (0, 0))
    in_specs = [pl.BlockSpec((tm, NIN), rev), pl.BlockSpec((tm, DMIX), rev),
                pl.BlockSpec((tm, D), rev), pl.BlockSpec((tm, D), rev),
                pl.BlockSpec((4, PG, PG), lambda i: (0, 0, 0)), vec,
                pl.BlockSpec((KS, D), lambda i: (0, 0))]
    out_specs = [pl.BlockSpec((tm, NIN), rev), pl.BlockSpec((4, PG, PG), lambda i: (0, 0, 0)), vec,
                 pl.BlockSpec((KS, D), lambda i: (0, 0)), vec]
    out_shape = [jax.ShapeDtypeStruct((s, NIN), BF16), jax.ShapeDtypeStruct((4, PG, PG), F32),
                 jax.ShapeDtypeStruct((1, D), F32), jax.ShapeDtypeStruct((KS, D), F32),
                 jax.ShapeDtypeStruct((1, D), F32)]
    args = [z, dycat, pooled, cv, pwb, pscale, sw]
    scratch = [pltpu.VMEM((tm + HP, D), F32), pltpu.VMEM((tm + HS, D), F32), pltpu.VMEM((tm, D), F32),
               pltpu.VMEM((tm, D), BF16), pltpu.VMEM((tm, D), F32), pltpu.VMEM((tm + HP, D), F32),
               pltpu.VMEM((tm + HP, D), F32)]
    if carry:
        in_specs.append(ANY)
        args.append(parts)
        out_specs.append(ANY)
        out_shape.append(jax.ShapeDtypeStruct(parts.shape, parts.dtype))
        scratch += _comm_scratch(1)
    return pl.pallas_call(
        body, name=f"mix_even_bwd_{layer}", grid=(nt,), in_specs=in_specs, out_specs=out_specs,
        out_shape=out_shape, scratch_shapes=scratch, compiler_params=_cp(("arbitrary",)),
    )(*args)


def _ln_rows(v):
    mu = jnp.mean(v, axis=-1, keepdims=True)
    d = v - mu
    var = jnp.mean(d * d, axis=-1, keepdims=True)
    rstd = lax.rsqrt(var + LN_EPS)
    return d * rstd, rstd


def _ln_rows_bwd(dn, xh, rstd, g):
    dxh = dn * g
    m1 = jnp.mean(dxh, axis=-1, keepdims=True)
    m2 = jnp.mean(dxh * xh, axis=-1, keepdims=True)
    return rstd * (dxh - m1 - xh * m2)


def _shifted_copies(ext, shifted, rows):
    for b in range(1, SUB):
        shifted[b - 1] = ext[b:b + rows, :]


def _tap(ext, shifted, off, r0, r1, cols=slice(None)):
    a, b = off // SUB, off % SUB
    if b == 0:
        return ext[SUB * a + r0:SUB * a + r1, cols]
    return shifted[b - 1, SUB * a + r0:SUB * a + r1, cols]


def mix_odd_bwd(z, dycat, cz, slg, slb, wsb, wstb, sbcol, dw, dng, dnb, layer, parts):
    s = z.shape[0]
    tm = min(s, 256)
    nt = s // tm

    def body(z_ref, dy_ref, cz_ref, slg_ref, slb_ref, ws_ref, wst_ref, sb_ref, dw_ref, dng_ref, dnb_ref, p_ref,
             dz_ref, dslg_ref, dslb_ref, dws_ref, dsb_ref, ddw_ref, ddcb_ref, ddng_ref, ddnb_ref, land_ref,
             dvn, edz, esh, ddw8, *sems):
        i = pl.program_id(0)
        start, finish = _exchange_plan(p_ref, land_ref, sems, 0)
        pl.when(i == 0)(start)

        @pl.when(i == 0)
        def _():
            edz[tm:tm + HC, :] = jnp.zeros((HC, D), F32)
            for ref in (dslg_ref, dslb_ref, dws_ref, dsb_ref, ddw8, ddcb_ref, ddng_ref, ddnb_ref):
                ref[...] = jnp.zeros_like(ref)

        vh, vrs = _ln_rows(z_ref[:, _cols(1)].astype(F32))
        vnb = (vh * slg_ref[...] + slb_ref[...]).astype(BF16)
        for n in range(tm // SGU_BLOCK):
            r0, r1 = n * SGU_BLOCK, (n + 1) * SGU_BLOCK
            for hd in range(HEADS):
                c0, c1 = hd * HD, (hd + 1) * HD
                vblk = vnb[r0:r1, c0:c1]
                sv = _dot(ws_ref[hd], vblk) + sb_ref[hd]
                gc = z_ref[r0:r1, 2 * D + c0:2 * D + c1].astype(F32)
                sg = _sig(gc)
                sil = gc * sg
                u = z_ref[r0:r1, c0:c1].astype(F32)
                dyc = dy_ref[r0:r1, c0:c1].astype(F32)
                dz_ref[r0:r1, c0:c1] = (dyc * sv * sil).astype(BF16)
                dz_ref[r0:r1, 2 * D + c0:2 * D + c1] = (dyc * u * sv * (sg * (1.0 + gc * (1.0 - sg)))).astype(BF16)
                dsv = dyc * u * sil
                dsb_ref[hd] += jnp.sum(dsv, axis=-1, keepdims=True)
                dsvb = dsv.astype(BF16)
                dws_ref[hd] += _dot_nt(dsvb, vblk)
                dvn[r0:r1, c0:c1] = _dot(wst_ref[hd], dsvb)
        dv = dvn[...]
        dslg_ref[...] += _rowsum(dv * vh)
        dslb_ref[...] += _rowsum(dv)
        dz_ref[:, _cols(1)] = _ln_rows_bwd(dv, vh, vrs, slg_ref[...]).astype(BF16)

        zh, zrs = _ln_rows(cz_ref[...])
        zn = zh * dng_ref[...] + dnb_ref[...]
        sgn = _sig(zn)
        gd = z_ref[:, _cols(5)].astype(F32)
        sgd = _sig(gd)
        dyd = dy_ref[:, D:2 * D].astype(F32)
        dz_ref[:, _cols(5)] = (dyd * (zn * sgn) * (sgd * (1.0 + gd * (1.0 - sgd)))).astype(BF16)
        dzn = dyd * (gd * sgd) * (sgn * (1.0 + zn * (1.0 - sgn)))
        ddng_ref[...] += _rowsum(dzn * zh)
        ddnb_ref[...] += _rowsum(dzn)
        dcz = _ln_rows_bwd(dzn, zh, zrs, dng_ref[...])
        ddcb_ref[...] += _rowsum(dcz)
        edz[0:tm, :] = dcz
        _shifted_copies(edz, esh, tm + HC - SUB)
        for r0 in range(0, tm, RC):
            for c0 in range(0, D, CB):
                r1, c1 = r0 + RC, c0 + CB
                a = z_ref[r0:r1, 3 * D + c0:3 * D + c1].astype(F32)
                sgb = _sig(z_ref[r0:r1, 4 * D + c0:4 * D + c1].astype(F32))
                zg = a * sgb
                dzg = jnp.zeros((RC, CB), F32)
                for k in range(KD):
                    sh = _tap(edz, esh, KD - 1 - k, r0, r1, slice(c0, c1))
                    dzg = dzg + dw_ref[k:k + 1, c0:c1] * sh
                    prod = zg * sh
                    ddw8[k, :, c0:c1] += prod[0:SUB] + prod[SUB:2 * SUB]
                dz_ref[r0:r1, 3 * D + c0:3 * D + c1] = (dzg * sgb).astype(BF16)
                dz_ref[r0:r1, 4 * D + c0:4 * D + c1] = (dzg * a * sgb * (1.0 - sgb)).astype(BF16)
        edz[tm:tm + HC, :] = edz[0:HC, :]

        @pl.when(i == nt - 1)
        def _():
            ddw_ref[...] = jnp.sum(ddw8[...], axis=1)

        pl.when(i == nt - 1)(finish)

    rev = lambda i: (nt - 1 - i, 0)
    vec = pl.BlockSpec((1, D), lambda i: (0, 0))
    wspec = pl.BlockSpec((HEADS, SGU_BLOCK, SGU_BLOCK), lambda i: (0, 0, 0))
    bspec = pl.BlockSpec((HEADS, SGU_BLOCK, 1), lambda i: (0, 0, 0))
    kspec = pl.BlockSpec((KD, D), lambda i: (0, 0))
    return pl.pallas_call(
        body, name=f"mix_odd_bwd_{layer}", grid=(nt,),
        in_specs=[pl.BlockSpec((tm, NIN), rev), pl.BlockSpec((tm, DMIX), rev), pl.BlockSpec((tm, D), rev),
                  vec, vec, wspec, wspec, bspec, kspec, vec, vec, ANY],
        out_specs=[pl.BlockSpec((tm, NIN), rev), vec, vec, wspec, bspec, kspec, vec, vec, vec, ANY],
        out_shape=[jax.ShapeDtypeStruct((s, NIN), BF16), jax.ShapeDtypeStruct((1, D), F32),
                   jax.ShapeDtypeStruct((1, D), F32),
                   jax.ShapeDtypeStruct((HEADS, SGU_BLOCK, SGU_BLOCK), F32),
                   jax.ShapeDtypeStruct((HEADS, SGU_BLOCK, 1), F32), jax.ShapeDtypeStruct((KD, D), F32),
                   jax.ShapeDtypeStruct((1, D), F32), jax.ShapeDtypeStruct((1, D), F32),
                   jax.ShapeDtypeStruct((1, D), F32), jax.ShapeDtypeStruct(parts.shape, parts.dtype)],
        scratch_shapes=[pltpu.VMEM((tm, D), F32), pltpu.VMEM((tm + HC, D), F32),
                        pltpu.VMEM((SUB - 1, tm + HC - SUB, D), F32), pltpu.VMEM((KD, SUB, D), F32)]
        + _comm_scratch(1),
        compiler_params=_cp(("arbitrary",)),
    )(z, dycat, cz, slg, slb, wsb, wstb, sbcol, dw, dng, dnb, parts)


def _adamw_math(w, g, m, v):
    m = ADAM_B1 * m + (1.0 - ADAM_B1) * g
    v = ADAM_B2 * v + (1.0 - ADAM_B2) * (g * g)
    m_hat = m / (1.0 - ADAM_B1 ** ADAM_STEP)
    v_hat = v / (1.0 - ADAM_B2 ** ADAM_STEP)
    delta = -ADAM_LR * (m_hat / (jnp.sqrt(v_hat) + ADAM_EPS) + ADAM_WD * w)
    return delta, m, v


def adamw_big(w, m, v, parts0, parts1, name):
    _, r, c = w.shape
    tr = min(r, 256)
    nr = r // tr

    def body(w_ref, m_ref, v_ref, p0_ref, p1_ref, g_ref, d_ref, nm_ref, nv_ref):
        i = pl.program_id(0)

        def total(p_ref):
            acc = p_ref[0].astype(F32)
            for j in range(1, NDEV):
                acc = acc + p_ref[j].astype(F32)
            return acc

        @pl.when(i == 0)
        def _():
            g_ref[...] = total(p0_ref)

        @pl.when(i == 1)
        def _():
            g_ref[...] = total(p1_ref)

        delta, nm, nv = _adamw_math(w_ref[...], g_ref[...], m_ref[...], v_ref[...])
        d_ref[...] = delta
        nm_ref[...] = nm
        nv_ref[...] = nv

    wspec = pl.BlockSpec((None, tr, c), lambda i, j: (i, j, 0))
    p0 = pl.BlockSpec((NDEV, tr, c), lambda i, j: (0, jnp.where(i == 0, j, nr - 1), 0))
    p1 = pl.BlockSpec((NDEV, tr, c), lambda i, j: (0, jnp.where(i == 1, j, 0), 0))
    shp = jax.ShapeDtypeStruct(w.shape, F32)
    return pl.pallas_call(
        body, name=name, grid=(2, nr), in_specs=[wspec, wspec, wspec, p0, p1],
        out_specs=[wspec] * 4, out_shape=[shp] * 4,
        compiler_params=_cp(("arbitrary", "arbitrary")),
    )(w, m, v, parts0, parts1)


def sum_parts(parts, name):
    _, r, c = parts.shape

    def body(p_ref, o_ref):
        acc = p_ref[0]
        for j in range(1, NDEV):
            acc = acc + p_ref[j]
        o_ref[...] = acc

    return pl.pallas_call(body, name=name, out_shape=jax.ShapeDtypeStruct((r, c), F32),
                          compiler_params=_cp())(parts)


def adamw_small(w, g, m, v, name):
    def body(w_ref, g_ref, m_ref, v_ref, d_ref, nm_ref, nv_ref):
        delta, nm, nv = _adamw_math(w_ref[...], g_ref[...], m_ref[...], v_ref[...])
        d_ref[...] = delta
        nm_ref[...] = nm
        nv_ref[...] = nv

    shp = jax.ShapeDtypeStruct(w.shape, F32)
    return pl.pallas_call(body, name=name, out_shape=[shp] * 3, compiler_params=_cp())(w, g, m, v)


def _size(shape):
    n = 1
    for d in shape:
        n *= d
    return n


def _pack(arrays, rows):
    flat = jnp.concatenate([a.reshape(-1) for a in arrays])
    return jnp.pad(flat, (0, rows * 128 - flat.shape[0])).reshape(rows, 128)


def _unpack(flat, shapes):
    out, o = [], 0
    for shp in shapes:
        out.append(flat[o:o + _size(shp)].reshape(shp))
        o += _size(shp)
    return out


def _rows_for(shapes):
    return -(-sum(_size(shp) for shp in shapes) // 1024) * 8


SHARDED_SMALL = (("pool_w", (2, 4, 256, 256), 2), ("sconv_w", (2, KS, D), 2), ("sgu_ln_g", (2, D), 1),
                 ("sgu_ln_b", (2, D), 1), ("dconv_w", (2, KD, D), 2), ("dconv_b", (2, D), 1),
                 ("dnorm_g", (2, D), 1), ("dnorm_b", (2, D), 1))
REPLICATED_SMALL = (("ln_g", (DEPTH, D)), ("ln_b", (DEPTH, D)), ("pool_scale", (2, D)), ("sconv_b", (2, D)),
                    ("sgu_w", (2, HEADS, SGU_BLOCK, SGU_BLOCK)), ("sgu_b", (2, HEADS, SGU_BLOCK)))


def _shard_shape(shape, axis):
    return tuple(d // NDEV if a == axis else d for a, d in enumerate(shape))


def _merge_gathered(g, shape, axis):
    return jnp.moveaxis(g, 0, axis).reshape(shape)


def kernel(x, ln_g, ln_b, w_in_even, w_out_even, pool_w, pool_scale, sconv_w, sconv_b, w_in_odd, w_out_odd, sgu_ln_g, sgu_ln_b, sgu_w, sgu_b, dconv_w, dconv_b, dnorm_g, dnorm_b, loss_target, m_ln_g, m_ln_b, m_w_in_even, m_w_out_even, m_pool_w, m_pool_scale, m_sconv_w, m_sconv_b, m_w_in_odd, m_w_out_odd, m_sgu_ln_g, m_sgu_ln_b, m_sgu_w, m_sgu_b, m_dconv_w, m_dconv_b, m_dnorm_g, m_dnorm_b, v_ln_g, v_ln_b, v_w_in_even, v_w_out_even, v_pool_w, v_pool_scale, v_sconv_w, v_sconv_b, v_w_in_odd, v_w_out_odd, v_sgu_ln_g, v_sgu_ln_b, v_sgu_w, v_sgu_b, v_dconv_w, v_dconv_b, v_dnorm_g, v_dnorm_b):
    given = dict(locals())
    me = 4 * lax.axis_index("x") + 2 * lax.axis_index("y") + lax.axis_index("c")
    xs = x[0]
    target = loss_target[0]

    wc_in = cast_weights(w_in_even, w_in_odd, "cast_w_in")
    wc_out = cast_weights(w_out_even, w_out_odd, "cast_w_out")
    shard_shapes = [_shard_shape(shp, ax) for _, shp, ax in SHARDED_SMALL]
    srows = _rows_for(shard_shapes)
    xb, wg_first, gathered = cast_x(xs, wc_in, _pack([given[n] for n, _, _ in SHARDED_SMALL], srows))
    wg_in = [wg_first] + [None] * (DEPTH - 1)
    wg_out = [None] * DEPTH
    gathered = gathered.reshape(NDEV, -1)
    full, o = {}, 0
    for (n, shp, ax), sshp in zip(SHARDED_SMALL, shard_shapes):
        full[n] = _merge_gathered(gathered[:, o:o + _size(sshp)].reshape((NDEV,) + sshp), shp, ax)
        o += _size(sshp)

    mask = (jnp.arange(SGU_BLOCK)[None, :] // 64) <= (jnp.arange(SGU_BLOCK)[:, None] // 64)
    ws = jnp.where(mask[None, None], sgu_w, 0.0)
    wsb = ws.astype(BF16)
    wstb = jnp.swapaxes(ws, -1, -2).astype(BF16)
    pwb = full["pool_w"].astype(BF16)

    ones = jnp.ones((1, D), F32)
    zeros = jnp.zeros((1, D), F32)

    xres, gp, bp = xs, ones, zeros
    saved, below = [], None
    for layer in range(DEPTH):
        i = layer // 2
        if layer % 2 == 0:
            params = (pwb[i], pool_scale[i][None], full["sconv_w"][i], sconv_b[i][None])
        else:
            params = (full["sgu_ln_g"][i][None], full["sgu_ln_b"][i][None], wsb[i], sgu_b[i][:, :, None],
                      full["dconv_w"][i], full["dconv_b"][i][None], full["dnorm_g"][i][None],
                      full["dnorm_b"][i][None])
        outs = list(proj_in(wg_in[layer], layer, wc_in, wc_out, xb=xb if below is None else None, below=below))
        if layer + 1 < DEPTH:
            wg_in[layer + 1] = outs.pop()
        wg_out[layer] = outs.pop()
        z = outs[0]
        if below is not None:
            xhat, rstd, xb = outs[1:4]
            saved[-1] += [xhat, rstd]
            xres, gp, bp = xhat, below[5], below[6]
        outs = mix_fwd(z, layer, params)
        ycat, extra = outs[0], tuple(outs[1:])
        saved.append([xb, z, ycat, extra])
        below = (ycat, wg_out[layer], xres, gp, bp, ln_g[layer][None], ln_b[layer][None])
    *top, loss_local = head_loss_bwd(below, target)

    gsmall = {n: [None] * shp[0] for n, shp in REPLICATED_SMALL}
    gsmall.update({n: [None] * shp[0] for n, shp, _ in SHARDED_SMALL})
    parts_in, parts_out, parts_pw = [None] * DEPTH, [None] * DEPTH, [None] * 2
    small_names = [n for n, _ in REPLICATED_SMALL] + [n for n, _, _ in SHARDED_SMALL]
    gathered_names = [n for n in small_names if n != "pool_w"]
    gathered_shapes = [shp for n, shp in REPLICATED_SMALL] + [shp for n, shp, _ in SHARDED_SMALL if n != "pool_w"]
    grows = _rows_for(gathered_shapes + [(1,)])
    pending, above = None, None
    for layer in reversed(range(DEPTH)):
        i = layer // 2
        xb, z, ycat, extra = saved[layer][:4]
        if above is None:
            dr, drb, dycat, dg, db = top
        else:
            dr, drb, dycat, dg, db = ln_bwd_dycat(*saved[layer][4:], ln_g[layer][None], wg_out[layer], layer, above)
        gsmall["ln_g"][layer], gsmall["ln_b"][layer] = dg[0], db[0]
        dwo = dw_out(ycat, drb, layer)
        own_out = pending is None
        riding = dwo if own_out else pending
        if layer % 2 == 0:
            dzb, dpw, dps, dsw, dsb, landed = mix_even_bwd(
                z, dycat, extra[0], extra[1], pwb[i], pool_scale[i][None], full["sconv_w"][i], layer, parts=riding)
            gsmall["pool_scale"][i] = dps[0]
            gsmall["sconv_w"][i], gsmall["sconv_b"][i] = dsw, dsb[0]
        else:
            dzb, dslg, dslb, dws, dsbc, ddw, ddcb, ddng, ddnb, landed = mix_odd_bwd(
                z, dycat, extra[0], full["sgu_ln_g"][i][None], full["sgu_ln_b"][i][None], wsb[i], wstb[i],
                sgu_b[i][:, :, None], full["dconv_w"][i], full["dnorm_g"][i][None], full["dnorm_b"][i][None],
                layer, riding)
            gsmall["sgu_ln_g"][i], gsmall["sgu_ln_b"][i] = dslg[0], dslb[0]
            gsmall["sgu_w"][i], gsmall["sgu_b"][i] = jnp.where(mask[None], dws, 0.0), dsbc[:, :, 0]
            gsmall["dconv_w"][i], gsmall["dconv_b"][i] = ddw, ddcb[0]
            gsmall["dnorm_g"][i], gsmall["dnorm_b"][i] = ddng[0], ddnb[0]
        riders = [] if own_out else [dwo]
        if layer % 2 == 0:
            riders.append(dpw.reshape(4, NDEV, PG // NDEV, PG).transpose(1, 0, 2, 3))
        small_partial = []
        if layer == 0:
            small_partial.append(_pack([jnp.stack(gsmall[n]) for n in gathered_names] + [loss_local], grows))
        outs = list(dw_in(xb, dzb, layer, riders, small_partial))
        if layer == 0:
            small_gathered = outs.pop()
        if layer % 2 == 0:
            parts_pw[i] = outs.pop()
        if own_out:
            parts_out[layer] = landed
        else:
            parts_in[layer + 1] = landed
            parts_out[layer] = outs[1]
        pending = outs[0]
        above = (dzb, wg_in[layer], dr)
    dxn, parts_in[0] = dx_in(dzb, wg_in[0], dr, pending)
    grad_x = dxn[None]

    big = {}
    big["w_in_even"] = adamw_big(w_in_even, m_w_in_even, v_w_in_even, parts_in[0], parts_in[2], "adamw_w_in_even")
    big["w_in_odd"] = adamw_big(w_in_odd, m_w_in_odd, v_w_in_odd, parts_in[1], parts_in[3], "adamw_w_in_odd")
    big["w_out_even"] = adamw_big(w_out_even, m_w_out_even, v_w_out_even, parts_out[0], parts_out[2], "adamw_w_out_even")
    big["w_out_odd"] = adamw_big(w_out_odd, m_w_out_odd, v_w_out_odd, parts_out[1], parts_out[3], "adamw_w_out_odd")

    gsum = sum_parts(small_gathered, "sum_small_grads")
    unpacked = _unpack(gsum.reshape(-1), gathered_shapes + [()])
    loss = unpacked.pop()
    gfull = dict(zip(gathered_names, unpacked))
    own_shapes = [shp for _, shp in REPLICATED_SMALL] + shard_shapes
    gown = {n: gfull[n] for n, _ in REPLICATED_SMALL}
    for (n, shp, ax), sshp in zip(SHARDED_SMALL, shard_shapes):
        if n == "pool_w":
            gown[n] = jnp.stack([sum_parts(p.reshape(NDEV, -1, 128), f"sum_pool_w_{j}").reshape(sshp[1:])
                                 for j, p in enumerate(parts_pw)])
        else:
            gown[n] = lax.dynamic_slice_in_dim(gfull[n], me * sshp[ax], sshp[ax], axis=ax)
    orows = _rows_for(own_shapes)
    packed = [_pack([src[n] for n in small_names], orows) for src in
              (given, gown, {n: given["m_" + n] for n in small_names}, {n: given["v_" + n] for n in small_names})]
    sd, sm, sv = adamw_small(*packed, "adamw_small")
    small = {}
    for n, d_, m_, v_ in zip(small_names, _unpack(sd.reshape(-1), own_shapes), _unpack(sm.reshape(-1), own_shapes),
                             _unpack(sv.reshape(-1), own_shapes)):
        small[n] = (gown[n], d_, m_, v_)

    weights = ['ln_g', 'ln_b', 'w_in_even', 'w_out_even', 'pool_w', 'pool_scale', 'sconv_w', 'sconv_b', 'w_in_odd',
               'w_out_odd', 'sgu_ln_g', 'sgu_ln_b', 'sgu_w', 'sgu_b', 'dconv_w', 'dconv_b', 'dnorm_g', 'dnorm_b']
    res = {n: (big[n] if n in big else small[n]) for n in weights}
    return (loss, grad_x, *[res[n][0] for n in weights], *[res[n][1] for n in weights],
            *[res[n][2] for n in weights], *[res[n][3] for n in weights])
```

```python
import jax
import jax.numpy as jnp
from jax import lax
from jax.experimental import pallas as pl
from jax.experimental.pallas import tpu as pltpu

F32 = jnp.float32
BF16 = jnp.bfloat16

D = 1024
DMIX = 2048
NIN = 6144
NDEV = 8
CW = NIN // NDEV
RW = DMIX // NDEV
DEPTH = 4
ALPHA = (2 * DEPTH) ** 0.25
LN_EPS = 1e-5
POOL_WINDOWS = (2, 4, 8, 16)
PG = 256
SGU_BLOCK = 128
HEADS = 4
HD = 256
KD = 31
KS = 3
SUB = 8
RC = 16
CB = 512
HP = 32
HS = 8
HC = 32
NSEM = 7

ADAM_LR = 0.001
ADAM_B1 = 0.9
ADAM_B2 = 0.999
ADAM_EPS = 1e-08
ADAM_WD = 0.01
ADAM_STEP = 10

VMEM_LIMIT = 56 * 1024 * 1024
MESH = pl.DeviceIdType.MESH
ANY = pl.BlockSpec(memory_space=pl.ANY)


def _cp(sem=None):
    if sem is None:
        return pltpu.CompilerParams(vmem_limit_bytes=VMEM_LIMIT)
    return pltpu.CompilerParams(dimension_semantics=sem, vmem_limit_bytes=VMEM_LIMIT)


def _sig(x):
    return 0.5 * jnp.tanh(0.5 * x) + 0.5


def _dot(a, b):
    return jnp.dot(a, b, preferred_element_type=F32)


def _dot_nt(a, b):
    return lax.dot_general(a, b, (((1,), (1,)), ((), ())), preferred_element_type=F32)


def _dot_tn(a, b):
    return lax.dot_general(a, b, (((0,), (0,)), ((), ())), preferred_element_type=F32)


def _rowsum(x):
    return jnp.sum(x, axis=0, keepdims=True)


def _comm_scratch(n):
    return [pltpu.SemaphoreType.DMA((n * NSEM,)), pltpu.SemaphoreType.DMA((n * NSEM,)),
            pltpu.SemaphoreType.DMA((n,))]


def _gather_plan(src_ref, out_ref, sems, n):
    send_sems, recv_sems, local_sems = sems
    base = n * NSEM
    x, y, c = lax.axis_index("x"), lax.axis_index("y"), lax.axis_index("c")
    me, sibling = (x, y, c), (x, y, 1 - c)
    chips = [(1 - x, y), (x, 1 - y), (1 - x, 1 - y)]

    def slot(px, py, pc):
        return out_ref.at[4 * px + 2 * py + pc]

    def copy(k, blk, to, src=None):
        return pltpu.make_async_remote_copy(
            src_ref=slot(*blk) if src is None else src, dst_ref=slot(*blk),
            send_sem=send_sems.at[base + k], recv_sem=recv_sems.at[base + k],
            device_id=to, device_id_type=MESH)

    def mine():
        return pltpu.make_async_copy(src_ref, slot(*me), local_sems.at[n])

    def start():
        mine().start()
        copy(0, me, sibling, src=src_ref).start()
        for j, chip in enumerate(chips):
            copy(1 + j, me, (*chip, c), src=src_ref).start()

    def finish():
        for j, chip in enumerate(chips):
            copy(1 + j, (*chip, c), me).wait_recv()
            copy(4 + j, (*chip, c), sibling).start()
        copy(0, sibling, me).wait_recv()
        for j, chip in enumerate(chips):
            copy(4 + j, (*chip, 1 - c), me).wait_recv()
        copy(0, me, sibling, src=src_ref).wait_send()
        for j, chip in enumerate(chips):
            copy(1 + j, me, (*chip, c), src=src_ref).wait_send()
            copy(4 + j, (*chip, c), sibling).wait_send()
        mine().wait()

    return start, finish


def _exchange_plan(p_ref, out_ref, sems, n):
    send_sems, recv_sems, local_sems = sems
    base = n * NSEM
    x, y, c = lax.axis_index("x"), lax.axis_index("y"), lax.axis_index("c")
    me = 4 * x + 2 * y + c

    def copy(r, landing):
        px, py, pc = x ^ (r >> 2), y ^ ((r >> 1) & 1), c ^ (r & 1)
        peer = 4 * px + 2 * py + pc
        return pltpu.make_async_remote_copy(
            src_ref=p_ref.at[peer], dst_ref=out_ref.at[peer if landing else me],
            send_sem=send_sems.at[base + r - 1], recv_sem=recv_sems.at[base + r - 1],
            device_id=(px, py, pc), device_id_type=MESH)

    def mine():
        return pltpu.make_async_copy(p_ref.at[me], out_ref.at[me], local_sems.at[n])

    def start():
        mine().start()
        for r in range(1, NDEV):
            copy(r, False).start()

    def finish():
        for r in range(1, NDEV):
            copy(r, True).wait_recv()
        for r in range(1, NDEV):
            copy(r, False).wait_send()
        mine().wait()

    return start, finish


def cast_x(x, w_first, small):
    s = x.shape[0]
    tm = min(s, 512)
    nt = s // tm

    def body(x_ref, wf_ref, sm_ref, o_ref, gw_ref, gs_ref, *sems):
        i = pl.program_id(0)
        plans = [_gather_plan(wf_ref.at[0], gw_ref, sems, 0), _gather_plan(sm_ref, gs_ref, sems, 1)]

        @pl.when(i == 0)
        def _():
            for start, _ in plans:
                start()

        o_ref[...] = x_ref[...].astype(BF16)

        @pl.when(i == nt - 1)
        def _():
            for _, finish in plans:
                finish()

    return pl.pallas_call(
        body, name="cast_x", grid=(nt,),
        in_specs=[pl.BlockSpec((tm, D), lambda i: (i, 0)), ANY, ANY],
        out_specs=[pl.BlockSpec((tm, D), lambda i: (i, 0)), ANY, ANY],
        out_shape=[jax.ShapeDtypeStruct((s, D), BF16),
                   jax.ShapeDtypeStruct((NDEV,) + w_first.shape[1:], w_first.dtype),
                   jax.ShapeDtypeStruct((NDEV,) + small.shape, small.dtype)],
        scratch_shapes=_comm_scratch(2), compiler_params=_cp(("arbitrary",)),
    )(x, w_first, small)


def cast_weights(w_even, w_odd, name):
    _, r, c = w_even.shape

    def body(e_ref, o_ref, out_ref):
        layer = pl.program_id(0)

        @pl.when(layer % 2 == 0)
        def _():
            out_ref[...] = e_ref[...].astype(BF16)

        @pl.when(layer % 2 == 1)
        def _():
            out_ref[...] = o_ref[...].astype(BF16)

    spec = pl.BlockSpec((None, r, c), lambda l: (l // 2, 0, 0))
    return pl.pallas_call(
        body, name=name, grid=(DEPTH,), in_specs=[spec, spec],
        out_specs=pl.BlockSpec((None, r, c), lambda l: (l, 0, 0)),
        out_shape=jax.ShapeDtypeStruct((DEPTH, r, c), BF16), compiler_params=_cp(("parallel",)),
    )(w_even, w_odd)


def _deepnorm(y_ref, wo_ref, xr_ref, gp_ref, bp_ref, g_ref, b_ref):
    xin = xr_ref[...] * gp_ref[...] + bp_ref[...]
    r = ALPHA * xin + _dot(y_ref[...], wo_ref[...].reshape(DMIX, D))
    mu = jnp.mean(r, axis=-1, keepdims=True)
    d = r - mu
    var = jnp.mean(d * d, axis=-1, keepdims=True)
    rstd = lax.rsqrt(var + LN_EPS)
    xh = d * rstd
    return xh, rstd, xh * g_ref[...] + b_ref[...]


def proj_in(wg, layer, wc_in, wc_out, xb=None, below=None):
    fused = below is not None
    s = below[0].shape[0] if fused else xb.shape[0]
    tm = min(s, 256 if fused else 512)
    nt = s // tm
    nxt = layer + 1 < DEPTH
    nhead = 7 if fused else 1

    def body(*refs):
        head, refs = refs[:nhead], refs[nhead:]
        if nxt:
            w_ref, wco_ref, wci_ref = refs[:3]
            refs = refs[3:]
        else:
            w_ref, wco_ref = refs[:2]
            refs = refs[2:]
        o_ref = refs[0]
        if fused:
            xh_ref, rs_ref, xb_ref = refs[1:4]
            refs = refs[4:]
        else:
            refs = refs[1:]
        go_ref = refs[0]
        sems = refs[-3:]
        i = pl.program_id(0)
        plans = [_gather_plan(wco_ref.at[layer], go_ref, sems, 0)]
        if nxt:
            plans.append(_gather_plan(wci_ref.at[layer + 1], refs[1], sems, 1))

        @pl.when(i == 0)
        def _():
            for start, _ in plans:
                start()

        if fused:
            xh, rstd, xn = _deepnorm(*head)
            xh_ref[...] = xh
            rs_ref[...] = rstd
            x = xn.astype(BF16)
            xb_ref[...] = x
        else:
            x = head[0][...]
        for k in range(NDEV):
            o_ref[:, k * CW:(k + 1) * CW] = _dot(x, w_ref[k]).astype(BF16)

        @pl.when(i == nt - 1)
        def _():
            for _, finish in plans:
                finish()

    row = pl.BlockSpec((tm, D), lambda i: (i, 0))
    vec = pl.BlockSpec((1, D), lambda i: (0, 0))
    if fused:
        in_specs = [pl.BlockSpec((tm, DMIX), lambda i: (i, 0)),
                    pl.BlockSpec((NDEV, RW, D), lambda i: (0, 0, 0), pipeline_mode=pl.Buffered(1)),
                    row, vec, vec, vec, vec]
        args = list(below)
    else:
        in_specs, args = [row], [xb]
    in_specs += [pl.BlockSpec((NDEV, D, CW), lambda i: (0, 0, 0), pipeline_mode=pl.Buffered(1)), ANY]
    args += [wg, wc_out]
    if nxt:
        in_specs.append(ANY)
        args.append(wc_in)
    out_specs = [pl.BlockSpec((tm, NIN), lambda i: (i, 0))]
    out_shape = [jax.ShapeDtypeStruct((s, NIN), BF16)]
    if fused:
        out_specs += [row, pl.BlockSpec((tm, 1), lambda i: (i, 0)), row]
        out_shape += [jax.ShapeDtypeStruct((s, D), F32), jax.ShapeDtypeStruct((s, 1), F32),
                      jax.ShapeDtypeStruct((s, D), BF16)]
    out_specs.append(ANY)
    out_shape.append(jax.ShapeDtypeStruct((NDEV, RW, D), BF16))
    if nxt:
        out_specs.append(ANY)
        out_shape.append(jax.ShapeDtypeStruct((NDEV, D, CW), BF16))
    return pl.pallas_call(
        body, name=f"proj_in_{layer}", grid=(nt,), in_specs=in_specs, out_specs=out_specs,
        out_shape=out_shape, scratch_shapes=_comm_scratch(2 if nxt else 1),
        compiler_params=_cp(("arbitrary",)),
    )(*args)


def mix_fwd(z, layer, params):
    s = z.shape[0]
    tm = min(s, 256)
    even = layer % 2 == 0
    npar = len(params)
    nout = 3 if even else 2

    def even_tile(zr, j, par, outs, scr):
        pw_ref, ps_ref, sw_ref, sb_ref = par
        yc_ref, pb_ref, cv_ref = outs
        exa, eq, ya0, la, lb = scr
        for c in range(tm // RC):
            r0, r1 = c * RC, (c + 1) * RC
            exa[HP + r0:HP + r1, :] = zr[r0:r1, _cols(0)].astype(F32)
            eq[HS + r0:HS + r1, :] = zr[r0:r1, _cols(4)].astype(F32) * zr[r0:r1, _cols(2)].astype(F32)

        def level(src, dst, shift, c0, start):
            for r0 in range(start, HP + tm, 2 * RC):
                r1 = min(r0 + 2 * RC, HP + tm)
                dst[r0:r1, c0:D] = src[r0:r1, c0:D] + src[r0 - shift:r1 - shift, c0:D]

        level(exa, la, 1, PG, 8)
        level(la, lb, 2, 2 * PG, 16)
        level(lb, la, 4, 3 * PG, 24)
        last = ((exa, 1), (la, 2), (lb, 4), (la, 8))
        for c in range(tm // RC):
            r0, r1 = HP + c * RC, HP + (c + 1) * RC
            pos = (j * tm + c * RC + lax.broadcasted_iota(jnp.int32, (RC, 1), 0) + 1).astype(F32)
            for gi, w in enumerate(POOL_WINDOWS):
                c0, c1 = gi * PG, (gi + 1) * PG
                src, shift = last[gi]
                acc = src[r0:r1, c0:c1] + src[r0 - shift:r1 - shift, c0:c1]
                pooled = acc / jnp.minimum(pos, float(w)) - exa[r0:r1, c0:c1]
                pb_ref[c * RC:(c + 1) * RC, c0:c1] = pooled.astype(BF16)
        for gi in range(len(POOL_WINDOWS)):
            c0, c1 = gi * PG, (gi + 1) * PG
            ya0[:, c0:c1] = _dot(pb_ref[:, c0:c1], pw_ref[gi])
        for c in range(tm // RC):
            r0, r1 = c * RC, (c + 1) * RC
            ga = zr[r0:r1, _cols(1)].astype(F32)
            yc_ref[r0:r1, 0:D] = (ya0[r0:r1, :] * ps_ref[...] * (ga * _sig(ga))).astype(BF16)
            cv = (sw_ref[2:3, :] * eq[HS + r0:HS + r1, :] + sw_ref[1:2, :] * eq[HS + r0 - 1:HS + r1 - 1, :]
                  + sw_ref[0:1, :] * eq[HS + r0 - 2:HS + r1 - 2, :] + sb_ref[...])
            cv_ref[r0:r1, :] = cv.astype(BF16)
            gb = zr[r0:r1, _cols(5)].astype(F32)
            yc_ref[r0:r1, D:2 * D] = (zr[r0:r1, _cols(3)].astype(F32) * cv * (gb * _sig(gb))).astype(BF16)
        exa[0:HP, :] = exa[tm:tm + HP, :]
        eq[0:HS, :] = eq[tm:tm + HS, :]

    def odd_tile(zr, j, par, outs, scr):
        slg_ref, slb_ref, ws_ref, sb_ref, dw_ref, dcb_ref, dng_ref, dnb_ref = par
        yc_ref, cz_ref = outs
        ezg, esh = scr
        vh, _ = _ln_rows(zr[:, _cols(1)].astype(F32))
        vnb = (vh * slg_ref[...] + slb_ref[...]).astype(BF16)
        for n in range(tm // SGU_BLOCK):
            r0, r1 = n * SGU_BLOCK, (n + 1) * SGU_BLOCK
            for hd in range(HEADS):
                c0, c1 = hd * HD, (hd + 1) * HD
                sv = _dot(ws_ref[hd], vnb[r0:r1, c0:c1]) + sb_ref[hd]
                gc = zr[r0:r1, 2 * D + c0:2 * D + c1].astype(F32)
                yc_ref[r0:r1, c0:c1] = (zr[r0:r1, c0:c1].astype(F32) * sv * (gc * _sig(gc))).astype(BF16)
        ezg[HC:HC + tm, :] = zr[:, _cols(3)].astype(F32) * _sig(zr[:, _cols(4)].astype(F32))
        _shifted_copies(ezg, esh, tm + HC - SUB)
        cz = jnp.zeros((tm, D), F32) + dcb_ref[...]
        for k in range(KD):
            cz = cz + dw_ref[k:k + 1, :] * _tap(ezg, esh, HC - (KD - 1) + k, 0, tm)
        cz_ref[...] = cz
        zh, _ = _ln_rows(cz)
        zn = zh * dng_ref[...] + dnb_ref[...]
        gd = zr[:, _cols(5)].astype(F32)
        yc_ref[:, D:2 * D] = ((zn * _sig(zn)) * (gd * _sig(gd))).astype(BF16)
        ezg[0:HC, :] = ezg[tm:tm + HC, :]

    def body(z_ref, *refs):
        par, outs, scr = refs[:npar], refs[npar:npar + nout], refs[npar + nout:]
        j = pl.program_id(0)

        @pl.when(j == 0)
        def _():
            if even:
                scr[0][0:HP, :] = jnp.zeros((HP, D), F32)
                scr[1][0:HS, :] = jnp.zeros((HS, D), F32)
            else:
                scr[0][0:HC, :] = jnp.zeros((HC, D), F32)

        (even_tile if even else odd_tile)(z_ref, j, par, outs, scr)

    row = lambda c: pl.BlockSpec((tm, c), lambda j: (j, 0))
    vec = pl.BlockSpec((1, D), lambda j: (0, 0))
    if even:
        par_specs = [pl.BlockSpec((4, PG, PG), lambda j: (0, 0, 0)), vec, pl.BlockSpec((KS, D), lambda j: (0, 0)), vec]
        out_specs = [row(DMIX), row(D), row(D)]
        out_shape = [jax.ShapeDtypeStruct((s, DMIX), BF16), jax.ShapeDtypeStruct((s, D), BF16),
                     jax.ShapeDtypeStruct((s, D), BF16)]
        scratch = [pltpu.VMEM((HP + tm, D), F32), pltpu.VMEM((HS + tm, D), F32), pltpu.VMEM((tm, D), F32),
                   pltpu.VMEM((HP + tm, D), F32), pltpu.VMEM((HP + tm, D), F32)]
    else:
        par_specs = [vec, vec, pl.BlockSpec((HEADS, SGU_BLOCK, SGU_BLOCK), lambda j: (0, 0, 0)),
                     pl.BlockSpec((HEADS, SGU_BLOCK, 1), lambda j: (0, 0, 0)),
                     pl.BlockSpec((KD, D), lambda j: (0, 0)), vec, vec, vec]
        out_specs = [row(DMIX), row(D)]
        out_shape = [jax.ShapeDtypeStruct((s, DMIX), BF16), jax.ShapeDtypeStruct((s, D), F32)]
        scratch = [pltpu.VMEM((HC + tm, D), F32), pltpu.VMEM((SUB - 1, tm + HC - SUB, D), F32)]
    return pl.pallas_call(
        body, name=f"mix_fwd_{layer}", grid=(s // tm,), in_specs=[row(NIN)] + par_specs, out_specs=out_specs,
        out_shape=out_shape, scratch_shapes=scratch, compiler_params=_cp(("arbitrary",)),
    )(z, *params)


def _zero_at_start(*refs):
    @pl.when(pl.program_id(0) == 0)
    def _():
        for ref in refs:
            ref[...] = jnp.zeros_like(ref)


def _ln_bwd_tile(dxo, xh, rstd, g_ref, wo_ref, dr_ref, drb_ref, dyc_ref, dg_ref, db_ref):
    dg_ref[...] += _rowsum(dxo * xh)
    db_ref[...] += _rowsum(dxo)
    dr = _ln_rows_bwd(dxo, xh, rstd, g_ref[...])
    dr_ref[...] = dr
    drb = dr.astype(BF16)
    drb_ref[...] = drb
    dyc_ref[...] = _dot_nt(drb, wo_ref[...].reshape(DMIX, D)).astype(BF16)


def _ln_bwd_outs(s, tm):
    row = pl.BlockSpec((tm, D), lambda i: (i, 0))
    vec = pl.BlockSpec((1, D), lambda i: (0, 0))
    return ([row, row, pl.BlockSpec((tm, DMIX), lambda i: (i, 0)), vec, vec],
            [jax.ShapeDtypeStruct((s, D), F32), jax.ShapeDtypeStruct((s, D), BF16),
             jax.ShapeDtypeStruct((s, DMIX), BF16), jax.ShapeDtypeStruct((1, D), F32),
             jax.ShapeDtypeStruct((1, D), F32)])


def head_loss_bwd(below, target):
    s = target.shape[0]
    tm = min(s, 256)

    def body(*refs):
        t_ref = refs[7]
        loss_ref = refs[-1]
        _zero_at_start(*refs[11:14])
        xh, rstd, xn = _deepnorm(*refs[:7])
        err = xn - t_ref[...]
        loss_ref[...] += 0.5 * jnp.sum(jnp.mean(err * err, axis=-1, keepdims=True), axis=0, keepdims=True)
        _ln_bwd_tile(err * (1.0 / D), xh, rstd, refs[5], refs[1], *refs[8:13])

    row = pl.BlockSpec((tm, D), lambda i: (i, 0))
    vec = pl.BlockSpec((1, D), lambda i: (0, 0))
    out_specs, out_shape = _ln_bwd_outs(s, tm)
    return pl.pallas_call(
        body, name="head_loss_bwd", grid=(s // tm,),
        in_specs=[pl.BlockSpec((tm, DMIX), lambda i: (i, 0)), pl.BlockSpec((NDEV, RW, D), lambda i: (0, 0, 0)),
                  row, vec, vec, vec, vec, row],
        out_specs=out_specs + [pl.BlockSpec((1, 1), lambda i: (0, 0))],
        out_shape=out_shape + [jax.ShapeDtypeStruct((1, 1), F32)],
        compiler_params=_cp(("arbitrary",)),
    )(*below, target)


def ln_bwd_dycat(xhat, rstd, g, wog, layer, upstream):
    s = xhat.shape[0]
    tm = min(s, 256)

    def body(dz_ref, wi_ref, dru_ref, xh_ref, rs_ref, g_ref, w_ref, *outs):
        _zero_at_start(*outs[3:5])
        dxo = ALPHA * dru_ref[...]
        for k in range(NDEV):
            dxo += _dot_nt(dz_ref[:, k * CW:(k + 1) * CW], wi_ref[k])
        _ln_bwd_tile(dxo, xh_ref[...], rs_ref[...], g_ref, w_ref, *outs)

    row = pl.BlockSpec((tm, D), lambda i: (i, 0))
    vec = pl.BlockSpec((1, D), lambda i: (0, 0))
    out_specs, out_shape = _ln_bwd_outs(s, tm)
    return pl.pallas_call(
        body, name=f"ln_bwd_dycat_{layer}", grid=(s // tm,),
        in_specs=[pl.BlockSpec((tm, NIN), lambda i: (i, 0)),
                  pl.BlockSpec((NDEV, D, CW), lambda i: (0, 0, 0), pipeline_mode=pl.Buffered(1)), row,
                  row, pl.BlockSpec((tm, 1), lambda i: (i, 0)), vec,
                  pl.BlockSpec((NDEV, RW, D), lambda i: (0, 0, 0))],
        out_specs=out_specs, out_shape=out_shape, compiler_params=_cp(("arbitrary",)),
    )(*upstream, xhat, rstd, g, wog)


def dx_in(dzb, wg, dr, parts):
    s = dzb.shape[0]
    tm = min(s, 256)
    nt = s // tm

    def body(dz_ref, w_ref, dr_ref, p_ref, o_ref, land_ref, *sems):
        start, finish = _exchange_plan(p_ref, land_ref, sems, 0)
        pl.when(pl.program_id(0) == 0)(start)
        acc = ALPHA * dr_ref[...]
        for k in range(NDEV):
            acc += _dot_nt(dz_ref[:, k * CW:(k + 1) * CW], w_ref[k])
        o_ref[...] = acc
        pl.when(pl.program_id(0) == nt - 1)(finish)

    row = pl.BlockSpec((tm, D), lambda i: (i, 0))
    return pl.pallas_call(
        body, name="dx_in_0", grid=(nt,),
        in_specs=[pl.BlockSpec((tm, NIN), lambda i: (i, 0)),
                  pl.BlockSpec((NDEV, D, CW), lambda i: (0, 0, 0), pipeline_mode=pl.Buffered(1)), row, ANY],
        out_specs=[row, ANY],
        out_shape=[jax.ShapeDtypeStruct((s, D), F32), jax.ShapeDtypeStruct(parts.shape, parts.dtype)],
        scratch_shapes=_comm_scratch(1), compiler_params=_cp(("arbitrary",)),
    )(dzb, wg, dr, parts)


def dw_in(xb, dzb, layer, parts=(), gathers=()):
    s = xb.shape[0]
    tm = min(s, 2048)
    nt = s // tm
    ne = len(parts)
    n = ne + len(gathers)

    def body(*refs):
        x_ref, dz_ref = refs[:2]
        p_refs = refs[2:2 + n]
        o_ref = refs[2 + n]
        land_refs = refs[3 + n:3 + 2 * n]
        acc = refs[3 + 2 * n]
        plans = [(_exchange_plan if j < ne else _gather_plan)(p, land, refs[-3:], j)
                 for j, (p, land) in enumerate(zip(p_refs, land_refs))]
        k, t = pl.program_id(0), pl.program_id(1)

        @pl.when((k == 0) & (t == 0))
        def _():
            for start, _ in plans:
                start()

        @pl.when(t == 0)
        def _():
            acc[...] = jnp.zeros_like(acc)

        acc[...] += _dot_tn(x_ref[...], dz_ref[...])

        @pl.when(t == nt - 1)
        def _():
            o_ref[...] = acc[...].astype(BF16)

        @pl.when((k == NDEV - 1) & (t == nt - 1))
        def _():
            for _, finish in plans:
                finish()

    return pl.pallas_call(
        body, name=f"dw_in_{layer}", grid=(NDEV, nt),
        in_specs=[pl.BlockSpec((tm, D), lambda k, t: (t, 0)), pl.BlockSpec((tm, CW), lambda k, t: (t, k))] + [ANY] * n,
        out_specs=[pl.BlockSpec((None, D, CW), lambda k, t: (k, 0, 0))] + [ANY] * n,
        out_shape=[jax.ShapeDtypeStruct((NDEV, D, CW), BF16)] + [jax.ShapeDtypeStruct(p.shape, p.dtype) for p in parts]
        + [jax.ShapeDtypeStruct((NDEV,) + g.shape, g.dtype) for g in gathers],
        scratch_shapes=[pltpu.VMEM((D, CW), F32)] + (_comm_scratch(n) if n else []),
        compiler_params=_cp(("arbitrary", "arbitrary")),
    )(xb, dzb, *parts, *gathers)


def dw_out(ycat, drb, layer):
    s = ycat.shape[0]
    tm = min(s, 1024)
    nt = s // tm

    def body(y_ref, dr_ref, o_ref, acc):
        t = pl.program_id(0)

        @pl.when(t == 0)
        def _():
            acc[...] = jnp.zeros_like(acc)

        acc[...] += _dot_tn(y_ref[...], dr_ref[...])

        @pl.when(t == nt - 1)
        def _():
            o_ref[...] = acc[...].reshape(NDEV, RW, D).astype(BF16)

    return pl.pallas_call(
        body, name=f"dw_out_{layer}", grid=(nt,),
        in_specs=[pl.BlockSpec((tm, DMIX), lambda t: (t, 0)), pl.BlockSpec((tm, D), lambda t: (t, 0))],
        out_specs=pl.BlockSpec((NDEV, RW, D), lambda t: (0, 0, 0)),
        out_shape=jax.ShapeDtypeStruct((NDEV, RW, D), BF16),
        scratch_shapes=[pltpu.VMEM((DMIX, D), F32)],
        compiler_params=_cp(("arbitrary",)),
    )(ycat, drb)


def _cols(j):
    return slice(j * D, (j + 1) * D)


def mix_even_bwd(z, dycat, pooled, cv, pwb, pscale, sw, layer, parts=None):
    s = z.shape[0]
    tm = min(s, 256)
    nt = s // tm
    carry = parts is not None

    def body(*refs):
        z_ref, dy_ref, pb_ref, cv_ref, pw_ref, ps_ref, sw_ref = refs[:7]
        if carry:
            p_ref = refs[7]
            dz_ref, dpw_ref, dps_ref, dsw_ref, dsb_ref, land_ref = refs[8:14]
            edp, edc, ya0, t1s, dpl, la, lb = refs[14:21]
            start, finish = _exchange_plan(p_ref, land_ref, refs[-3:], 0)
        else:
            dz_ref, dpw_ref, dps_ref, dsw_ref, dsb_ref = refs[7:12]
            edp, edc, ya0, t1s, dpl, la, lb = refs[12:19]
        i = pl.program_id(0)
        if carry:
            pl.when(i == 0)(start)

        @pl.when(i == 0)
        def _():
            edp[tm:tm + HP, :] = jnp.zeros((HP, D), F32)
            edc[tm:tm + HS, :] = jnp.zeros((HS, D), F32)
            dpw_ref[...] = jnp.zeros_like(dpw_ref)
            dps_ref[...] = jnp.zeros_like(dps_ref)
            dsw_ref[...] = jnp.zeros_like(dsw_ref)
            dsb_ref[...] = jnp.zeros_like(dsb_ref)

        blocks = [(r * RC, (r + 1) * RC, c * CB, (c + 1) * CB) for r in range(tm // RC) for c in range(D // CB)]
        for gi in range(len(POOL_WINDOWS)):
            c0, c1 = gi * PG, (gi + 1) * PG
            ya0[:, c0:c1] = _dot(pb_ref[:, c0:c1], pw_ref[gi])
        for r0, r1, c0, c1 in blocks:
            ga = z_ref[r0:r1, D + c0:D + c1].astype(F32)
            sg = _sig(ga)
            sil = ga * sg
            dya = dy_ref[r0:r1, c0:c1].astype(F32)
            y0 = ya0[r0:r1, c0:c1]
            ps = ps_ref[:, c0:c1]
            dps_ref[:, c0:c1] += _rowsum(dya * y0 * sil)
            dz_ref[r0:r1, D + c0:D + c1] = (dya * y0 * ps * (sg * (1.0 + ga * (1.0 - sg)))).astype(BF16)
            t1s[r0:r1, c0:c1] = (dya * ps * sil).astype(BF16)
        for gi in range(len(POOL_WINDOWS)):
            c0, c1 = gi * PG, (gi + 1) * PG
            dpl[:, c0:c1] = _dot_nt(t1s[:, c0:c1], pw_ref[gi])
            dpw_ref[gi] += _dot_tn(pb_ref[:, c0:c1], t1s[:, c0:c1])
        for r in range(tm // RC):
            r0, r1 = r * RC, (r + 1) * RC
            pos = ((nt - 1 - i) * tm + r0 + lax.broadcasted_iota(jnp.int32, (RC, 1), 0) + 1).astype(F32)
            for gi, w in enumerate(POOL_WINDOWS):
                c0, c1 = gi * PG, (gi + 1) * PG
                edp[r0:r1, c0:c1] = dpl[r0:r1, c0:c1] / jnp.minimum(pos, float(w))

        def level(src, dst, shift, c0, end):
            for r0 in range(0, end, 2 * RC):
                r1 = min(r0 + 2 * RC, end)
                dst[r0:r1, c0:D] = src[r0:r1, c0:D] + src[r0 + shift:r1 + shift, c0:D]

        level(edp, la, 1, PG, tm + 24)
        level(la, lb, 2, 2 * PG, tm + 16)
        level(lb, la, 4, 3 * PG, tm + 8)
        last = ((edp, 1), (la, 2), (lb, 4), (la, 8))
        for r in range(tm // RC):
            r0, r1 = r * RC, (r + 1) * RC
            for gi in range(len(POOL_WINDOWS)):
                c0, c1 = gi * PG, (gi + 1) * PG
                src, shift = last[gi]
                acc = src[r0:r1, c0:c1] + src[r0 + shift:r1 + shift, c0:c1] - dpl[r0:r1, c0:c1]
                dz_ref[r0:r1, c0:c1] = acc.astype(BF16)

        for r0, r1, c0, c1 in blocks:
            gb = z_ref[r0:r1, 5 * D + c0:5 * D + c1].astype(F32)
            sgb = _sig(gb)
            silb = gb * sgb
            dyb = dy_ref[r0:r1, D + c0:D + c1].astype(F32)
            cvv = cv_ref[r0:r1, c0:c1].astype(F32)
            bg = z_ref[r0:r1, 3 * D + c0:3 * D + c1].astype(F32)
            dz_ref[r0:r1, 3 * D + c0:3 * D + c1] = (dyb * cvv * silb).astype(BF16)
            dz_ref[r0:r1, 5 * D + c0:5 * D + c1] = (dyb * bg * cvv * (sgb * (1.0 + gb * (1.0 - sgb)))).astype(BF16)
            dcv = dyb * bg * silb
            dsb_ref[:, c0:c1] += _rowsum(dcv)
            edc[r0:r1, c0:c1] = dcv
        for r0, r1, c0, c1 in blocks:
            h = z_ref[r0:r1, 2 * D + c0:2 * D + c1].astype(F32)
            cg = z_ref[r0:r1, 4 * D + c0:4 * D + c1].astype(F32)
            q = cg * h
            d0 = edc[r0:r1, c0:c1]
            d1 = edc[r0 + 1:r1 + 1, c0:c1]
            d2 = edc[r0 + 2:r1 + 2, c0:c1]
            dq = sw_ref[2:3, c0:c1] * d0 + sw_ref[1:2, c0:c1] * d1 + sw_ref[0:1, c0:c1] * d2
            dsw_ref[2:3, c0:c1] += _rowsum(q * d0)
            dsw_ref[1:2, c0:c1] += _rowsum(q * d1)
            dsw_ref[0:1, c0:c1] += _rowsum(q * d2)
            dz_ref[r0:r1, 4 * D + c0:4 * D + c1] = (dq * h).astype(BF16)
            dz_ref[r0:r1, 2 * D + c0:2 * D + c1] = (dq * cg).astype(BF16)
        edp[tm:tm + HP, :] = edp[0:HP, :]
        edc[tm:tm + HS, :] = edc[0:HS, :]
        if carry:
            pl.when(i == nt - 1)(finish)

    rev = lambda i: (nt - 1 - i, 0)
    vec = pl.BlockSpec((1, D), lambda i: (0, 0))
    in_specs = [pl.BlockSpec((tm, NIN), rev), pl.BlockSpec((tm, DMIX), rev),
                pl.BlockSpec((tm, D), rev), pl.BlockSpec((tm, D), rev),
                pl.BlockSpec((4, PG, PG), lambda i: (0, 0, 0)), vec,
                pl.BlockSpec((KS, D), lambda i: (0, 0))]
    out_specs = [pl.BlockSpec((tm, NIN), rev), pl.BlockSpec((4, PG, PG), lambda i: (0, 0, 0)), vec,
                 pl.BlockSpec((KS, D), lambda i: (0, 0)), vec]
    out_shape = [jax.ShapeDtypeStruct((s, NIN), BF16), jax.ShapeDtypeStruct((4, PG, PG), F32),
                 jax.ShapeDtypeStruct((1, D), F32), jax.ShapeDtypeStruct((KS, D), F32),
                 jax.ShapeDtypeStruct((1, D), F32)]
    args = [z, dycat, pooled, cv, pwb, pscale, sw]
    scratch = [pltpu.VMEM((tm + HP, D), F32), pltpu.VMEM((tm + HS, D), F32), pltpu.VMEM((tm, D), F32),
               pltpu.VMEM((tm, D), BF16), pltpu.VMEM((tm, D), F32), pltpu.VMEM((tm + HP, D), F32),
               pltpu.VMEM((tm + HP, D), F32)]
    if carry:
        in_specs.append(ANY)
        args.append(parts)
        out_specs.append(ANY)
        out_shape.append(jax.ShapeDtypeStruct(parts.shape, parts.dtype))
        scratch += _comm_scratch(1)
    return pl.pallas_call(
        body, name=f"mix_even_bwd_{layer}", grid=(nt,), in_specs=in_specs, out_specs=out_specs,
        out_shape=out_shape, scratch_shapes=scratch, compiler_params=_cp(("arbitrary",)),
    )(*args)


def _ln_rows(v):
    mu = jnp.mean(v, axis=-1, keepdims=True)
    d = v - mu
    var = jnp.mean(d * d, axis=-1, keepdims=True)
    rstd = lax.rsqrt(var + LN_EPS)
    return d * rstd, rstd


def _ln_rows_bwd(dn, xh, rstd, g):
    dxh = dn * g
    m1 = jnp.mean(dxh, axis=-1, keepdims=True)
    m2 = jnp.mean(dxh * xh, axis=-1, keepdims=True)
    return rstd * (dxh - m1 - xh * m2)


def _shifted_copies(ext, shifted, rows):
    for b in range(1, SUB):
        shifted[b - 1] = ext[b:b + rows, :]


def _tap(ext, shifted, off, r0, r1, cols=slice(None)):
    a, b = off // SUB, off % SUB
    if b == 0:
        return ext[SUB * a + r0:SUB * a + r1, cols]
    return shifted[b - 1, SUB * a + r0:SUB * a + r1, cols]


def mix_odd_bwd(z, dycat, cz, slg, slb, wsb, wstb, sbcol, dw, dng, dnb, layer, parts):
    s = z.shape[0]
    tm = min(s, 256)
    nt = s // tm

    def body(z_ref, dy_ref, cz_ref, slg_ref, slb_ref, ws_ref, wst_ref, sb_ref, dw_ref, dng_ref, dnb_ref, p_ref,
             dz_ref, dslg_ref, dslb_ref, dws_ref, dsb_ref, ddw_ref, ddcb_ref, ddng_ref, ddnb_ref, land_ref,
             dvn, edz, esh, ddw8, *sems):
        i = pl.program_id(0)
        start, finish = _exchange_plan(p_ref, land_ref, sems, 0)
        pl.when(i == 0)(start)

        @pl.when(i == 0)
        def _():
            edz[tm:tm + HC, :] = jnp.zeros((HC, D), F32)
            for ref in (dslg_ref, dslb_ref, dws_ref, dsb_ref, ddw8, ddcb_ref, ddng_ref, ddnb_ref):
                ref[...] = jnp.zeros_like(ref)

        vh, vrs = _ln_rows(z_ref[:, _cols(1)].astype(F32))
        vnb = (vh * slg_ref[...] + slb_ref[...]).astype(BF16)
        for n in range(tm // SGU_BLOCK):
            r0, r1 = n * SGU_BLOCK, (n + 1) * SGU_BLOCK
            for hd in range(HEADS):
                c0, c1 = hd * HD, (hd + 1) * HD
                vblk = vnb[r0:r1, c0:c1]
                sv = _dot(ws_ref[hd], vblk) + sb_ref[hd]
                gc = z_ref[r0:r1, 2 * D + c0:2 * D + c1].astype(F32)
                sg = _sig(gc)
                sil = gc * sg
                u = z_ref[r0:r1, c0:c1].astype(F32)
                dyc = dy_ref[r0:r1, c0:c1].astype(F32)
                dz_ref[r0:r1, c0:c1] = (dyc * sv * sil).astype(BF16)
                dz_ref[r0:r1, 2 * D + c0:2 * D + c1] = (dyc * u * sv * (sg * (1.0 + gc * (1.0 - sg)))).astype(BF16)
                dsv = dyc * u * sil
                dsb_ref[hd] += jnp.sum(dsv, axis=-1, keepdims=True)
                dsvb = dsv.astype(BF16)
                dws_ref[hd] += _dot_nt(dsvb, vblk)
                dvn[r0:r1, c0:c1] = _dot(wst_ref[hd], dsvb)
        dv = dvn[...]
        dslg_ref[...] += _rowsum(dv * vh)
        dslb_ref[...] += _rowsum(dv)
        dz_ref[:, _cols(1)] = _ln_rows_bwd(dv, vh, vrs, slg_ref[...]).astype(BF16)

        zh, zrs = _ln_rows(cz_ref[...])
        zn = zh * dng_ref[...] + dnb_ref[...]
        sgn = _sig(zn)
        gd = z_ref[:, _cols(5)].astype(F32)
        sgd = _sig(gd)
        dyd = dy_ref[:, D:2 * D].astype(F32)
        dz_ref[:, _cols(5)] = (dyd * (zn * sgn) * (sgd * (1.0 + gd * (1.0 - sgd)))).astype(BF16)
        dzn = dyd * (gd * sgd) * (sgn * (1.0 + zn * (1.0 - sgn)))
        ddng_ref[...] += _rowsum(dzn * zh)
        ddnb_ref[...] += _rowsum(dzn)
        dcz = _ln_rows_bwd(dzn, zh, zrs, dng_ref[...])
        ddcb_ref[...] += _rowsum(dcz)
        edz[0:tm, :] = dcz
        _shifted_copies(edz, esh, tm + HC - SUB)
        for r0 in range(0, tm, RC):
            for c0 in range(0, D, CB):
                r1, c1 = r0 + RC, c0 + CB
                a = z_ref[r0:r1, 3 * D + c0:3 * D + c1].astype(F32)
                sgb = _sig(z_ref[r0:r1, 4 * D + c0:4 * D + c1].astype(F32))
                zg = a * sgb
                dzg = jnp.zeros((RC, CB), F32)
                for k in range(KD):
                    sh = _tap(edz, esh, KD - 1 - k, r0, r1, slice(c0, c1))
                    dzg = dzg + dw_ref[k:k + 1, c0:c1] * sh
                    prod = zg * sh
                    ddw8[k, :, c0:c1] += prod[0:SUB] + prod[SUB:2 * SUB]
                dz_ref[r0:r1, 3 * D + c0:3 * D + c1] = (dzg * sgb).astype(BF16)
                dz_ref[r0:r1, 4 * D + c0:4 * D + c1] = (dzg * a * sgb * (1.0 - sgb)).astype(BF16)
        edz[tm:tm + HC, :] = edz[0:HC, :]

        @pl.when(i == nt - 1)
        def _():
            ddw_ref[...] = jnp.sum(ddw8[...], axis=1)

        pl.when(i == nt - 1)(finish)

    rev = lambda i: (nt - 1 - i, 0)
    vec = pl.BlockSpec((1, D), lambda i: (0, 0))
    wspec = pl.BlockSpec((HEADS, SGU_BLOCK, SGU_BLOCK), lambda i: (0, 0, 0))
    bspec = pl.BlockSpec((HEADS, SGU_BLOCK, 1), lambda i: (0, 0, 0))
    kspec = pl.BlockSpec((KD, D), lambda i: (0, 0))
    return pl.pallas_call(
        body, name=f"mix_odd_bwd_{layer}", grid=(nt,),
        in_specs=[pl.BlockSpec((tm, NIN), rev), pl.BlockSpec((tm, DMIX), rev), pl.BlockSpec((tm, D), rev),
                  vec, vec, wspec, wspec, bspec, kspec, vec, vec, ANY],
        out_specs=[pl.BlockSpec((tm, NIN), rev), vec, vec, wspec, bspec, kspec, vec, vec, vec, ANY],
        out_shape=[jax.ShapeDtypeStruct((s, NIN), BF16), jax.ShapeDtypeStruct((1, D), F32),
                   jax.ShapeDtypeStruct((1, D), F32),
                   jax.ShapeDtypeStruct((HEADS, SGU_BLOCK, SGU_BLOCK), F32),
                   jax.ShapeDtypeStruct((HEADS, SGU_BLOCK, 1), F32), jax.ShapeDtypeStruct((KD, D), F32),
                   jax.ShapeDtypeStruct((1, D), F32), jax.ShapeDtypeStruct((1, D), F32),
                   jax.ShapeDtypeStruct((1, D), F32), jax.ShapeDtypeStruct(parts.shape, parts.dtype)],
        scratch_shapes=[pltpu.VMEM((tm, D), F32), pltpu.VMEM((tm + HC, D), F32),
                        pltpu.VMEM((SUB - 1, tm + HC - SUB, D), F32), pltpu.VMEM((KD, SUB, D), F32)]
        + _comm_scratch(1),
        compiler_params=_cp(("arbitrary",)),
    )(z, dycat, cz, slg, slb, wsb, wstb, sbcol, dw, dng, dnb, parts)


def _adamw_math(w, g, m, v):
    m = ADAM_B1 * m + (1.0 - ADAM_B1) * g
    v = ADAM_B2 * v + (1.0 - ADAM_B2) * (g * g)
    m_hat = m / (1.0 - ADAM_B1 ** ADAM_STEP)
    v_hat = v / (1.0 - ADAM_B2 ** ADAM_STEP)
    delta = -ADAM_LR * (m_hat / (jnp.sqrt(v_hat) + ADAM_EPS) + ADAM_WD * w)
    return delta, m, v


def adamw_big(w, m, v, parts0, parts1, name):
    _, r, c = w.shape
    tr = min(r, 256)
    nr = r // tr

    def body(w_ref, m_ref, v_ref, p0_ref, p1_ref, g_ref, d_ref, nm_ref, nv_ref):
        i = pl.program_id(0)

        def total(p_ref):
            acc = p_ref[0].astype(F32)
            for j in range(1, NDEV):
                acc = acc + p_ref[j].astype(F32)
            return acc

        @pl.when(i == 0)
        def _():
            g_ref[...] = total(p0_ref)

        @pl.when(i == 1)
        def _():
            g_ref[...] = total(p1_ref)

        delta, nm, nv = _adamw_math(w_ref[...], g_ref[...], m_ref[...], v_ref[...])
        d_ref[...] = delta
        nm_ref[...] = nm
        nv_ref[...] = nv

    wspec = pl.BlockSpec((None, tr, c), lambda i, j: (i, j, 0))
    p0 = pl.BlockSpec((NDEV, tr, c), lambda i, j: (0, jnp.where(i == 0, j, nr - 1), 0))
    p1 = pl.BlockSpec((NDEV, tr, c), lambda i, j: (0, jnp.where(i == 1, j, 0), 0))
    shp = jax.ShapeDtypeStruct(w.shape, F32)
    return pl.pallas_call(
        body, name=name, grid=(2, nr), in_specs=[wspec, wspec, wspec, p0, p1],
        out_specs=[wspec] * 4, out_shape=[shp] * 4,
        compiler_params=_cp(("arbitrary", "arbitrary")),
    )(w, m, v, parts0, parts1)


def sum_parts(parts, name):
    _, r, c = parts.shape

    def body(p_ref, o_ref):
        acc = p_ref[0]
        for j in range(1, NDEV):
            acc = acc + p_ref[j]
        o_ref[...] = acc

    return pl.pallas_call(body, name=name, out_shape=jax.ShapeDtypeStruct((r, c), F32),
                          compiler_params=_cp())(parts)


def adamw_small(w, g, m, v, name):
    def body(w_ref, g_ref, m_ref, v_ref, d_ref, nm_ref, nv_ref):
        delta, nm, nv = _adamw_math(w_ref[...], g_ref[...], m_ref[...], v_ref[...])
        d_ref[...] = delta
        nm_ref[...] = nm
        nv_ref[...] = nv

    shp = jax.ShapeDtypeStruct(w.shape, F32)
    return pl.pallas_call(body, name=name, out_shape=[shp] * 3, compiler_params=_cp())(w, g, m, v)


def _size(shape):
    n = 1
    for d in shape:
        n *= d
    return n


def _pack(arrays, rows):
    flat = jnp.concatenate([a.reshape(-1) for a in arrays])
    return jnp.pad(flat, (0, rows * 128 - flat.shape[0])).reshape(rows, 128)


def _unpack(flat, shapes):
    out, o = [], 0
    for shp in shapes:
        out.append(flat[o:o + _size(shp)].reshape(shp))
        o += _size(shp)
    return out


def _rows_for(shapes):
    return -(-sum(_size(shp) for shp in shapes) // 1024) * 8


SHARDED_SMALL = (("pool_w", (2, 4, 256, 256), 2), ("sconv_w", (2, KS, D), 2), ("sgu_ln_g", (2, D), 1),
                 ("sgu_ln_b", (2, D), 1), ("dconv_w", (2, KD, D), 2), ("dconv_b", (2, D), 1),
                 ("dnorm_g", (2, D), 1), ("dnorm_b", (2, D), 1))
REPLICATED_SMALL = (("ln_g", (DEPTH, D)), ("ln_b", (DEPTH, D)), ("pool_scale", (2, D)), ("sconv_b", (2, D)),
                    ("sgu_w", (2, HEADS, SGU_BLOCK, SGU_BLOCK)), ("sgu_b", (2, HEADS, SGU_BLOCK)))


def _shard_shape(shape, axis):
    return tuple(d // NDEV if a == axis else d for a, d in enumerate(shape))


def _merge_gathered(g, shape, axis):
    return jnp.moveaxis(g, 0, axis).reshape(shape)


def kernel(x, ln_g, ln_b, w_in_even, w_out_even, pool_w, pool_scale, sconv_w, sconv_b, w_in_odd, w_out_odd, sgu_ln_g, sgu_ln_b, sgu_w, sgu_b, dconv_w, dconv_b, dnorm_g, dnorm_b, loss_target, m_ln_g, m_ln_b, m_w_in_even, m_w_out_even, m_pool_w, m_pool_scale, m_sconv_w, m_sconv_b, m_w_in_odd, m_w_out_odd, m_sgu_ln_g, m_sgu_ln_b, m_sgu_w, m_sgu_b, m_dconv_w, m_dconv_b, m_dnorm_g, m_dnorm_b, v_ln_g, v_ln_b, v_w_in_even, v_w_out_even, v_pool_w, v_pool_scale, v_sconv_w, v_sconv_b, v_w_in_odd, v_w_out_odd, v_sgu_ln_g, v_sgu_ln_b, v_sgu_w, v_sgu_b, v_dconv_w, v_dconv_b, v_dnorm_g, v_dnorm_b):
    given = dict(locals())
    me = 4 * lax.axis_index("x") + 2 * lax.axis_index("y") + lax.axis_index("c")
    xs = x[0]
    target = loss_target[0]

    wc_in = cast_weights(w_in_even, w_in_odd, "cast_w_in")
    wc_out = cast_weights(w_out_even, w_out_odd, "cast_w_out")
    shard_shapes = [_shard_shape(shp, ax) for _, shp, ax in SHARDED_SMALL]
    srows = _rows_for(shard_shapes)
    xb, wg_first, gathered = cast_x(xs, wc_in, _pack([given[n] for n, _, _ in SHARDED_SMALL], srows))
    wg_in = [wg_first] + [None] * (DEPTH - 1)
    wg_out = [None] * DEPTH
    gathered = gathered.reshape(NDEV, -1)
    full, o = {}, 0
    for (n, shp, ax), sshp in zip(SHARDED_SMALL, shard_shapes):
        full[n] = _merge_gathered(gathered[:, o:o + _size(sshp)].reshape((NDEV,) + sshp), shp, ax)
        o += _size(sshp)

    mask = (jnp.arange(SGU_BLOCK)[None, :] // 64) <= (jnp.arange(SGU_BLOCK)[:, None] // 64)
    ws = jnp.where(mask[None, None], sgu_w, 0.0)
    wsb = ws.astype(BF16)
    wstb = jnp.swapaxes(ws, -1, -2).astype(BF16)
    pwb = full["pool_w"].astype(BF16)

    ones = jnp.ones((1, D), F32)
    zeros = jnp.zeros((1, D), F32)

    xres, gp, bp = xs, ones, zeros
    saved, below = [], None
    for layer in range(DEPTH):
        i = layer // 2
        if layer % 2 == 0:
            params = (pwb[i], pool_scale[i][None], full["sconv_w"][i], sconv_b[i][None])
        else:
            params = (full["sgu_ln_g"][i][None], full["sgu_ln_b"][i][None], wsb[i], sgu_b[i][:, :, None],
                      full["dconv_w"][i], full["dconv_b"][i][None], full["dnorm_g"][i][None],
                      full["dnorm_b"][i][None])
        outs = list(proj_in(wg_in[layer], layer, wc_in, wc_out, xb=xb if below is None else None, below=below))
        if layer + 1 < DEPTH:
            wg_in[layer + 1] = outs.pop()
        wg_out[layer] = outs.pop()
        z = outs[0]
        if below is not None:
            xhat, rstd, xb = outs[1:4]
            saved[-1] += [xhat, rstd]
            xres, gp, bp = xhat, below[5], below[6]
        outs = mix_fwd(z, layer, params)
        ycat, extra = outs[0], tuple(outs[1:])
        saved.append([xb, z, ycat, extra])
        below = (ycat, wg_out[layer], xres, gp, bp, ln_g[layer][None], ln_b[layer][None])
    *top, loss_local = head_loss_bwd(below, target)

    gsmall = {n: [None] * shp[0] for n, shp in REPLICATED_SMALL}
    gsmall.update({n: [None] * shp[0] for n, shp, _ in SHARDED_SMALL})
    parts_in, parts_out, parts_pw = [None] * DEPTH, [None] * DEPTH, [None] * 2
    small_names = [n for n, _ in REPLICATED_SMALL] + [n for n, _, _ in SHARDED_SMALL]
    gathered_names = [n for n in small_names if n != "pool_w"]
    gathered_shapes = [shp for n, shp in REPLICATED_SMALL] + [shp for n, shp, _ in SHARDED_SMALL if n != "pool_w"]
    grows = _rows_for(gathered_shapes + [(1,)])
    pending, above = None, None
    for layer in reversed(range(DEPTH)):
        i = layer // 2
        xb, z, ycat, extra = saved[layer][:4]
        if above is None:
            dr, drb, dycat, dg, db = top
        else:
            dr, drb, dycat, dg, db = ln_bwd_dycat(*saved[layer][4:], ln_g[layer][None], wg_out[layer], layer, above)
        gsmall["ln_g"][layer], gsmall["ln_b"][layer] = dg[0], db[0]
        dwo = dw_out(ycat, drb, layer)
        own_out = pending is None
        riding = dwo if own_out else pending
        if layer % 2 == 0:
            dzb, dpw, dps, dsw, dsb, landed = mix_even_bwd(
                z, dycat, extra[0], extra[1], pwb[i], pool_scale[i][None], full["sconv_w"][i], layer, parts=riding)
            gsmall["pool_scale"][i] = dps[0]
            gsmall["sconv_w"][i], gsmall["sconv_b"][i] = dsw, dsb[0]
        else:
            dzb, dslg, dslb, dws, dsbc, ddw, ddcb, ddng, ddnb, landed = mix_odd_bwd(
                z, dycat, extra[0], full["sgu_ln_g"][i][None], full["sgu_ln_b"][i][None], wsb[i], wstb[i],
                sgu_b[i][:, :, None], full["dconv_w"][i], full["dnorm_g"][i][None], full["dnorm_b"][i][None],
                layer, riding)
            gsmall["sgu_ln_g"][i], gsmall["sgu_ln_b"][i] = dslg[0], dslb[0]
            gsmall["sgu_w"][i], gsmall["sgu_b"][i] = jnp.where(mask[None], dws, 0.0), dsbc[:, :, 0]
            gsmall["dconv_w"][i], gsmall["dconv_b"][i] = ddw, ddcb[0]
            gsmall["dnorm_g"][i], gsmall["dnorm_b"][i] = ddng[0], ddnb[0]
        riders = [] if own_out else [dwo]
        if layer % 2 == 0:
            riders.append(dpw.reshape(4, NDEV, PG // NDEV, PG).transpose(1, 0, 2, 3))
        small_partial = []
        if layer == 0:
            small_partial.append(_pack([jnp.stack(gsmall[n]) for n in gathered_names] + [loss_local], grows))
        outs = list(dw_in(xb, dzb, layer, riders, small_partial))
        if layer == 0:
            small_gathered = outs.pop()
        if layer % 2 == 0:
            parts_pw[i] = outs.pop()
        if own_out:
            parts_out[layer] = landed
        else:
            parts_in[layer + 1] = landed
            parts_out[layer] = outs[1]
        pending = outs[0]
        above = (dzb, wg_in[layer], dr)
    dxn, parts_in[0] = dx_in(dzb, wg_in[0], dr, pending)
    grad_x = dxn[None]

    big = {}
    big["w_in_even"] = adamw_big(w_in_even, m_w_in_even, v_w_in_even, parts_in[0], parts_in[2], "adamw_w_in_even")
    big["w_in_odd"] = adamw_big(w_in_odd, m_w_in_odd, v_w_in_odd, parts_in[1], parts_in[3], "adamw_w_in_odd")
    big["w_out_even"] = adamw_big(w_out_even, m_w_out_even, v_w_out_even, parts_out[0], parts_out[2], "adamw_w_out_even")
    big["w_out_odd"] = adamw_big(w_out_odd, m_w_out_odd, v_w_out_odd, parts_out[1], parts_out[3], "adamw_w_out_odd")

    gsum = sum_parts(small_gathered, "sum_small_grads")
    unpacked = _unpack(gsum.reshape(-1), gathered_shapes + [()])
    loss = unpacked.pop()
    gfull = dict(zip(gathered_names, unpacked))
    own_shapes = [shp for _, shp in REPLICATED_SMALL] + shard_shapes
    gown = {n: gfull[n] for n, _ in REPLICATED_SMALL}
    for (n, shp, ax), sshp in zip(SHARDED_SMALL, shard_shapes):
        if n == "pool_w":
            gown[n] = jnp.stack([sum_parts(p.reshape(NDEV, -1, 128), f"sum_pool_w_{j}").reshape(sshp[1:])
                                 for j, p in enumerate(parts_pw)])
        else:
            gown[n] = lax.dynamic_slice_in_dim(gfull[n], me * sshp[ax], sshp[ax], axis=ax)
    orows = _rows_for(own_shapes)
    packed = [_pack([src[n] for n in small_names], orows) for src in
              (given, gown, {n: given["m_" + n] for n in small_names}, {n: given["v_" + n] for n in small_names})]
    sd, sm, sv = adamw_small(*packed, "adamw_small")
    small = {}
    for n, d_, m_, v_ in zip(small_names, _unpack(sd.reshape(-1), own_shapes), _unpack(sm.reshape(-1), own_shapes),
                             _unpack(sv.reshape(-1), own_shapes)):
        small[n] = (gown[n], d_, m_, v_)

    weights = ['ln_g', 'ln_b', 'w_in_even', 'w_out_even', 'pool_w', 'pool_scale', 'sconv_w', 'sconv_b', 'w_in_odd',
               'w_out_odd', 'sgu_ln_g', 'sgu_ln_b', 'sgu_w', 'sgu_b', 'dconv_w', 'dconv_b', 'dnorm_g', 'dnorm_b']
    res = {n: (big[n] if n in big else small[n]) for n in weights}
    return (loss, grad_x, *[res[n][0] for n in weights], *[res[n][1] for n in weights],
            *[res[n][2] for n in weights], *[res[n][3] for n in weights])
```

```python
import jax
import jax.numpy as jnp
from jax import lax
from jax.experimental import pallas as pl
from jax.experimental.pallas import tpu as pltpu

F32 = jnp.float32
BF16 = jnp.bfloat16

D = 1024
DMIX = 2048
NIN = 6144
NDEV = 8
CW = NIN // NDEV
RW = DMIX // NDEV
DEPTH = 4
ALPHA = (2 * DEPTH) ** 0.25
LN_EPS = 1e-5
POOL_WINDOWS = (2, 4, 8, 16)
PG = 256
SGU_BLOCK = 128
HEADS = 4
HD = 256
KD = 31
KS = 3
SUB = 8
RC = 16
CB = 512
HP = 32
HS = 8
HC = 32
NSEM = 7

ADAM_LR = 0.001
ADAM_B1 = 0.9
ADAM_B2 = 0.999
ADAM_EPS = 1e-08
ADAM_WD = 0.01
ADAM_STEP = 10

VMEM_LIMIT = 56 * 1024 * 1024
MESH = pl.DeviceIdType.MESH
ANY = pl.BlockSpec(memory_space=pl.ANY)


def _cp(sem=None):
    if sem is None:
        return pltpu.CompilerParams(vmem_limit_bytes=VMEM_LIMIT)
    return pltpu.CompilerParams(dimension_semantics=sem, vmem_limit_bytes=VMEM_LIMIT)


def _sig(x):
    return 0.5 * jnp.tanh(0.5 * x) + 0.5


def _dot(a, b):
    return jnp.dot(a, b, preferred_element_type=F32)


def _dot_nt(a, b):
    return lax.dot_general(a, b, (((1,), (1,)), ((), ())), preferred_element_type=F32)


def _dot_tn(a, b):
    return lax.dot_general(a, b, (((0,), (0,)), ((), ())), preferred_element_type=F32)


def _rowsum(x):
    return jnp.sum(x, axis=0, keepdims=True)


def _comm_scratch(n):
    return [pltpu.SemaphoreType.DMA((n * NSEM,)), pltpu.SemaphoreType.DMA((n * NSEM,)),
            pltpu.SemaphoreType.DMA((n,))]


def _gather_plan(src_ref, out_ref, sems, n):
    send_sems, recv_sems, local_sems = sems
    base = n * NSEM
    x, y, c = lax.axis_index("x"), lax.axis_index("y"), lax.axis_index("c")
    me, sibling = (x, y, c), (x, y, 1 - c)
    chips = [(1 - x, y), (x, 1 - y), (1 - x, 1 - y)]

    def slot(px, py, pc):
        return out_ref.at[4 * px + 2 * py + pc]

    def copy(k, blk, to, src=None):
        return pltpu.make_async_remote_copy(
            src_ref=slot(*blk) if src is None else src, dst_ref=slot(*blk),
            send_sem=send_sems.at[base + k], recv_sem=recv_sems.at[base + k],
            device_id=to, device_id_type=MESH)

    def mine():
        return pltpu.make_async_copy(src_ref, slot(*me), local_sems.at[n])

    def start():
        mine().start()
        copy(0, me, sibling, src=src_ref).start()
        for j, chip in enumerate(chips):
            copy(1 + j, me, (*chip, c), src=src_ref).start()

    def finish():
        for j, chip in enumerate(chips):
            copy(1 + j, (*chip, c), me).wait_recv()
            copy(4 + j, (*chip, c), sibling).start()
        copy(0, sibling, me).wait_recv()
        for j, chip in enumerate(chips):
            copy(4 + j, (*chip, 1 - c), me).wait_recv()
        copy(0, me, sibling, src=src_ref).wait_send()
        for j, chip in enumerate(chips):
            copy(1 + j, me, (*chip, c), src=src_ref).wait_send()
            copy(4 + j, (*chip, c), sibling).wait_send()
        mine().wait()

    return start, finish


def _exchange_plan(p_ref, out_ref, sems, n):
    send_sems, recv_sems, local_sems = sems
    base = n * NSEM
    x, y, c = lax.axis_index("x"), lax.axis_index("y"), lax.axis_index("c")
    me = 4 * x + 2 * y + c

    def copy(r, landing):
        px, py, pc = x ^ (r >> 2), y ^ ((r >> 1) & 1), c ^ (r & 1)
        peer = 4 * px + 2 * py + pc
        return pltpu.make_async_remote_copy(
            src_ref=p_ref.at[peer], dst_ref=out_ref.at[peer if landing else me],
            send_sem=send_sems.at[base + r - 1], recv_sem=recv_sems.at[base + r - 1],
            device_id=(px, py, pc), device_id_type=MESH)

    def mine():
        return pltpu.make_async_copy(p_ref.at[me], out_ref.at[me], local_sems.at[n])

    def start():
        mine().start()
        for r in range(1, NDEV):
            copy(r, False).start()

    def finish():
        for r in range(1, NDEV):
            copy(r, True).wait_recv()
        for r in range(1, NDEV):
            copy(r, False).wait_send()
        mine().wait()

    return start, finish


def cast_x(x, w_first, small):
    s = x.shape[0]
    tm = min(s, 512)
    nt = s // tm

    def body(x_ref, wf_ref, sm_ref, o_ref, gw_ref, gs_ref, *sems):
        i = pl.program_id(0)
        plans = [_gather_plan(wf_ref.at[0], gw_ref, sems, 0), _gather_plan(sm_ref, gs_ref, sems, 1)]

        @pl.when(i == 0)
        def _():
            for start, _ in plans:
                start()

        o_ref[...] = x_ref[...].astype(BF16)

        @pl.when(i == nt - 1)
        def _():
            for _, finish in plans:
                finish()

    return pl.pallas_call(
        body, name="cast_x", grid=(nt,),
        in_specs=[pl.BlockSpec((tm, D), lambda i: (i, 0)), ANY, ANY],
        out_specs=[pl.BlockSpec((tm, D), lambda i: (i, 0)), ANY, ANY],
        out_shape=[jax.ShapeDtypeStruct((s, D), BF16),
                   jax.ShapeDtypeStruct((NDEV,) + w_first.shape[1:], w_first.dtype),
                   jax.ShapeDtypeStruct((NDEV,) + small.shape, small.dtype)],
        scratch_shapes=_comm_scratch(2), compiler_params=_cp(("arbitrary",)),
    )(x, w_first, small)


def cast_weights(w_even, w_odd, name):
    _, r, c = w_even.shape

    def body(e_ref, o_ref, out_ref):
        layer = pl.program_id(0)

        @pl.when(layer % 2 == 0)
        def _():
            out_ref[...] = e_ref[...].astype(BF16)

        @pl.when(layer % 2 == 1)
        def _():
            out_ref[...] = o_ref[...].astype(BF16)

    spec = pl.BlockSpec((None, r, c), lambda l: (l // 2, 0, 0))
    return pl.pallas_call(
        body, name=name, grid=(DEPTH,), in_specs=[spec, spec],
        out_specs=pl.BlockSpec((None, r, c), lambda l: (l, 0, 0)),
        out_shape=jax.ShapeDtypeStruct((DEPTH, r, c), BF16), compiler_params=_cp(("parallel",)),
    )(w_even, w_odd)


def _deepnorm(y_ref, wo_ref, xr_ref, gp_ref, bp_ref, g_ref, b_ref):
    xin = xr_ref[...] * gp_ref[...] + bp_ref[...]
    r = ALPHA * xin + _dot(y_ref[...], wo_ref[...].reshape(DMIX, D))
    mu = jnp.mean(r, axis=-1, keepdims=True)
    d = r - mu
    var = jnp.mean(d * d, axis=-1, keepdims=True)
    rstd = lax.rsqrt(var + LN_EPS)
    xh = d * rstd
    return xh, rstd, xh * g_ref[...] + b_ref[...]


def proj_in(wg, layer, wc_in, wc_out, xb=None, below=None):
    fused = below is not None
    s = below[0].shape[0] if fused else xb.shape[0]
    tm = min(s, 512)
    nt = s // tm
    nxt = layer + 1 < DEPTH
    nhead = 7 if fused else 1

    def body(*refs):
        head, refs = refs[:nhead], refs[nhead:]
        if nxt:
            w_ref, wco_ref, wci_ref = refs[:3]
            refs = refs[3:]
        else:
            w_ref, wco_ref = refs[:2]
            refs = refs[2:]
        o_ref = refs[0]
        if fused:
            xh_ref, rs_ref, xb_ref = refs[1:4]
            refs = refs[4:]
        else:
            refs = refs[1:]
        go_ref = refs[0]
        sems = refs[-3:]
        i = pl.program_id(0)
        plans = [_gather_plan(wco_ref.at[layer], go_ref, sems, 0)]
        if nxt:
            plans.append(_gather_plan(wci_ref.at[layer + 1], refs[1], sems, 1))

        @pl.when(i == 0)
        def _():
            for start, _ in plans:
                start()

        if fused:
            xh, rstd, xn = _deepnorm(*head)
            xh_ref[...] = xh
            rs_ref[...] = rstd
            x = xn.astype(BF16)
            xb_ref[...] = x
        else:
            x = head[0][...]
        for k in range(NDEV):
            o_ref[:, k * CW:(k + 1) * CW] = _dot(x, w_ref[k]).astype(BF16)

        @pl.when(i == nt - 1)
        def _():
            for _, finish in plans:
                finish()

    row = pl.BlockSpec((tm, D), lambda i: (i, 0))
    vec = pl.BlockSpec((1, D), lambda i: (0, 0))
    if fused:
        in_specs = [pl.BlockSpec((tm, DMIX), lambda i: (i, 0)),
                    pl.BlockSpec((NDEV, RW, D), lambda i: (0, 0, 0), pipeline_mode=pl.Buffered(1)),
                    row, vec, vec, vec, vec]
        args = list(below)
    else:
        in_specs, args = [row], [xb]
    in_specs += [pl.BlockSpec((NDEV, D, CW), lambda i: (0, 0, 0), pipeline_mode=pl.Buffered(1)), ANY]
    args += [wg, wc_out]
    if nxt:
        in_specs.append(ANY)
        args.append(wc_in)
    out_specs = [pl.BlockSpec((tm, NIN), lambda i: (i, 0))]
    out_shape = [jax.ShapeDtypeStruct((s, NIN), BF16)]
    if fused:
        out_specs += [row, pl.BlockSpec((tm, 1), lambda i: (i, 0)), row]
        out_shape += [jax.ShapeDtypeStruct((s, D), F32), jax.ShapeDtypeStruct((s, 1), F32),
                      jax.ShapeDtypeStruct((s, D), BF16)]
    out_specs.append(ANY)
    out_shape.append(jax.ShapeDtypeStruct((NDEV, RW, D), BF16))
    if nxt:
        out_specs.append(ANY)
        out_shape.append(jax.ShapeDtypeStruct((NDEV, D, CW), BF16))
    return pl.pallas_call(
        body, name=f"proj_in_{layer}", grid=(nt,), in_specs=in_specs, out_specs=out_specs,
        out_shape=out_shape, scratch_shapes=_comm_scratch(2 if nxt else 1),
        compiler_params=_cp(("arbitrary",)),
    )(*args)


def mix_fwd(z, layer, params):
    s = z.shape[0]
    tm = min(s, 512)
    even = layer % 2 == 0
    npar = len(params)
    nout = 3 if even else 2

    def even_tile(zr, j, par, outs, scr):
        pw_ref, ps_ref, sw_ref, sb_ref = par
        yc_ref, pb_ref, cv_ref = outs
        exa, eq, ya0, la, lb = scr
        for c in range(tm // RC):
            r0, r1 = c * RC, (c + 1) * RC
            exa[HP + r0:HP + r1, :] = zr[r0:r1, _cols(0)].astype(F32)
            eq[HS + r0:HS + r1, :] = zr[r0:r1, _cols(4)].astype(F32) * zr[r0:r1, _cols(2)].astype(F32)

        def level(src, dst, shift, c0, start):
            for r0 in range(start, HP + tm, 2 * RC):
                r1 = min(r0 + 2 * RC, HP + tm)
                dst[r0:r1, c0:D] = src[r0:r1, c0:D] + src[r0 - shift:r1 - shift, c0:D]

        level(exa, la, 1, PG, 8)
        level(la, lb, 2, 2 * PG, 16)
        level(lb, la, 4, 3 * PG, 24)
        last = ((exa, 1), (la, 2), (lb, 4), (la, 8))
        for c in range(tm // RC):
            r0, r1 = HP + c * RC, HP + (c + 1) * RC
            pos = (j * tm + c * RC + lax.broadcasted_iota(jnp.int32, (RC, 1), 0) + 1).astype(F32)
            for gi, w in enumerate(POOL_WINDOWS):
                c0, c1 = gi * PG, (gi + 1) * PG
                src, shift = last[gi]
                acc = src[r0:r1, c0:c1] + src[r0 - shift:r1 - shift, c0:c1]
                pooled = acc / jnp.minimum(pos, float(w)) - exa[r0:r1, c0:c1]
                pb_ref[c * RC:(c + 1) * RC, c0:c1] = pooled.astype(BF16)
        for gi in range(len(POOL_WINDOWS)):
            c0, c1 = gi * PG, (gi + 1) * PG
            ya0[:, c0:c1] = _dot(pb_ref[:, c0:c1], pw_ref[gi])
        for c in range(tm // RC):
            r0, r1 = c * RC, (c + 1) * RC
            ga = zr[r0:r1, _cols(1)].astype(F32)
            yc_ref[r0:r1, 0:D] = (ya0[r0:r1, :] * ps_ref[...] * (ga * _sig(ga))).astype(BF16)
            cv = (sw_ref[2:3, :] * eq[HS + r0:HS + r1, :] + sw_ref[1:2, :] * eq[HS + r0 - 1:HS + r1 - 1, :]
                  + sw_ref[0:1, :] * eq[HS + r0 - 2:HS + r1 - 2, :] + sb_ref[...])
            cv_ref[r0:r1, :] = cv.astype(BF16)
            gb = zr[r0:r1, _cols(5)].astype(F32)
            yc_ref[r0:r1, D:2 * D] = (zr[r0:r1, _cols(3)].astype(F32) * cv * (gb * _sig(gb))).astype(BF16)
        exa[0:HP, :] = exa[tm:tm + HP, :]
        eq[0:HS, :] = eq[tm:tm + HS, :]

    def odd_tile(zr, j, par, outs, scr):
        slg_ref, slb_ref, ws_ref, sb_ref, dw_ref, dcb_ref, dng_ref, dnb_ref = par
        yc_ref, cz_ref = outs
        ezg, esh = scr
        vh, _ = _ln_rows(zr[:, _cols(1)].astype(F32))
        vnb = (vh * slg_ref[...] + slb_ref[...]).astype(BF16)
        for n in range(tm // SGU_BLOCK):
            r0, r1 = n * SGU_BLOCK, (n + 1) * SGU_BLOCK
            for hd in range(HEADS):
                c0, c1 = hd * HD, (hd + 1) * HD
                sv = _dot(ws_ref[hd], vnb[r0:r1, c0:c1]) + sb_ref[hd]
                gc = zr[r0:r1, 2 * D + c0:2 * D + c1].astype(F32)
                yc_ref[r0:r1, c0:c1] = (zr[r0:r1, c0:c1].astype(F32) * sv * (gc * _sig(gc))).astype(BF16)
        ezg[HC:HC + tm, :] = zr[:, _cols(3)].astype(F32) * _sig(zr[:, _cols(4)].astype(F32))
        _shifted_copies(ezg, esh, tm + HC - SUB)
        cz = jnp.zeros((tm, D), F32) + dcb_ref[...]
        for k in range(KD):
            cz = cz + dw_ref[k:k + 1, :] * _tap(ezg, esh, HC - (KD - 1) + k, 0, tm)
        cz_ref[...] = cz
        zh, _ = _ln_rows(cz)
        zn = zh * dng_ref[...] + dnb_ref[...]
        gd = zr[:, _cols(5)].astype(F32)
        yc_ref[:, D:2 * D] = ((zn * _sig(zn)) * (gd * _sig(gd))).astype(BF16)
        ezg[0:HC, :] = ezg[tm:tm + HC, :]

    def body(z_ref, *refs):
        par, outs, scr = refs[:npar], refs[npar:npar + nout], refs[npar + nout:]
        j = pl.program_id(0)

        @pl.when(j == 0)
        def _():
            if even:
                scr[0][0:HP, :] = jnp.zeros((HP, D), F32)
                scr[1][0:HS, :] = jnp.zeros((HS, D), F32)
            else:
                scr[0][0:HC, :] = jnp.zeros((HC, D), F32)

        (even_tile if even else odd_tile)(z_ref, j, par, outs, scr)

    row = lambda c: pl.BlockSpec((tm, c), lambda j: (j, 0))
    vec = pl.BlockSpec((1, D), lambda j: (0, 0))
    if even:
        par_specs = [pl.BlockSpec((4, PG, PG), lambda j: (0, 0, 0)), vec, pl.BlockSpec((KS, D), lambda j: (0, 0)), vec]
        out_specs = [row(DMIX), row(D), row(D)]
        out_shape = [jax.ShapeDtypeStruct((s, DMIX), BF16), jax.ShapeDtypeStruct((s, D), BF16),
                     jax.ShapeDtypeStruct((s, D), BF16)]
        scratch = [pltpu.VMEM((HP + tm, D), F32), pltpu.VMEM((HS + tm, D), F32), pltpu.VMEM((tm, D), F32),
                   pltpu.VMEM((HP + tm, D), F32), pltpu.VMEM((HP + tm, D), F32)]
    else:
        par_specs = [vec, vec, pl.BlockSpec((HEADS, SGU_BLOCK, SGU_BLOCK), lambda j: (0, 0, 0)),
                     pl.BlockSpec((HEADS, SGU_BLOCK, 1), lambda j: (0, 0, 0)),
                     pl.BlockSpec((KD, D), lambda j: (0, 0)), vec, vec, vec]
        out_specs = [row(DMIX), row(D)]
        out_shape = [jax.ShapeDtypeStruct((s, DMIX), BF16), jax.ShapeDtypeStruct((s, D), F32)]
        scratch = [pltpu.VMEM((HC + tm, D), F32), pltpu.VMEM((SUB - 1, tm + HC - SUB, D), F32)]
    return pl.pallas_call(
        body, name=f"mix_fwd_{layer}", grid=(s // tm,), in_specs=[row(NIN)] + par_specs, out_specs=out_specs,
        out_shape=out_shape, scratch_shapes=scratch, compiler_params=_cp(("arbitrary",)),
    )(z, *params)


def _zero_at_start(*refs):
    @pl.when(pl.program_id(0) == 0)
    def _():
        for ref in refs:
            ref[...] = jnp.zeros_like(ref)


def _ln_bwd_tile(dxo, xh, rstd, g_ref, wo_ref, dr_ref, drb_ref, dyc_ref, dg_ref, db_ref):
    dg_ref[...] += _rowsum(dxo * xh)
    db_ref[...] += _rowsum(dxo)
    dr = _ln_rows_bwd(dxo, xh, rstd, g_ref[...])
    dr_ref[...] = dr
    drb = dr.astype(BF16)
    drb_ref[...] = drb
    dyc_ref[...] = _dot_nt(drb, wo_ref[...].reshape(DMIX, D)).astype(BF16)


def _ln_bwd_outs(s, tm):
    row = pl.BlockSpec((tm, D), lambda i: (i, 0))
    vec = pl.BlockSpec((1, D), lambda i: (0, 0))
    return ([row, row, pl.BlockSpec((tm, DMIX), lambda i: (i, 0)), vec, vec],
            [jax.ShapeDtypeStruct((s, D), F32), jax.ShapeDtypeStruct((s, D), BF16),
             jax.ShapeDtypeStruct((s, DMIX), BF16), jax.ShapeDtypeStruct((1, D), F32),
             jax.ShapeDtypeStruct((1, D), F32)])


def head_loss_bwd(below, target):
    s = target.shape[0]
    tm = min(s, 512)

    def body(*refs):
        t_ref = refs[7]
        loss_ref = refs[-1]
        _zero_at_start(*refs[11:14])
        xh, rstd, xn = _deepnorm(*refs[:7])
        err = xn - t_ref[...]
        loss_ref[...] += 0.5 * jnp.sum(jnp.mean(err * err, axis=-1, keepdims=True), axis=0, keepdims=True)
        _ln_bwd_tile(err * (1.0 / D), xh, rstd, refs[5], refs[1], *refs[8:13])

    row = pl.BlockSpec((tm, D), lambda i: (i, 0))
    vec = pl.BlockSpec((1, D), lambda i: (0, 0))
    out_specs, out_shape = _ln_bwd_outs(s, tm)
    return pl.pallas_call(
        body, name="head_loss_bwd", grid=(s // tm,),
        in_specs=[pl.BlockSpec((tm, DMIX), lambda i: (i, 0)), pl.BlockSpec((NDEV, RW, D), lambda i: (0, 0, 0)),
                  row, vec, vec, vec, vec, row],
        out_specs=out_specs + [pl.BlockSpec((1, 1), lambda i: (0, 0))],
        out_shape=out_shape + [jax.ShapeDtypeStruct((1, 1), F32)],
        compiler_params=_cp(("arbitrary",)),
    )(*below, target)


def ln_bwd_dycat(xhat, rstd, g, wog, layer, upstream):
    s = xhat.shape[0]
    tm = min(s, 512)

    def body(dz_ref, wi_ref, dru_ref, xh_ref, rs_ref, g_ref, w_ref, *outs):
        _zero_at_start(*outs[3:5])
        dxo = ALPHA * dru_ref[...]
        for k in range(NDEV):
            dxo += _dot_nt(dz_ref[:, k * CW:(k + 1) * CW], wi_ref[k])
        _ln_bwd_tile(dxo, xh_ref[...], rs_ref[...], g_ref, w_ref, *outs)

    row = pl.BlockSpec((tm, D), lambda i: (i, 0))
    vec = pl.BlockSpec((1, D), lambda i: (0, 0))
    out_specs, out_shape = _ln_bwd_outs(s, tm)
    return pl.pallas_call(
        body, name=f"ln_bwd_dycat_{layer}", grid=(s // tm,),
        in_specs=[pl.BlockSpec((tm, NIN), lambda i: (i, 0)),
                  pl.BlockSpec((NDEV, D, CW), lambda i: (0, 0, 0), pipeline_mode=pl.Buffered(1)), row,
                  row, pl.BlockSpec((tm, 1), lambda i: (i, 0)), vec,
                  pl.BlockSpec((NDEV, RW, D), lambda i: (0, 0, 0), pipeline_mode=pl.Buffered(1))],
        out_specs=out_specs, out_shape=out_shape, compiler_params=_cp(("arbitrary",)),
    )(*upstream, xhat, rstd, g, wog)


def dx_in(dzb, wg, dr, parts):
    s = dzb.shape[0]
    tm = min(s, 256)
    nt = s // tm

    def body(dz_ref, w_ref, dr_ref, p_ref, o_ref, land_ref, *sems):
        start, finish = _exchange_plan(p_ref, land_ref, sems, 0)
        pl.when(pl.program_id(0) == 0)(start)
        acc = ALPHA * dr_ref[...]
        for k in range(NDEV):
            acc += _dot_nt(dz_ref[:, k * CW:(k + 1) * CW], w_ref[k])
        o_ref[...] = acc
        pl.when(pl.program_id(0) == nt - 1)(finish)

    row = pl.BlockSpec((tm, D), lambda i: (i, 0))
    return pl.pallas_call(
        body, name="dx_in_0", grid=(nt,),
        in_specs=[pl.BlockSpec((tm, NIN), lambda i: (i, 0)),
                  pl.BlockSpec((NDEV, D, CW), lambda i: (0, 0, 0), pipeline_mode=pl.Buffered(1)), row, ANY],
        out_specs=[row, ANY],
        out_shape=[jax.ShapeDtypeStruct((s, D), F32), jax.ShapeDtypeStruct(parts.shape, parts.dtype)],
        scratch_shapes=_comm_scratch(1), compiler_params=_cp(("arbitrary",)),
    )(dzb, wg, dr, parts)


def dw_in(xb, dzb, layer, parts=(), gathers=()):
    s = xb.shape[0]
    tm = min(s, 4096)
    nt = s // tm
    ne = len(parts)
    n = ne + len(gathers)

    def body(*refs):
        x_ref, dz_ref = refs[:2]
        p_refs = refs[2:2 + n]
        o_ref = refs[2 + n]
        land_refs = refs[3 + n:3 + 2 * n]
        acc = refs[3 + 2 * n]
        plans = [(_exchange_plan if j < ne else _gather_plan)(p, land, refs[-3:], j)
                 for j, (p, land) in enumerate(zip(p_refs, land_refs))]
        k, t = pl.program_id(0), pl.program_id(1)

        @pl.when((k == 0) & (t == 0))
        def _():
            for start, _ in plans:
                start()

        @pl.when(t == 0)
        def _():
            acc[...] = jnp.zeros_like(acc)

        acc[...] += _dot_tn(x_ref[...], dz_ref[...])

        @pl.when(t == nt - 1)
        def _():
            o_ref[...] = acc[...].astype(BF16)

        @pl.when((k == NDEV - 1) & (t == nt - 1))
        def _():
            for _, finish in plans:
                finish()

    return pl.pallas_call(
        body, name=f"dw_in_{layer}", grid=(NDEV, nt),
        in_specs=[pl.BlockSpec((tm, D), lambda k, t: (t, 0)), pl.BlockSpec((tm, CW), lambda k, t: (t, k))] + [ANY] * n,
        out_specs=[pl.BlockSpec((None, D, CW), lambda k, t: (k, 0, 0))] + [ANY] * n,
        out_shape=[jax.ShapeDtypeStruct((NDEV, D, CW), BF16)] + [jax.ShapeDtypeStruct(p.shape, p.dtype) for p in parts]
        + [jax.ShapeDtypeStruct((NDEV,) + g.shape, g.dtype) for g in gathers],
        scratch_shapes=[pltpu.VMEM((D, CW), F32)] + (_comm_scratch(n) if n else []),
        compiler_params=_cp(("arbitrary", "arbitrary")),
    )(xb, dzb, *parts, *gathers)


def dw_out(ycat, drb, layer):
    s = ycat.shape[0]
    tm = min(s, 1024)
    nt = s // tm

    def body(y_ref, dr_ref, o_ref, acc):
        t = pl.program_id(0)

        @pl.when(t == 0)
        def _():
            acc[...] = jnp.zeros_like(acc)

        acc[...] += _dot_tn(y_ref[...], dr_ref[...])

        @pl.when(t == nt - 1)
        def _():
            o_ref[...] = acc[...].reshape(NDEV, RW, D).astype(BF16)

    return pl.pallas_call(
        body, name=f"dw_out_{layer}", grid=(nt,),
        in_specs=[pl.BlockSpec((tm, DMIX), lambda t: (t, 0)), pl.BlockSpec((tm, D), lambda t: (t, 0))],
        out_specs=pl.BlockSpec((NDEV, RW, D), lambda t: (0, 0, 0)),
        out_shape=jax.ShapeDtypeStruct((NDEV, RW, D), BF16),
        scratch_shapes=[pltpu.VMEM((DMIX, D), F32)],
        compiler_params=_cp(("arbitrary",)),
    )(ycat, drb)


def _cols(j):
    return slice(j * D, (j + 1) * D)


def mix_even_bwd(z, dycat, pooled, cv, pwb, pscale, sw, layer, parts=None):
    s = z.shape[0]
    tm = min(s, 256)
    nt = s // tm
    carry = parts is not None

    def body(*refs):
        z_ref, dy_ref, pb_ref, cv_ref, pw_ref, ps_ref, sw_ref = refs[:7]
        if carry:
            p_ref = refs[7]
            dz_ref, dpw_ref, dps_ref, dsw_ref, dsb_ref, land_ref = refs[8:14]
            edp, edc, ya0, t1s, dpl, la, lb = refs[14:21]
            start, finish = _exchange_plan(p_ref, land_ref, refs[-3:], 0)
        else:
            dz_ref, dpw_ref, dps_ref, dsw_ref, dsb_ref = refs[7:12]
            edp, edc, ya0, t1s, dpl, la, lb = refs[12:19]
        i = pl.program_id(0)
        if carry:
            pl.when(i == 0)(start)

        @pl.when(i == 0)
        def _():
            edp[tm:tm + HP, :] = jnp.zeros((HP, D), F32)
            edc[tm:tm + HS, :] = jnp.zeros((HS, D), F32)
            dpw_ref[...] = jnp.zeros_like(dpw_ref)
            dps_ref[...] = jnp.zeros_like(dps_ref)
            dsw_ref[...] = jnp.zeros_like(dsw_ref)
            dsb_ref[...] = jnp.zeros_like(dsb_ref)

        blocks = [(r * RC, (r + 1) * RC, c * CB, (c + 1) * CB) for r in range(tm // RC) for c in range(D // CB)]
        for gi in range(len(POOL_WINDOWS)):
            c0, c1 = gi * PG, (gi + 1) * PG
            ya0[:, c0:c1] = _dot(pb_ref[:, c0:c1], pw_ref[gi])
        for r0, r1, c0, c1 in blocks:
            ga = z_ref[r0:r1, D + c0:D + c1].astype(F32)
            sg = _sig(ga)
            sil = ga * sg
            dya = dy_ref[r0:r1, c0:c1].astype(F32)
            y0 = ya0[r0:r1, c0:c1]
            ps = ps_ref[:, c0:c1]
            dps_ref[:, c0:c1] += _rowsum(dya * y0 * sil)
            dz_ref[r0:r1, D + c0:D + c1] = (dya * y0 * ps * (sg * (1.0 + ga * (1.0 - sg)))).astype(BF16)
            t1s[r0:r1, c0:c1] = (dya * ps * sil).astype(BF16)
        for gi in range(len(POOL_WINDOWS)):
            c0, c1 = gi * PG, (gi + 1) * PG
            dpl[:, c0:c1] = _dot_nt(t1s[:, c0:c1], pw_ref[gi])
            dpw_ref[gi] += _dot_tn(pb_ref[:, c0:c1], t1s[:, c0:c1])
        for r in range(tm // RC):
            r0, r1 = r * RC, (r + 1) * RC
            pos = ((nt - 1 - i) * tm + r0 + lax.broadcasted_iota(jnp.int32, (RC, 1), 0) + 1).astype(F32)
            for gi, w in enumerate(POOL_WINDOWS):
                c0, c1 = gi * PG, (gi + 1) * PG
                edp[r0:r1, c0:c1] = dpl[r0:r1, c0:c1] / jnp.minimum(pos, float(w))

        def level(src, dst, shift, c0, end):
            for r0 in range(0, end, 2 * RC):
                r1 = min(r0 + 2 * RC, end)
                dst[r0:r1, c0:D] = src[r0:r1, c0:D] + src[r0 + shift:r1 + shift, c0:D]

        level(edp, la, 1, PG, tm + 24)
        level(la, lb, 2, 2 * PG, tm + 16)
        level(lb, la, 4, 3 * PG, tm + 8)
        last = ((edp, 1), (la, 2), (lb, 4), (la, 8))
        for r in range(tm // RC):
            r0, r1 = r * RC, (r + 1) * RC
            for gi in range(len(POOL_WINDOWS)):
                c0, c1 = gi * PG, (gi + 1) * PG
                src, shift = last[gi]
                acc = src[r0:r1, c0:c1] + src[r0 + shift:r1 + shift, c0:c1] - dpl[r0:r1, c0:c1]
                dz_ref[r0:r1, c0:c1] = acc.astype(BF16)

        for r0, r1, c0, c1 in blocks:
            gb = z_ref[r0:r1, 5 * D + c0:5 * D + c1].astype(F32)
            sgb = _sig(gb)
            silb = gb * sgb
            dyb = dy_ref[r0:r1, D + c0:D + c1].astype(F32)
            cvv = cv_ref[r0:r1, c0:c1].astype(F32)
            bg = z_ref[r0:r1, 3 * D + c0:3 * D + c1].astype(F32)
            dz_ref[r0:r1, 3 * D + c0:3 * D + c1] = (dyb * cvv * silb).astype(BF16)
            dz_ref[r0:r1, 5 * D + c0:5 * D + c1] = (dyb * bg * cvv * (sgb * (1.0 + gb * (1.0 - sgb)))).astype(BF16)
            dcv = dyb * bg * silb
            dsb_ref[:, c0:c1] += _rowsum(dcv)
            edc[r0:r1, c0:c1] = dcv
        for r0, r1, c0, c1 in blocks:
            h = z_ref[r0:r1, 2 * D + c0:2 * D + c1].astype(F32)
            cg = z_ref[r0:r1, 4 * D + c0:4 * D + c1].astype(F32)
            q = cg * h
            d0 = edc[r0:r1, c0:c1]
            d1 = edc[r0 + 1:r1 + 1, c0:c1]
            d2 = edc[r0 + 2:r1 + 2, c0:c1]
            dq = sw_ref[2:3, c0:c1] * d0 + sw_ref[1:2, c0:c1] * d1 + sw_ref[0:1, c0:c1] * d2
            dsw_ref[2:3, c0:c1] += _rowsum(q * d0)
            dsw_ref[1:2, c0:c1] += _rowsum(q * d1)
            dsw_ref[0:1, c0:c1] += _rowsum(q * d2)
            dz_ref[r0:r1, 4 * D + c0:4 * D + c1] = (dq * h).astype(BF16)
            dz_ref[r0:r1, 2 * D + c0:2 * D + c1] = (dq * cg).astype(BF16)
        edp[tm:tm + HP, :] = edp[0:HP, :]
        edc[tm:tm + HS, :] = edc[0:HS, :]
        if carry:
            pl.when(i == nt - 1)(finish)

    rev = lambda i: (nt - 1 - i, 0)
    vec = pl.BlockSpec((1, D), lambda i: (0, 0))
    in_specs = [pl.BlockSpec((tm, NIN), rev), pl.BlockSpec((tm, DMIX), rev),
                pl.BlockSpec((tm, D), rev), pl.BlockSpec((tm, D), rev),
                pl.BlockSpec((4, PG, PG), lambda i: (0, 0, 0)), vec,
                pl.BlockSpec((KS, D), lambda i: (0, 0))]
    out_specs = [pl.BlockSpec((tm, NIN), rev), pl.BlockSpec((4, PG, PG), lambda i: (0, 0, 0)), vec,
                 pl.BlockSpec((KS, D), lambda i: (0, 0)), vec]
    out_shape = [jax.ShapeDtypeStruct((s, NIN), BF16), jax.ShapeDtypeStruct((4, PG, PG), F32),
                 jax.ShapeDtypeStruct((1, D), F32), jax.ShapeDtypeStruct((KS, D), F32),
                 jax.ShapeDtypeStruct((1, D), F32)]
    args = [z, dycat, pooled, cv, pwb, pscale, sw]
    scratch = [pltpu.VMEM((tm + HP, D), F32), pltpu.VMEM((tm + HS, D), F32), pltpu.VMEM((tm, D), F32),
               pltpu.VMEM((tm, D), BF16), pltpu.VMEM((tm, D), F32), pltpu.VMEM((tm + HP, D), F32),
               pltpu.VMEM((tm + HP, D), F32)]
    if carry:
        in_specs.append(ANY)
        args.append(parts)
        out_specs.append(ANY)
        out_shape.append(jax.ShapeDtypeStruct(parts.shape, parts.dtype))
        scratch += _comm_scratch(1)
    return pl.pallas_call(
        body, name=f"mix_even_bwd_{layer}", grid=(nt,), in_specs=in_specs, out_specs=out_specs,
        out_shape=out_shape, scratch_shapes=scratch, compiler_params=_cp(("arbitrary",)),
    )(*args)


def _ln_rows(v):
    mu = jnp.mean(v, axis=-1, keepdims=True)
    d = v - mu
    var = jnp.mean(d * d, axis=-1, keepdims=True)
    rstd = lax.rsqrt(var + LN_EPS)
    return d * rstd, rstd


def _ln_rows_bwd(dn, xh, rstd, g):
    dxh = dn * g
    m1 = jnp.mean(dxh, axis=-1, keepdims=True)
    m2 = jnp.mean(dxh * xh, axis=-1, keepdims=True)
    return rstd * (dxh - m1 - xh * m2)


def _shifted_copies(ext, shifted, rows):
    for b in range(1, SUB):
        shifted[b - 1] = ext[b:b + rows, :]


def _tap(ext, shifted, off, r0, r1, cols=slice(None)):
    a, b = off // SUB, off % SUB
    if b == 0:
        return ext[SUB * a + r0:SUB * a + r1, cols]
    return shifted[b - 1, SUB * a + r0:SUB * a + r1, cols]


def mix_odd_bwd(z, dycat, cz, slg, slb, wsb, wstb, sbcol, dw, dng, dnb, layer, parts):
    s = z.shape[0]
    tm = min(s, 256)
    nt = s // tm

    def body(z_ref, dy_ref, cz_ref, slg_ref, slb_ref, ws_ref, wst_ref, sb_ref, dw_ref, dng_ref, dnb_ref, p_ref,
             dz_ref, dslg_ref, dslb_ref, dws_ref, dsb_ref, ddw_ref, ddcb_ref, ddng_ref, ddnb_ref, land_ref,
             dvn, edz, esh, ddw8, *sems):
        i = pl.program_id(0)
        start, finish = _exchange_plan(p_ref, land_ref, sems, 0)
        pl.when(i == 0)(start)

        @pl.when(i == 0)
        def _():
            edz[tm:tm + HC, :] = jnp.zeros((HC, D), F32)
            for ref in (dslg_ref, dslb_ref, dws_ref, dsb_ref, ddw8, ddcb_ref, ddng_ref, ddnb_ref):
                ref[...] = jnp.zeros_like(ref)

        vh, vrs = _ln_rows(z_ref[:, _cols(1)].astype(F32))
        vnb = (vh * slg_ref[...] + slb_ref[...]).astype(BF16)
        for n in range(tm // SGU_BLOCK):
            r0, r1 = n * SGU_BLOCK, (n + 1) * SGU_BLOCK
            for hd in range(HEADS):
                c0, c1 = hd * HD, (hd + 1) * HD
                vblk = vnb[r0:r1, c0:c1]
                sv = _dot(ws_ref[hd], vblk) + sb_ref[hd]
                gc = z_ref[r0:r1, 2 * D + c0:2 * D + c1].astype(F32)
                sg = _sig(gc)
                sil = gc * sg
                u = z_ref[r0:r1, c0:c1].astype(F32)
                dyc = dy_ref[r0:r1, c0:c1].astype(F32)
                dz_ref[r0:r1, c0:c1] = (dyc * sv * sil).astype(BF16)
                dz_ref[r0:r1, 2 * D + c0:2 * D + c1] = (dyc * u * sv * (sg * (1.0 + gc * (1.0 - sg)))).astype(BF16)
                dsv = dyc * u * sil
                dsb_ref[hd] += jnp.sum(dsv, axis=-1, keepdims=True)
                dsvb = dsv.astype(BF16)
                dws_ref[hd] += _dot_nt(dsvb, vblk)
                dvn[r0:r1, c0:c1] = _dot(wst_ref[hd], dsvb)
        dv = dvn[...]
        dslg_ref[...] += _rowsum(dv * vh)
        dslb_ref[...] += _rowsum(dv)
        dz_ref[:, _cols(1)] = _ln_rows_bwd(dv, vh, vrs, slg_ref[...]).astype(BF16)

        zh, zrs = _ln_rows(cz_ref[...])
        zn = zh * dng_ref[...] + dnb_ref[...]
        sgn = _sig(zn)
        gd = z_ref[:, _cols(5)].astype(F32)
        sgd = _sig(gd)
        dyd = dy_ref[:, D:2 * D].astype(F32)
        dz_ref[:, _cols(5)] = (dyd * (zn * sgn) * (sgd * (1.0 + gd * (1.0 - sgd)))).astype(BF16)
        dzn = dyd * (gd * sgd) * (sgn * (1.0 + zn * (1.0 - sgn)))
        ddng_ref[...] += _rowsum(dzn * zh)
        ddnb_ref[...] += _rowsum(dzn)
        dcz = _ln_rows_bwd(dzn, zh, zrs, dng_ref[...])
        ddcb_ref[...] += _rowsum(dcz)
        edz[0:tm, :] = dcz
        _shifted_copies(edz, esh, tm + HC - SUB)
        for r0 in range(0, tm, RC):
            for c0 in range(0, D, CB):
                r1, c1 = r0 + RC, c0 + CB
                a = z_ref[r0:r1, 3 * D + c0:3 * D + c1].astype(F32)
                sgb = _sig(z_ref[r0:r1, 4 * D + c0:4 * D + c1].astype(F32))
                zg = a * sgb
                dzg = jnp.zeros((RC, CB), F32)
                for k in range(KD):
                    sh = _tap(edz, esh, KD - 1 - k, r0, r1, slice(c0, c1))
                    dzg = dzg + dw_ref[k:k + 1, c0:c1] * sh
                    prod = zg * sh
                    ddw8[k, :, c0:c1] += prod[0:SUB] + prod[SUB:2 * SUB]
                dz_ref[r0:r1, 3 * D + c0:3 * D + c1] = (dzg * sgb).astype(BF16)
                dz_ref[r0:r1, 4 * D + c0:4 * D + c1] = (dzg * a * sgb * (1.0 - sgb)).astype(BF16)
        edz[tm:tm + HC, :] = edz[0:HC, :]

        @pl.when(i == nt - 1)
        def _():
            ddw_ref[...] = jnp.sum(ddw8[...], axis=1)

        pl.when(i == nt - 1)(finish)

    rev = lambda i: (nt - 1 - i, 0)
    vec = pl.BlockSpec((1, D), lambda i: (0, 0))
    wspec = pl.BlockSpec((HEADS, SGU_BLOCK, SGU_BLOCK), lambda i: (0, 0, 0))
    bspec = pl.BlockSpec((HEADS, SGU_BLOCK, 1), lambda i: (0, 0, 0))
    kspec = pl.BlockSpec((KD, D), lambda i: (0, 0))
    return pl.pallas_call(
        body, name=f"mix_odd_bwd_{layer}", grid=(nt,),
        in_specs=[pl.BlockSpec((tm, NIN), rev), pl.BlockSpec((tm, DMIX), rev), pl.BlockSpec((tm, D), rev),
                  vec, vec, wspec, wspec, bspec, kspec, vec, vec, ANY],
        out_specs=[pl.BlockSpec((tm, NIN), rev), vec, vec, wspec, bspec, kspec, vec, vec, vec, ANY],
        out_shape=[jax.ShapeDtypeStruct((s, NIN), BF16), jax.ShapeDtypeStruct((1, D), F32),
                   jax.ShapeDtypeStruct((1, D), F32),
                   jax.ShapeDtypeStruct((HEADS, SGU_BLOCK, SGU_BLOCK), F32),
                   jax.ShapeDtypeStruct((HEADS, SGU_BLOCK, 1), F32), jax.ShapeDtypeStruct((KD, D), F32),
                   jax.ShapeDtypeStruct((1, D), F32), jax.ShapeDtypeStruct((1, D), F32),
                   jax.ShapeDtypeStruct((1, D), F32), jax.ShapeDtypeStruct(parts.shape, parts.dtype)],
        scratch_shapes=[pltpu.VMEM((tm, D), F32), pltpu.VMEM((tm + HC, D), F32),
                        pltpu.VMEM((SUB - 1, tm + HC - SUB, D), F32), pltpu.VMEM((KD, SUB, D), F32)]
        + _comm_scratch(1),
        compiler_params=_cp(("arbitrary",)),
    )(z, dycat, cz, slg, slb, wsb, wstb, sbcol, dw, dng, dnb, parts)


def _adamw_math(w, g, m, v):
    m = ADAM_B1 * m + (1.0 - ADAM_B1) * g
    v = ADAM_B2 * v + (1.0 - ADAM_B2) * (g * g)
    m_hat = m / (1.0 - ADAM_B1 ** ADAM_STEP)
    v_hat = v / (1.0 - ADAM_B2 ** ADAM_STEP)
    delta = -ADAM_LR * (m_hat / (jnp.sqrt(v_hat) + ADAM_EPS) + ADAM_WD * w)
    return delta, m, v


def adamw_big(w, m, v, parts0, parts1, name):
    _, r, c = w.shape
    tr = min(r, 256)
    nr = r // tr

    def body(w_ref, m_ref, v_ref, p0_ref, p1_ref, g_ref, d_ref, nm_ref, nv_ref):
        i = pl.program_id(0)

        def total(p_ref):
            acc = p_ref[0].astype(F32)
            for j in range(1, NDEV):
                acc = acc + p_ref[j].astype(F32)
            return acc

        @pl.when(i == 0)
        def _():
            g_ref[...] = total(p0_ref)

        @pl.when(i == 1)
        def _():
            g_ref[...] = total(p1_ref)

        delta, nm, nv = _adamw_math(w_ref[...], g_ref[...], m_ref[...], v_ref[...])
        d_ref[...] = delta
        nm_ref[...] = nm
        nv_ref[...] = nv

    wspec = pl.BlockSpec((None, tr, c), lambda i, j: (i, j, 0))
    p0 = pl.BlockSpec((NDEV, tr, c), lambda i, j: (0, jnp.where(i == 0, j, nr - 1), 0))
    p1 = pl.BlockSpec((NDEV, tr, c), lambda i, j: (0, jnp.where(i == 1, j, 0), 0))
    shp = jax.ShapeDtypeStruct(w.shape, F32)
    return pl.pallas_call(
        body, name=name, grid=(2, nr), in_specs=[wspec, wspec, wspec, p0, p1],
        out_specs=[wspec] * 4, out_shape=[shp] * 4,
        compiler_params=_cp(("arbitrary", "arbitrary")),
    )(w, m, v, parts0, parts1)


def sum_parts(parts, name):
    _, r, c = parts.shape

    def body(p_ref, o_ref):
        acc = p_ref[0]
        for j in range(1, NDEV):
            acc = acc + p_ref[j]
        o_ref[...] = acc

    return pl.pallas_call(body, name=name, out_shape=jax.ShapeDtypeStruct((r, c), F32),
                          compiler_params=_cp())(parts)


def adamw_small(w, g, m, v, name):
    def body(w_ref, g_ref, m_ref, v_ref, d_ref, nm_ref, nv_ref):
        delta, nm, nv = _adamw_math(w_ref[...], g_ref[...], m_ref[...], v_ref[...])
        d_ref[...] = delta
        nm_ref[...] = nm
        nv_ref[...] = nv

    shp = jax.ShapeDtypeStruct(w.shape, F32)
    return pl.pallas_call(body, name=name, out_shape=[shp] * 3, compiler_params=_cp())(w, g, m, v)


def _size(shape):
    n = 1
    for d in shape:
        n *= d
    return n


def _pack(arrays, rows):
    flat = jnp.concatenate([a.reshape(-1) for a in arrays])
    return jnp.pad(flat, (0, rows * 128 - flat.shape[0])).reshape(rows, 128)


def _unpack(flat, shapes):
    out, o = [], 0
    for shp in shapes:
        out.append(flat[o:o + _size(shp)].reshape(shp))
        o += _size(shp)
    return out


def _rows_for(shapes):
    return -(-sum(_size(shp) for shp in shapes) // 1024) * 8


SHARDED_SMALL = (("pool_w", (2, 4, 256, 256), 2), ("sconv_w", (2, KS, D), 2), ("sgu_ln_g", (2, D), 1),
                 ("sgu_ln_b", (2, D), 1), ("dconv_w", (2, KD, D), 2), ("dconv_b", (2, D), 1),
                 ("dnorm_g", (2, D), 1), ("dnorm_b", (2, D), 1))
REPLICATED_SMALL = (("ln_g", (DEPTH, D)), ("ln_b", (DEPTH, D)), ("pool_scale", (2, D)), ("sconv_b", (2, D)),
                    ("sgu_w", (2, HEADS, SGU_BLOCK, SGU_BLOCK)), ("sgu_b", (2, HEADS, SGU_BLOCK)))


def _shard_shape(shape, axis):
    return tuple(d // NDEV if a == axis else d for a, d in enumerate(shape))


def _merge_gathered(g, shape, axis):
    return jnp.moveaxis(g, 0, axis).reshape(shape)


def kernel(x, ln_g, ln_b, w_in_even, w_out_even, pool_w, pool_scale, sconv_w, sconv_b, w_in_odd, w_out_odd, sgu_ln_g, sgu_ln_b, sgu_w, sgu_b, dconv_w, dconv_b, dnorm_g, dnorm_b, loss_target, m_ln_g, m_ln_b, m_w_in_even, m_w_out_even, m_pool_w, m_pool_scale, m_sconv_w, m_sconv_b, m_w_in_odd, m_w_out_odd, m_sgu_ln_g, m_sgu_ln_b, m_sgu_w, m_sgu_b, m_dconv_w, m_dconv_b, m_dnorm_g, m_dnorm_b, v_ln_g, v_ln_b, v_w_in_even, v_w_out_even, v_pool_w, v_pool_scale, v_sconv_w, v_sconv_b, v_w_in_odd, v_w_out_odd, v_sgu_ln_g, v_sgu_ln_b, v_sgu_w, v_sgu_b, v_dconv_w, v_dconv_b, v_dnorm_g, v_dnorm_b):
    given = dict(locals())
    me = 4 * lax.axis_index("x") + 2 * lax.axis_index("y") + lax.axis_index("c")
    xs = x[0]
    target = loss_target[0]

    wc_in = cast_weights(w_in_even, w_in_odd, "cast_w_in")
    wc_out = cast_weights(w_out_even, w_out_odd, "cast_w_out")
    shard_shapes = [_shard_shape(shp, ax) for _, shp, ax in SHARDED_SMALL]
    srows = _rows_for(shard_shapes)
    xb, wg_first, gathered = cast_x(xs, wc_in, _pack([given[n] for n, _, _ in SHARDED_SMALL], srows))
    wg_in = [wg_first] + [None] * (DEPTH - 1)
    wg_out = [None] * DEPTH
    gathered = gathered.reshape(NDEV, -1)
    full, o = {}, 0
    for (n, shp, ax), sshp in zip(SHARDED_SMALL, shard_shapes):
        full[n] = _merge_gathered(gathered[:, o:o + _size(sshp)].reshape((NDEV,) + sshp), shp, ax)
        o += _size(sshp)

    mask = (jnp.arange(SGU_BLOCK)[None, :] // 64) <= (jnp.arange(SGU_BLOCK)[:, None] // 64)
    ws = jnp.where(mask[None, None], sgu_w, 0.0)
    wsb = ws.astype(BF16)
    wstb = jnp.swapaxes(ws, -1, -2).astype(BF16)
    pwb = full["pool_w"].astype(BF16)

    ones = jnp.ones((1, D), F32)
    zeros = jnp.zeros((1, D), F32)

    xres, gp, bp = xs, ones, zeros
    saved, below = [], None
    for layer in range(DEPTH):
        i = layer // 2
        if layer % 2 == 0:
            params = (pwb[i], pool_scale[i][None], full["sconv_w"][i], sconv_b[i][None])
        else:
            params = (full["sgu_ln_g"][i][None], full["sgu_ln_b"][i][None], wsb[i], sgu_b[i][:, :, None],
                      full["dconv_w"][i], full["dconv_b"][i][None], full["dnorm_g"][i][None],
                      full["dnorm_b"][i][None])
        outs = list(proj_in(wg_in[layer], layer, wc_in, wc_out, xb=xb if below is None else None, below=below))
        if layer + 1 < DEPTH:
            wg_in[layer + 1] = outs.pop()
        wg_out[layer] = outs.pop()
        z = outs[0]
        if below is not None:
            xhat, rstd, xb = outs[1:4]
            saved[-1] += [xhat, rstd]
            xres, gp, bp = xhat, below[5], below[6]
        outs = mix_fwd(z, layer, params)
        ycat, extra = outs[0], tuple(outs[1:])
        saved.append([xb, z, ycat, extra])
        below = (ycat, wg_out[layer], xres, gp, bp, ln_g[layer][None], ln_b[layer][None])
    *top, loss_local = head_loss_bwd(below, target)

    gsmall = {n: [None] * shp[0] for n, shp in REPLICATED_SMALL}
    gsmall.update({n: [None] * shp[0] for n, shp, _ in SHARDED_SMALL})
    parts_in, parts_out, parts_pw = [None] * DEPTH, [None] * DEPTH, [None] * 2
    small_names = [n for n, _ in REPLICATED_SMALL] + [n for n, _, _ in SHARDED_SMALL]
    gathered_names = [n for n in small_names if n != "pool_w"]
    gathered_shapes = [shp for n, shp in REPLICATED_SMALL] + [shp for n, shp, _ in SHARDED_SMALL if n != "pool_w"]
    grows = _rows_for(gathered_shapes + [(1,)])
    pending, above = None, None
    for layer in reversed(range(DEPTH)):
        i = layer // 2
        xb, z, ycat, extra = saved[layer][:4]
        if above is None:
            dr, drb, dycat, dg, db = top
        else:
            dr, drb, dycat, dg, db = ln_bwd_dycat(*saved[layer][4:], ln_g[layer][None], wg_out[layer], layer, above)
        gsmall["ln_g"][layer], gsmall["ln_b"][layer] = dg[0], db[0]
        dwo = dw_out(ycat, drb, layer)
        own_out = pending is None
        riding = dwo if own_out else pending
        if layer % 2 == 0:
            dzb, dpw, dps, dsw, dsb, landed = mix_even_bwd(
                z, dycat, extra[0], extra[1], pwb[i], pool_scale[i][None], full["sconv_w"][i], layer, parts=riding)
            gsmall["pool_scale"][i] = dps[0]
            gsmall["sconv_w"][i], gsmall["sconv_b"][i] = dsw, dsb[0]
        else:
            dzb, dslg, dslb, dws, dsbc, ddw, ddcb, ddng, ddnb, landed = mix_odd_bwd(
                z, dycat, extra[0], full["sgu_ln_g"][i][None], full["sgu_ln_b"][i][None], wsb[i], wstb[i],
                sgu_b[i][:, :, None], full["dconv_w"][i], full["dnorm_g"][i][None], full["dnorm_b"][i][None],
                layer, riding)
            gsmall["sgu_ln_g"][i], gsmall["sgu_ln_b"][i] = dslg[0], dslb[0]
            gsmall["sgu_w"][i], gsmall["sgu_b"][i] = jnp.where(mask[None], dws, 0.0), dsbc[:, :, 0]
            gsmall["dconv_w"][i], gsmall["dconv_b"][i] = ddw, ddcb[0]
            gsmall["dnorm_g"][i], gsmall["dnorm_b"][i] = ddng[0], ddnb[0]
        riders = [] if own_out else [dwo]
        if layer % 2 == 0:
            riders.append(dpw.reshape(4, NDEV, PG // NDEV, PG).transpose(1, 0, 2, 3))
        small_partial = []
        if layer == 0:
            small_partial.append(_pack([jnp.stack(gsmall[n]) for n in gathered_names] + [loss_local], grows))
        outs = list(dw_in(xb, dzb, layer, riders, small_partial))
        if layer == 0:
            small_gathered = outs.pop()
        if layer % 2 == 0:
            parts_pw[i] = outs.pop()
        if own_out:
            parts_out[layer] = landed
        else:
            parts_in[layer + 1] = landed
            parts_out[layer] = outs[1]
        pending = outs[0]
        above = (dzb, wg_in[layer], dr)
    dxn, parts_in[0] = dx_in(dzb, wg_in[0], dr, pending)
    grad_x = dxn[None]

    big = {}
    big["w_in_even"] = adamw_big(w_in_even, m_w_in_even, v_w_in_even, parts_in[0], parts_in[2], "adamw_w_in_even")
    big["w_in_odd"] = adamw_big(w_in_odd, m_w_in_odd, v_w_in_odd, parts_in[1], parts_in[3], "adamw_w_in_odd")
    big["w_out_even"] = adamw_big(w_out_even, m_w_out_even, v_w_out_even, parts_out[0], parts_out[2], "adamw_w_out_even")
    big["w_out_odd"] = adamw_big(w_out_odd, m_w_out_odd, v_w_out_odd, parts_out[1], parts_out[3], "adamw_w_out_odd")

    gsum = sum_parts(small_gathered, "sum_small_grads")
    unpacked = _unpack(gsum.reshape(-1), gathered_shapes + [()])
    loss = unpacked.pop()
    gfull = dict(zip(gathered_names, unpacked))
    own_shapes = [shp for _, shp in REPLICATED_SMALL] + shard_shapes
    gown = {n: gfull[n] for n, _ in REPLICATED_SMALL}
    for (n, shp, ax), sshp in zip(SHARDED_SMALL, shard_shapes):
        if n == "pool_w":
            gown[n] = jnp.stack([sum_parts(p.reshape(NDEV, -1, 128), f"sum_pool_w_{j}").reshape(sshp[1:])
                                 for j, p in enumerate(parts_pw)])
        else:
            gown[n] = lax.dynamic_slice_in_dim(gfull[n], me * sshp[ax], sshp[ax], axis=ax)
    orows = _rows_for(own_shapes)
    packed = [_pack([src[n] for n in small_names], orows) for src in
              (given, gown, {n: given["m_" + n] for n in small_names}, {n: given["v_" + n] for n in small_names})]
    sd, sm, sv = adamw_small(*packed, "adamw_small")
    small = {}
    for n, d_, m_, v_ in zip(small_names, _unpack(sd.reshape(-1), own_shapes), _unpack(sm.reshape(-1), own_shapes),
                             _unpack(sv.reshape(-1), own_shapes)):
        small[n] = (gown[n], d_, m_, v_)

    weights = ['ln_g', 'ln_b', 'w_in_even', 'w_out_even', 'pool_w', 'pool_scale', 'sconv_w', 'sconv_b', 'w_in_odd',
               'w_out_odd', 'sgu_ln_g', 'sgu_ln_b', 'sgu_w', 'sgu_b', 'dconv_w', 'dconv_b', 'dnorm_g', 'dnorm_b']
    res = {n: (big[n] if n in big else small[n]) for n in weights}
    return (loss, grad_x, *[res[n][0] for n in weights], *[res[n][1] for n in weights],
            *[res[n][2] for n in weights], *[res[n][3] for n in weights])
```

```python
import jax
import jax.numpy as jnp
from jax import lax
from jax.experimental import pallas as pl
from jax.experimental.pallas import tpu as pltpu

F32 = jnp.float32
BF16 = jnp.bfloat16

D = 1024
DMIX = 2048
NIN = 6144
NDEV = 8
CW = NIN // NDEV
RW = DMIX // NDEV
DEPTH = 4
ALPHA = (2 * DEPTH) ** 0.25
LN_EPS = 1e-5
POOL_WINDOWS = (2, 4, 8, 16)
PG = 256
SGU_BLOCK = 128
HEADS = 4
HD = 256
KD = 31
KS = 3
SUB = 8
RC = 16
CB = 512
HP = 32
HS = 8
HC = 32
NSEM = 7

ADAM_LR = 0.001
ADAM_B1 = 0.9
ADAM_B2 = 0.999
ADAM_EPS = 1e-08
ADAM_WD = 0.01
ADAM_STEP = 10

VMEM_LIMIT = 56 * 1024 * 1024
MESH = pl.DeviceIdType.MESH
ANY = pl.BlockSpec(memory_space=pl.ANY)


def _cp(sem=None):
    if sem is None:
        return pltpu.CompilerParams(vmem_limit_bytes=VMEM_LIMIT)
    return pltpu.CompilerParams(dimension_semantics=sem, vmem_limit_bytes=VMEM_LIMIT)


def _sig(x):
    return 0.5 * jnp.tanh(0.5 * x) + 0.5


def _silu(x):
    h = 0.5 * x
    return h + h * jnp.tanh(h)


def _dsilu(sg, sil):
    return sg + sil * (1.0 - sg)


def _dot(a, b):
    return jnp.dot(a, b, preferred_element_type=F32)


def _dot_nt(a, b):
    return lax.dot_general(a, b, (((1,), (1,)), ((), ())), preferred_element_type=F32)


def _dot_tn(a, b):
    return lax.dot_general(a, b, (((0,), (0,)), ((), ())), preferred_element_type=F32)


def _rowsum(x):
    return jnp.sum(x, axis=0, keepdims=True)


def _comm_scratch(n):
    return [pltpu.SemaphoreType.DMA((n * NSEM,)), pltpu.SemaphoreType.DMA((n * NSEM,)),
            pltpu.SemaphoreType.DMA((n,))]


def _gather_plan(src_ref, out_ref, sems, n):
    send_sems, recv_sems, local_sems = sems
    base = n * NSEM
    x, y, c = lax.axis_index("x"), lax.axis_index("y"), lax.axis_index("c")
    me, sibling = (x, y, c), (x, y, 1 - c)
    chips = [(1 - x, y), (x, 1 - y), (1 - x, 1 - y)]

    def slot(px, py, pc):
        return out_ref.at[4 * px + 2 * py + pc]

    def copy(k, blk, to, src=None):
        return pltpu.make_async_remote_copy(
            src_ref=slot(*blk) if src is None else src, dst_ref=slot(*blk),
            send_sem=send_sems.at[base + k], recv_sem=recv_sems.at[base + k],
            device_id=to, device_id_type=MESH)

    def mine():
        return pltpu.make_async_copy(src_ref, slot(*me), local_sems.at[n])

    def start():
        mine().start()
        copy(0, me, sibling, src=src_ref).start()
        for j, chip in enumerate(chips):
            copy(1 + j, me, (*chip, c), src=src_ref).start()

    def finish():
        for j, chip in enumerate(chips):
            copy(1 + j, (*chip, c), me).wait_recv()
            copy(4 + j, (*chip, c), sibling).start()
        copy(0, sibling, me).wait_recv()
        for j, chip in enumerate(chips):
            copy(4 + j, (*chip, 1 - c), me).wait_recv()
        copy(0, me, sibling, src=src_ref).wait_send()
        for j, chip in enumerate(chips):
            copy(1 + j, me, (*chip, c), src=src_ref).wait_send()
            copy(4 + j, (*chip, c), sibling).wait_send()
        mine().wait()

    return start, finish


def _exchange_plan(p_ref, out_ref, sems, n):
    send_sems, recv_sems, local_sems = sems
    base = n * NSEM
    x, y, c = lax.axis_index("x"), lax.axis_index("y"), lax.axis_index("c")
    me = 4 * x + 2 * y + c

    def copy(r, landing):
        px, py, pc = x ^ (r >> 2), y ^ ((r >> 1) & 1), c ^ (r & 1)
        peer = 4 * px + 2 * py + pc
        return pltpu.make_async_remote_copy(
            src_ref=p_ref.at[peer], dst_ref=out_ref.at[peer if landing else me],
            send_sem=send_sems.at[base + r - 1], recv_sem=recv_sems.at[base + r - 1],
            device_id=(px, py, pc), device_id_type=MESH)

    def mine():
        return pltpu.make_async_copy(p_ref.at[me], out_ref.at[me], local_sems.at[n])

    def start():
        mine().start()
        for r in range(1, NDEV):
            copy(r, False).start()

    def finish():
        for r in range(1, NDEV):
            copy(r, True).wait_recv()
        for r in range(1, NDEV):
            copy(r, False).wait_send()
        mine().wait()

    return start, finish


def cast_x(x, w_first, small):
    s = x.shape[0]
    tm = min(s, 512)
    nt = s // tm

    def body(x_ref, wf_ref, sm_ref, o_ref, gw_ref, gs_ref, *sems):
        i = pl.program_id(0)
        plans = [_gather_plan(wf_ref.at[0], gw_ref, sems, 0), _gather_plan(sm_ref, gs_ref, sems, 1)]

        @pl.when(i == 0)
        def _():
            for start, _ in plans:
                start()

        o_ref[...] = x_ref[...].astype(BF16)

        @pl.when(i == nt - 1)
        def _():
            for _, finish in plans:
                finish()

    return pl.pallas_call(
        body, name="cast_x", grid=(nt,),
        in_specs=[pl.BlockSpec((tm, D), lambda i: (i, 0)), ANY, ANY],
        out_specs=[pl.BlockSpec((tm, D), lambda i: (i, 0)), ANY, ANY],
        out_shape=[jax.ShapeDtypeStruct((s, D), BF16),
                   jax.ShapeDtypeStruct((NDEV,) + w_first.shape[1:], w_first.dtype),
                   jax.ShapeDtypeStruct((NDEV,) + small.shape, small.dtype)],
        scratch_shapes=_comm_scratch(2), compiler_params=_cp(("arbitrary",)),
    )(x, w_first, small)


def cast_weights(w_even, w_odd, name):
    _, r, c = w_even.shape

    def body(e_ref, o_ref, out_ref):
        layer = pl.program_id(0)

        @pl.when(layer % 2 == 0)
        def _():
            out_ref[...] = e_ref[...].astype(BF16)

        @pl.when(layer % 2 == 1)
        def _():
            out_ref[...] = o_ref[...].astype(BF16)

    spec = pl.BlockSpec((None, r, c), lambda l: (l // 2, 0, 0))
    return pl.pallas_call(
        body, name=name, grid=(DEPTH,), in_specs=[spec, spec],
        out_specs=pl.BlockSpec((None, r, c), lambda l: (l, 0, 0)),
        out_shape=jax.ShapeDtypeStruct((DEPTH, r, c), BF16), compiler_params=_cp(("parallel",)),
    )(w_even, w_odd)


def _deepnorm(y_ref, wo_ref, xr_ref, gp_ref, bp_ref, g_ref, b_ref):
    xin = xr_ref[...] * gp_ref[...] + bp_ref[...]
    r = ALPHA * xin + _dot(y_ref[...], wo_ref[...].reshape(DMIX, D))
    mu = jnp.mean(r, axis=-1, keepdims=True)
    d = r - mu
    var = jnp.mean(d * d, axis=-1, keepdims=True)
    rstd = lax.rsqrt(var + LN_EPS)
    xh = d * rstd
    return xh, rstd, xh * g_ref[...] + b_ref[...]


def proj_in(wg, layer, wc_in, wc_out, xb=None, below=None):
    fused = below is not None
    s = below[0].shape[0] if fused else xb.shape[0]
    tm = min(s, 512)
    nt = s // tm
    nxt = layer + 1 < DEPTH
    nhead = 7 if fused else 1

    def body(*refs):
        head, refs = refs[:nhead], refs[nhead:]
        if nxt:
            w_ref, wco_ref, wci_ref = refs[:3]
            refs = refs[3:]
        else:
            w_ref, wco_ref = refs[:2]
            refs = refs[2:]
        o_ref = refs[0]
        if fused:
            xh_ref, rs_ref, xb_ref = refs[1:4]
            refs = refs[4:]
        else:
            refs = refs[1:]
        go_ref = refs[0]
        sems = refs[-3:]
        i = pl.program_id(0)
        plans = [_gather_plan(wco_ref.at[layer], go_ref, sems, 0)]
        if nxt:
            plans.append(_gather_plan(wci_ref.at[layer + 1], refs[1], sems, 1))

        @pl.when(i == 0)
        def _():
            for start, _ in plans:
                start()

        if fused:
            xh, rstd, xn = _deepnorm(*head)
            xh_ref[...] = xh
            rs_ref[...] = rstd
            x = xn.astype(BF16)
            xb_ref[...] = x
        else:
            x = head[0][...]
        for k in range(NDEV):
            o_ref[:, k * CW:(k + 1) * CW] = _dot(x, w_ref[k]).astype(BF16)

        @pl.when(i == nt - 1)
        def _():
            for _, finish in plans:
                finish()

    row = pl.BlockSpec((tm, D), lambda i: (i, 0))
    vec = pl.BlockSpec((1, D), lambda i: (0, 0))
    if fused:
        in_specs = [pl.BlockSpec((tm, DMIX), lambda i: (i, 0)),
                    pl.BlockSpec((NDEV, RW, D), lambda i: (0, 0, 0), pipeline_mode=pl.Buffered(1)),
                    row, vec, vec, vec, vec]
        args = list(below)
    else:
        in_specs, args = [row], [xb]
    in_specs += [pl.BlockSpec((NDEV, D, CW), lambda i: (0, 0, 0), pipeline_mode=pl.Buffered(1)), ANY]
    args += [wg, wc_out]
    if nxt:
        in_specs.append(ANY)
        args.append(wc_in)
    out_specs = [pl.BlockSpec((tm, NIN), lambda i: (i, 0))]
    out_shape = [jax.ShapeDtypeStruct((s, NIN), BF16)]
    if fused:
        out_specs += [row, pl.BlockSpec((tm, 1), lambda i: (i, 0)), row]
        out_shape += [jax.ShapeDtypeStruct((s, D), F32), jax.ShapeDtypeStruct((s, 1), F32),
                      jax.ShapeDtypeStruct((s, D), BF16)]
    out_specs.append(ANY)
    out_shape.append(jax.ShapeDtypeStruct((NDEV, RW, D), BF16))
    if nxt:
        out_specs.append(ANY)
        out_shape.append(jax.ShapeDtypeStruct((NDEV, D, CW), BF16))
    return pl.pallas_call(
        body, name=f"proj_in_{layer}", grid=(nt,), in_specs=in_specs, out_specs=out_specs,
        out_shape=out_shape, scratch_shapes=_comm_scratch(2 if nxt else 1),
        compiler_params=_cp(("arbitrary",)),
    )(*args)


def mix_fwd(z, layer, params):
    s = z.shape[0]
    tm = min(s, 512)
    even = layer % 2 == 0
    npar = len(params)
    nout = 3 if even else 2

    def even_tile(zr, j, par, outs, scr):
        pw_ref, ps_ref, sw_ref, sb_ref = par
        yc_ref, pb_ref, cv_ref = outs
        exa, eq, ya0, la, lb = scr
        for c in range(tm // RC):
            r0, r1 = c * RC, (c + 1) * RC
            exa[HP + r0:HP + r1, :] = zr[r0:r1, _cols(0)].astype(F32)
            eq[HS + r0:HS + r1, :] = zr[r0:r1, _cols(4)].astype(F32) * zr[r0:r1, _cols(2)].astype(F32)

        def level(src, dst, shift, c0, start):
            for r0 in range(start, HP + tm, 2 * RC):
                r1 = min(r0 + 2 * RC, HP + tm)
                dst[r0:r1, c0:D] = src[r0:r1, c0:D] + src[r0 - shift:r1 - shift, c0:D]

        level(exa, la, 1, PG, 8)
        level(la, lb, 2, 2 * PG, 16)
        level(lb, la, 4, 3 * PG, 24)
        last = ((exa, 1), (la, 2), (lb, 4), (la, 8))
        for c in range(tm // RC):
            r0, r1 = HP + c * RC, HP + (c + 1) * RC
            pos = (j * tm + c * RC + lax.broadcasted_iota(jnp.int32, (RC, 1), 0) + 1).astype(F32)
            for gi, w in enumerate(POOL_WINDOWS):
                c0, c1 = gi * PG, (gi + 1) * PG
                src, shift = last[gi]
                acc = src[r0:r1, c0:c1] + src[r0 - shift:r1 - shift, c0:c1]
                pooled = acc / jnp.minimum(pos, float(w)) - exa[r0:r1, c0:c1]
                pb_ref[c * RC:(c + 1) * RC, c0:c1] = pooled.astype(BF16)
        for gi in range(len(POOL_WINDOWS)):
            c0, c1 = gi * PG, (gi + 1) * PG
            ya0[:, c0:c1] = _dot(pb_ref[:, c0:c1], pw_ref[gi])
        for c in range(tm // RC):
            r0, r1 = c * RC, (c + 1) * RC
            ga = zr[r0:r1, _cols(1)].astype(F32)
            yc_ref[r0:r1, 0:D] = (ya0[r0:r1, :] * ps_ref[...] * (ga * _sig(ga))).astype(BF16)
            cv = (sw_ref[2:3, :] * eq[HS + r0:HS + r1, :] + sw_ref[1:2, :] * eq[HS + r0 - 1:HS + r1 - 1, :]
                  + sw_ref[0:1, :] * eq[HS + r0 - 2:HS + r1 - 2, :] + sb_ref[...])
            cv_ref[r0:r1, :] = cv.astype(BF16)
            gb = zr[r0:r1, _cols(5)].astype(F32)
            yc_ref[r0:r1, D:2 * D] = (zr[r0:r1, _cols(3)].astype(F32) * cv * (gb * _sig(gb))).astype(BF16)
        exa[0:HP, :] = exa[tm:tm + HP, :]
        eq[0:HS, :] = eq[tm:tm + HS, :]

    def odd_tile(zr, j, par, outs, scr):
        slg_ref, slb_ref, ws_ref, sb_ref, dw_ref, dcb_ref, dng_ref, dnb_ref = par
        yc_ref, cz_ref = outs
        ezg, esh = scr
        vh, _ = _ln_rows(zr[:, _cols(1)].astype(F32))
        vnb = (vh * slg_ref[...] + slb_ref[...]).astype(BF16)
        for n in range(tm // SGU_BLOCK):
            r0, r1 = n * SGU_BLOCK, (n + 1) * SGU_BLOCK
            for hd in range(HEADS):
                c0, c1 = hd * HD, (hd + 1) * HD
                sv = _dot(ws_ref[hd], vnb[r0:r1, c0:c1]) + sb_ref[hd]
                gc = zr[r0:r1, 2 * D + c0:2 * D + c1].astype(F32)
                yc_ref[r0:r1, c0:c1] = (zr[r0:r1, c0:c1].astype(F32) * sv * _silu(gc)).astype(BF16)
        ezg[HC:HC + tm, :] = zr[:, _cols(3)].astype(F32) * _sig(zr[:, _cols(4)].astype(F32))
        _shifted_copies(ezg, esh, tm + HC - SUB)
        cz = jnp.zeros((tm, D), F32) + dcb_ref[...]
        for k in range(KD):
            cz = cz + dw_ref[k:k + 1, :] * _tap(ezg, esh, HC - (KD - 1) + k, 0, tm)
        cz_ref[...] = cz
        zh, _ = _ln_rows(cz)
        zn = zh * dng_ref[...] + dnb_ref[...]
        gd = zr[:, _cols(5)].astype(F32)
        yc_ref[:, D:2 * D] = (_silu(zn) * _silu(gd)).astype(BF16)
        ezg[0:HC, :] = ezg[tm:tm + HC, :]

    def body(z_ref, *refs):
        par, outs, scr = refs[:npar], refs[npar:npar + nout], refs[npar + nout:]
        j = pl.program_id(0)

        @pl.when(j == 0)
        def _():
            if even:
                scr[0][0:HP, :] = jnp.zeros((HP, D), F32)
                scr[1][0:HS, :] = jnp.zeros((HS, D), F32)
            else:
                scr[0][0:HC, :] = jnp.zeros((HC, D), F32)

        (even_tile if even else odd_tile)(z_ref, j, par, outs, scr)

    row = lambda c: pl.BlockSpec((tm, c), lambda j: (j, 0))
    vec = pl.BlockSpec((1, D), lambda j: (0, 0))
    if even:
        par_specs = [pl.BlockSpec((4, PG, PG), lambda j: (0, 0, 0)), vec, pl.BlockSpec((KS, D), lambda j: (0, 0)), vec]
        out_specs = [row(DMIX), row(D), row(D)]
        out_shape = [jax.ShapeDtypeStruct((s, DMIX), BF16), jax.ShapeDtypeStruct((s, D), BF16),
                     jax.ShapeDtypeStruct((s, D), BF16)]
        scratch = [pltpu.VMEM((HP + tm, D), F32), pltpu.VMEM((HS + tm, D), F32), pltpu.VMEM((tm, D), F32),
                   pltpu.VMEM((HP + tm, D), F32), pltpu.VMEM((HP + tm, D), F32)]
    else:
        par_specs = [vec, vec, pl.BlockSpec((HEADS, SGU_BLOCK, SGU_BLOCK), lambda j: (0, 0, 0)),
                     pl.BlockSpec((HEADS, SGU_BLOCK, 1), lambda j: (0, 0, 0)),
                     pl.BlockSpec((KD, D), lambda j: (0, 0)), vec, vec, vec]
        out_specs = [row(DMIX), row(D)]
        out_shape = [jax.ShapeDtypeStruct((s, DMIX), BF16), jax.ShapeDtypeStruct((s, D), F32)]
        scratch = [pltpu.VMEM((HC + tm, D), F32), pltpu.VMEM((SUB - 1, tm + HC - SUB, D), F32)]
    return pl.pallas_call(
        body, name=f"mix_fwd_{layer}", grid=(s // tm,), in_specs=[row(NIN)] + par_specs, out_specs=out_specs,
        out_shape=out_shape, scratch_shapes=scratch, compiler_params=_cp(("arbitrary",)),
    )(z, *params)


def _zero_at_start(*refs):
    @pl.when(pl.program_id(0) == 0)
    def _():
        for ref in refs:
            ref[...] = jnp.zeros_like(ref)


def _ln_bwd_tile(dxo, xh, rstd, g_ref, wo_ref, dr_ref, drb_ref, dyc_ref, dg_ref, db_ref):
    dg_ref[...] += _rowsum(dxo * xh)
    db_ref[...] += _rowsum(dxo)
    dr = _ln_rows_bwd(dxo, xh, rstd, g_ref[...])
    dr_ref[...] = dr
    drb = dr.astype(BF16)
    drb_ref[...] = drb
    dyc_ref[...] = _dot_nt(drb, wo_ref[...].reshape(DMIX, D)).astype(BF16)


def _ln_bwd_outs(s, tm):
    row = pl.BlockSpec((tm, D), lambda i: (i, 0))
    vec = pl.BlockSpec((1, D), lambda i: (0, 0))
    return ([row, row, pl.BlockSpec((tm, DMIX), lambda i: (i, 0)), vec, vec],
            [jax.ShapeDtypeStruct((s, D), F32), jax.ShapeDtypeStruct((s, D), BF16),
             jax.ShapeDtypeStruct((s, DMIX), BF16), jax.ShapeDtypeStruct((1, D), F32),
             jax.ShapeDtypeStruct((1, D), F32)])


def head_loss_bwd(below, target):
    s = target.shape[0]
    tm = min(s, 512)

    def body(*refs):
        t_ref = refs[7]
        loss_ref = refs[-1]
        _zero_at_start(*refs[11:14])
        xh, rstd, xn = _deepnorm(*refs[:7])
        err = xn - t_ref[...]
        loss_ref[...] += 0.5 * jnp.sum(jnp.mean(err * err, axis=-1, keepdims=True), axis=0, keepdims=True)
        _ln_bwd_tile(err * (1.0 / D), xh, rstd, refs[5], refs[1], *refs[8:13])

    row = pl.BlockSpec((tm, D), lambda i: (i, 0))
    vec = pl.BlockSpec((1, D), lambda i: (0, 0))
    out_specs, out_shape = _ln_bwd_outs(s, tm)
    return pl.pallas_call(
        body, name="head_loss_bwd", grid=(s // tm,),
        in_specs=[pl.BlockSpec((tm, DMIX), lambda i: (i, 0)), pl.BlockSpec((NDEV, RW, D), lambda i: (0, 0, 0)),
                  row, vec, vec, vec, vec, row],
        out_specs=out_specs + [pl.BlockSpec((1, 1), lambda i: (0, 0))],
        out_shape=out_shape + [jax.ShapeDtypeStruct((1, 1), F32)],
        compiler_params=_cp(("arbitrary",)),
    )(*below, target)


def ln_bwd_dycat(xhat, rstd, g, wog, layer, upstream):
    s = xhat.shape[0]
    tm = min(s, 512)

    def body(dz_ref, wi_ref, dru_ref, xh_ref, rs_ref, g_ref, w_ref, *outs):
        _zero_at_start(*outs[3:5])
        dxo = ALPHA * dru_ref[...]
        for k in range(NDEV):
            dxo += _dot_nt(dz_ref[:, k * CW:(k + 1) * CW], wi_ref[k])
        _ln_bwd_tile(dxo, xh_ref[...], rs_ref[...], g_ref, w_ref, *outs)

    row = pl.BlockSpec((tm, D), lambda i: (i, 0))
    vec = pl.BlockSpec((1, D), lambda i: (0, 0))
    out_specs, out_shape = _ln_bwd_outs(s, tm)
    return pl.pallas_call(
        body, name=f"ln_bwd_dycat_{layer}", grid=(s // tm,),
        in_specs=[pl.BlockSpec((tm, NIN), lambda i: (i, 0)),
                  pl.BlockSpec((NDEV, D, CW), lambda i: (0, 0, 0), pipeline_mode=pl.Buffered(1)), row,
                  row, pl.BlockSpec((tm, 1), lambda i: (i, 0)), vec,
                  pl.BlockSpec((NDEV, RW, D), lambda i: (0, 0, 0), pipeline_mode=pl.Buffered(1))],
        out_specs=out_specs, out_shape=out_shape, compiler_params=_cp(("arbitrary",)),
    )(*upstream, xhat, rstd, g, wog)


def dx_in(dzb, wg, dr, parts):
    s = dzb.shape[0]
    tm = min(s, 256)
    nt = s // tm

    def body(dz_ref, w_ref, dr_ref, p_ref, o_ref, land_ref, *sems):
        start, finish = _exchange_plan(p_ref, land_ref, sems, 0)
        pl.when(pl.program_id(0) == 0)(start)
        acc = ALPHA * dr_ref[...]
        for k in range(NDEV):
            acc += _dot_nt(dz_ref[:, k * CW:(k + 1) * CW], w_ref[k])
        o_ref[...] = acc
        pl.when(pl.program_id(0) == nt - 1)(finish)

    row = pl.BlockSpec((tm, D), lambda i: (i, 0))
    return pl.pallas_call(
        body, name="dx_in_0", grid=(nt,),
        in_specs=[pl.BlockSpec((tm, NIN), lambda i: (i, 0)),
                  pl.BlockSpec((NDEV, D, CW), lambda i: (0, 0, 0), pipeline_mode=pl.Buffered(1)), row, ANY],
        out_specs=[row, ANY],
        out_shape=[jax.ShapeDtypeStruct((s, D), F32), jax.ShapeDtypeStruct(parts.shape, parts.dtype)],
        scratch_shapes=_comm_scratch(1), compiler_params=_cp(("arbitrary",)),
    )(dzb, wg, dr, parts)


def dw_in(xb, dzb, layer, parts=(), gathers=()):
    s = xb.shape[0]
    tm = min(s, 4096)
    nt = s // tm
    ne = len(parts)
    n = ne + len(gathers)

    def body(*refs):
        x_ref, dz_ref = refs[:2]
        p_refs = refs[2:2 + n]
        o_ref = refs[2 + n]
        land_refs = refs[3 + n:3 + 2 * n]
        acc = refs[3 + 2 * n]
        plans = [(_exchange_plan if j < ne else _gather_plan)(p, land, refs[-3:], j)
                 for j, (p, land) in enumerate(zip(p_refs, land_refs))]
        k, t = pl.program_id(0), pl.program_id(1)

        @pl.when((k == 0) & (t == 0))
        def _():
            for start, _ in plans:
                start()

        @pl.when(t == 0)
        def _():
            acc[...] = jnp.zeros_like(acc)

        acc[...] += _dot_tn(x_ref[...], dz_ref[...])

        @pl.when(t == nt - 1)
        def _():
            o_ref[...] = acc[...].astype(BF16)

        @pl.when((k == NDEV - 1) & (t == nt - 1))
        def _():
            for _, finish in plans:
                finish()

    return pl.pallas_call(
        body, name=f"dw_in_{layer}", grid=(NDEV, nt),
        in_specs=[pl.BlockSpec((tm, D), lambda k, t: (t, 0)), pl.BlockSpec((tm, CW), lambda k, t: (t, k))] + [ANY] * n,
        out_specs=[pl.BlockSpec((None, D, CW), lambda k, t: (k, 0, 0))] + [ANY] * n,
        out_shape=[jax.ShapeDtypeStruct((NDEV, D, CW), BF16)] + [jax.ShapeDtypeStruct(p.shape, p.dtype) for p in parts]
        + [jax.ShapeDtypeStruct((NDEV,) + g.shape, g.dtype) for g in gathers],
        scratch_shapes=[pltpu.VMEM((D, CW), F32)] + (_comm_scratch(n) if n else []),
        compiler_params=_cp(("arbitrary", "arbitrary")),
    )(xb, dzb, *parts, *gathers)


def dw_out(ycat, drb, layer):
    s = ycat.shape[0]
    tm = min(s, 1024)
    nt = s // tm

    def body(y_ref, dr_ref, o_ref, acc):
        t = pl.program_id(0)

        @pl.when(t == 0)
        def _():
            acc[...] = jnp.zeros_like(acc)

        acc[...] += _dot_tn(y_ref[...], dr_ref[...])

        @pl.when(t == nt - 1)
        def _():
            o_ref[...] = acc[...].reshape(NDEV, RW, D).astype(BF16)

    return pl.pallas_call(
        body, name=f"dw_out_{layer}", grid=(nt,),
        in_specs=[pl.BlockSpec((tm, DMIX), lambda t: (t, 0)), pl.BlockSpec((tm, D), lambda t: (t, 0))],
        out_specs=pl.BlockSpec((NDEV, RW, D), lambda t: (0, 0, 0)),
        out_shape=jax.ShapeDtypeStruct((NDEV, RW, D), BF16),
        scratch_shapes=[pltpu.VMEM((DMIX, D), F32)],
        compiler_params=_cp(("arbitrary",)),
    )(ycat, drb)


def _cols(j):
    return slice(j * D, (j + 1) * D)


def mix_even_bwd(z, dycat, pooled, cv, pwb, pscale, sw, layer, parts=None):
    s = z.shape[0]
    tm = min(s, 256)
    nt = s // tm
    carry = parts is not None

    def body(*refs):
        z_ref, dy_ref, pb_ref, cv_ref, pw_ref, ps_ref, sw_ref = refs[:7]
        if carry:
            p_ref = refs[7]
            dz_ref, dpw_ref, dps_ref, dsw_ref, dsb_ref, land_ref = refs[8:14]
            edp, edc, ya0, t1s, dpl, la, lb = refs[14:21]
            start, finish = _exchange_plan(p_ref, land_ref, refs[-3:], 0)
        else:
            dz_ref, dpw_ref, dps_ref, dsw_ref, dsb_ref = refs[7:12]
            edp, edc, ya0, t1s, dpl, la, lb = refs[12:19]
        i = pl.program_id(0)
        if carry:
            pl.when(i == 0)(start)

        @pl.when(i == 0)
        def _():
            edp[tm:tm + HP, :] = jnp.zeros((HP, D), F32)
            edc[tm:tm + HS, :] = jnp.zeros((HS, D), F32)
            dpw_ref[...] = jnp.zeros_like(dpw_ref)
            dps_ref[...] = jnp.zeros_like(dps_ref)
            dsw_ref[...] = jnp.zeros_like(dsw_ref)
            dsb_ref[...] = jnp.zeros_like(dsb_ref)

        blocks = [(r * RC, (r + 1) * RC, c * CB, (c + 1) * CB) for r in range(tm // RC) for c in range(D // CB)]
        for gi in range(len(POOL_WINDOWS)):
            c0, c1 = gi * PG, (gi + 1) * PG
            ya0[:, c0:c1] = _dot(pb_ref[:, c0:c1], pw_ref[gi])
        for r0, r1, c0, c1 in blocks:
            ga = z_ref[r0:r1, D + c0:D + c1].astype(F32)
            sg = _sig(ga)
            sil = ga * sg
            dya = dy_ref[r0:r1, c0:c1].astype(F32)
            y0 = ya0[r0:r1, c0:c1]
            ps = ps_ref[:, c0:c1]
            dps_ref[:, c0:c1] += _rowsum(dya * y0 * sil)
            dz_ref[r0:r1, D + c0:D + c1] = (dya * y0 * ps * (sg * (1.0 + ga * (1.0 - sg)))).astype(BF16)
            t1s[r0:r1, c0:c1] = (dya * ps * sil).astype(BF16)
        for gi in range(len(POOL_WINDOWS)):
            c0, c1 = gi * PG, (gi + 1) * PG
            dpl[:, c0:c1] = _dot_nt(t1s[:, c0:c1], pw_ref[gi])
            dpw_ref[gi] += _dot_tn(pb_ref[:, c0:c1], t1s[:, c0:c1])
        for r in range(tm // RC):
            r0, r1 = r * RC, (r + 1) * RC
            pos = ((nt - 1 - i) * tm + r0 + lax.broadcasted_iota(jnp.int32, (RC, 1), 0) + 1).astype(F32)
            for gi, w in enumerate(POOL_WINDOWS):
                c0, c1 = gi * PG, (gi + 1) * PG
                edp[r0:r1, c0:c1] = dpl[r0:r1, c0:c1] / jnp.minimum(pos, float(w))

        def level(src, dst, shift, c0, end):
            for r0 in range(0, end, 2 * RC):
                r1 = min(r0 + 2 * RC, end)
                dst[r0:r1, c0:D] = src[r0:r1, c0:D] + src[r0 + shift:r1 + shift, c0:D]

        level(edp, la, 1, PG, tm + 24)
        level(la, lb, 2, 2 * PG, tm + 16)
        level(lb, la, 4, 3 * PG, tm + 8)
        last = ((edp, 1), (la, 2), (lb, 4), (la, 8))
        for r in range(tm // RC):
            r0, r1 = r * RC, (r + 1) * RC
            for gi in range(len(POOL_WINDOWS)):
                c0, c1 = gi * PG, (gi + 1) * PG
                src, shift = last[gi]
                acc = src[r0:r1, c0:c1] + src[r0 + shift:r1 + shift, c0:c1] - dpl[r0:r1, c0:c1]
                dz_ref[r0:r1, c0:c1] = acc.astype(BF16)

        for r0, r1, c0, c1 in blocks:
            gb = z_ref[r0:r1, 5 * D + c0:5 * D + c1].astype(F32)
            sgb = _sig(gb)
            silb = gb * sgb
            dyb = dy_ref[r0:r1, D + c0:D + c1].astype(F32)
            cvv = cv_ref[r0:r1, c0:c1].astype(F32)
            bg = z_ref[r0:r1, 3 * D + c0:3 * D + c1].astype(F32)
            dz_ref[r0:r1, 3 * D + c0:3 * D + c1] = (dyb * cvv * silb).astype(BF16)
            dz_ref[r0:r1, 5 * D + c0:5 * D + c1] = (dyb * bg * cvv * (sgb * (1.0 + gb * (1.0 - sgb)))).astype(BF16)
            dcv = dyb * bg * silb
            dsb_ref[:, c0:c1] += _rowsum(dcv)
            edc[r0:r1, c0:c1] = dcv
        for r0, r1, c0, c1 in blocks:
            h = z_ref[r0:r1, 2 * D + c0:2 * D + c1].astype(F32)
            cg = z_ref[r0:r1, 4 * D + c0:4 * D + c1].astype(F32)
            q = cg * h
            d0 = edc[r0:r1, c0:c1]
            d1 = edc[r0 + 1:r1 + 1, c0:c1]
            d2 = edc[r0 + 2:r1 + 2, c0:c1]
            dq = sw_ref[2:3, c0:c1] * d0 + sw_ref[1:2, c0:c1] * d1 + sw_ref[0:1, c0:c1] * d2
            dsw_ref[2:3, c0:c1] += _rowsum(q * d0)
            dsw_ref[1:2, c0:c1] += _rowsum(q * d1)
            dsw_ref[0:1, c0:c1] += _rowsum(q * d2)
            dz_ref[r0:r1, 4 * D + c0:4 * D + c1] = (dq * h).astype(BF16)
            dz_ref[r0:r1, 2 * D + c0:2 * D + c1] = (dq * cg).astype(BF16)
        edp[tm:tm + HP, :] = edp[0:HP, :]
        edc[tm:tm + HS, :] = edc[0:HS, :]
        if carry:
            pl.when(i == nt - 1)(finish)

    rev = lambda i: (nt - 1 - i, 0)
    vec = pl.BlockSpec((1, D), lambda i: (0, 0))
    in_specs = [pl.BlockSpec((tm, NIN), rev), pl.BlockSpec((tm, DMIX), rev),
                pl.BlockSpec((tm, D), rev), pl.BlockSpec((tm, D), rev),
                pl.BlockSpec((4, PG, PG), lambda i: (0, 0, 0)), vec,
                pl.BlockSpec((KS, D), lambda i: (0, 0))]
    out_specs = [pl.BlockSpec((tm, NIN), rev), pl.BlockSpec((4, PG, PG), lambda i: (0, 0, 0)), vec,
                 pl.BlockSpec((KS, D), lambda i: (0, 0)), vec]
    out_shape = [jax.ShapeDtypeStruct((s, NIN), BF16), jax.ShapeDtypeStruct((4, PG, PG), F32),
                 jax.ShapeDtypeStruct((1, D), F32), jax.ShapeDtypeStruct((KS, D), F32),
                 jax.ShapeDtypeStruct((1, D), F32)]
    args = [z, dycat, pooled, cv, pwb, pscale, sw]
    scratch = [pltpu.VMEM((tm + HP, D), F32), pltpu.VMEM((tm + HS, D), F32), pltpu.VMEM((tm, D), F32),
               pltpu.VMEM((tm, D), BF16), pltpu.VMEM((tm, D), F32), pltpu.VMEM((tm + HP, D), F32),
               pltpu.VMEM((tm + HP, D), F32)]
    if carry:
        in_specs.append(ANY)
        args.append(parts)
        out_specs.append(ANY)
        out_shape.append(jax.ShapeDtypeStruct(parts.shape, parts.dtype))
        scratch += _comm_scratch(1)
    return pl.pallas_call(
        body, name=f"mix_even_bwd_{layer}", grid=(nt,), in_specs=in_specs, out_specs=out_specs,
        out_shape=out_shape, scratch_shapes=scratch, compiler_params=_cp(("arbitrary",)),
    )(*args)


def _ln_rows(v):
    mu = jnp.mean(v, axis=-1, keepdims=True)
    d = v - mu
    var = jnp.mean(d * d, axis=-1, keepdims=True)
    rstd = lax.rsqrt(var + LN_EPS)
    return d * rstd, rstd


def _ln_rows_bwd(dn, xh, rstd, g):
    dxh = dn * g
    m1 = jnp.mean(dxh, axis=-1, keepdims=True)
    m2 = jnp.mean(dxh * xh, axis=-1, keepdims=True)
    return rstd * (dxh - m1 - xh * m2)


def _shifted_copies(ext, shifted, rows):
    for b in range(1, SUB):
        shifted[b - 1] = ext[b:b + rows, :]


def _tap(ext, shifted, off, r0, r1, cols=slice(None)):
    a, b = off // SUB, off % SUB
    if b == 0:
        return ext[SUB * a + r0:SUB * a + r1, cols]
    return shifted[b - 1, SUB * a + r0:SUB * a + r1, cols]


def mix_odd_bwd(z, dycat, cz, slg, slb, wsb, wstb, sbcol, dw, dng, dnb, layer, parts):
    s = z.shape[0]
    tm = min(s, 256)
    nt = s // tm

    def body(z_ref, dy_ref, cz_ref, slg_ref, slb_ref, ws_ref, wst_ref, sb_ref, dw_ref, dng_ref, dnb_ref, p_ref,
             dz_ref, dslg_ref, dslb_ref, dws_ref, dsb_ref, ddw_ref, ddcb_ref, ddng_ref, ddnb_ref, land_ref,
             dvn, edz, esh, ddw8, *sems):
        i = pl.program_id(0)
        start, finish = _exchange_plan(p_ref, land_ref, sems, 0)
        pl.when(i == 0)(start)

        @pl.when(i == 0)
        def _():
            edz[tm:tm + HC, :] = jnp.zeros((HC, D), F32)
            for ref in (dslg_ref, dslb_ref, dws_ref, dsb_ref, ddw8, ddcb_ref, ddng_ref, ddnb_ref):
                ref[...] = jnp.zeros_like(ref)

        vh, vrs = _ln_rows(z_ref[:, _cols(1)].astype(F32))
        vnb = (vh * slg_ref[...] + slb_ref[...]).astype(BF16)
        for n in range(tm // SGU_BLOCK):
            r0, r1 = n * SGU_BLOCK, (n + 1) * SGU_BLOCK
            for hd in range(HEADS):
                c0, c1 = hd * HD, (hd + 1) * HD
                vblk = vnb[r0:r1, c0:c1]
                sv = _dot(ws_ref[hd], vblk) + sb_ref[hd]
                gc = z_ref[r0:r1, 2 * D + c0:2 * D + c1].astype(F32)
                sg = _sig(gc)
                sil = gc * sg
                u = z_ref[r0:r1, c0:c1].astype(F32)
                dyc = dy_ref[r0:r1, c0:c1].astype(F32)
                dz_ref[r0:r1, c0:c1] = (dyc * sv * sil).astype(BF16)
                dz_ref[r0:r1, 2 * D + c0:2 * D + c1] = (dyc * u * sv * _dsilu(sg, sil)).astype(BF16)
                dsv = dyc * u * sil
                dsb_ref[hd] += jnp.sum(dsv, axis=-1, keepdims=True)
                dsvb = dsv.astype(BF16)
                dws_ref[hd] += _dot_nt(dsvb, vblk)
                dvn[r0:r1, c0:c1] = _dot(wst_ref[hd], dsvb)
        dv = dvn[...]
        dslg_ref[...] += _rowsum(dv * vh)
        dslb_ref[...] += _rowsum(dv)
        dz_ref[:, _cols(1)] = _ln_rows_bwd(dv, vh, vrs, slg_ref[...]).astype(BF16)

        zh, zrs = _ln_rows(cz_ref[...])
        zn = zh * dng_ref[...] + dnb_ref[...]
        sgn = _sig(zn)
        gd = z_ref[:, _cols(5)].astype(F32)
        sgd = _sig(gd)
        dyd = dy_ref[:, D:2 * D].astype(F32)
        silz, sild = zn * sgn, gd * sgd
        dz_ref[:, _cols(5)] = (dyd * silz * _dsilu(sgd, sild)).astype(BF16)
        dzn = dyd * sild * _dsilu(sgn, silz)
        ddng_ref[...] += _rowsum(dzn * zh)
        ddnb_ref[...] += _rowsum(dzn)
        dcz = _ln_rows_bwd(dzn, zh, zrs, dng_ref[...])
        ddcb_ref[...] += _rowsum(dcz)
        edz[0:tm, :] = dcz
        _shifted_copies(edz, esh, tm + HC - SUB)
        for r0 in range(0, tm, RC):
            for c0 in range(0, D, CB):
                r1, c1 = r0 + RC, c0 + CB
                a = z_ref[r0:r1, 3 * D + c0:3 * D + c1].astype(F32)
                sgb = _sig(z_ref[r0:r1, 4 * D + c0:4 * D + c1].astype(F32))
                zg = a * sgb
                dzg = jnp.zeros((RC, CB), F32)
                for k in range(KD):
                    sh = _tap(edz, esh, KD - 1 - k, r0, r1, slice(c0, c1))
                    dzg = dzg + dw_ref[k:k + 1, c0:c1] * sh
                    prod = zg * sh
                    ddw8[k, :, c0:c1] += prod[0:SUB] + prod[SUB:2 * SUB]
                dz_ref[r0:r1, 3 * D + c0:3 * D + c1] = (dzg * sgb).astype(BF16)
                dz_ref[r0:r1, 4 * D + c0:4 * D + c1] = (dzg * a * sgb * (1.0 - sgb)).astype(BF16)
        edz[tm:tm + HC, :] = edz[0:HC, :]

        @pl.when(i == nt - 1)
        def _():
            ddw_ref[...] = jnp.sum(ddw8[...], axis=1)

        pl.when(i == nt - 1)(finish)

    rev = lambda i: (nt - 1 - i, 0)
    vec = pl.BlockSpec((1, D), lambda i: (0, 0))
    wspec = pl.BlockSpec((HEADS, SGU_BLOCK, SGU_BLOCK), lambda i: (0, 0, 0))
    bspec = pl.BlockSpec((HEADS, SGU_BLOCK, 1), lambda i: (0, 0, 0))
    kspec = pl.BlockSpec((KD, D), lambda i: (0, 0))
    return pl.pallas_call(
        body, name=f"mix_odd_bwd_{layer}", grid=(nt,),
        in_specs=[pl.BlockSpec((tm, NIN), rev), pl.BlockSpec((tm, DMIX), rev), pl.BlockSpec((tm, D), rev),
                  vec, vec, wspec, wspec, bspec, kspec, vec, vec, ANY],
        out_specs=[pl.BlockSpec((tm, NIN), rev), vec, vec, wspec, bspec, kspec, vec, vec, vec, ANY],
        out_shape=[jax.ShapeDtypeStruct((s, NIN), BF16), jax.ShapeDtypeStruct((1, D), F32),
                   jax.ShapeDtypeStruct((1, D), F32),
                   jax.ShapeDtypeStruct((HEADS, SGU_BLOCK, SGU_BLOCK), F32),
                   jax.ShapeDtypeStruct((HEADS, SGU_BLOCK, 1), F32), jax.ShapeDtypeStruct((KD, D), F32),
                   jax.ShapeDtypeStruct((1, D), F32), jax.ShapeDtypeStruct((1, D), F32),
                   jax.ShapeDtypeStruct((1, D), F32), jax.ShapeDtypeStruct(parts.shape, parts.dtype)],
        scratch_shapes=[pltpu.VMEM((tm, D), F32), pltpu.VMEM((tm + HC, D), F32),
                        pltpu.VMEM((SUB - 1, tm + HC - SUB, D), F32), pltpu.VMEM((KD, SUB, D), F32)]
        + _comm_scratch(1),
        compiler_params=_cp(("arbitrary",)),
    )(z, dycat, cz, slg, slb, wsb, wstb, sbcol, dw, dng, dnb, parts)


def _adamw_math(w, g, m, v):
    m = ADAM_B1 * m + (1.0 - ADAM_B1) * g
    v = ADAM_B2 * v + (1.0 - ADAM_B2) * (g * g)
    m_hat = m / (1.0 - ADAM_B1 ** ADAM_STEP)
    v_hat = v / (1.0 - ADAM_B2 ** ADAM_STEP)
    delta = -ADAM_LR * (m_hat / (jnp.sqrt(v_hat) + ADAM_EPS) + ADAM_WD * w)
    return delta, m, v


def adamw_big(w, m, v, parts0, parts1, name):
    _, r, c = w.shape
    tr = min(r, 256)
    nr = r // tr

    def body(w_ref, m_ref, v_ref, p0_ref, p1_ref, g_ref, d_ref, nm_ref, nv_ref):
        i = pl.program_id(0)

        def total(p_ref):
            acc = p_ref[0].astype(F32)
            for j in range(1, NDEV):
                acc = acc + p_ref[j].astype(F32)
            return acc

        @pl.when(i == 0)
        def _():
            g_ref[...] = total(p0_ref)

        @pl.when(i == 1)
        def _():
            g_ref[...] = total(p1_ref)

        delta, nm, nv = _adamw_math(w_ref[...], g_ref[...], m_ref[...], v_ref[...])
        d_ref[...] = delta
        nm_ref[...] = nm
        nv_ref[...] = nv

    wspec = pl.BlockSpec((None, tr, c), lambda i, j: (i, j, 0))
    p0 = pl.BlockSpec((NDEV, tr, c), lambda i, j: (0, jnp.where(i == 0, j, nr - 1), 0))
    p1 = pl.BlockSpec((NDEV, tr, c), lambda i, j: (0, jnp.where(i == 1, j, 0), 0))
    shp = jax.ShapeDtypeStruct(w.shape, F32)
    return pl.pallas_call(
        body, name=name, grid=(2, nr), in_specs=[wspec, wspec, wspec, p0, p1],
        out_specs=[wspec] * 4, out_shape=[shp] * 4,
        compiler_params=_cp(("arbitrary", "arbitrary")),
    )(w, m, v, parts0, parts1)


def sum_parts(parts, name):
    _, r, c = parts.shape

    def body(p_ref, o_ref):
        acc = p_ref[0]
        for j in range(1, NDEV):
            acc = acc + p_ref[j]
        o_ref[...] = acc

    return pl.pallas_call(body, name=name, out_shape=jax.ShapeDtypeStruct((r, c), F32),
                          compiler_params=_cp())(parts)


def adamw_small(w, g, m, v, name):
    def body(w_ref, g_ref, m_ref, v_ref, d_ref, nm_ref, nv_ref):
        delta, nm, nv = _adamw_math(w_ref[...], g_ref[...], m_ref[...], v_ref[...])
        d_ref[...] = delta
        nm_ref[...] = nm
        nv_ref[...] = nv

    shp = jax.ShapeDtypeStruct(w.shape, F32)
    return pl.pallas_call(body, name=name, out_shape=[shp] * 3, compiler_params=_cp())(w, g, m, v)


def _size(shape):
    n = 1
    for d in shape:
        n *= d
    return n


def _pack(arrays, rows):
    flat = jnp.concatenate([a.reshape(-1) for a in arrays])
    return jnp.pad(flat, (0, rows * 128 - flat.shape[0])).reshape(rows, 128)


def _unpack(flat, shapes):
    out, o = [], 0
    for shp in shapes:
        out.append(flat[o:o + _size(shp)].reshape(shp))
        o += _size(shp)
    return out


def _rows_for(shapes):
    return -(-sum(_size(shp) for shp in shapes) // 1024) * 8


SHARDED_SMALL = (("pool_w", (2, 4, 256, 256), 2), ("sconv_w", (2, KS, D), 2), ("sgu_ln_g", (2, D), 1),
                 ("sgu_ln_b", (2, D), 1), ("dconv_w", (2, KD, D), 2), ("dconv_b", (2, D), 1),
                 ("dnorm_g", (2, D), 1), ("dnorm_b", (2, D), 1))
REPLICATED_SMALL = (("ln_g", (DEPTH, D)), ("ln_b", (DEPTH, D)), ("pool_scale", (2, D)), ("sconv_b", (2, D)),
                    ("sgu_w", (2, HEADS, SGU_BLOCK, SGU_BLOCK)), ("sgu_b", (2, HEADS, SGU_BLOCK)))


def _shard_shape(shape, axis):
    return tuple(d // NDEV if a == axis else d for a, d in enumerate(shape))


def _merge_gathered(g, shape, axis):
    return jnp.moveaxis(g, 0, axis).reshape(shape)


def kernel(x, ln_g, ln_b, w_in_even, w_out_even, pool_w, pool_scale, sconv_w, sconv_b, w_in_odd, w_out_odd, sgu_ln_g, sgu_ln_b, sgu_w, sgu_b, dconv_w, dconv_b, dnorm_g, dnorm_b, loss_target, m_ln_g, m_ln_b, m_w_in_even, m_w_out_even, m_pool_w, m_pool_scale, m_sconv_w, m_sconv_b, m_w_in_odd, m_w_out_odd, m_sgu_ln_g, m_sgu_ln_b, m_sgu_w, m_sgu_b, m_dconv_w, m_dconv_b, m_dnorm_g, m_dnorm_b, v_ln_g, v_ln_b, v_w_in_even, v_w_out_even, v_pool_w, v_pool_scale, v_sconv_w, v_sconv_b, v_w_in_odd, v_w_out_odd, v_sgu_ln_g, v_sgu_ln_b, v_sgu_w, v_sgu_b, v_dconv_w, v_dconv_b, v_dnorm_g, v_dnorm_b):
    given = dict(locals())
    me = 4 * lax.axis_index("x") + 2 * lax.axis_index("y") + lax.axis_index("c")
    xs = x[0]
    target = loss_target[0]

    wc_in = cast_weights(w_in_even, w_in_odd, "cast_w_in")
    wc_out = cast_weights(w_out_even, w_out_odd, "cast_w_out")
    shard_shapes = [_shard_shape(shp, ax) for _, shp, ax in SHARDED_SMALL]
    srows = _rows_for(shard_shapes)
    xb, wg_first, gathered = cast_x(xs, wc_in, _pack([given[n] for n, _, _ in SHARDED_SMALL], srows))
    wg_in = [wg_first] + [None] * (DEPTH - 1)
    wg_out = [None] * DEPTH
    gathered = gathered.reshape(NDEV, -1)
    full, o = {}, 0
    for (n, shp, ax), sshp in zip(SHARDED_SMALL, shard_shapes):
        full[n] = _merge_gathered(gathered[:, o:o + _size(sshp)].reshape((NDEV,) + sshp), shp, ax)
        o += _size(sshp)

    mask = (jnp.arange(SGU_BLOCK)[None, :] // 64) <= (jnp.arange(SGU_BLOCK)[:, None] // 64)
    ws = jnp.where(mask[None, None], sgu_w, 0.0)
    wsb = ws.astype(BF16)
    wstb = jnp.swapaxes(ws, -1, -2).astype(BF16)
    pwb = full["pool_w"].astype(BF16)

    ones = jnp.ones((1, D), F32)
    zeros = jnp.zeros((1, D), F32)

    xres, gp, bp = xs, ones, zeros
    saved, below = [], None
    for layer in range(DEPTH):
        i = layer // 2
        if layer % 2 == 0:
            params = (pwb[i], pool_scale[i][None], full["sconv_w"][i], sconv_b[i][None])
        else:
            params = (full["sgu_ln_g"][i][None], full["sgu_ln_b"][i][None], wsb[i], sgu_b[i][:, :, None],
                      full["dconv_w"][i], full["dconv_b"][i][None], full["dnorm_g"][i][None],
                      full["dnorm_b"][i][None])
        outs = list(proj_in(wg_in[layer], layer, wc_in, wc_out, xb=xb if below is None else None, below=below))
        if layer + 1 < DEPTH:
            wg_in[layer + 1] = outs.pop()
        wg_out[layer] = outs.pop()
        z = outs[0]
        if below is not None:
            xhat, rstd, xb = outs[1:4]
            saved[-1] += [xhat, rstd]
            xres, gp, bp = xhat, below[5], below[6]
        outs = mix_fwd(z, layer, params)
        ycat, extra = outs[0], tuple(outs[1:])
        saved.append([xb, z, ycat, extra])
        below = (ycat, wg_out[layer], xres, gp, bp, ln_g[layer][None], ln_b[layer][None])
    *top, loss_local = head_loss_bwd(below, target)

    gsmall = {n: [None] * shp[0] for n, shp in REPLICATED_SMALL}
    gsmall.update({n: [None] * shp[0] for n, shp, _ in SHARDED_SMALL})
    parts_in, parts_out, parts_pw = [None] * DEPTH, [None] * DEPTH, [None] * 2
    small_names = [n for n, _ in REPLICATED_SMALL] + [n for n, _, _ in SHARDED_SMALL]
    gathered_names = [n for n in small_names if n != "pool_w"]
    gathered_shapes = [shp for n, shp in REPLICATED_SMALL] + [shp for n, shp, _ in SHARDED_SMALL if n != "pool_w"]
    grows = _rows_for(gathered_shapes + [(1,)])
    pending, above = None, None
    for layer in reversed(range(DEPTH)):
        i = layer // 2
        xb, z, ycat, extra = saved[layer][:4]
        if above is None:
            dr, drb, dycat, dg, db = top
        else:
            dr, drb, dycat, dg, db = ln_bwd_dycat(*saved[layer][4:], ln_g[layer][None], wg_out[layer], layer, above)
        gsmall["ln_g"][layer], gsmall["ln_b"][layer] = dg[0], db[0]
        dwo = dw_out(ycat, drb, layer)
        own_out = pending is None
        riding = dwo if own_out else pending
        if layer % 2 == 0:
            dzb, dpw, dps, dsw, dsb, landed = mix_even_bwd(
                z, dycat, extra[0], extra[1], pwb[i], pool_scale[i][None], full["sconv_w"][i], layer, parts=riding)
            gsmall["pool_scale"][i] = dps[0]
            gsmall["sconv_w"][i], gsmall["sconv_b"][i] = dsw, dsb[0]
        else:
            dzb, dslg, dslb, dws, dsbc, ddw, ddcb, ddng, ddnb, landed = mix_odd_bwd(
                z, dycat, extra[0], full["sgu_ln_g"][i][None], full["sgu_ln_b"][i][None], wsb[i], wstb[i],
                sgu_b[i][:, :, None], full["dconv_w"][i], full["dnorm_g"][i][None], full["dnorm_b"][i][None],
                layer, riding)
            gsmall["sgu_ln_g"][i], gsmall["sgu_ln_b"][i] = dslg[0], dslb[0]
            gsmall["sgu_w"][i], gsmall["sgu_b"][i] = jnp.where(mask[None], dws, 0.0), dsbc[:, :, 0]
            gsmall["dconv_w"][i], gsmall["dconv_b"][i] = ddw, ddcb[0]
            gsmall["dnorm_g"][i], gsmall["dnorm_b"][i] = ddng[0], ddnb[0]
        riders = [] if own_out else [dwo]
        if layer % 2 == 0:
            riders.append(dpw.reshape(4, NDEV, PG // NDEV, PG).transpose(1, 0, 2, 3))
        small_partial = []
        if layer == 0:
            small_partial.append(_pack([jnp.stack(gsmall[n]) for n in gathered_names] + [loss_local], grows))
        outs = list(dw_in(xb, dzb, layer, riders, small_partial))
        if layer == 0:
            small_gathered = outs.pop()
        if layer % 2 == 0:
            parts_pw[i] = outs.pop()
        if own_out:
            parts_out[layer] = landed
        else:
            parts_in[layer + 1] = landed
            parts_out[layer] = outs[1]
        pending = outs[0]
        above = (dzb, wg_in[layer], dr)
    dxn, parts_in[0] = dx_in(dzb, wg_in[0], dr, pending)
    grad_x = dxn[None]

    big = {}
    big["w_in_even"] = adamw_big(w_in_even, m_w_in_even, v_w_in_even, parts_in[0], parts_in[2], "adamw_w_in_even")
    big["w_in_odd"] = adamw_big(w_in_odd, m_w_in_odd, v_w_in_odd, parts_in[1], parts_in[3], "adamw_w_in_odd")
    big["w_out_even"] = adamw_big(w_out_even, m_w_out_even, v_w_out_even, parts_out[0], parts_out[2], "adamw_w_out_even")
    big["w_out_odd"] = adamw_big(w_out_odd, m_w_out_odd, v_w_out_odd, parts_out[1], parts_out[3], "adamw_w_out_odd")

    gsum = sum_parts(small_gathered, "sum_small_grads")
    unpacked = _unpack(gsum.reshape(-1), gathered_shapes + [()])
    loss = unpacked.pop()
    gfull = dict(zip(gathered_names, unpacked))
    own_shapes = [shp for _, shp in REPLICATED_SMALL] + shard_shapes
    gown = {n: gfull[n] for n, _ in REPLICATED_SMALL}
    for (n, shp, ax), sshp in zip(SHARDED_SMALL, shard_shapes):
        if n == "pool_w":
            gown[n] = jnp.stack([sum_parts(p.reshape(NDEV, -1, 128), f"sum_pool_w_{j}").reshape(sshp[1:])
                                 for j, p in enumerate(parts_pw)])
        else:
            gown[n] = lax.dynamic_slice_in_dim(gfull[n], me * sshp[ax], sshp[ax], axis=ax)
    orows = _rows_for(own_shapes)
    packed = [_pack([src[n] for n in small_names], orows) for src in
              (given, gown, {n: given["m_" + n] for n in small_names}, {n: given["v_" + n] for n in small_names})]
    sd, sm, sv = adamw_small(*packed, "adamw_small")
    small = {}
    for n, d_, m_, v_ in zip(small_names, _unpack(sd.reshape(-1), own_shapes), _unpack(sm.reshape(-1), own_shapes),
                             _unpack(sv.reshape(-1), own_shapes)):
        small[n] = (gown[n], d_, m_, v_)

    weights = ['ln_g', 'ln_b', 'w_in_even', 'w_out_even', 'pool_w', 'pool_scale', 'sconv_w', 'sconv_b', 'w_in_odd',
               'w_out_odd', 'sgu_ln_g', 'sgu_ln_b', 'sgu_w', 'sgu_b', 'dconv_w', 'dconv_b', 'dnorm_g', 'dnorm_b']
    res = {n: (big[n] if n in big else small[n]) for n in weights}
    return (loss, grad_x, *[res[n][0] for n in weights], *[res[n][1] for n in weights],
            *[res[n][2] for n in weights], *[res[n][3] for n in weights])
```

```python
import jax
import jax.numpy as jnp
from jax import lax
from jax.experimental import pallas as pl
from jax.experimental.pallas import tpu as pltpu

F32 = jnp.float32
BF16 = jnp.bfloat16

D = 1024
DMIX = 2048
NIN = 6144
NDEV = 8
CW = NIN // NDEV
RW = DMIX // NDEV
DEPTH = 4
ALPHA = (2 * DEPTH) ** 0.25
LN_EPS = 1e-5
POOL_WINDOWS = (2, 4, 8, 16)
PG = 256
SGU_BLOCK = 128
HEADS = 4
HD = 256
KD = 31
KS = 3
SUB = 8
RC = 16
CB = 256
HP = 32
HS = 8
HC = 32
NSEM = 7

ADAM_LR = 0.001
ADAM_B1 = 0.9
ADAM_B2 = 0.999
ADAM_EPS = 1e-08
ADAM_WD = 0.01
ADAM_STEP = 10

VMEM_LIMIT = 56 * 1024 * 1024
MESH = pl.DeviceIdType.MESH
ANY = pl.BlockSpec(memory_space=pl.ANY)


def _cp(sem=None):
    if sem is None:
        return pltpu.CompilerParams(vmem_limit_bytes=VMEM_LIMIT)
    return pltpu.CompilerParams(dimension_semantics=sem, vmem_limit_bytes=VMEM_LIMIT)


def _sig(x):
    return 0.5 * jnp.tanh(0.5 * x) + 0.5


def _silu(x):
    h = 0.5 * x
    return h + h * jnp.tanh(h)


def _dsilu(sg, sil):
    return sg + sil * (1.0 - sg)


def _dot(a, b):
    return jnp.dot(a, b, preferred_element_type=F32)


def _dot_nt(a, b):
    return lax.dot_general(a, b, (((1,), (1,)), ((), ())), preferred_element_type=F32)


def _dot_tn(a, b):
    return lax.dot_general(a, b, (((0,), (0,)), ((), ())), preferred_element_type=F32)


def _rowsum(x):
    return jnp.sum(x, axis=0, keepdims=True)


def _comm_scratch(n):
    return [pltpu.SemaphoreType.DMA((n * NSEM,)), pltpu.SemaphoreType.DMA((n * NSEM,)),
            pltpu.SemaphoreType.DMA((n,))]


def _gather_plan(src_ref, out_ref, sems, n):
    send_sems, recv_sems, local_sems = sems
    base = n * NSEM
    x, y, c = lax.axis_index("x"), lax.axis_index("y"), lax.axis_index("c")
    me, sibling = (x, y, c), (x, y, 1 - c)
    chips = [(1 - x, y), (x, 1 - y), (1 - x, 1 - y)]

    def slot(px, py, pc):
        return out_ref.at[4 * px + 2 * py + pc]

    def copy(k, blk, to, src=None):
        return pltpu.make_async_remote_copy(
            src_ref=slot(*blk) if src is None else src, dst_ref=slot(*blk),
            send_sem=send_sems.at[base + k], recv_sem=recv_sems.at[base + k],
            device_id=to, device_id_type=MESH)

    def mine():
        return pltpu.make_async_copy(src_ref, slot(*me), local_sems.at[n])

    def start():
        mine().start()
        copy(0, me, sibling, src=src_ref).start()
        for j, chip in enumerate(chips):
            copy(1 + j, me, (*chip, c), src=src_ref).start()

    def finish():
        for j, chip in enumerate(chips):
            copy(1 + j, (*chip, c), me).wait_recv()
            copy(4 + j, (*chip, c), sibling).start()
        copy(0, sibling, me).wait_recv()
        for j, chip in enumerate(chips):
            copy(4 + j, (*chip, 1 - c), me).wait_recv()
        copy(0, me, sibling, src=src_ref).wait_send()
        for j, chip in enumerate(chips):
            copy(1 + j, me, (*chip, c), src=src_ref).wait_send()
            copy(4 + j, (*chip, c), sibling).wait_send()
        mine().wait()

    return start, finish


def _exchange_plan(p_ref, out_ref, sems, n):
    send_sems, recv_sems, local_sems = sems
    base = n * NSEM
    x, y, c = lax.axis_index("x"), lax.axis_index("y"), lax.axis_index("c")
    me = 4 * x + 2 * y + c

    def copy(r, landing):
        px, py, pc = x ^ (r >> 2), y ^ ((r >> 1) & 1), c ^ (r & 1)
        peer = 4 * px + 2 * py + pc
        return pltpu.make_async_remote_copy(
            src_ref=p_ref.at[peer], dst_ref=out_ref.at[peer if landing else me],
            send_sem=send_sems.at[base + r - 1], recv_sem=recv_sems.at[base + r - 1],
            device_id=(px, py, pc), device_id_type=MESH)

    def mine():
        return pltpu.make_async_copy(p_ref.at[me], out_ref.at[me], local_sems.at[n])

    def start():
        mine().start()
        for r in range(1, NDEV):
            copy(r, False).start()

    def finish():
        for r in range(1, NDEV):
            copy(r, True).wait_recv()
        for r in range(1, NDEV):
            copy(r, False).wait_send()
        mine().wait()

    return start, finish


def cast_x(x, w_first, small):
    s = x.shape[0]
    tm = min(s, 512)
    nt = s // tm

    def body(x_ref, wf_ref, sm_ref, o_ref, gw_ref, gs_ref, *sems):
        i = pl.program_id(0)
        plans = [_gather_plan(wf_ref.at[0], gw_ref, sems, 0), _gather_plan(sm_ref, gs_ref, sems, 1)]

        @pl.when(i == 0)
        def _():
            for start, _ in plans:
                start()

        o_ref[...] = x_ref[...].astype(BF16)

        @pl.when(i == nt - 1)
        def _():
            for _, finish in plans:
                finish()

    return pl.pallas_call(
        body, name="cast_x", grid=(nt,),
        in_specs=[pl.BlockSpec((tm, D), lambda i: (i, 0)), ANY, ANY],
        out_specs=[pl.BlockSpec((tm, D), lambda i: (i, 0)), ANY, ANY],
        out_shape=[jax.ShapeDtypeStruct((s, D), BF16),
                   jax.ShapeDtypeStruct((NDEV,) + w_first.shape[1:], w_first.dtype),
                   jax.ShapeDtypeStruct((NDEV,) + small.shape, small.dtype)],
        scratch_shapes=_comm_scratch(2), compiler_params=_cp(("arbitrary",)),
    )(x, w_first, small)


def cast_weights(w_even, w_odd, name):
    _, r, c = w_even.shape

    def body(e_ref, o_ref, out_ref):
        layer = pl.program_id(0)

        @pl.when(layer % 2 == 0)
        def _():
            out_ref[...] = e_ref[...].astype(BF16)

        @pl.when(layer % 2 == 1)
        def _():
            out_ref[...] = o_ref[...].astype(BF16)

    spec = pl.BlockSpec((None, r, c), lambda l: (l // 2, 0, 0))
    return pl.pallas_call(
        body, name=name, grid=(DEPTH,), in_specs=[spec, spec],
        out_specs=pl.BlockSpec((None, r, c), lambda l: (l, 0, 0)),
        out_shape=jax.ShapeDtypeStruct((DEPTH, r, c), BF16), compiler_params=_cp(("parallel",)),
    )(w_even, w_odd)


def _deepnorm(y_ref, wo_ref, xr_ref, gp_ref, bp_ref, g_ref, b_ref):
    xin = xr_ref[...] * gp_ref[...] + bp_ref[...]
    r = ALPHA * xin + _dot(y_ref[...], wo_ref[...].reshape(DMIX, D))
    mu = jnp.mean(r, axis=-1, keepdims=True)
    d = r - mu
    var = jnp.mean(d * d, axis=-1, keepdims=True)
    rstd = lax.rsqrt(var + LN_EPS)
    xh = d * rstd
    return xh, rstd, xh * g_ref[...] + b_ref[...]


def proj_in(wg, layer, wc_in, wc_out, xb=None, below=None):
    fused = below is not None
    s = below[0].shape[0] if fused else xb.shape[0]
    tm = min(s, 512)
    nt = s // tm
    nxt = layer + 1 < DEPTH
    nhead = 7 if fused else 1

    def body(*refs):
        head, refs = refs[:nhead], refs[nhead:]
        if nxt:
            w_ref, wco_ref, wci_ref = refs[:3]
            refs = refs[3:]
        else:
            w_ref, wco_ref = refs[:2]
            refs = refs[2:]
        o_ref = refs[0]
        if fused:
            xh_ref, rs_ref, xb_ref = refs[1:4]
            refs = refs[4:]
        else:
            refs = refs[1:]
        go_ref = refs[0]
        sems = refs[-3:]
        i = pl.program_id(0)
        plans = [_gather_plan(wco_ref.at[layer], go_ref, sems, 0)]
        if nxt:
            plans.append(_gather_plan(wci_ref.at[layer + 1], refs[1], sems, 1))

        @pl.when(i == 0)
        def _():
            for start, _ in plans:
                start()

        if fused:
            xh, rstd, xn = _deepnorm(*head)
            xh_ref[...] = xh
            rs_ref[...] = rstd
            x = xn.astype(BF16)
            xb_ref[...] = x
        else:
            x = head[0][...]
        for k in range(NDEV):
            o_ref[:, k * CW:(k + 1) * CW] = _dot(x, w_ref[k]).astype(BF16)

        @pl.when(i == nt - 1)
        def _():
            for _, finish in plans:
                finish()

    row = pl.BlockSpec((tm, D), lambda i: (i, 0))
    vec = pl.BlockSpec((1, D), lambda i: (0, 0))
    if fused:
        in_specs = [pl.BlockSpec((tm, DMIX), lambda i: (i, 0)),
                    pl.BlockSpec((NDEV, RW, D), lambda i: (0, 0, 0), pipeline_mode=pl.Buffered(1)),
                    row, vec, vec, vec, vec]
        args = list(below)
    else:
        in_specs, args = [row], [xb]
    in_specs += [pl.BlockSpec((NDEV, D, CW), lambda i: (0, 0, 0), pipeline_mode=pl.Buffered(1)), ANY]
    args += [wg, wc_out]
    if nxt:
        in_specs.append(ANY)
        args.append(wc_in)
    out_specs = [pl.BlockSpec((tm, NIN), lambda i: (i, 0))]
    out_shape = [jax.ShapeDtypeStruct((s, NIN), BF16)]
    if fused:
        out_specs += [row, pl.BlockSpec((tm, 1), lambda i: (i, 0)), row]
        out_shape += [jax.ShapeDtypeStruct((s, D), F32), jax.ShapeDtypeStruct((s, 1), F32),
                      jax.ShapeDtypeStruct((s, D), BF16)]
    out_specs.append(ANY)
    out_shape.append(jax.ShapeDtypeStruct((NDEV, RW, D), BF16))
    if nxt:
        out_specs.append(ANY)
        out_shape.append(jax.ShapeDtypeStruct((NDEV, D, CW), BF16))
    return pl.pallas_call(
        body, name=f"proj_in_{layer}", grid=(nt,), in_specs=in_specs, out_specs=out_specs,
        out_shape=out_shape, scratch_shapes=_comm_scratch(2 if nxt else 1),
        compiler_params=_cp(("arbitrary",)),
    )(*args)


def mix_fwd(z, layer, params):
    s = z.shape[0]
    tm = min(s, 512)
    even = layer % 2 == 0
    npar = len(params)
    nout = 3 if even else 2

    def even_tile(zr, j, par, outs, scr):
        pw_ref, ps_ref, sw_ref, sb_ref = par
        yc_ref, pb_ref, cv_ref = outs
        exa, eq, ya0, la, lb = scr
        for c in range(tm // RC):
            r0, r1 = c * RC, (c + 1) * RC
            exa[HP + r0:HP + r1, :] = zr[r0:r1, _cols(0)].astype(F32)
            eq[HS + r0:HS + r1, :] = zr[r0:r1, _cols(4)].astype(F32) * zr[r0:r1, _cols(2)].astype(F32)

        def level(src, dst, shift, c0, start):
            for r0 in range(start, HP + tm, 2 * RC):
                r1 = min(r0 + 2 * RC, HP + tm)
                dst[r0:r1, c0:D] = src[r0:r1, c0:D] + src[r0 - shift:r1 - shift, c0:D]

        level(exa, la, 1, PG, 8)
        level(la, lb, 2, 2 * PG, 16)
        level(lb, la, 4, 3 * PG, 24)
        last = ((exa, 1), (la, 2), (lb, 4), (la, 8))
        for c in range(tm // RC):
            r0, r1 = HP + c * RC, HP + (c + 1) * RC
            pos = (j * tm + c * RC + lax.broadcasted_iota(jnp.int32, (RC, 1), 0) + 1).astype(F32)
            for gi, w in enumerate(POOL_WINDOWS):
                c0, c1 = gi * PG, (gi + 1) * PG
                src, shift = last[gi]
                acc = src[r0:r1, c0:c1] + src[r0 - shift:r1 - shift, c0:c1]
                pooled = acc / jnp.minimum(pos, float(w)) - exa[r0:r1, c0:c1]
                pb_ref[c * RC:(c + 1) * RC, c0:c1] = pooled.astype(BF16)
        for gi in range(len(POOL_WINDOWS)):
            c0, c1 = gi * PG, (gi + 1) * PG
            ya0[:, c0:c1] = _dot(pb_ref[:, c0:c1], pw_ref[gi])
        for c in range(tm // RC):
            r0, r1 = c * RC, (c + 1) * RC
            ga = zr[r0:r1, _cols(1)].astype(F32)
            yc_ref[r0:r1, 0:D] = (ya0[r0:r1, :] * ps_ref[...] * (ga * _sig(ga))).astype(BF16)
            cv = (sw_ref[2:3, :] * eq[HS + r0:HS + r1, :] + sw_ref[1:2, :] * eq[HS + r0 - 1:HS + r1 - 1, :]
                  + sw_ref[0:1, :] * eq[HS + r0 - 2:HS + r1 - 2, :] + sb_ref[...])
            cv_ref[r0:r1, :] = cv.astype(BF16)
            gb = zr[r0:r1, _cols(5)].astype(F32)
            yc_ref[r0:r1, D:2 * D] = (zr[r0:r1, _cols(3)].astype(F32) * cv * (gb * _sig(gb))).astype(BF16)
        exa[0:HP, :] = exa[tm:tm + HP, :]
        eq[0:HS, :] = eq[tm:tm + HS, :]

    def odd_tile(zr, j, par, outs, scr):
        slg_ref, slb_ref, ws_ref, sb_ref, dw_ref, dcb_ref, dng_ref, dnb_ref = par
        yc_ref, cz_ref = outs
        ezg, esh = scr
        vh, _ = _ln_rows(zr[:, _cols(1)].astype(F32))
        vnb = (vh * slg_ref[...] + slb_ref[...]).astype(BF16)
        for n in range(tm // SGU_BLOCK):
            r0, r1 = n * SGU_BLOCK, (n + 1) * SGU_BLOCK
            for hd in range(HEADS):
                c0, c1 = hd * HD, (hd + 1) * HD
                sv = _dot(ws_ref[hd], vnb[r0:r1, c0:c1]) + sb_ref[hd]
                gc = zr[r0:r1, 2 * D + c0:2 * D + c1].astype(F32)
                yc_ref[r0:r1, c0:c1] = (zr[r0:r1, c0:c1].astype(F32) * sv * _silu(gc)).astype(BF16)
        ezg[HC:HC + tm, :] = zr[:, _cols(3)].astype(F32) * _sig(zr[:, _cols(4)].astype(F32))
        _shifted_copies(ezg, esh, tm + HC - SUB)
        cz = jnp.zeros((tm, D), F32) + dcb_ref[...]
        for k in range(KD):
            cz = cz + dw_ref[k:k + 1, :] * _tap(ezg, esh, HC - (KD - 1) + k, 0, tm)
        cz_ref[...] = cz
        zh, _ = _ln_rows(cz)
        zn = zh * dng_ref[...] + dnb_ref[...]
        gd = zr[:, _cols(5)].astype(F32)
        yc_ref[:, D:2 * D] = (_silu(zn) * _silu(gd)).astype(BF16)
        ezg[0:HC, :] = ezg[tm:tm + HC, :]

    def body(z_ref, *refs):
        par, outs, scr = refs[:npar], refs[npar:npar + nout], refs[npar + nout:]
        j = pl.program_id(0)

        @pl.when(j == 0)
        def _():
            if even:
                scr[0][0:HP, :] = jnp.zeros((HP, D), F32)
                scr[1][0:HS, :] = jnp.zeros((HS, D), F32)
            else:
                scr[0][0:HC, :] = jnp.zeros((HC, D), F32)

        (even_tile if even else odd_tile)(z_ref, j, par, outs, scr)

    row = lambda c: pl.BlockSpec((tm, c), lambda j: (j, 0))
    vec = pl.BlockSpec((1, D), lambda j: (0, 0))
    if even:
        par_specs = [pl.BlockSpec((4, PG, PG), lambda j: (0, 0, 0)), vec, pl.BlockSpec((KS, D), lambda j: (0, 0)), vec]
        out_specs = [row(DMIX), row(D), row(D)]
        out_shape = [jax.ShapeDtypeStruct((s, DMIX), BF16), jax.ShapeDtypeStruct((s, D), BF16),
                     jax.ShapeDtypeStruct((s, D), BF16)]
        scratch = [pltpu.VMEM((HP + tm, D), F32), pltpu.VMEM((HS + tm, D), F32), pltpu.VMEM((tm, D), F32),
                   pltpu.VMEM((HP + tm, D), F32), pltpu.VMEM((HP + tm, D), F32)]
    else:
        par_specs = [vec, vec, pl.BlockSpec((HEADS, SGU_BLOCK, SGU_BLOCK), lambda j: (0, 0, 0)),
                     pl.BlockSpec((HEADS, SGU_BLOCK, 1), lambda j: (0, 0, 0)),
                     pl.BlockSpec((KD, D), lambda j: (0, 0)), vec, vec, vec]
        out_specs = [row(DMIX), row(D)]
        out_shape = [jax.ShapeDtypeStruct((s, DMIX), BF16), jax.ShapeDtypeStruct((s, D), F32)]
        scratch = [pltpu.VMEM((HC + tm, D), F32), pltpu.VMEM((SUB - 1, tm + HC - SUB, D), F32)]
    return pl.pallas_call(
        body, name=f"mix_fwd_{layer}", grid=(s // tm,), in_specs=[row(NIN)] + par_specs, out_specs=out_specs,
        out_shape=out_shape, scratch_shapes=scratch, compiler_params=_cp(("arbitrary",)),
    )(z, *params)


def _zero_at_start(*refs):
    @pl.when(pl.program_id(0) == 0)
    def _():
        for ref in refs:
            ref[...] = jnp.zeros_like(ref)


def _ln_bwd_tile(dxo, xh, rstd, g_ref, wo_ref, dr_ref, drb_ref, dyc_ref, dg_ref, db_ref):
    dg_ref[...] += _rowsum(dxo * xh)
    db_ref[...] += _rowsum(dxo)
    dr = _ln_rows_bwd(dxo, xh, rstd, g_ref[...])
    dr_ref[...] = dr
    drb = dr.astype(BF16)
    drb_ref[...] = drb
    dyc_ref[...] = _dot_nt(drb, wo_ref[...].reshape(DMIX, D)).astype(BF16)


def _ln_bwd_outs(s, tm):
    row = pl.BlockSpec((tm, D), lambda i: (i, 0))
    vec = pl.BlockSpec((1, D), lambda i: (0, 0))
    return ([row, row, pl.BlockSpec((tm, DMIX), lambda i: (i, 0)), vec, vec],
            [jax.ShapeDtypeStruct((s, D), F32), jax.ShapeDtypeStruct((s, D), BF16),
             jax.ShapeDtypeStruct((s, DMIX), BF16), jax.ShapeDtypeStruct((1, D), F32),
             jax.ShapeDtypeStruct((1, D), F32)])


def head_loss_bwd(below, target):
    s = target.shape[0]
    tm = min(s, 512)

    def body(*refs):
        t_ref = refs[7]
        loss_ref = refs[-1]
        _zero_at_start(*refs[11:14])
        xh, rstd, xn = _deepnorm(*refs[:7])
        err = xn - t_ref[...]
        loss_ref[...] += 0.5 * jnp.sum(jnp.mean(err * err, axis=-1, keepdims=True), axis=0, keepdims=True)
        _ln_bwd_tile(err * (1.0 / D), xh, rstd, refs[5], refs[1], *refs[8:13])

    row = pl.BlockSpec((tm, D), lambda i: (i, 0))
    vec = pl.BlockSpec((1, D), lambda i: (0, 0))
    out_specs, out_shape = _ln_bwd_outs(s, tm)
    return pl.pallas_call(
        body, name="head_loss_bwd", grid=(s // tm,),
        in_specs=[pl.BlockSpec((tm, DMIX), lambda i: (i, 0)), pl.BlockSpec((NDEV, RW, D), lambda i: (0, 0, 0)),
                  row, vec, vec, vec, vec, row],
        out_specs=out_specs + [pl.BlockSpec((1, 1), lambda i: (0, 0))],
        out_shape=out_shape + [jax.ShapeDtypeStruct((1, 1), F32)],
        compiler_params=_cp(("arbitrary",)),
    )(*below, target)


def ln_bwd_dycat(xhat, rstd, g, wog, layer, upstream):
    s = xhat.shape[0]
    tm = min(s, 512)

    def body(dz_ref, wi_ref, dru_ref, xh_ref, rs_ref, g_ref, w_ref, *outs):
        _zero_at_start(*outs[3:5])
        dxo = ALPHA * dru_ref[...]
        for k in range(NDEV):
            dxo += _dot_nt(dz_ref[:, k * CW:(k + 1) * CW], wi_ref[k])
        _ln_bwd_tile(dxo, xh_ref[...], rs_ref[...], g_ref, w_ref, *outs)

    row = pl.BlockSpec((tm, D), lambda i: (i, 0))
    vec = pl.BlockSpec((1, D), lambda i: (0, 0))
    out_specs, out_shape = _ln_bwd_outs(s, tm)
    return pl.pallas_call(
        body, name=f"ln_bwd_dycat_{layer}", grid=(s // tm,),
        in_specs=[pl.BlockSpec((tm, NIN), lambda i: (i, 0)),
                  pl.BlockSpec((NDEV, D, CW), lambda i: (0, 0, 0), pipeline_mode=pl.Buffered(1)), row,
                  row, pl.BlockSpec((tm, 1), lambda i: (i, 0)), vec,
                  pl.BlockSpec((NDEV, RW, D), lambda i: (0, 0, 0), pipeline_mode=pl.Buffered(1))],
        out_specs=out_specs, out_shape=out_shape, compiler_params=_cp(("arbitrary",)),
    )(*upstream, xhat, rstd, g, wog)


def dx_in(dzb, wg, dr, parts):
    s = dzb.shape[0]
    tm = min(s, 256)
    nt = s // tm

    def body(dz_ref, w_ref, dr_ref, p_ref, o_ref, land_ref, *sems):
        start, finish = _exchange_plan(p_ref, land_ref, sems, 0)
        pl.when(pl.program_id(0) == 0)(start)
        acc = ALPHA * dr_ref[...]
        for k in range(NDEV):
            acc += _dot_nt(dz_ref[:, k * CW:(k + 1) * CW], w_ref[k])
        o_ref[...] = acc
        pl.when(pl.program_id(0) == nt - 1)(finish)

    row = pl.BlockSpec((tm, D), lambda i: (i, 0))
    return pl.pallas_call(
        body, name="dx_in_0", grid=(nt,),
        in_specs=[pl.BlockSpec((tm, NIN), lambda i: (i, 0)),
                  pl.BlockSpec((NDEV, D, CW), lambda i: (0, 0, 0), pipeline_mode=pl.Buffered(1)), row, ANY],
        out_specs=[row, ANY],
        out_shape=[jax.ShapeDtypeStruct((s, D), F32), jax.ShapeDtypeStruct(parts.shape, parts.dtype)],
        scratch_shapes=_comm_scratch(1), compiler_params=_cp(("arbitrary",)),
    )(dzb, wg, dr, parts)


def dw_in(xb, dzb, layer, parts=(), gathers=()):
    s = xb.shape[0]
    tm = min(s, 4096)
    nt = s // tm
    ne = len(parts)
    n = ne + len(gathers)

    def body(*refs):
        x_ref, dz_ref = refs[:2]
        p_refs = refs[2:2 + n]
        o_ref = refs[2 + n]
        land_refs = refs[3 + n:3 + 2 * n]
        acc = refs[3 + 2 * n]
        plans = [(_exchange_plan if j < ne else _gather_plan)(p, land, refs[-3:], j)
                 for j, (p, land) in enumerate(zip(p_refs, land_refs))]
        k, t = pl.program_id(0), pl.program_id(1)

        @pl.when((k == 0) & (t == 0))
        def _():
            for start, _ in plans:
                start()

        @pl.when(t == 0)
        def _():
            acc[...] = jnp.zeros_like(acc)

        acc[...] += _dot_tn(x_ref[...], dz_ref[...])

        @pl.when(t == nt - 1)
        def _():
            o_ref[...] = acc[...].astype(BF16)

        @pl.when((k == NDEV - 1) & (t == nt - 1))
        def _():
            for _, finish in plans:
                finish()

    return pl.pallas_call(
        body, name=f"dw_in_{layer}", grid=(NDEV, nt),
        in_specs=[pl.BlockSpec((tm, D), lambda k, t: (t, 0)), pl.BlockSpec((tm, CW), lambda k, t: (t, k))] + [ANY] * n,
        out_specs=[pl.BlockSpec((None, D, CW), lambda k, t: (k, 0, 0))] + [ANY] * n,
        out_shape=[jax.ShapeDtypeStruct((NDEV, D, CW), BF16)] + [jax.ShapeDtypeStruct(p.shape, p.dtype) for p in parts]
        + [jax.ShapeDtypeStruct((NDEV,) + g.shape, g.dtype) for g in gathers],
        scratch_shapes=[pltpu.VMEM((D, CW), F32)] + (_comm_scratch(n) if n else []),
        compiler_params=_cp(("arbitrary", "arbitrary")),
    )(xb, dzb, *parts, *gathers)


def dw_out(ycat, drb, layer):
    s = ycat.shape[0]
    tm = min(s, 1024)
    nt = s // tm

    def body(y_ref, dr_ref, o_ref, acc):
        t = pl.program_id(0)

        @pl.when(t == 0)
        def _():
            acc[...] = jnp.zeros_like(acc)

        acc[...] += _dot_tn(y_ref[...], dr_ref[...])

        @pl.when(t == nt - 1)
        def _():
            o_ref[...] = acc[...].reshape(NDEV, RW, D).astype(BF16)

    return pl.pallas_call(
        body, name=f"dw_out_{layer}", grid=(nt,),
        in_specs=[pl.BlockSpec((tm, DMIX), lambda t: (t, 0)), pl.BlockSpec((tm, D), lambda t: (t, 0))],
        out_specs=pl.BlockSpec((NDEV, RW, D), lambda t: (0, 0, 0)),
        out_shape=jax.ShapeDtypeStruct((NDEV, RW, D), BF16),
        scratch_shapes=[pltpu.VMEM((DMIX, D), F32)],
        compiler_params=_cp(("arbitrary",)),
    )(ycat, drb)


def _cols(j):
    return slice(j * D, (j + 1) * D)


def mix_even_bwd(z, dycat, pooled, cv, pwb, pscale, sw, layer, parts=None):
    s = z.shape[0]
    tm = min(s, 256)
    nt = s // tm
    carry = parts is not None

    def body(*refs):
        z_ref, dy_ref, pb_ref, cv_ref, pw_ref, ps_ref, sw_ref = refs[:7]
        if carry:
            p_ref = refs[7]
            dz_ref, dpw_ref, dps_ref, dsw_ref, dsb_ref, land_ref = refs[8:14]
            edp, edc, ya0, t1s, dpl, la, lb = refs[14:21]
            start, finish = _exchange_plan(p_ref, land_ref, refs[-3:], 0)
        else:
            dz_ref, dpw_ref, dps_ref, dsw_ref, dsb_ref = refs[7:12]
            edp, edc, ya0, t1s, dpl, la, lb = refs[12:19]
        i = pl.program_id(0)
        if carry:
            pl.when(i == 0)(start)

        @pl.when(i == 0)
        def _():
            edp[tm:tm + HP, :] = jnp.zeros((HP, D), F32)
            edc[tm:tm + HS, :] = jnp.zeros((HS, D), F32)
            dpw_ref[...] = jnp.zeros_like(dpw_ref)
            dps_ref[...] = jnp.zeros_like(dps_ref)
            dsw_ref[...] = jnp.zeros_like(dsw_ref)
            dsb_ref[...] = jnp.zeros_like(dsb_ref)

        blocks = [(r * RC, (r + 1) * RC, c * CB, (c + 1) * CB) for r in range(tm // RC) for c in range(D // CB)]
        for gi in range(len(POOL_WINDOWS)):
            c0, c1 = gi * PG, (gi + 1) * PG
            ya0[:, c0:c1] = _dot(pb_ref[:, c0:c1], pw_ref[gi])
        for r0, r1, c0, c1 in blocks:
            ga = z_ref[r0:r1, D + c0:D + c1].astype(F32)
            sg = _sig(ga)
            sil = ga * sg
            dya = dy_ref[r0:r1, c0:c1].astype(F32)
            y0 = ya0[r0:r1, c0:c1]
            ps = ps_ref[:, c0:c1]
            dps_ref[:, c0:c1] += _rowsum(dya * y0 * sil)
            dz_ref[r0:r1, D + c0:D + c1] = (dya * y0 * ps * (sg * (1.0 + ga * (1.0 - sg)))).astype(BF16)
            t1s[r0:r1, c0:c1] = (dya * ps * sil).astype(BF16)
        for gi in range(len(POOL_WINDOWS)):
            c0, c1 = gi * PG, (gi + 1) * PG
            dpl[:, c0:c1] = _dot_nt(t1s[:, c0:c1], pw_ref[gi])
            dpw_ref[gi] += _dot_tn(pb_ref[:, c0:c1], t1s[:, c0:c1])
        for r in range(tm // RC):
            r0, r1 = r * RC, (r + 1) * RC
            pos = ((nt - 1 - i) * tm + r0 + lax.broadcasted_iota(jnp.int32, (RC, 1), 0) + 1).astype(F32)
            for gi, w in enumerate(POOL_WINDOWS):
                c0, c1 = gi * PG, (gi + 1) * PG
                edp[r0:r1, c0:c1] = dpl[r0:r1, c0:c1] / jnp.minimum(pos, float(w))

        def level(src, dst, shift, c0, end):
            for r0 in range(0, end, 2 * RC):
                r1 = min(r0 + 2 * RC, end)
                dst[r0:r1, c0:D] = src[r0:r1, c0:D] + src[r0 + shift:r1 + shift, c0:D]

        level(edp, la, 1, PG, tm + 24)
        level(la, lb, 2, 2 * PG, tm + 16)
        level(lb, la, 4, 3 * PG, tm + 8)
        last = ((edp, 1), (la, 2), (lb, 4), (la, 8))
        for r in range(tm // RC):
            r0, r1 = r * RC, (r + 1) * RC
            for gi in range(len(POOL_WINDOWS)):
                c0, c1 = gi * PG, (gi + 1) * PG
                src, shift = last[gi]
                acc = src[r0:r1, c0:c1] + src[r0 + shift:r1 + shift, c0:c1] - dpl[r0:r1, c0:c1]
                dz_ref[r0:r1, c0:c1] = acc.astype(BF16)

        for r0, r1, c0, c1 in blocks:
            gb = z_ref[r0:r1, 5 * D + c0:5 * D + c1].astype(F32)
            sgb = _sig(gb)
            silb = gb * sgb
            dyb = dy_ref[r0:r1, D + c0:D + c1].astype(F32)
            cvv = cv_ref[r0:r1, c0:c1].astype(F32)
            bg = z_ref[r0:r1, 3 * D + c0:3 * D + c1].astype(F32)
            dz_ref[r0:r1, 3 * D + c0:3 * D + c1] = (dyb * cvv * silb).astype(BF16)
            dz_ref[r0:r1, 5 * D + c0:5 * D + c1] = (dyb * bg * cvv * (sgb * (1.0 + gb * (1.0 - sgb)))).astype(BF16)
            dcv = dyb * bg * silb
            dsb_ref[:, c0:c1] += _rowsum(dcv)
            edc[r0:r1, c0:c1] = dcv
        for r0, r1, c0, c1 in blocks:
            h = z_ref[r0:r1, 2 * D + c0:2 * D + c1].astype(F32)
            cg = z_ref[r0:r1, 4 * D + c0:4 * D + c1].astype(F32)
            q = cg * h
            d0 = edc[r0:r1, c0:c1]
            d1 = edc[r0 + 1:r1 + 1, c0:c1]
            d2 = edc[r0 + 2:r1 + 2, c0:c1]
            dq = sw_ref[2:3, c0:c1] * d0 + sw_ref[1:2, c0:c1] * d1 + sw_ref[0:1, c0:c1] * d2
            dsw_ref[2:3, c0:c1] += _rowsum(q * d0)
            dsw_ref[1:2, c0:c1] += _rowsum(q * d1)
            dsw_ref[0:1, c0:c1] += _rowsum(q * d2)
            dz_ref[r0:r1, 4 * D + c0:4 * D + c1] = (dq * h).astype(BF16)
            dz_ref[r0:r1, 2 * D + c0:2 * D + c1] = (dq * cg).astype(BF16)
        edp[tm:tm + HP, :] = edp[0:HP, :]
        edc[tm:tm + HS, :] = edc[0:HS, :]
        if carry:
            pl.when(i == nt - 1)(finish)

    rev = lambda i: (nt - 1 - i, 0)
    vec = pl.BlockSpec((1, D), lambda i: (0, 0))
    in_specs = [pl.BlockSpec((tm, NIN), rev), pl.BlockSpec((tm, DMIX), rev),
                pl.BlockSpec((tm, D), rev), pl.BlockSpec((tm, D), rev),
                pl.BlockSpec((4, PG, PG), lambda i: (0, 0, 0)), vec,
                pl.BlockSpec((KS, D), lambda i: (0, 0))]
    out_specs = [pl.BlockSpec((tm, NIN), rev), pl.BlockSpec((4, PG, PG), lambda i: (0, 0, 0)), vec,
                 pl.BlockSpec((KS, D), lambda i: (0, 0)), vec]
    out_shape = [jax.ShapeDtypeStruct((s, NIN), BF16), jax.ShapeDtypeStruct((4, PG, PG), F32),
                 jax.ShapeDtypeStruct((1, D), F32), jax.ShapeDtypeStruct((KS, D), F32),
                 jax.ShapeDtypeStruct((1, D), F32)]
    args = [z, dycat, pooled, cv, pwb, pscale, sw]
    scratch = [pltpu.VMEM((tm + HP, D), F32), pltpu.VMEM((tm + HS, D), F32), pltpu.VMEM((tm, D), F32),
               pltpu.VMEM((tm, D), BF16), pltpu.VMEM((tm, D), F32), pltpu.VMEM((tm + HP, D), F32),
               pltpu.VMEM((tm + HP, D), F32)]
    if carry:
        in_specs.append(ANY)
        args.append(parts)
        out_specs.append(ANY)
        out_shape.append(jax.ShapeDtypeStruct(parts.shape, parts.dtype))
        scratch += _comm_scratch(1)
    return pl.pallas_call(
        body, name=f"mix_even_bwd_{layer}", grid=(nt,), in_specs=in_specs, out_specs=out_specs,
        out_shape=out_shape, scratch_shapes=scratch, compiler_params=_cp(("arbitrary",)),
    )(*args)


def _ln_rows(v):
    mu = jnp.mean(v, axis=-1, keepdims=True)
    d = v - mu
    var = jnp.mean(d * d, axis=-1, keepdims=True)
    rstd = lax.rsqrt(var + LN_EPS)
    return d * rstd, rstd


def _ln_rows_bwd(dn, xh, rstd, g):
    dxh = dn * g
    m1 = jnp.mean(dxh, axis=-1, keepdims=True)
    m2 = jnp.mean(dxh * xh, axis=-1, keepdims=True)
    return rstd * (dxh - m1 - xh * m2)


def _shifted_copies(ext, shifted, rows):
    for b in range(1, SUB):
        shifted[b - 1] = ext[b:b + rows, :]


def _tap(ext, shifted, off, r0, r1, cols=slice(None)):
    a, b = off // SUB, off % SUB
    if b == 0:
        return ext[SUB * a + r0:SUB * a + r1, cols]
    return shifted[b - 1, SUB * a + r0:SUB * a + r1, cols]


def mix_odd_bwd(z, dycat, cz, slg, slb, wsb, wstb, sbcol, dw, dng, dnb, layer, parts):
    s = z.shape[0]
    tm = min(s, 256)
    nt = s // tm

    def body(z_ref, dy_ref, cz_ref, slg_ref, slb_ref, ws_ref, wst_ref, sb_ref, dw_ref, dng_ref, dnb_ref, p_ref,
             dz_ref, dslg_ref, dslb_ref, dws_ref, dsb_ref, ddw_ref, ddcb_ref, ddng_ref, ddnb_ref, land_ref,
             dvn, edz, esh, ddw8, *sems):
        i = pl.program_id(0)
        start, finish = _exchange_plan(p_ref, land_ref, sems, 0)
        pl.when(i == 0)(start)

        @pl.when(i == 0)
        def _():
            edz[tm:tm + HC, :] = jnp.zeros((HC, D), F32)
            for ref in (dslg_ref, dslb_ref, dws_ref, dsb_ref, ddw8, ddcb_ref, ddng_ref, ddnb_ref):
                ref[...] = jnp.zeros_like(ref)

        vh, vrs = _ln_rows(z_ref[:, _cols(1)].astype(F32))
        vnb = (vh * slg_ref[...] + slb_ref[...]).astype(BF16)
        for n in range(tm // SGU_BLOCK):
            r0, r1 = n * SGU_BLOCK, (n + 1) * SGU_BLOCK
            for hd in range(HEADS):
                c0, c1 = hd * HD, (hd + 1) * HD
                vblk = vnb[r0:r1, c0:c1]
                sv = _dot(ws_ref[hd], vblk) + sb_ref[hd]
                gc = z_ref[r0:r1, 2 * D + c0:2 * D + c1].astype(F32)
                sg = _sig(gc)
                sil = gc * sg
                u = z_ref[r0:r1, c0:c1].astype(F32)
                dyc = dy_ref[r0:r1, c0:c1].astype(F32)
                dz_ref[r0:r1, c0:c1] = (dyc * sv * sil).astype(BF16)
                dz_ref[r0:r1, 2 * D + c0:2 * D + c1] = (dyc * u * sv * _dsilu(sg, sil)).astype(BF16)
                dsv = dyc * u * sil
                dsb_ref[hd] += jnp.sum(dsv, axis=-1, keepdims=True)
                dsvb = dsv.astype(BF16)
                dws_ref[hd] += _dot_nt(dsvb, vblk)
                dvn[r0:r1, c0:c1] = _dot(wst_ref[hd], dsvb)
        dv = dvn[...]
        dslg_ref[...] += _rowsum(dv * vh)
        dslb_ref[...] += _rowsum(dv)
        dz_ref[:, _cols(1)] = _ln_rows_bwd(dv, vh, vrs, slg_ref[...]).astype(BF16)

        zh, zrs = _ln_rows(cz_ref[...])
        zn = zh * dng_ref[...] + dnb_ref[...]
        sgn = _sig(zn)
        gd = z_ref[:, _cols(5)].astype(F32)
        sgd = _sig(gd)
        dyd = dy_ref[:, D:2 * D].astype(F32)
        silz, sild = zn * sgn, gd * sgd
        dz_ref[:, _cols(5)] = (dyd * silz * _dsilu(sgd, sild)).astype(BF16)
        dzn = dyd * sild * _dsilu(sgn, silz)
        ddng_ref[...] += _rowsum(dzn * zh)
        ddnb_ref[...] += _rowsum(dzn)
        dcz = _ln_rows_bwd(dzn, zh, zrs, dng_ref[...])
        ddcb_ref[...] += _rowsum(dcz)
        edz[0:tm, :] = dcz
        _shifted_copies(edz, esh, tm + HC - SUB)
        for r0 in range(0, tm, RC):
            for c0 in range(0, D, CB):
                r1, c1 = r0 + RC, c0 + CB
                a = z_ref[r0:r1, 3 * D + c0:3 * D + c1].astype(F32)
                sgb = _sig(z_ref[r0:r1, 4 * D + c0:4 * D + c1].astype(F32))
                zg = a * sgb
                dzg = jnp.zeros((RC, CB), F32)
                for k in range(KD):
                    sh = _tap(edz, esh, KD - 1 - k, r0, r1, slice(c0, c1))
                    dzg = dzg + dw_ref[k:k + 1, c0:c1] * sh
                    prod = zg * sh
                    ddw8[k, :, c0:c1] += prod[0:SUB] + prod[SUB:2 * SUB]
                dz_ref[r0:r1, 3 * D + c0:3 * D + c1] = (dzg * sgb).astype(BF16)
                dz_ref[r0:r1, 4 * D + c0:4 * D + c1] = (dzg * a * sgb * (1.0 - sgb)).astype(BF16)
        edz[tm:tm + HC, :] = edz[0:HC, :]

        @pl.when(i == nt - 1)
        def _():
            ddw_ref[...] = jnp.sum(ddw8[...], axis=1)

        pl.when(i == nt - 1)(finish)

    rev = lambda i: (nt - 1 - i, 0)
    vec = pl.BlockSpec((1, D), lambda i: (0, 0))
    wspec = pl.BlockSpec((HEADS, SGU_BLOCK, SGU_BLOCK), lambda i: (0, 0, 0))
    bspec = pl.BlockSpec((HEADS, SGU_BLOCK, 1), lambda i: (0, 0, 0))
    kspec = pl.BlockSpec((KD, D), lambda i: (0, 0))
    return pl.pallas_call(
        body, name=f"mix_odd_bwd_{layer}", grid=(nt,),
        in_specs=[pl.BlockSpec((tm, NIN), rev), pl.BlockSpec((tm, DMIX), rev), pl.BlockSpec((tm, D), rev),
                  vec, vec, wspec, wspec, bspec, kspec, vec, vec, ANY],
        out_specs=[pl.BlockSpec((tm, NIN), rev), vec, vec, wspec, bspec, kspec, vec, vec, vec, ANY],
        out_shape=[jax.ShapeDtypeStruct((s, NIN), BF16), jax.ShapeDtypeStruct((1, D), F32),
                   jax.ShapeDtypeStruct((1, D), F32),
                   jax.ShapeDtypeStruct((HEADS, SGU_BLOCK, SGU_BLOCK), F32),
                   jax.ShapeDtypeStruct((HEADS, SGU_BLOCK, 1), F32), jax.ShapeDtypeStruct((KD, D), F32),
                   jax.ShapeDtypeStruct((1, D), F32), jax.ShapeDtypeStruct((1, D), F32),
                   jax.ShapeDtypeStruct((1, D), F32), jax.ShapeDtypeStruct(parts.shape, parts.dtype)],
        scratch_shapes=[pltpu.VMEM((tm, D), F32), pltpu.VMEM((tm + HC, D), F32),
                        pltpu.VMEM((SUB - 1, tm + HC - SUB, D), F32), pltpu.VMEM((KD, SUB, D), F32)]
        + _comm_scratch(1),
        compiler_params=_cp(("arbitrary",)),
    )(z, dycat, cz, slg, slb, wsb, wstb, sbcol, dw, dng, dnb, parts)


def _adamw_math(w, g, m, v):
    m = ADAM_B1 * m + (1.0 - ADAM_B1) * g
    v = ADAM_B2 * v + (1.0 - ADAM_B2) * (g * g)
    m_hat = m / (1.0 - ADAM_B1 ** ADAM_STEP)
    v_hat = v / (1.0 - ADAM_B2 ** ADAM_STEP)
    delta = -ADAM_LR * (m_hat / (jnp.sqrt(v_hat) + ADAM_EPS) + ADAM_WD * w)
    return delta, m, v


def adamw_big(w, m, v, parts0, parts1, name):
    _, r, c = w.shape
    tr = min(r, 256)
    nr = r // tr

    def body(w_ref, m_ref, v_ref, p0_ref, p1_ref, g_ref, d_ref, nm_ref, nv_ref):
        i = pl.program_id(0)

        def total(p_ref):
            acc = p_ref[0].astype(F32)
            for j in range(1, NDEV):
                acc = acc + p_ref[j].astype(F32)
            return acc

        @pl.when(i == 0)
        def _():
            g_ref[...] = total(p0_ref)

        @pl.when(i == 1)
        def _():
            g_ref[...] = total(p1_ref)

        delta, nm, nv = _adamw_math(w_ref[...], g_ref[...], m_ref[...], v_ref[...])
        d_ref[...] = delta
        nm_ref[...] = nm
        nv_ref[...] = nv

    wspec = pl.BlockSpec((None, tr, c), lambda i, j: (i, j, 0))
    p0 = pl.BlockSpec((NDEV, tr, c), lambda i, j: (0, jnp.where(i == 0, j, nr - 1), 0))
    p1 = pl.BlockSpec((NDEV, tr, c), lambda i, j: (0, jnp.where(i == 1, j, 0), 0))
    shp = jax.ShapeDtypeStruct(w.shape, F32)
    return pl.pallas_call(
        body, name=name, grid=(2, nr), in_specs=[wspec, wspec, wspec, p0, p1],
        out_specs=[wspec] * 4, out_shape=[shp] * 4,
        compiler_params=_cp(("arbitrary", "arbitrary")),
    )(w, m, v, parts0, parts1)


def sum_parts(parts, name):
    _, r, c = parts.shape

    def body(p_ref, o_ref):
        acc = p_ref[0]
        for j in range(1, NDEV):
            acc = acc + p_ref[j]
        o_ref[...] = acc

    return pl.pallas_call(body, name=name, out_shape=jax.ShapeDtypeStruct((r, c), F32),
                          compiler_params=_cp())(parts)


def adamw_small(w, g, m, v, name):
    def body(w_ref, g_ref, m_ref, v_ref, d_ref, nm_ref, nv_ref):
        delta, nm, nv = _adamw_math(w_ref[...], g_ref[...], m_ref[...], v_ref[...])
        d_ref[...] = delta
        nm_ref[...] = nm
        nv_ref[...] = nv

    shp = jax.ShapeDtypeStruct(w.shape, F32)
    return pl.pallas_call(body, name=name, out_shape=[shp] * 3, compiler_params=_cp())(w, g, m, v)


def _size(shape):
    n = 1
    for d in shape:
        n *= d
    return n


def _pack(arrays, rows):
    flat = jnp.concatenate([a.reshape(-1) for a in arrays])
    return jnp.pad(flat, (0, rows * 128 - flat.shape[0])).reshape(rows, 128)


def _unpack(flat, shapes):
    out, o = [], 0
    for shp in shapes:
        out.append(flat[o:o + _size(shp)].reshape(shp))
        o += _size(shp)
    return out


def _rows_for(shapes):
    return -(-sum(_size(shp) for shp in shapes) // 1024) * 8


SHARDED_SMALL = (("pool_w", (2, 4, 256, 256), 2), ("sconv_w", (2, KS, D), 2), ("sgu_ln_g", (2, D), 1),
                 ("sgu_ln_b", (2, D), 1), ("dconv_w", (2, KD, D), 2), ("dconv_b", (2, D), 1),
                 ("dnorm_g", (2, D), 1), ("dnorm_b", (2, D), 1))
REPLICATED_SMALL = (("ln_g", (DEPTH, D)), ("ln_b", (DEPTH, D)), ("pool_scale", (2, D)), ("sconv_b", (2, D)),
                    ("sgu_w", (2, HEADS, SGU_BLOCK, SGU_BLOCK)), ("sgu_b", (2, HEADS, SGU_BLOCK)))


def _shard_shape(shape, axis):
    return tuple(d // NDEV if a == axis else d for a, d in enumerate(shape))


def _merge_gathered(g, shape, axis):
    return jnp.moveaxis(g, 0, axis).reshape(shape)


def kernel(x, ln_g, ln_b, w_in_even, w_out_even, pool_w, pool_scale, sconv_w, sconv_b, w_in_odd, w_out_odd, sgu_ln_g, sgu_ln_b, sgu_w, sgu_b, dconv_w, dconv_b, dnorm_g, dnorm_b, loss_target, m_ln_g, m_ln_b, m_w_in_even, m_w_out_even, m_pool_w, m_pool_scale, m_sconv_w, m_sconv_b, m_w_in_odd, m_w_out_odd, m_sgu_ln_g, m_sgu_ln_b, m_sgu_w, m_sgu_b, m_dconv_w, m_dconv_b, m_dnorm_g, m_dnorm_b, v_ln_g, v_ln_b, v_w_in_even, v_w_out_even, v_pool_w, v_pool_scale, v_sconv_w, v_sconv_b, v_w_in_odd, v_w_out_odd, v_sgu_ln_g, v_sgu_ln_b, v_sgu_w, v_sgu_b, v_dconv_w, v_dconv_b, v_dnorm_g, v_dnorm_b):
    given = dict(locals())
    me = 4 * lax.axis_index("x") + 2 * lax.axis_index("y") + lax.axis_index("c")
    xs = x[0]
    target = loss_target[0]

    wc_in = cast_weights(w_in_even, w_in_odd, "cast_w_in")
    wc_out = cast_weights(w_out_even, w_out_odd, "cast_w_out")
    shard_shapes = [_shard_shape(shp, ax) for _, shp, ax in SHARDED_SMALL]
    srows = _rows_for(shard_shapes)
    xb, wg_first, gathered = cast_x(xs, wc_in, _pack([given[n] for n, _, _ in SHARDED_SMALL], srows))
    wg_in = [wg_first] + [None] * (DEPTH - 1)
    wg_out = [None] * DEPTH
    gathered = gathered.reshape(NDEV, -1)
    full, o = {}, 0
    for (n, shp, ax), sshp in zip(SHARDED_SMALL, shard_shapes):
        full[n] = _merge_gathered(gathered[:, o:o + _size(sshp)].reshape((NDEV,) + sshp), shp, ax)
        o += _size(sshp)

    mask = (jnp.arange(SGU_BLOCK)[None, :] // 64) <= (jnp.arange(SGU_BLOCK)[:, None] // 64)
    ws = jnp.where(mask[None, None], sgu_w, 0.0)
    wsb = ws.astype(BF16)
    wstb = jnp.swapaxes(ws, -1, -2).astype(BF16)
    pwb = full["pool_w"].astype(BF16)

    ones = jnp.ones((1, D), F32)
    zeros = jnp.zeros((1, D), F32)

    xres, gp, bp = xs, ones, zeros
    saved, below = [], None
    for layer in range(DEPTH):
        i = layer // 2
        if layer % 2 == 0:
            params = (pwb[i], pool_scale[i][None], full["sconv_w"][i], sconv_b[i][None])
        else:
            params = (full["sgu_ln_g"][i][None], full["sgu_ln_b"][i][None], wsb[i], sgu_b[i][:, :, None],
                      full["dconv_w"][i], full["dconv_b"][i][None], full["dnorm_g"][i][None],
                      full["dnorm_b"][i][None])
        outs = list(proj_in(wg_in[layer], layer, wc_in, wc_out, xb=xb if below is None else None, below=below))
        if layer + 1 < DEPTH:
            wg_in[layer + 1] = outs.pop()
        wg_out[layer] = outs.pop()
        z = outs[0]
        if below is not None:
            xhat, rstd, xb = outs[1:4]
            saved[-1] += [xhat, rstd]
            xres, gp, bp = xhat, below[5], below[6]
        outs = mix_fwd(z, layer, params)
        ycat, extra = outs[0], tuple(outs[1:])
        saved.append([xb, z, ycat, extra])
        below = (ycat, wg_out[layer], xres, gp, bp, ln_g[layer][None], ln_b[layer][None])
    *top, loss_local = head_loss_bwd(below, target)

    gsmall = {n: [None] * shp[0] for n, shp in REPLICATED_SMALL}
    gsmall.update({n: [None] * shp[0] for n, shp, _ in SHARDED_SMALL})
    parts_in, parts_out, parts_pw = [None] * DEPTH, [None] * DEPTH, [None] * 2
    small_names = [n for n, _ in REPLICATED_SMALL] + [n for n, _, _ in SHARDED_SMALL]
    gathered_names = [n for n in small_names if n != "pool_w"]
    gathered_shapes = [shp for n, shp in REPLICATED_SMALL] + [shp for n, shp, _ in SHARDED_SMALL if n != "pool_w"]
    grows = _rows_for(gathered_shapes + [(1,)])
    pending, above = None, None
    for layer in reversed(range(DEPTH)):
        i = layer // 2
        xb, z, ycat, extra = saved[layer][:4]
        if above is None:
            dr, drb, dycat, dg, db = top
        else:
            dr, drb, dycat, dg, db = ln_bwd_dycat(*saved[layer][4:], ln_g[layer][None], wg_out[layer], layer, above)
        gsmall["ln_g"][layer], gsmall["ln_b"][layer] = dg[0], db[0]
        dwo = dw_out(ycat, drb, layer)
        own_out = pending is None
        riding = dwo if own_out else pending
        if layer % 2 == 0:
            dzb, dpw, dps, dsw, dsb, landed = mix_even_bwd(
                z, dycat, extra[0], extra[1], pwb[i], pool_scale[i][None], full["sconv_w"][i], layer, parts=riding)
            gsmall["pool_scale"][i] = dps[0]
            gsmall["sconv_w"][i], gsmall["sconv_b"][i] = dsw, dsb[0]
        else:
            dzb, dslg, dslb, dws, dsbc, ddw, ddcb, ddng, ddnb, landed = mix_odd_bwd(
                z, dycat, extra[0], full["sgu_ln_g"][i][None], full["sgu_ln_b"][i][None], wsb[i], wstb[i],
                sgu_b[i][:, :, None], full["dconv_w"][i], full["dnorm_g"][i][None], full["dnorm_b"][i][None],
                layer, riding)
            gsmall["sgu_ln_g"][i], gsmall["sgu_ln_b"][i] = dslg[0], dslb[0]
            gsmall["sgu_w"][i], gsmall["sgu_b"][i] = jnp.where(mask[None], dws, 0.0), dsbc[:, :, 0]
            gsmall["dconv_w"][i], gsmall["dconv_b"][i] = ddw, ddcb[0]
            gsmall["dnorm_g"][i], gsmall["dnorm_b"][i] = ddng[0], ddnb[0]
        riders = [] if own_out else [dwo]
        if layer % 2 == 0:
            riders.append(dpw.reshape(4, NDEV, PG // NDEV, PG).transpose(1, 0, 2, 3))
        small_partial = []
        if layer == 0:
            small_partial.append(_pack([jnp.stack(gsmall[n]) for n in gathered_names] + [loss_local], grows))
        outs = list(dw_in(xb, dzb, layer, riders, small_partial))
        if layer == 0:
            small_gathered = outs.pop()
        if layer % 2 == 0:
            parts_pw[i] = outs.pop()
        if own_out:
            parts_out[layer] = landed
        else:
            parts_in[layer + 1] = landed
            parts_out[layer] = outs[1]
        pending = outs[0]
        above = (dzb, wg_in[layer], dr)
    dxn, parts_in[0] = dx_in(dzb, wg_in[0], dr, pending)
    grad_x = dxn[None]

    big = {}
    big["w_in_even"] = adamw_big(w_in_even, m_w_in_even, v_w_in_even, parts_in[0], parts_in[2], "adamw_w_in_even")
    big["w_in_odd"] = adamw_big(w_in_odd, m_w_in_odd, v_w_in_odd, parts_in[1], parts_in[3], "adamw_w_in_odd")
    big["w_out_even"] = adamw_big(w_out_even, m_w_out_even, v_w_out_even, parts_out[0], parts_out[2], "adamw_w_out_even")
    big["w_out_odd"] = adamw_big(w_out_odd, m_w_out_odd, v_w_out_odd, parts_out[1], parts_out[3], "adamw_w_out_odd")

    gsum = sum_parts(small_gathered, "sum_small_grads")
    unpacked = _unpack(gsum.reshape(-1), gathered_shapes + [()])
    loss = unpacked.pop()
    gfull = dict(zip(gathered_names, unpacked))
    own_shapes = [shp for _, shp in REPLICATED_SMALL] + shard_shapes
    gown = {n: gfull[n] for n, _ in REPLICATED_SMALL}
    for (n, shp, ax), sshp in zip(SHARDED_SMALL, shard_shapes):
        if n == "pool_w":
            gown[n] = jnp.stack([sum_parts(p.reshape(NDEV, -1, 128), f"sum_pool_w_{j}").reshape(sshp[1:])
                                 for j, p in enumerate(parts_pw)])
        else:
            gown[n] = lax.dynamic_slice_in_dim(gfull[n], me * sshp[ax], sshp[ax], axis=ax)
    orows = _rows_for(own_shapes)
    packed = [_pack([src[n] for n in small_names], orows) for src in
              (given, gown, {n: given["m_" + n] for n in small_names}, {n: given["v_" + n] for n in small_names})]
    sd, sm, sv = adamw_small(*packed, "adamw_small")
    small = {}
    for n, d_, m_, v_ in zip(small_names, _unpack(sd.reshape(-1), own_shapes), _unpack(sm.reshape(-1), own_shapes),
                             _unpack(sv.reshape(-1), own_shapes)):
        small[n] = (gown[n], d_, m_, v_)

    weights = ['ln_g', 'ln_b', 'w_in_even', 'w_out_even', 'pool_w', 'pool_scale', 'sconv_w', 'sconv_b', 'w_in_odd',
               'w_out_odd', 'sgu_ln_g', 'sgu_ln_b', 'sgu_w', 'sgu_b', 'dconv_w', 'dconv_b', 'dnorm_g', 'dnorm_b']
    res = {n: (big[n] if n in big else small[n]) for n in weights}
    return (loss, grad_x, *[res[n][0] for n in weights], *[res[n][1] for n in weights],
            *[res[n][2] for n in weights], *[res[n][3] for n in weights])
```

```python
import jax
import jax.numpy as jnp
from jax import lax
from jax.experimental import pallas as pl
from jax.experimental.pallas import tpu as pltpu

F32 = jnp.float32
BF16 = jnp.bfloat16

D = 1024
DMIX = 2048
NIN = 6144
NDEV = 8
CW = NIN // NDEV
RW = DMIX // NDEV
DEPTH = 4
ALPHA = (2 * DEPTH) ** 0.25
LN_EPS = 1e-5
POOL_WINDOWS = (2, 4, 8, 16)
PG = 256
SGU_BLOCK = 128
HEADS = 4
HD = 256
KD = 31
KS = 3
SUB = 8
RC = 16
CB = 256
HP = 32
HS = 8
HC = 32
NSEM = 7
NBUF = 3

ADAM_LR = 0.001
ADAM_B1 = 0.9
ADAM_B2 = 0.999
ADAM_EPS = 1e-08
ADAM_WD = 0.01
ADAM_STEP = 10

VMEM_LIMIT = 56 * 1024 * 1024
MESH = pl.DeviceIdType.MESH
ANY = pl.BlockSpec(memory_space=pl.ANY)


def _cp(sem=None):
    if sem is None:
        return pltpu.CompilerParams(vmem_limit_bytes=VMEM_LIMIT)
    return pltpu.CompilerParams(dimension_semantics=sem, vmem_limit_bytes=VMEM_LIMIT)


def _sig(x):
    return 0.5 * jnp.tanh(0.5 * x) + 0.5


def _silu(x):
    h = 0.5 * x
    return h + h * jnp.tanh(h)


def _dsilu(sg, sil):
    return sg + sil * (1.0 - sg)


def _dot(a, b):
    return jnp.dot(a, b, preferred_element_type=F32)


def _dot_nt(a, b):
    return lax.dot_general(a, b, (((1,), (1,)), ((), ())), preferred_element_type=F32)


def _dot_tn(a, b):
    return lax.dot_general(a, b, (((0,), (0,)), ((), ())), preferred_element_type=F32)


def _rowsum(x):
    return jnp.sum(x, axis=0, keepdims=True)


def _comm_scratch(n):
    return [pltpu.SemaphoreType.DMA((n * NSEM,)), pltpu.SemaphoreType.DMA((n * NSEM,)),
            pltpu.SemaphoreType.DMA((n,))]


def _gather_plan(src_ref, out_ref, sems, n):
    send_sems, recv_sems, local_sems = sems
    base = n * NSEM
    x, y, c = lax.axis_index("x"), lax.axis_index("y"), lax.axis_index("c")
    me, sibling = (x, y, c), (x, y, 1 - c)
    chips = [(1 - x, y), (x, 1 - y), (1 - x, 1 - y)]

    def slot(px, py, pc):
        return out_ref.at[4 * px + 2 * py + pc]

    def copy(k, blk, to, src=None):
        return pltpu.make_async_remote_copy(
            src_ref=slot(*blk) if src is None else src, dst_ref=slot(*blk),
            send_sem=send_sems.at[base + k], recv_sem=recv_sems.at[base + k],
            device_id=to, device_id_type=MESH)

    def mine():
        return pltpu.make_async_copy(src_ref, slot(*me), local_sems.at[n])

    def start():
        mine().start()
        copy(0, me, sibling, src=src_ref).start()
        for j, chip in enumerate(chips):
            copy(1 + j, me, (*chip, c), src=src_ref).start()

    def finish():
        for j, chip in enumerate(chips):
            copy(1 + j, (*chip, c), me).wait_recv()
            copy(4 + j, (*chip, c), sibling).start()
        copy(0, sibling, me).wait_recv()
        for j, chip in enumerate(chips):
            copy(4 + j, (*chip, 1 - c), me).wait_recv()
        copy(0, me, sibling, src=src_ref).wait_send()
        for j, chip in enumerate(chips):
            copy(1 + j, me, (*chip, c), src=src_ref).wait_send()
            copy(4 + j, (*chip, c), sibling).wait_send()
        mine().wait()

    return start, finish


def _exchange_plan(p_ref, out_ref, sems, n):
    send_sems, recv_sems, local_sems = sems
    base = n * NSEM
    x, y, c = lax.axis_index("x"), lax.axis_index("y"), lax.axis_index("c")
    me = 4 * x + 2 * y + c

    def copy(r, landing):
        px, py, pc = x ^ (r >> 2), y ^ ((r >> 1) & 1), c ^ (r & 1)
        peer = 4 * px + 2 * py + pc
        return pltpu.make_async_remote_copy(
            src_ref=p_ref.at[peer], dst_ref=out_ref.at[peer if landing else me],
            send_sem=send_sems.at[base + r - 1], recv_sem=recv_sems.at[base + r - 1],
            device_id=(px, py, pc), device_id_type=MESH)

    def mine():
        return pltpu.make_async_copy(p_ref.at[me], out_ref.at[me], local_sems.at[n])

    def start():
        mine().start()
        for r in range(1, NDEV):
            copy(r, False).start()

    def finish():
        for r in range(1, NDEV):
            copy(r, True).wait_recv()
        for r in range(1, NDEV):
            copy(r, False).wait_send()
        mine().wait()

    return start, finish


def cast_x(x, w_first, small):
    s = x.shape[0]
    tm = min(s, 512)
    nt = s // tm

    def body(x_ref, wf_ref, sm_ref, o_ref, gw_ref, gs_ref, *sems):
        i = pl.program_id(0)
        plans = [_gather_plan(wf_ref.at[0], gw_ref, sems, 0), _gather_plan(sm_ref, gs_ref, sems, 1)]

        @pl.when(i == 0)
        def _():
            for start, _ in plans:
                start()

        o_ref[...] = x_ref[...].astype(BF16)

        @pl.when(i == nt - 1)
        def _():
            for _, finish in plans:
                finish()

    return pl.pallas_call(
        body, name="cast_x", grid=(nt,),
        in_specs=[pl.BlockSpec((tm, D), lambda i: (i, 0)), ANY, ANY],
        out_specs=[pl.BlockSpec((tm, D), lambda i: (i, 0)), ANY, ANY],
        out_shape=[jax.ShapeDtypeStruct((s, D), BF16),
                   jax.ShapeDtypeStruct((NDEV,) + w_first.shape[1:], w_first.dtype),
                   jax.ShapeDtypeStruct((NDEV,) + small.shape, small.dtype)],
        scratch_shapes=_comm_scratch(2), compiler_params=_cp(("arbitrary",)),
    )(x, w_first, small)


def cast_weights(w_even, w_odd, name):
    _, r, c = w_even.shape

    def body(e_ref, o_ref, out_ref):
        layer = pl.program_id(0)

        @pl.when(layer % 2 == 0)
        def _():
            out_ref[...] = e_ref[...].astype(BF16)

        @pl.when(layer % 2 == 1)
        def _():
            out_ref[...] = o_ref[...].astype(BF16)

    spec = pl.BlockSpec((None, r, c), lambda l: (l // 2, 0, 0))
    return pl.pallas_call(
        body, name=name, grid=(DEPTH,), in_specs=[spec, spec],
        out_specs=pl.BlockSpec((None, r, c), lambda l: (l, 0, 0)),
        out_shape=jax.ShapeDtypeStruct((DEPTH, r, c), BF16), compiler_params=_cp(("parallel",)),
    )(w_even, w_odd)


def _deepnorm(y_ref, wo_ref, xr_ref, gp_ref, bp_ref, g_ref, b_ref):
    xin = xr_ref[...] * gp_ref[...] + bp_ref[...]
    r = ALPHA * xin + _dot(y_ref[...], wo_ref[...].reshape(DMIX, D))
    mu = jnp.mean(r, axis=-1, keepdims=True)
    d = r - mu
    var = jnp.mean(d * d, axis=-1, keepdims=True)
    rstd = lax.rsqrt(var + LN_EPS)
    xh = d * rstd
    return xh, rstd, xh * g_ref[...] + b_ref[...]


def proj_in(wg, layer, wc_in, wc_out, xb=None, below=None):
    fused = below is not None
    s = below[0].shape[0] if fused else xb.shape[0]
    tm = min(s, 512)
    nt = s // tm
    nxt = layer + 1 < DEPTH
    nhead = 7 if fused else 1

    def body(*refs):
        head, refs = refs[:nhead], refs[nhead:]
        if nxt:
            w_ref, wco_ref, wci_ref = refs[:3]
            refs = refs[3:]
        else:
            w_ref, wco_ref = refs[:2]
            refs = refs[2:]
        o_ref = refs[0]
        if fused:
            xh_ref, rs_ref, xb_ref = refs[1:4]
            refs = refs[4:]
        else:
            refs = refs[1:]
        go_ref = refs[0]
        sems = refs[-3:]
        i = pl.program_id(0)
        plans = [_gather_plan(wco_ref.at[layer], go_ref, sems, 0)]
        if nxt:
            plans.append(_gather_plan(wci_ref.at[layer + 1], refs[1], sems, 1))

        @pl.when(i == 0)
        def _():
            for start, _ in plans:
                start()

        if fused:
            xh, rstd, xn = _deepnorm(*head)
            xh_ref[...] = xh
            rs_ref[...] = rstd
            x = xn.astype(BF16)
            xb_ref[...] = x
        else:
            x = head[0][...]
        for k in range(NDEV):
            o_ref[:, k * CW:(k + 1) * CW] = _dot(x, w_ref[k]).astype(BF16)

        @pl.when(i == nt - 1)
        def _():
            for _, finish in plans:
                finish()

    row = pl.BlockSpec((tm, D), lambda i: (i, 0))
    vec = pl.BlockSpec((1, D), lambda i: (0, 0))
    if fused:
        in_specs = [pl.BlockSpec((tm, DMIX), lambda i: (i, 0)),
                    pl.BlockSpec((NDEV, RW, D), lambda i: (0, 0, 0), pipeline_mode=pl.Buffered(1)),
                    row, vec, vec, vec, vec]
        args = list(below)
    else:
        in_specs, args = [row], [xb]
    in_specs += [pl.BlockSpec((NDEV, D, CW), lambda i: (0, 0, 0), pipeline_mode=pl.Buffered(1)), ANY]
    args += [wg, wc_out]
    if nxt:
        in_specs.append(ANY)
        args.append(wc_in)
    out_specs = [pl.BlockSpec((tm, NIN), lambda i: (i, 0))]
    out_shape = [jax.ShapeDtypeStruct((s, NIN), BF16)]
    if fused:
        out_specs += [row, pl.BlockSpec((tm, 1), lambda i: (i, 0)), row]
        out_shape += [jax.ShapeDtypeStruct((s, D), F32), jax.ShapeDtypeStruct((s, 1), F32),
                      jax.ShapeDtypeStruct((s, D), BF16)]
    out_specs.append(ANY)
    out_shape.append(jax.ShapeDtypeStruct((NDEV, RW, D), BF16))
    if nxt:
        out_specs.append(ANY)
        out_shape.append(jax.ShapeDtypeStruct((NDEV, D, CW), BF16))
    return pl.pallas_call(
        body, name=f"proj_in_{layer}", grid=(nt,), in_specs=in_specs, out_specs=out_specs,
        out_shape=out_shape, scratch_shapes=_comm_scratch(2 if nxt else 1),
        compiler_params=_cp(("arbitrary",)),
    )(*args)


def mix_fwd(z, layer, params):
    s = z.shape[0]
    tm = min(s, 512)
    even = layer % 2 == 0
    npar = len(params)
    nout = 3 if even else 2

    def even_tile(zr, j, par, outs, scr):
        pw_ref, ps_ref, sw_ref, sb_ref = par
        yc_ref, pb_ref, cv_ref = outs
        exa, eq, ya0, la, lb = scr
        for c in range(tm // RC):
            r0, r1 = c * RC, (c + 1) * RC
            exa[HP + r0:HP + r1, :] = zr[r0:r1, _cols(0)].astype(F32)
            eq[HS + r0:HS + r1, :] = zr[r0:r1, _cols(4)].astype(F32) * zr[r0:r1, _cols(2)].astype(F32)

        def level(src, dst, shift, c0, start):
            for r0 in range(start, HP + tm, 2 * RC):
                r1 = min(r0 + 2 * RC, HP + tm)
                dst[r0:r1, c0:D] = src[r0:r1, c0:D] + src[r0 - shift:r1 - shift, c0:D]

        level(exa, la, 1, PG, 8)
        level(la, lb, 2, 2 * PG, 16)
        level(lb, la, 4, 3 * PG, 24)
        last = ((exa, 1), (la, 2), (lb, 4), (la, 8))
        for c in range(tm // RC):
            r0, r1 = HP + c * RC, HP + (c + 1) * RC
            pos = (j * tm + c * RC + lax.broadcasted_iota(jnp.int32, (RC, 1), 0) + 1).astype(F32)
            for gi, w in enumerate(POOL_WINDOWS):
                c0, c1 = gi * PG, (gi + 1) * PG
                src, shift = last[gi]
                acc = src[r0:r1, c0:c1] + src[r0 - shift:r1 - shift, c0:c1]
                pooled = acc / jnp.minimum(pos, float(w)) - exa[r0:r1, c0:c1]
                pb_ref[c * RC:(c + 1) * RC, c0:c1] = pooled.astype(BF16)
        for gi in range(len(POOL_WINDOWS)):
            c0, c1 = gi * PG, (gi + 1) * PG
            ya0[:, c0:c1] = _dot(pb_ref[:, c0:c1], pw_ref[gi])
        for c in range(tm // RC):
            r0, r1 = c * RC, (c + 1) * RC
            ga = zr[r0:r1, _cols(1)].astype(F32)
            yc_ref[r0:r1, 0:D] = (ya0[r0:r1, :] * ps_ref[...] * (ga * _sig(ga))).astype(BF16)
            cv = (sw_ref[2:3, :] * eq[HS + r0:HS + r1, :] + sw_ref[1:2, :] * eq[HS + r0 - 1:HS + r1 - 1, :]
                  + sw_ref[0:1, :] * eq[HS + r0 - 2:HS + r1 - 2, :] + sb_ref[...])
            cv_ref[r0:r1, :] = cv.astype(BF16)
            gb = zr[r0:r1, _cols(5)].astype(F32)
            yc_ref[r0:r1, D:2 * D] = (zr[r0:r1, _cols(3)].astype(F32) * cv * (gb * _sig(gb))).astype(BF16)
        exa[0:HP, :] = exa[tm:tm + HP, :]
        eq[0:HS, :] = eq[tm:tm + HS, :]

    def odd_tile(zr, j, par, outs, scr):
        slg_ref, slb_ref, ws_ref, sb_ref, dw_ref, dcb_ref, dng_ref, dnb_ref = par
        yc_ref, cz_ref = outs
        ezg, esh = scr
        vh, _ = _ln_rows(zr[:, _cols(1)].astype(F32))
        vnb = (vh * slg_ref[...] + slb_ref[...]).astype(BF16)
        for n in range(tm // SGU_BLOCK):
            r0, r1 = n * SGU_BLOCK, (n + 1) * SGU_BLOCK
            for hd in range(HEADS):
                c0, c1 = hd * HD, (hd + 1) * HD
                sv = _dot(ws_ref[hd], vnb[r0:r1, c0:c1]) + sb_ref[hd]
                gc = zr[r0:r1, 2 * D + c0:2 * D + c1].astype(F32)
                yc_ref[r0:r1, c0:c1] = (zr[r0:r1, c0:c1].astype(F32) * sv * _silu(gc)).astype(BF16)
        ezg[HC:HC + tm, :] = zr[:, _cols(3)].astype(F32) * _sig(zr[:, _cols(4)].astype(F32))
        _shifted_copies(ezg, esh, tm + HC - SUB)
        cz = jnp.zeros((tm, D), F32) + dcb_ref[...]
        for k in range(KD):
            cz = cz + dw_ref[k:k + 1, :] * _tap(ezg, esh, HC - (KD - 1) + k, 0, tm)
        cz_ref[...] = cz
        zh, _ = _ln_rows(cz)
        zn = zh * dng_ref[...] + dnb_ref[...]
        gd = zr[:, _cols(5)].astype(F32)
        yc_ref[:, D:2 * D] = (_silu(zn) * _silu(gd)).astype(BF16)
        ezg[0:HC, :] = ezg[tm:tm + HC, :]

    nt = s // tm
    nbuf = min(NBUF, nt)

    def body(z_hbm, *refs):
        par, outs, scr = refs[:npar], refs[npar:npar + nout], refs[npar + nout:-2]
        zbuf, zsem = refs[-2:]
        j = pl.program_id(0)

        def fetch(tile, slot):
            return pltpu.make_async_copy(z_hbm.at[pl.ds(tile * tm, tm), :], zbuf.at[slot], zsem.at[slot])

        @pl.when(j == 0)
        def _():
            for t in range(nbuf):
                fetch(t, t).start()
            if even:
                scr[0][0:HP, :] = jnp.zeros((HP, D), F32)
                scr[1][0:HS, :] = jnp.zeros((HS, D), F32)
            else:
                scr[0][0:HC, :] = jnp.zeros((HC, D), F32)

        slot = j % nbuf
        fetch(j, slot).wait()
        (even_tile if even else odd_tile)(zbuf.at[slot], j, par, outs, scr)

        @pl.when(j + nbuf < nt)
        def _():
            fetch(j + nbuf, slot).start()

    row = lambda c: pl.BlockSpec((tm, c), lambda j: (j, 0))
    vec = pl.BlockSpec((1, D), lambda j: (0, 0))
    if even:
        par_specs = [pl.BlockSpec((4, PG, PG), lambda j: (0, 0, 0)), vec, pl.BlockSpec((KS, D), lambda j: (0, 0)), vec]
        out_specs = [row(DMIX), row(D), row(D)]
        out_shape = [jax.ShapeDtypeStruct((s, DMIX), BF16), jax.ShapeDtypeStruct((s, D), BF16),
                     jax.ShapeDtypeStruct((s, D), BF16)]
        scratch = [pltpu.VMEM((HP + tm, D), F32), pltpu.VMEM((HS + tm, D), F32), pltpu.VMEM((tm, D), F32),
                   pltpu.VMEM((HP + tm, D), F32), pltpu.VMEM((HP + tm, D), F32)]
    else:
        par_specs = [vec, vec, pl.BlockSpec((HEADS, SGU_BLOCK, SGU_BLOCK), lambda j: (0, 0, 0)),
                     pl.BlockSpec((HEADS, SGU_BLOCK, 1), lambda j: (0, 0, 0)),
                     pl.BlockSpec((KD, D), lambda j: (0, 0)), vec, vec, vec]
        out_specs = [row(DMIX), row(D)]
        out_shape = [jax.ShapeDtypeStruct((s, DMIX), BF16), jax.ShapeDtypeStruct((s, D), F32)]
        scratch = [pltpu.VMEM((HC + tm, D), F32), pltpu.VMEM((SUB - 1, tm + HC - SUB, D), F32)]
    return pl.pallas_call(
        body, name=f"mix_fwd_{layer}", grid=(nt,), in_specs=[ANY] + par_specs, out_specs=out_specs,
        out_shape=out_shape,
        scratch_shapes=scratch + [pltpu.VMEM((nbuf, tm, NIN), BF16), pltpu.SemaphoreType.DMA((nbuf,))],
        compiler_params=_cp(("arbitrary",)),
    )(z, *params)


def _zero_at_start(*refs):
    @pl.when(pl.program_id(0) == 0)
    def _():
        for ref in refs:
            ref[...] = jnp.zeros_like(ref)


def _ln_bwd_tile(dxo, xh, rstd, g_ref, wo_ref, dr_ref, drb_ref, dyc_ref, dg_ref, db_ref):
    dg_ref[...] += _rowsum(dxo * xh)
    db_ref[...] += _rowsum(dxo)
    dr = _ln_rows_bwd(dxo, xh, rstd, g_ref[...])
    dr_ref[...] = dr
    drb = dr.astype(BF16)
    drb_ref[...] = drb
    dyc_ref[...] = _dot_nt(drb, wo_ref[...].reshape(DMIX, D)).astype(BF16)


def _ln_bwd_outs(s, tm):
    row = pl.BlockSpec((tm, D), lambda i: (i, 0))
    vec = pl.BlockSpec((1, D), lambda i: (0, 0))
    return ([row, row, pl.BlockSpec((tm, DMIX), lambda i: (i, 0)), vec, vec],
            [jax.ShapeDtypeStruct((s, D), F32), jax.ShapeDtypeStruct((s, D), BF16),
             jax.ShapeDtypeStruct((s, DMIX), BF16), jax.ShapeDtypeStruct((1, D), F32),
             jax.ShapeDtypeStruct((1, D), F32)])


def head_loss_bwd(below, target):
    s = target.shape[0]
    tm = min(s, 512)

    def body(*refs):
        t_ref = refs[7]
        loss_ref = refs[-1]
        _zero_at_start(*refs[11:14])
        xh, rstd, xn = _deepnorm(*refs[:7])
        err = xn - t_ref[...]
        loss_ref[...] += 0.5 * jnp.sum(jnp.mean(err * err, axis=-1, keepdims=True), axis=0, keepdims=True)
        _ln_bwd_tile(err * (1.0 / D), xh, rstd, refs[5], refs[1], *refs[8:13])

    row = pl.BlockSpec((tm, D), lambda i: (i, 0))
    vec = pl.BlockSpec((1, D), lambda i: (0, 0))
    out_specs, out_shape = _ln_bwd_outs(s, tm)
    return pl.pallas_call(
        body, name="head_loss_bwd", grid=(s // tm,),
        in_specs=[pl.BlockSpec((tm, DMIX), lambda i: (i, 0)), pl.BlockSpec((NDEV, RW, D), lambda i: (0, 0, 0)),
                  row, vec, vec, vec, vec, row],
        out_specs=out_specs + [pl.BlockSpec((1, 1), lambda i: (0, 0))],
        out_shape=out_shape + [jax.ShapeDtypeStruct((1, 1), F32)],
        compiler_params=_cp(("arbitrary",)),
    )(*below, target)


def ln_bwd_dycat(xhat, rstd, g, wog, layer, upstream):
    s = xhat.shape[0]
    tm = min(s, 512)

    def body(dz_ref, wi_ref, dru_ref, xh_ref, rs_ref, g_ref, w_ref, *outs):
        _zero_at_start(*outs[3:5])
        dxo = ALPHA * dru_ref[...]
        for k in range(NDEV):
            dxo += _dot_nt(dz_ref[:, k * CW:(k + 1) * CW], wi_ref[k])
        _ln_bwd_tile(dxo, xh_ref[...], rs_ref[...], g_ref, w_ref, *outs)

    row = pl.BlockSpec((tm, D), lambda i: (i, 0))
    vec = pl.BlockSpec((1, D), lambda i: (0, 0))
    out_specs, out_shape = _ln_bwd_outs(s, tm)
    return pl.pallas_call(
        body, name=f"ln_bwd_dycat_{layer}", grid=(s // tm,),
        in_specs=[pl.BlockSpec((tm, NIN), lambda i: (i, 0)),
                  pl.BlockSpec((NDEV, D, CW), lambda i: (0, 0, 0), pipeline_mode=pl.Buffered(1)), row,
                  row, pl.BlockSpec((tm, 1), lambda i: (i, 0)), vec,
                  pl.BlockSpec((NDEV, RW, D), lambda i: (0, 0, 0), pipeline_mode=pl.Buffered(1))],
        out_specs=out_specs, out_shape=out_shape, compiler_params=_cp(("arbitrary",)),
    )(*upstream, xhat, rstd, g, wog)


def dx_in(dzb, wg, dr, parts):
    s = dzb.shape[0]
    tm = min(s, 256)
    nt = s // tm

    def body(dz_ref, w_ref, dr_ref, p_ref, o_ref, land_ref, *sems):
        start, finish = _exchange_plan(p_ref, land_ref, sems, 0)
        pl.when(pl.program_id(0) == 0)(start)
        acc = ALPHA * dr_ref[...]
        for k in range(NDEV):
            acc += _dot_nt(dz_ref[:, k * CW:(k + 1) * CW], w_ref[k])
        o_ref[...] = acc
        pl.when(pl.program_id(0) == nt - 1)(finish)

    row = pl.BlockSpec((tm, D), lambda i: (i, 0))
    return pl.pallas_call(
        body, name="dx_in_0", grid=(nt,),
        in_specs=[pl.BlockSpec((tm, NIN), lambda i: (i, 0)),
                  pl.BlockSpec((NDEV, D, CW), lambda i: (0, 0, 0), pipeline_mode=pl.Buffered(1)), row, ANY],
        out_specs=[row, ANY],
        out_shape=[jax.ShapeDtypeStruct((s, D), F32), jax.ShapeDtypeStruct(parts.shape, parts.dtype)],
        scratch_shapes=_comm_scratch(1), compiler_params=_cp(("arbitrary",)),
    )(dzb, wg, dr, parts)


def dw_in(xb, dzb, layer, parts=(), gathers=()):
    s = xb.shape[0]
    tm = min(s, 4096)
    nt = s // tm
    ne = len(parts)
    n = ne + len(gathers)

    def body(*refs):
        x_ref, dz_ref = refs[:2]
        p_refs = refs[2:2 + n]
        o_ref = refs[2 + n]
        land_refs = refs[3 + n:3 + 2 * n]
        acc = refs[3 + 2 * n]
        plans = [(_exchange_plan if j < ne else _gather_plan)(p, land, refs[-3:], j)
                 for j, (p, land) in enumerate(zip(p_refs, land_refs))]
        k, t = pl.program_id(0), pl.program_id(1)

        @pl.when((k == 0) & (t == 0))
        def _():
            for start, _ in plans:
                start()

        @pl.when(t == 0)
        def _():
            acc[...] = jnp.zeros_like(acc)

        acc[...] += _dot_tn(x_ref[...], dz_ref[...])

        @pl.when(t == nt - 1)
        def _():
            o_ref[...] = acc[...].astype(BF16)

        @pl.when((k == NDEV - 1) & (t == nt - 1))
        def _():
            for _, finish in plans:
                finish()

    return pl.pallas_call(
        body, name=f"dw_in_{layer}", grid=(NDEV, nt),
        in_specs=[pl.BlockSpec((tm, D), lambda k, t: (t, 0)), pl.BlockSpec((tm, CW), lambda k, t: (t, k))] + [ANY] * n,
        out_specs=[pl.BlockSpec((None, D, CW), lambda k, t: (k, 0, 0))] + [ANY] * n,
        out_shape=[jax.ShapeDtypeStruct((NDEV, D, CW), BF16)] + [jax.ShapeDtypeStruct(p.shape, p.dtype) for p in parts]
        + [jax.ShapeDtypeStruct((NDEV,) + g.shape, g.dtype) for g in gathers],
        scratch_shapes=[pltpu.VMEM((D, CW), F32)] + (_comm_scratch(n) if n else []),
        compiler_params=_cp(("arbitrary", "arbitrary")),
    )(xb, dzb, *parts, *gathers)


def dw_out(ycat, drb, layer):
    s = ycat.shape[0]
    tm = min(s, 1024)
    nt = s // tm

    def body(y_ref, dr_ref, o_ref, acc):
        t = pl.program_id(0)

        @pl.when(t == 0)
        def _():
            acc[...] = jnp.zeros_like(acc)

        acc[...] += _dot_tn(y_ref[...], dr_ref[...])

        @pl.when(t == nt - 1)
        def _():
            o_ref[...] = acc[...].reshape(NDEV, RW, D).astype(BF16)

    return pl.pallas_call(
        body, name=f"dw_out_{layer}", grid=(nt,),
        in_specs=[pl.BlockSpec((tm, DMIX), lambda t: (t, 0)), pl.BlockSpec((tm, D), lambda t: (t, 0))],
        out_specs=pl.BlockSpec((NDEV, RW, D), lambda t: (0, 0, 0)),
        out_shape=jax.ShapeDtypeStruct((NDEV, RW, D), BF16),
        scratch_shapes=[pltpu.VMEM((DMIX, D), F32)],
        compiler_params=_cp(("arbitrary",)),
    )(ycat, drb)


def _cols(j):
    return slice(j * D, (j + 1) * D)


def mix_even_bwd(z, dycat, pooled, cv, pwb, pscale, sw, layer, parts=None):
    s = z.shape[0]
    tm = min(s, 256)
    nt = s // tm
    carry = parts is not None

    def body(*refs):
        z_ref, dy_ref, pb_ref, cv_ref, pw_ref, ps_ref, sw_ref = refs[:7]
        if carry:
            p_ref = refs[7]
            dz_ref, dpw_ref, dps_ref, dsw_ref, dsb_ref, land_ref = refs[8:14]
            edp, edc, ya0, t1s, dpl, la, lb = refs[14:21]
            start, finish = _exchange_plan(p_ref, land_ref, refs[-3:], 0)
        else:
            dz_ref, dpw_ref, dps_ref, dsw_ref, dsb_ref = refs[7:12]
            edp, edc, ya0, t1s, dpl, la, lb = refs[12:19]
        i = pl.program_id(0)
        if carry:
            pl.when(i == 0)(start)

        @pl.when(i == 0)
        def _():
            edp[tm:tm + HP, :] = jnp.zeros((HP, D), F32)
            edc[tm:tm + HS, :] = jnp.zeros((HS, D), F32)
            dpw_ref[...] = jnp.zeros_like(dpw_ref)
            dps_ref[...] = jnp.zeros_like(dps_ref)
            dsw_ref[...] = jnp.zeros_like(dsw_ref)
            dsb_ref[...] = jnp.zeros_like(dsb_ref)

        blocks = [(r * RC, (r + 1) * RC, c * CB, (c + 1) * CB) for r in range(tm // RC) for c in range(D // CB)]
        for gi in range(len(POOL_WINDOWS)):
            c0, c1 = gi * PG, (gi + 1) * PG
            ya0[:, c0:c1] = _dot(pb_ref[:, c0:c1], pw_ref[gi])
        for r0, r1, c0, c1 in blocks:
            ga = z_ref[r0:r1, D + c0:D + c1].astype(F32)
            sg = _sig(ga)
            sil = ga * sg
            dya = dy_ref[r0:r1, c0:c1].astype(F32)
            y0 = ya0[r0:r1, c0:c1]
            ps = ps_ref[:, c0:c1]
            dps_ref[:, c0:c1] += _rowsum(dya * y0 * sil)
            dz_ref[r0:r1, D + c0:D + c1] = (dya * y0 * ps * (sg * (1.0 + ga * (1.0 - sg)))).astype(BF16)
            t1s[r0:r1, c0:c1] = (dya * ps * sil).astype(BF16)
        for gi in range(len(POOL_WINDOWS)):
            c0, c1 = gi * PG, (gi + 1) * PG
            dpl[:, c0:c1] = _dot_nt(t1s[:, c0:c1], pw_ref[gi])
            dpw_ref[gi] += _dot_tn(pb_ref[:, c0:c1], t1s[:, c0:c1])
        for r in range(tm // RC):
            r0, r1 = r * RC, (r + 1) * RC
            pos = ((nt - 1 - i) * tm + r0 + lax.broadcasted_iota(jnp.int32, (RC, 1), 0) + 1).astype(F32)
            for gi, w in enumerate(POOL_WINDOWS):
                c0, c1 = gi * PG, (gi + 1) * PG
                edp[r0:r1, c0:c1] = dpl[r0:r1, c0:c1] / jnp.minimum(pos, float(w))

        def level(src, dst, shift, c0, end):
            for r0 in range(0, end, 2 * RC):
                r1 = min(r0 + 2 * RC, end)
                dst[r0:r1, c0:D] = src[r0:r1, c0:D] + src[r0 + shift:r1 + shift, c0:D]

        level(edp, la, 1, PG, tm + 24)
        level(la, lb, 2, 2 * PG, tm + 16)
        level(lb, la, 4, 3 * PG, tm + 8)
        last = ((edp, 1), (la, 2), (lb, 4), (la, 8))
        for r in range(tm // RC):
            r0, r1 = r * RC, (r + 1) * RC
            for gi in range(len(POOL_WINDOWS)):
                c0, c1 = gi * PG, (gi + 1) * PG
                src, shift = last[gi]
                acc = src[r0:r1, c0:c1] + src[r0 + shift:r1 + shift, c0:c1] - dpl[r0:r1, c0:c1]
                dz_ref[r0:r1, c0:c1] = acc.astype(BF16)

        for r0, r1, c0, c1 in blocks:
            gb = z_ref[r0:r1, 5 * D + c0:5 * D + c1].astype(F32)
            sgb = _sig(gb)
            silb = gb * sgb
            dyb = dy_ref[r0:r1, D + c0:D + c1].astype(F32)
            cvv = cv_ref[r0:r1, c0:c1].astype(F32)
            bg = z_ref[r0:r1, 3 * D + c0:3 * D + c1].astype(F32)
            dz_ref[r0:r1, 3 * D + c0:3 * D + c1] = (dyb * cvv * silb).astype(BF16)
            dz_ref[r0:r1, 5 * D + c0:5 * D + c1] = (dyb * bg * cvv * (sgb * (1.0 + gb * (1.0 - sgb)))).astype(BF16)
            dcv = dyb * bg * silb
            dsb_ref[:, c0:c1] += _rowsum(dcv)
            edc[r0:r1, c0:c1] = dcv
        for r0, r1, c0, c1 in blocks:
            h = z_ref[r0:r1, 2 * D + c0:2 * D + c1].astype(F32)
            cg = z_ref[r0:r1, 4 * D + c0:4 * D + c1].astype(F32)
            q = cg * h
            d0 = edc[r0:r1, c0:c1]
            d1 = edc[r0 + 1:r1 + 1, c0:c1]
            d2 = edc[r0 + 2:r1 + 2, c0:c1]
            dq = sw_ref[2:3, c0:c1] * d0 + sw_ref[1:2, c0:c1] * d1 + sw_ref[0:1, c0:c1] * d2
            dsw_ref[2:3, c0:c1] += _rowsum(q * d0)
            dsw_ref[1:2, c0:c1] += _rowsum(q * d1)
            dsw_ref[0:1, c0:c1] += _rowsum(q * d2)
            dz_ref[r0:r1, 4 * D + c0:4 * D + c1] = (dq * h).astype(BF16)
            dz_ref[r0:r1, 2 * D + c0:2 * D + c1] = (dq * cg).astype(BF16)
        edp[tm:tm + HP, :] = edp[0:HP, :]
        edc[tm:tm + HS, :] = edc[0:HS, :]
        if carry:
            pl.when(i == nt - 1)(finish)

    rev = lambda i: (nt - 1 - i, 0)
    vec = pl.BlockSpec((1, D), lambda i: (0, 0))
    in_specs = [pl.BlockSpec((tm, NIN), rev), pl.BlockSpec((tm, DMIX), rev),
                pl.BlockSpec((tm, D), rev), pl.BlockSpec((tm, D), rev),
                pl.BlockSpec((4, PG, PG), lambda i: (0, 0, 0)), vec,
                pl.BlockSpec((KS, D), lambda i: (0, 0))]
    out_specs = [pl.BlockSpec((tm, NIN), rev), pl.BlockSpec((4, PG, PG), lambda i: (0, 0, 0)), vec,
                 pl.BlockSpec((KS, D), lambda i: (0, 0)), vec]
    out_shape = [jax.ShapeDtypeStruct((s, NIN), BF16), jax.ShapeDtypeStruct((4, PG, PG), F32),
                 jax.ShapeDtypeStruct((1, D), F32), jax.ShapeDtypeStruct((KS, D), F32),
                 jax.ShapeDtypeStruct((1, D), F32)]
    args = [z, dycat, pooled, cv, pwb, pscale, sw]
    scratch = [pltpu.VMEM((tm + HP, D), F32), pltpu.VMEM((tm + HS, D), F32), pltpu.VMEM((tm, D), F32),
               pltpu.VMEM((tm, D), BF16), pltpu.VMEM((tm, D), F32), pltpu.VMEM((tm + HP, D), F32),
               pltpu.VMEM((tm + HP, D), F32)]
    if carry:
        in_specs.append(ANY)
        args.append(parts)
        out_specs.append(ANY)
        out_shape.append(jax.ShapeDtypeStruct(parts.shape, parts.dtype))
        scratch += _comm_scratch(1)
    return pl.pallas_call(
        body, name=f"mix_even_bwd_{layer}", grid=(nt,), in_specs=in_specs, out_specs=out_specs,
        out_shape=out_shape, scratch_shapes=scratch, compiler_params=_cp(("arbitrary",)),
    )(*args)


def _ln_rows(v):
    mu = jnp.mean(v, axis=-1, keepdims=True)
    d = v - mu
    var = jnp.mean(d * d, axis=-1, keepdims=True)
    rstd = lax.rsqrt(var + LN_EPS)
    return d * rstd, rstd


def _ln_rows_bwd(dn, xh, rstd, g):
    dxh = dn * g
    m1 = jnp.mean(dxh, axis=-1, keepdims=True)
    m2 = jnp.mean(dxh * xh, axis=-1, keepdims=True)
    return rstd * (dxh - m1 - xh * m2)


def _shifted_copies(ext, shifted, rows):
    for b in range(1, SUB):
        shifted[b - 1] = ext[b:b + rows, :]


def _tap(ext, shifted, off, r0, r1, cols=slice(None)):
    a, b = off // SUB, off % SUB
    if b == 0:
        return ext[SUB * a + r0:SUB * a + r1, cols]
    return shifted[b - 1, SUB * a + r0:SUB * a + r1, cols]


def mix_odd_bwd(z, dycat, cz, slg, slb, wsb, wstb, sbcol, dw, dng, dnb, layer, parts):
    s = z.shape[0]
    tm = min(s, 256)
    nt = s // tm

    def body(z_ref, dy_ref, cz_ref, slg_ref, slb_ref, ws_ref, wst_ref, sb_ref, dw_ref, dng_ref, dnb_ref, p_ref,
             dz_ref, dslg_ref, dslb_ref, dws_ref, dsb_ref, ddw_ref, ddcb_ref, ddng_ref, ddnb_ref, land_ref,
             dvn, edz, esh, ddw8, *sems):
        i = pl.program_id(0)
        start, finish = _exchange_plan(p_ref, land_ref, sems, 0)
        pl.when(i == 0)(start)

        @pl.when(i == 0)
        def _():
            edz[tm:tm + HC, :] = jnp.zeros((HC, D), F32)
            for ref in (dslg_ref, dslb_ref, dws_ref, dsb_ref, ddw8, ddcb_ref, ddng_ref, ddnb_ref):
                ref[...] = jnp.zeros_like(ref)

        vh, vrs = _ln_rows(z_ref[:, _cols(1)].astype(F32))
        vnb = (vh * slg_ref[...] + slb_ref[...]).astype(BF16)
        for n in range(tm // SGU_BLOCK):
            r0, r1 = n * SGU_BLOCK, (n + 1) * SGU_BLOCK
            for hd in range(HEADS):
                c0, c1 = hd * HD, (hd + 1) * HD
                vblk = vnb[r0:r1, c0:c1]
                sv = _dot(ws_ref[hd], vblk) + sb_ref[hd]
                gc = z_ref[r0:r1, 2 * D + c0:2 * D + c1].astype(F32)
                sg = _sig(gc)
                sil = gc * sg
                u = z_ref[r0:r1, c0:c1].astype(F32)
                dyc = dy_ref[r0:r1, c0:c1].astype(F32)
                dz_ref[r0:r1, c0:c1] = (dyc * sv * sil).astype(BF16)
                dz_ref[r0:r1, 2 * D + c0:2 * D + c1] = (dyc * u * sv * _dsilu(sg, sil)).astype(BF16)
                dsv = dyc * u * sil
                dsb_ref[hd] += jnp.sum(dsv, axis=-1, keepdims=True)
                dsvb = dsv.astype(BF16)
                dws_ref[hd] += _dot_nt(dsvb, vblk)
                dvn[r0:r1, c0:c1] = _dot(wst_ref[hd], dsvb)
        dv = dvn[...]
        dslg_ref[...] += _rowsum(dv * vh)
        dslb_ref[...] += _rowsum(dv)
        dz_ref[:, _cols(1)] = _ln_rows_bwd(dv, vh, vrs, slg_ref[...]).astype(BF16)

        zh, zrs = _ln_rows(cz_ref[...])
        zn = zh * dng_ref[...] + dnb_ref[...]
        sgn = _sig(zn)
        gd = z_ref[:, _cols(5)].astype(F32)
        sgd = _sig(gd)
        dyd = dy_ref[:, D:2 * D].astype(F32)
        silz, sild = zn * sgn, gd * sgd
        dz_ref[:, _cols(5)] = (dyd * silz * _dsilu(sgd, sild)).astype(BF16)
        dzn = dyd * sild * _dsilu(sgn, silz)
        ddng_ref[...] += _rowsum(dzn * zh)
        ddnb_ref[...] += _rowsum(dzn)
        dcz = _ln_rows_bwd(dzn, zh, zrs, dng_ref[...])
        ddcb_ref[...] += _rowsum(dcz)
        edz[0:tm, :] = dcz
        _shifted_copies(edz, esh, tm + HC - SUB)
        for r0 in range(0, tm, RC):
            for c0 in range(0, D, CB):
                r1, c1 = r0 + RC, c0 + CB
                a = z_ref[r0:r1, 3 * D + c0:3 * D + c1].astype(F32)
                sgb = _sig(z_ref[r0:r1, 4 * D + c0:4 * D + c1].astype(F32))
                zg = a * sgb
                dzg = jnp.zeros((RC, CB), F32)
                for k in range(KD):
                    sh = _tap(edz, esh, KD - 1 - k, r0, r1, slice(c0, c1))
                    dzg = dzg + dw_ref[k:k + 1, c0:c1] * sh
                    prod = zg * sh
                    ddw8[k, :, c0:c1] += prod[0:SUB] + prod[SUB:2 * SUB]
                dz_ref[r0:r1, 3 * D + c0:3 * D + c1] = (dzg * sgb).astype(BF16)
                dz_ref[r0:r1, 4 * D + c0:4 * D + c1] = (dzg * a * sgb * (1.0 - sgb)).astype(BF16)
        edz[tm:tm + HC, :] = edz[0:HC, :]

        @pl.when(i == nt - 1)
        def _():
            ddw_ref[...] = jnp.sum(ddw8[...], axis=1)

        pl.when(i == nt - 1)(finish)

    rev = lambda i: (nt - 1 - i, 0)
    vec = pl.BlockSpec((1, D), lambda i: (0, 0))
    wspec = pl.BlockSpec((HEADS, SGU_BLOCK, SGU_BLOCK), lambda i: (0, 0, 0))
    bspec = pl.BlockSpec((HEADS, SGU_BLOCK, 1), lambda i: (0, 0, 0))
    kspec = pl.BlockSpec((KD, D), lambda i: (0, 0))
    return pl.pallas_call(
        body, name=f"mix_odd_bwd_{layer}", grid=(nt,),
        in_specs=[pl.BlockSpec((tm, NIN), rev), pl.BlockSpec((tm, DMIX), rev), pl.BlockSpec((tm, D), rev),
                  vec, vec, wspec, wspec, bspec, kspec, vec, vec, ANY],
        out_specs=[pl.BlockSpec((tm, NIN), rev), vec, vec, wspec, bspec, kspec, vec, vec, vec, ANY],
        out_shape=[jax.ShapeDtypeStruct((s, NIN), BF16), jax.ShapeDtypeStruct((1, D), F32),
                   jax.ShapeDtypeStruct((1, D), F32),
                   jax.ShapeDtypeStruct((HEADS, SGU_BLOCK, SGU_BLOCK), F32),
                   jax.ShapeDtypeStruct((HEADS, SGU_BLOCK, 1), F32), jax.ShapeDtypeStruct((KD, D), F32),
                   jax.ShapeDtypeStruct((1, D), F32), jax.ShapeDtypeStruct((1, D), F32),
                   jax.ShapeDtypeStruct((1, D), F32), jax.ShapeDtypeStruct(parts.shape, parts.dtype)],
        scratch_shapes=[pltpu.VMEM((tm, D), F32), pltpu.VMEM((tm + HC, D), F32),
                        pltpu.VMEM((SUB - 1, tm + HC - SUB, D), F32), pltpu.VMEM((KD, SUB, D), F32)]
        + _comm_scratch(1),
        compiler_params=_cp(("arbitrary",)),
    )(z, dycat, cz, slg, slb, wsb, wstb, sbcol, dw, dng, dnb, parts)


def _adamw_math(w, g, m, v):
    m = ADAM_B1 * m + (1.0 - ADAM_B1) * g
    v = ADAM_B2 * v + (1.0 - ADAM_B2) * (g * g)
    m_hat = m / (1.0 - ADAM_B1 ** ADAM_STEP)
    v_hat = v / (1.0 - ADAM_B2 ** ADAM_STEP)
    delta = -ADAM_LR * (m_hat / (jnp.sqrt(v_hat) + ADAM_EPS) + ADAM_WD * w)
    return delta, m, v


def adamw_big(w, m, v, parts0, parts1, name):
    _, r, c = w.shape
    tr = min(r, 256)
    nr = r // tr

    def body(w_ref, m_ref, v_ref, p0_ref, p1_ref, g_ref, d_ref, nm_ref, nv_ref):
        i = pl.program_id(0)

        def total(p_ref):
            acc = p_ref[0].astype(F32)
            for j in range(1, NDEV):
                acc = acc + p_ref[j].astype(F32)
            return acc

        @pl.when(i == 0)
        def _():
            g_ref[...] = total(p0_ref)

        @pl.when(i == 1)
        def _():
            g_ref[...] = total(p1_ref)

        delta, nm, nv = _adamw_math(w_ref[...], g_ref[...], m_ref[...], v_ref[...])
        d_ref[...] = delta
        nm_ref[...] = nm
        nv_ref[...] = nv

    wspec = pl.BlockSpec((None, tr, c), lambda i, j: (i, j, 0))
    p0 = pl.BlockSpec((NDEV, tr, c), lambda i, j: (0, jnp.where(i == 0, j, nr - 1), 0))
    p1 = pl.BlockSpec((NDEV, tr, c), lambda i, j: (0, jnp.where(i == 1, j, 0), 0))
    shp = jax.ShapeDtypeStruct(w.shape, F32)
    return pl.pallas_call(
        body, name=name, grid=(2, nr), in_specs=[wspec, wspec, wspec, p0, p1],
        out_specs=[wspec] * 4, out_shape=[shp] * 4,
        compiler_params=_cp(("arbitrary", "arbitrary")),
    )(w, m, v, parts0, parts1)


def sum_parts(parts, name):
    _, r, c = parts.shape

    def body(p_ref, o_ref):
        acc = p_ref[0]
        for j in range(1, NDEV):
            acc = acc + p_ref[j]
        o_ref[...] = acc

    return pl.pallas_call(body, name=name, out_shape=jax.ShapeDtypeStruct((r, c), F32),
                          compiler_params=_cp())(parts)


def adamw_small(w, g, m, v, name):
    def body(w_ref, g_ref, m_ref, v_ref, d_ref, nm_ref, nv_ref):
        delta, nm, nv = _adamw_math(w_ref[...], g_ref[...], m_ref[...], v_ref[...])
        d_ref[...] = delta
        nm_ref[...] = nm
        nv_ref[...] = nv

    shp = jax.ShapeDtypeStruct(w.shape, F32)
    return pl.pallas_call(body, name=name, out_shape=[shp] * 3, compiler_params=_cp())(w, g, m, v)


def _size(shape):
    n = 1
    for d in shape:
        n *= d
    return n


def _pack(arrays, rows):
    flat = jnp.concatenate([a.reshape(-1) for a in arrays])
    return jnp.pad(flat, (0, rows * 128 - flat.shape[0])).reshape(rows, 128)


def _unpack(flat, shapes):
    out, o = [], 0
    for shp in shapes:
        out.append(flat[o:o + _size(shp)].reshape(shp))
        o += _size(shp)
    return out


def _rows_for(shapes):
    return -(-sum(_size(shp) for shp in shapes) // 1024) * 8


SHARDED_SMALL = (("pool_w", (2, 4, 256, 256), 2), ("sconv_w", (2, KS, D), 2), ("sgu_ln_g", (2, D), 1),
                 ("sgu_ln_b", (2, D), 1), ("dconv_w", (2, KD, D), 2), ("dconv_b", (2, D), 1),
                 ("dnorm_g", (2, D), 1), ("dnorm_b", (2, D), 1))
REPLICATED_SMALL = (("ln_g", (DEPTH, D)), ("ln_b", (DEPTH, D)), ("pool_scale", (2, D)), ("sconv_b", (2, D)),
                    ("sgu_w", (2, HEADS, SGU_BLOCK, SGU_BLOCK)), ("sgu_b", (2, HEADS, SGU_BLOCK)))


def _shard_shape(shape, axis):
    return tuple(d // NDEV if a == axis else d for a, d in enumerate(shape))


def _merge_gathered(g, shape, axis):
    return jnp.moveaxis(g, 0, axis).reshape(shape)


def kernel(x, ln_g, ln_b, w_in_even, w_out_even, pool_w, pool_scale, sconv_w, sconv_b, w_in_odd, w_out_odd, sgu_ln_g, sgu_ln_b, sgu_w, sgu_b, dconv_w, dconv_b, dnorm_g, dnorm_b, loss_target, m_ln_g, m_ln_b, m_w_in_even, m_w_out_even, m_pool_w, m_pool_scale, m_sconv_w, m_sconv_b, m_w_in_odd, m_w_out_odd, m_sgu_ln_g, m_sgu_ln_b, m_sgu_w, m_sgu_b, m_dconv_w, m_dconv_b, m_dnorm_g, m_dnorm_b, v_ln_g, v_ln_b, v_w_in_even, v_w_out_even, v_pool_w, v_pool_scale, v_sconv_w, v_sconv_b, v_w_in_odd, v_w_out_odd, v_sgu_ln_g, v_sgu_ln_b, v_sgu_w, v_sgu_b, v_dconv_w, v_dconv_b, v_dnorm_g, v_dnorm_b):
    given = dict(locals())
    me = 4 * lax.axis_index("x") + 2 * lax.axis_index("y") + lax.axis_index("c")
    xs = x[0]
    target = loss_target[0]

    wc_in = cast_weights(w_in_even, w_in_odd, "cast_w_in")
    wc_out = cast_weights(w_out_even, w_out_odd, "cast_w_out")
    shard_shapes = [_shard_shape(shp, ax) for _, shp, ax in SHARDED_SMALL]
    srows = _rows_for(shard_shapes)
    xb, wg_first, gathered = cast_x(xs, wc_in, _pack([given[n] for n, _, _ in SHARDED_SMALL], srows))
    wg_in = [wg_first] + [None] * (DEPTH - 1)
    wg_out = [None] * DEPTH
    gathered = gathered.reshape(NDEV, -1)
    full, o = {}, 0
    for (n, shp, ax), sshp in zip(SHARDED_SMALL, shard_shapes):
        full[n] = _merge_gathered(gathered[:, o:o + _size(sshp)].reshape((NDEV,) + sshp), shp, ax)
        o += _size(sshp)

    mask = (jnp.arange(SGU_BLOCK)[None, :] // 64) <= (jnp.arange(SGU_BLOCK)[:, None] // 64)
    ws = jnp.where(mask[None, None], sgu_w, 0.0)
    wsb = ws.astype(BF16)
    wstb = jnp.swapaxes(ws, -1, -2).astype(BF16)
    pwb = full["pool_w"].astype(BF16)

    ones = jnp.ones((1, D), F32)
    zeros = jnp.zeros((1, D), F32)

    xres, gp, bp = xs, ones, zeros
    saved, below = [], None
    for layer in range(DEPTH):
        i = layer // 2
        if layer % 2 == 0:
            params = (pwb[i], pool_scale[i][None], full["sconv_w"][i], sconv_b[i][None])
        else:
            params = (full["sgu_ln_g"][i][None], full["sgu_ln_b"][i][None], wsb[i], sgu_b[i][:, :, None],
                      full["dconv_w"][i], full["dconv_b"][i][None], full["dnorm_g"][i][None],
                      full["dnorm_b"][i][None])
        outs = list(proj_in(wg_in[layer], layer, wc_in, wc_out, xb=xb if below is None else None, below=below))
        if layer + 1 < DEPTH:
            wg_in[layer + 1] = outs.pop()
        wg_out[layer] = outs.pop()
        z = outs[0]
        if below is not None:
            xhat, rstd, xb = outs[1:4]
            saved[-1] += [xhat, rstd]
            xres, gp, bp = xhat, below[5], below[6]
        outs = mix_fwd(z, layer, params)
        ycat, extra = outs[0], tuple(outs[1:])
        saved.append([xb, z, ycat, extra])
        below = (ycat, wg_out[layer], xres, gp, bp, ln_g[layer][None], ln_b[layer][None])
    *top, loss_local = head_loss_bwd(below, target)

    gsmall = {n: [None] * shp[0] for n, shp in REPLICATED_SMALL}
    gsmall.update({n: [None] * shp[0] for n, shp, _ in SHARDED_SMALL})
    parts_in, parts_out, parts_pw = [None] * DEPTH, [None] * DEPTH, [None] * 2
    small_names = [n for n, _ in REPLICATED_SMALL] + [n for n, _, _ in SHARDED_SMALL]
    gathered_names = [n for n in small_names if n != "pool_w"]
    gathered_shapes = [shp for n, shp in REPLICATED_SMALL] + [shp for n, shp, _ in SHARDED_SMALL if n != "pool_w"]
    grows = _rows_for(gathered_shapes + [(1,)])
    pending, above = None, None
    for layer in reversed(range(DEPTH)):
        i = layer // 2
        xb, z, ycat, extra = saved[layer][:4]
        if above is None:
            dr, drb, dycat, dg, db = top
        else:
            dr, drb, dycat, dg, db = ln_bwd_dycat(*saved[layer][4:], ln_g[layer][None], wg_out[layer], layer, above)
        gsmall["ln_g"][layer], gsmall["ln_b"][layer] = dg[0], db[0]
        dwo = dw_out(ycat, drb, layer)
        own_out = pending is None
        riding = dwo if own_out else pending
        if layer % 2 == 0:
            dzb, dpw, dps, dsw, dsb, landed = mix_even_bwd(
                z, dycat, extra[0], extra[1], pwb[i], pool_scale[i][None], full["sconv_w"][i], layer, parts=riding)
            gsmall["pool_scale"][i] = dps[0]
            gsmall["sconv_w"][i], gsmall["sconv_b"][i] = dsw, dsb[0]
        else:
            dzb, dslg, dslb, dws, dsbc, ddw, ddcb, ddng, ddnb, landed = mix_odd_bwd(
                z, dycat, extra[0], full["sgu_ln_g"][i][None], full["sgu_ln_b"][i][None], wsb[i], wstb[i],
                sgu_b[i][:, :, None], full["dconv_w"][i], full["dnorm_g"][i][None], full["dnorm_b"][i][None],
                layer, riding)
            gsmall["sgu_ln_g"][i], gsmall["sgu_ln_b"][i] = dslg[0], dslb[0]
            gsmall["sgu_w"][i], gsmall["sgu_b"][i] = jnp.where(mask[None], dws, 0.0), dsbc[:, :, 0]
            gsmall["dconv_w"][i], gsmall["dconv_b"][i] = ddw, ddcb[0]
            gsmall["dnorm_g"][i], gsmall["dnorm_b"][i] = ddng[0], ddnb[0]
        riders = [] if own_out else [dwo]
        if layer % 2 == 0:
            riders.append(dpw.reshape(4, NDEV, PG // NDEV, PG).transpose(1, 0, 2, 3))
        small_partial = []
        if layer == 0:
            small_partial.append(_pack([jnp.stack(gsmall[n]) for n in gathered_names] + [loss_local], grows))
        outs = list(dw_in(xb, dzb, layer, riders, small_partial))
        if layer == 0:
            small_gathered = outs.pop()
        if layer % 2 == 0:
            parts_pw[i] = outs.pop()
        if own_out:
            parts_out[layer] = landed
        else:
            parts_in[layer + 1] = landed
            parts_out[layer] = outs[1]
        pending = outs[0]
        above = (dzb, wg_in[layer], dr)
    dxn, parts_in[0] = dx_in(dzb, wg_in[0], dr, pending)
    grad_x = dxn[None]

    big = {}
    big["w_in_even"] = adamw_big(w_in_even, m_w_in_even, v_w_in_even, parts_in[0], parts_in[2], "adamw_w_in_even")
    big["w_in_odd"] = adamw_big(w_in_odd, m_w_in_odd, v_w_in_odd, parts_in[1], parts_in[3], "adamw_w_in_odd")
    big["w_out_even"] = adamw_big(w_out_even, m_w_out_even, v_w_out_even, parts_out[0], parts_out[2], "adamw_w_out_even")
    big["w_out_odd"] = adamw_big(w_out_odd, m_w_out_odd, v_w_out_odd, parts_out[1], parts_out[3], "adamw_w_out_odd")

    gsum = sum_parts(small_gathered, "sum_small_grads")
    unpacked = _unpack(gsum.reshape(-1), gathered_shapes + [()])
    loss = unpacked.pop()
    gfull = dict(zip(gathered_names, unpacked))
    own_shapes = [shp for _, shp in REPLICATED_SMALL] + shard_shapes
    gown = {n: gfull[n] for n, _ in REPLICATED_SMALL}
    for (n, shp, ax), sshp in zip(SHARDED_SMALL, shard_shapes):
        if n == "pool_w":
            gown[n] = jnp.stack([sum_parts(p.reshape(NDEV, -1, 128), f"sum_pool_w_{j}").reshape(sshp[1:])
                                 for j, p in enumerate(parts_pw)])
        else:
            gown[n] = lax.dynamic_slice_in_dim(gfull[n], me * sshp[ax], sshp[ax], axis=ax)
    orows = _rows_for(own_shapes)
    packed = [_pack([src[n] for n in small_names], orows) for src in
              (given, gown, {n: given["m_" + n] for n in small_names}, {n: given["v_" + n] for n in small_names})]
    sd, sm, sv = adamw_small(*packed, "adamw_small")
    small = {}
    for n, d_, m_, v_ in zip(small_names, _unpack(sd.reshape(-1), own_shapes), _unpack(sm.reshape(-1), own_shapes),
                             _unpack(sv.reshape(-1), own_shapes)):
        small[n] = (gown[n], d_, m_, v_)

    weights = ['ln_g', 'ln_b', 'w_in_even', 'w_out_even', 'pool_w', 'pool_scale', 'sconv_w', 'sconv_b', 'w_in_odd',
               'w_out_odd', 'sgu_ln_g', 'sgu_ln_b', 'sgu_w', 'sgu_b', 'dconv_w', 'dconv_b', 'dnorm_g', 'dnorm_b']
    res = {n: (big[n] if n in big else small[n]) for n in weights}
    return (loss, grad_x, *[res[n][0] for n in weights], *[res[n][1] for n in weights],
            *[res[n][2] for n in weights], *[res[n][3] for n in weights])
```
